```python
import math
import jax, jax.numpy as jnp
from jax import lax
import numpy as np

D_MODEL = 1024
BATCH = 8
SEQ = 2048
DEPTH = 1

HEAD_DIM = 64
FOX_HEADS = 8
FOX_WIDTH = FOX_HEADS * HEAD_DIM
MLSTM_HEADS = 8
MLSTM_QK_DIM = 32
MLSTM_V_DIM = 64
MLSTM_QK_WIDTH = MLSTM_HEADS * MLSTM_QK_DIM
MLSTM_V_WIDTH = MLSTM_HEADS * MLSTM_V_DIM
MIX_WIDTH = FOX_WIDTH + MLSTM_V_WIDTH
CONV_WIDTH = 4
FOX_BLOCK = 128
MLSTM_CHUNK = 64
SPLIT_WIDTHS = (FOX_WIDTH, FOX_WIDTH, FOX_WIDTH, FOX_HEADS,
                MLSTM_QK_WIDTH, MLSTM_QK_WIDTH, MLSTM_V_WIDTH,
                MLSTM_HEADS, MLSTM_HEADS, MLSTM_V_WIDTH)
IN_PROJ_WIDTH = sum(SPLIT_WIDTHS)
N_EXPERTS = 32
TOP_K = 4
D_EXPERT = D_MODEL
SWIGLU_ALPHA = 1.702
SWIGLU_LIMIT = 7.0
MOE_BLOCK = 128
NORM_EPS = 1e-5

kernel_name = "hymba_fox_mlstm_moe_layer"


def rms_norm(x, w):
    xf = x.astype(jnp.float32)
    y = xf * lax.rsqrt(jnp.mean(xf * xf, axis=-1, keepdims=True) + NORM_EPS)
    return (y * w.astype(jnp.float32)).astype(x.dtype)


def to_heads(t, n_heads):
    b, s, w = t.shape
    return t.reshape(b, s, n_heads, w // n_heads).transpose(0, 2, 1, 3)


def head_group_norm(t, w):
    b, h, s, d = t.shape
    tf = t.astype(jnp.float32)
    tf = tf * lax.rsqrt(jnp.mean(tf * tf, axis=-1, keepdims=True) + NORM_EPS)
    return tf.transpose(0, 2, 1, 3).reshape(b, s, h * d) * w.astype(jnp.float32)


def causal_depthwise_conv(u, w):
    c = u.shape[-1]
    return lax.conv_general_dilated(
        u, w.astype(u.dtype)[:, None, :], window_strides=(1,),
        padding=[(CONV_WIDTH - 1, 0)], dimension_numbers=('NWC', 'WIO', 'NWC'),
        feature_group_count=c)


def forgetting_attention(q, k, v, log_f):
    seq = q.shape[2]
    cum = jnp.cumsum(log_f, axis=-1)
    scale = 1.0 / math.sqrt(HEAD_DIM)
    outs = []
    for blk in range(seq // FOX_BLOCK):
        start, end = blk * FOX_BLOCK, (blk + 1) * FOX_BLOCK
        qb = q[:, :, start:end].astype(jnp.float32)
        kb = k[:, :, :end].astype(jnp.float32)
        vb = v[:, :, :end].astype(jnp.float32)
        scores = jnp.einsum('bhqd,bhkd->bhqk', qb, kb) * scale
        scores = scores + cum[:, :, start:end, None] - cum[:, :, None, :end]
        q_pos = start + jnp.arange(FOX_BLOCK)
        mask = jnp.arange(end)[None, :] <= q_pos[:, None]
        scores = jnp.where(mask, scores, -jnp.inf)
        probs = jax.nn.softmax(scores, axis=-1)
        outs.append(jnp.einsum('bhqk,bhkd->bhqd', probs, vb))
    return jnp.concatenate(outs, axis=2)


def mlstm_chunkwise(q, k, v, i_pre, log_f):
    b, h, seq, dk = q.shape
    dv = v.shape[-1]
    nc = seq // MLSTM_CHUNK
    k = k * (1.0 / math.sqrt(dk))

    def chunks(t):
        return jnp.moveaxis(t.reshape(t.shape[:2] + (nc, MLSTM_CHUNK) + t.shape[3:]), 2, 0)

    qs, ks, vs = (chunks(t.astype(jnp.float32)) for t in (q, k, v))
    is_, fs = chunks(i_pre), chunks(log_f)
    causal = jnp.tril(jnp.ones((MLSTM_CHUNK, MLSTM_CHUNK), dtype=bool))

    def step(carry, inp):
        c_s, n_s, m = carry
        qc, kc, vc, ic, fc = inp
        bcum = jnp.cumsum(fc, axis=-1)
        dmat = bcum[..., :, None] - bcum[..., None, :] + ic[..., None, :]
        dmat = jnp.where(causal, dmat, -jnp.inf)
        inter = bcum + m[..., None]
        m_t = jnp.maximum(inter, jnp.max(dmat, axis=-1))
        w_inter = jnp.exp(inter - m_t)
        s = jnp.einsum('bhtd,bhsd->bhts', qc, kc) * jnp.exp(dmat - m_t[..., None])
        num = (w_inter[..., None] * jnp.einsum('bhtd,bhde->bhte', qc, c_s)
               + jnp.einsum('bhts,bhse->bhte', s, vc))
        den = w_inter * jnp.einsum('bhtd,bhd->bht', qc, n_s) + jnp.sum(s, axis=-1)
        h_out = num / jnp.maximum(jnp.abs(den), jnp.exp(-m_t))[..., None]
        b_last = bcum[..., -1]
        g = b_last[..., None] - bcum + ic
        m_new = jnp.maximum(b_last + m, jnp.max(g, axis=-1))
        decay = jnp.exp(b_last + m - m_new)
        wk = jnp.exp(g - m_new[..., None])
        c_new = decay[..., None, None] * c_s + jnp.einsum('bhs,bhsd,bhse->bhde', wk, kc, vc)
        n_new = decay[..., None] * n_s + jnp.einsum('bhs,bhsd->bhd', wk, kc)
        return (c_new, n_new, m_new), h_out

    init = (jnp.zeros((b, h, dk, dv), jnp.float32), jnp.zeros((b, h, dk), jnp.float32),
            jnp.zeros((b, h), jnp.float32))
    _, hs = lax.scan(step, init, (qs, ks, vs, is_, fs))
    return jnp.moveaxis(hs, 0, 2).reshape(b, h, seq, dv)


def moe_ffn(h, router_w, router_b, w_gu, b_gu, w_down, b_down):
    bsz, seq, d = h.shape
    n_tok = bsz * seq
    hf = h.reshape(n_tok, d)
    logits = (hf @ router_w).astype(jnp.float32) + router_b.astype(jnp.float32)
    top_v, top_i = lax.top_k(logits, TOP_K)
    gates = jax.nn.softmax(top_v, axis=-1)

    n_slot = n_tok * TOP_K
    buf_len = -(-n_slot // MOE_BLOCK) * MOE_BLOCK + N_EXPERTS * MOE_BLOCK
    n_blocks = buf_len // MOE_BLOCK
    e_flat = top_i.reshape(-1)
    tok_flat = jnp.arange(n_slot, dtype=jnp.int32) // TOP_K
    order = jnp.argsort(e_flat)
    e_sorted, tok_sorted, w_sorted = e_flat[order], tok_flat[order], gates.reshape(-1)[order]
    counts = jnp.bincount(e_flat, length=N_EXPERTS)
    starts = jnp.cumsum(counts) - counts
    padded = ((counts + MOE_BLOCK - 1) // MOE_BLOCK) * MOE_BLOCK
    pad_end = jnp.cumsum(padded)
    pad_start = pad_end - padded
    pos = pad_start[e_sorted] + (jnp.arange(n_slot) - starts[e_sorted])
    buf_tok = jnp.zeros((buf_len,), jnp.int32).at[pos].set(tok_sorted)
    buf_w = jnp.zeros((buf_len,), jnp.float32).at[pos].set(w_sorted)
    block_e = jnp.minimum(
        jnp.searchsorted(pad_end, jnp.arange(n_blocks) * MOE_BLOCK, side='right'), N_EXPERTS - 1)

    def expert_block(args):
        idx, e = args
        xb = hf[idx]
        gu = xb @ w_gu[e] + b_gu[e]
        gate = jnp.minimum(gu[:, :D_EXPERT], SWIGLU_LIMIT)
        up = jnp.clip(gu[:, D_EXPERT:], -SWIGLU_LIMIT, SWIGLU_LIMIT)
        act = (up + 1.0) * (gate * jax.nn.sigmoid(SWIGLU_ALPHA * gate))
        return act @ w_down[e] + b_down[e]

    y = lax.map(expert_block, (buf_tok.reshape(n_blocks, MOE_BLOCK), block_e))
    y = y.reshape(buf_len, d).astype(jnp.float32) * buf_w[:, None]
    out = jnp.zeros((n_tok, d), jnp.float32).at[buf_tok].add(y)
    return out.reshape(bsz, seq, d).astype(h.dtype)


def setup_inputs(seed: int = 0) -> dict:
    key = jax.random.key(seed)
    ks = jax.random.split(key, 20)
    n = jax.random.normal
    L = DEPTH
    return {
        "x": n(ks[0], (BATCH, SEQ, D_MODEL), jnp.float32),
        "attn_norm_w": 1.0 + 0.02 * n(ks[1], (L, D_MODEL), jnp.float32),
        "w_in": n(ks[2], (L, D_MODEL, IN_PROJ_WIDTH), jnp.float32) * D_MODEL ** -0.5,
        "fox_f_bias": 3.0 + 0.5 * n(ks[3], (L, FOX_HEADS), jnp.float32),
        "fox_q_norm_w": 1.0 + 0.02 * n(ks[4], (L, HEAD_DIM), jnp.float32),
        "fox_k_norm_w": 1.0 + 0.02 * n(ks[5], (L, HEAD_DIM), jnp.float32),
        "fox_out_norm_w": 1.0 + 0.02 * n(ks[6], (L, FOX_WIDTH), jnp.float32),
        "mlstm_conv_w": 0.5 * n(ks[7], (L, CONV_WIDTH, 2 * MLSTM_QK_WIDTH), jnp.float32),
        "mlstm_i_bias": 0.1 * n(ks[8], (L, MLSTM_HEADS), jnp.float32),
        "mlstm_f_bias": 3.0 + 0.5 * n(ks[9], (L, MLSTM_HEADS), jnp.float32),
        "mlstm_out_norm_w": 1.0 + 0.02 * n(ks[10], (L, MLSTM_V_WIDTH), jnp.float32),
        "w_out": n(ks[11], (L, MIX_WIDTH, D_MODEL), jnp.float32) * MIX_WIDTH ** -0.5,
        "moe_norm_w": 1.0 + 0.02 * n(ks[12], (L, D_MODEL), jnp.float32),
        "router_w": n(ks[13], (L, D_MODEL, N_EXPERTS), jnp.float32) * D_MODEL ** -0.5,
        "router_b": 0.01 * n(ks[14], (L, N_EXPERTS), jnp.float32),
        "expert_w_gate_up": n(ks[15], (L, N_EXPERTS, D_MODEL, 2 * D_EXPERT), jnp.float32) * D_MODEL ** -0.5,
        "expert_b_gate_up": 0.01 * n(ks[16], (L, N_EXPERTS, 2 * D_EXPERT), jnp.float32),
        "expert_w_down": n(ks[17], (L, N_EXPERTS, D_EXPERT, D_MODEL), jnp.float32) * D_EXPERT ** -0.5,
        "expert_b_down": 0.01 * n(ks[18], (L, N_EXPERTS, D_MODEL), jnp.float32),
    }


def reference(x, attn_norm_w, w_in, fox_f_bias, fox_q_norm_w, fox_k_norm_w, fox_out_norm_w,
              mlstm_conv_w, mlstm_i_bias, mlstm_f_bias, mlstm_out_norm_w, w_out, moe_norm_w,
              router_w, router_b, expert_w_gate_up, expert_b_gate_up, expert_w_down,
              expert_b_down):
    split_at = [int(s) for s in np.cumsum(SPLIT_WIDTHS)[:-1]]
    for layer in range(DEPTH):
        h = rms_norm(x, attn_norm_w[layer])
        proj = h @ w_in[layer]
        (fq, fk, fv, ff, mq, mk, mv, mi, mf, mo) = jnp.split(proj, split_at, axis=-1)

        fq = rms_norm(to_heads(fq, FOX_HEADS), fox_q_norm_w[layer])
        fk = rms_norm(to_heads(fk, FOX_HEADS), fox_k_norm_w[layer])
        fv = to_heads(fv, FOX_HEADS)
        fox_log_f = jax.nn.log_sigmoid(
            (ff + fox_f_bias[layer]).astype(jnp.float32)).transpose(0, 2, 1)
        fox_y = head_group_norm(forgetting_attention(fq, fk, fv, fox_log_f),
                                fox_out_norm_w[layer])

        qk = jax.nn.silu(causal_depthwise_conv(jnp.concatenate([mq, mk], axis=-1),
                                               mlstm_conv_w[layer]))
        mq, mk = qk[..., :MLSTM_QK_WIDTH], qk[..., MLSTM_QK_WIDTH:]
        m_i = (mi + mlstm_i_bias[layer]).astype(jnp.float32).transpose(0, 2, 1)
        m_log_f = jax.nn.log_sigmoid(
            (mf + mlstm_f_bias[layer]).astype(jnp.float32)).transpose(0, 2, 1)
        m_h = mlstm_chunkwise(to_heads(mq, MLSTM_HEADS), to_heads(mk, MLSTM_HEADS),
                              to_heads(mv, MLSTM_HEADS), m_i, m_log_f)
        mlstm_y = head_group_norm(m_h, mlstm_out_norm_w[layer]) * jax.nn.sigmoid(
            mo.astype(jnp.float32))

        mixed = jnp.concatenate([fox_y, mlstm_y], axis=-1).astype(x.dtype)
        x = x + mixed @ w_out[layer]

        h2 = rms_norm(x, moe_norm_w[layer])
        x = x + moe_ffn(h2, router_w[layer], router_b[layer], expert_w_gate_up[layer],
                        expert_b_gate_up[layer], expert_w_down[layer], expert_b_down[layer])
    return x
```

```python
import functools
import math

import jax
import jax.numpy as jnp
from jax import lax
from jax.experimental import pallas as pl
from jax.experimental.pallas import tpu as pltpu

F32 = jnp.float32
BF16 = jnp.bfloat16

D_MODEL = 1024
SEQ = 2048
HEAD_DIM = 64
FOX_HEADS = 8
FOX_WIDTH = FOX_HEADS * HEAD_DIM
MLSTM_HEADS = 8
MLSTM_QK_DIM = 32
MLSTM_V_DIM = 64
MLSTM_QK_WIDTH = MLSTM_HEADS * MLSTM_QK_DIM
MLSTM_V_WIDTH = MLSTM_HEADS * MLSTM_V_DIM
CONV_WIDTH = 4
MLSTM_CHUNK = 64
SPLIT_WIDTHS = (FOX_WIDTH, FOX_WIDTH, FOX_WIDTH, FOX_HEADS,
                MLSTM_QK_WIDTH, MLSTM_QK_WIDTH, MLSTM_V_WIDTH,
                MLSTM_HEADS, MLSTM_HEADS, MLSTM_V_WIDTH)
N_EXPERTS = 32
TOP_K = 4
D_EXPERT = D_MODEL
SWIGLU_ALPHA = 1.702
SWIGLU_LIMIT = 7.0
NORM_EPS = 1e-5

LANES = 128
SUBLANES = 8
V7X_VMEM_BYTES = 64 * 1024 * 1024

MAIN_WIDTH = 3 * FOX_WIDTH + 2 * MLSTM_QK_WIDTH + 2 * MLSTM_V_WIDTH
GATE_ROWS = 32

NT_DIMS = (((1,), (1,)), ((), ()))


def _vmem_limit(nbytes):
    return int(min(nbytes + (8 << 20), V7X_VMEM_BYTES - (4 << 20)))


def _log_sigmoid(x):
    return jnp.minimum(x, 0.0) - jnp.log(1.0 + jnp.exp(-jnp.abs(x)))


def _split_hi_lo(x):
    hi = x.astype(BF16)
    lo = (x - hi.astype(F32)).astype(BF16)
    return hi, lo


IN_TM = 512


def _in_proj_kernel(x_ref, nw_ref, w_ref, wgt_hi_ref, wgt_lo_ref, bt_ref,
                    wg_hi_ref, wg_lo_ref, br_ref, proj_ref, gt_ref, gtm_ref):
    x = x_ref[...]
    ms = jnp.mean(x * x, axis=-1, keepdims=True)
    y = x * lax.rsqrt(ms + NORM_EPS) * nw_ref[...]
    h_hi, h_lo = _split_hi_lo(y)
    proj_ref[...] = jnp.dot(h_hi, w_ref[...], preferred_element_type=F32)

    wt_hi, wt_lo = wgt_hi_ref[...], wgt_lo_ref[...]
    gt = (lax.dot_general(wt_hi, h_hi, NT_DIMS, preferred_element_type=F32)
          + lax.dot_general(wt_lo, h_hi, NT_DIMS, preferred_element_type=F32)
          + lax.dot_general(wt_hi, h_lo, NT_DIMS, preferred_element_type=F32))
    gt = gt + bt_ref[...]
    row = lax.broadcasted_iota(jnp.int32, gt.shape, 0)
    is_input_gate = (row >= FOX_HEADS) & (row < FOX_HEADS + MLSTM_HEADS)
    gt_ref[...] = jnp.where(is_input_gate, gt, _log_sigmoid(gt))

    w_hi, w_lo = wg_hi_ref[...], wg_lo_ref[...]
    g = (jnp.dot(h_hi, w_hi, preferred_element_type=F32)
         + jnp.dot(h_hi, w_lo, preferred_element_type=F32)
         + jnp.dot(h_lo, w_hi, preferred_element_type=F32))
    g = g + br_ref[...]
    lane = lax.broadcasted_iota(jnp.int32, g.shape, 1)
    is_input_gate = (lane >= FOX_HEADS) & (lane < FOX_HEADS + MLSTM_HEADS)
    gtm_ref[...] = jnp.where(is_input_gate, g, _log_sigmoid(g))


def _in_proj(x2d, norm_w, w_main, wgt_hi, wgt_lo, bias_t, wg_hi, wg_lo, bias_r):
    n = x2d.shape[0]
    tm = IN_TM
    const = lambda i: (0, 0)
    vmem = 2 * (tm * D_MODEL * 4 + D_MODEL * MAIN_WIDTH * 2 + tm * MAIN_WIDTH * 4
                + GATE_ROWS * tm * 4 + tm * LANES * 4) + tm * MAIN_WIDTH * 4
    return pl.pallas_call(
        _in_proj_kernel,
        grid=(n // tm,),
        in_specs=[
            pl.BlockSpec((tm, D_MODEL), lambda i: (i, 0)),
            pl.BlockSpec((1, D_MODEL), const),
            pl.BlockSpec((D_MODEL, MAIN_WIDTH), const),
            pl.BlockSpec((GATE_ROWS, D_MODEL), const),
            pl.BlockSpec((GATE_ROWS, D_MODEL), const),
            pl.BlockSpec((GATE_ROWS, 1), const),
            pl.BlockSpec((D_MODEL, LANES), const),
            pl.BlockSpec((D_MODEL, LANES), const),
            pl.BlockSpec((1, LANES), const),
        ],
        out_specs=[
            pl.BlockSpec((tm, MAIN_WIDTH), lambda i: (i, 0)),
            pl.BlockSpec((GATE_ROWS, tm), lambda i: (0, i)),
            pl.BlockSpec((tm, LANES), lambda i: (i, 0)),
        ],
        out_shape=[
            jax.ShapeDtypeStruct((n, MAIN_WIDTH), F32),
            jax.ShapeDtypeStruct((GATE_ROWS, n), F32),
            jax.ShapeDtypeStruct((n, LANES), F32),
        ],
        compiler_params=pltpu.CompilerParams(
            dimension_semantics=("arbitrary",), vmem_limit_bytes=_vmem_limit(vmem)),
        name="in_proj",
    )(x2d, norm_w, w_main, wgt_hi, wgt_lo, bias_t, wg_hi, wg_lo, bias_r)


FOX_TQ = 256
SCAN_SHIFTS = tuple(1 << i for i in range(int(math.log2(SEQ))))


def _rms(x, w):
    return x * lax.rsqrt(jnp.mean(x * x, axis=-1, keepdims=True) + NORM_EPS) * w


def _fox_kernel(q_ref, k_ref, v_ref, gt_ref, gtm_ref, qw_ref, kw_ref, ow_ref, o_ref,
                cumr_ref, cumc_ref):
    hp = pl.program_id(1)

    @pl.when(hp == 0)
    def _():
        x = gt_ref[...]
        lane = lax.broadcasted_iota(jnp.int32, x.shape, 1)
        for s in SCAN_SHIFTS:
            x = x + jnp.where(lane >= s, pltpu.roll(x, s, axis=1), 0.0)
        cumr_ref[...] = x
        y = gtm_ref[...]
        rowi = lax.broadcasted_iota(jnp.int32, y.shape, 0)
        for s in SCAN_SHIFTS:
            y = y + jnp.where(rowi >= s, pltpu.roll(y, s, axis=0), 0.0)
        cumc_ref[...] = y

    lane_t = lax.broadcasted_iota(jnp.int32, (SEQ, LANES), 1)
    for j in range(2):
        h = 2 * hp + j
        cs = slice(j * HEAD_DIM, (j + 1) * HEAD_DIM)
        kn = _rms(k_ref[0, :, cs], kw_ref[...]).astype(BF16)
        vb = v_ref[0, :, cs].astype(BF16)
        cum_row = cumr_ref[pl.ds(h, 1), :]
        cum_col = jnp.sum(jnp.where(lane_t == h, cumc_ref[...], 0.0),
                          axis=1, keepdims=True)
        for i in range(SEQ // FOX_TQ):
            qs = slice(i * FOX_TQ, (i + 1) * FOX_TQ)
            n = (i + 1) * FOX_TQ
            qn = (_rms(q_ref[0, qs, cs], qw_ref[...]) * (HEAD_DIM ** -0.5)).astype(BF16)
            s = lax.dot_general(qn, kn[:n], NT_DIMS, preferred_element_type=F32)
            s = s + (cum_col[qs] - cum_row[:, :n])
            col = lax.broadcasted_iota(jnp.int32, s.shape, 1)
            row = lax.broadcasted_iota(jnp.int32, s.shape, 0) + i * FOX_TQ
            s = jnp.where(col <= row, s, -jnp.inf)
            m = jnp.max(s, axis=-1, keepdims=True)
            p = jnp.exp(s - m)
            l = jnp.sum(p, axis=-1, keepdims=True)
            o = jnp.dot(p.astype(BF16), vb[:n], preferred_element_type=F32) / l
            o_ref[0, qs, cs] = _rms(o, ow_ref[:, cs])


def _fox(proj3, gt, gtm, qw, kw, ow):
    b = proj3.shape[0]
    nq = FOX_WIDTH // LANES
    blk = (1, SEQ, LANES)
    vmem = 2 * 5 * SEQ * LANES * 4 + 2 * SEQ * LANES * 4 + 6 * FOX_TQ * SEQ * 4
    return pl.pallas_call(
        _fox_kernel,
        grid=(b, nq),
        in_specs=[
            pl.BlockSpec(blk, lambda bi, hp: (bi, 0, hp)),
            pl.BlockSpec(blk, lambda bi, hp: (bi, 0, nq + hp)),
            pl.BlockSpec(blk, lambda bi, hp: (bi, 0, 2 * nq + hp)),
            pl.BlockSpec((SUBLANES, SEQ), lambda bi, hp: (0, bi)),
            pl.BlockSpec((SEQ, LANES), lambda bi, hp: (bi, 0)),
            pl.BlockSpec((1, HEAD_DIM), lambda bi, hp: (0, 0)),
            pl.BlockSpec((1, HEAD_DIM), lambda bi, hp: (0, 0)),
            pl.BlockSpec((1, LANES), lambda bi, hp: (0, hp)),
        ],
        out_specs=pl.BlockSpec(blk, lambda bi, hp: (bi, 0, hp)),
        out_shape=jax.ShapeDtypeStruct((b, SEQ, FOX_WIDTH), F32),
        scratch_shapes=[pltpu.VMEM((SUBLANES, SEQ), F32), pltpu.VMEM((SEQ, LANES), F32)],
        compiler_params=pltpu.CompilerParams(
            dimension_semantics=("arbitrary", "arbitrary"), vmem_limit_bytes=_vmem_limit(vmem)),
        name="fox",
    )(proj3, proj3, proj3, gt, gtm, qw, kw, ow)


ML_L = MLSTM_CHUNK
ML_PAIRS = SEQ // (2 * ML_L)
ML_HL = FOX_HEADS
ML_AUG = MLSTM_V_WIDTH + LANES
ML_TILE = 256
SEG_SHIFTS = tuple(1 << i for i in range(int(math.log2(ML_L))))


def _split3(x):
    a = x.astype(BF16)
    r = x - a.astype(F32)
    b = r.astype(BF16)
    c = (r - b.astype(F32)).astype(BF16)
    return a, b, c


def _expand_heads(x, exp_bf):
    a, b, c = _split3(x)
    return (jnp.dot(a, exp_bf, preferred_element_type=F32)
            + jnp.dot(b, exp_bf, preferred_element_type=F32)
            + jnp.dot(c, exp_bf, preferred_element_type=F32))


def _seg_scan(x, axis, op, ident):
    idx = lax.broadcasted_iota(jnp.int32, x.shape, axis) % ML_L
    for s in SEG_SHIFTS:
        x = op(x, jnp.where(idx >= s, pltpu.roll(x, s, axis=axis), ident))
    return x


def _mlstm_kernel(q_ref, k_ref, v_ref, gi_ref, gf_ref, gtm_ref, cw_ref, o_ref, den_ref,
                  qc_ref, kc_ref, kt_ref, rr_ref, cmr_ref, bcr_ref,
                  ealpha_ref, ew_ref, wint_ref, floor_ref, mfull_ref, caug_ref):
    def conv_silu(u, w):
        rowi = lax.broadcasted_iota(jnp.int32, u.shape, 0)
        acc = u * w[CONV_WIDTH - 1:CONV_WIDTH, :]
        for d in range(1, CONV_WIDTH):
            sh = jnp.where(rowi >= d, pltpu.roll(u, d, axis=0), 0.0)
            acc = acc + sh * w[CONV_WIDTH - 1 - d:CONV_WIDTH - d, :]
        return acc / (1.0 + jnp.exp(-acc))

    cw = cw_ref[...]
    qc_ref[...] = conv_silu(q_ref[0], cw[:, :MLSTM_QK_WIDTH]).astype(BF16)
    kc = conv_silu(k_ref[0], cw[:, MLSTM_QK_WIDTH:]) * (MLSTM_QK_DIM ** -0.5)
    kc_ref[...] = kc.astype(BF16)
    kt = kc.T
    for p in range(ML_PAIRS):
        kt_ref[p] = kt[:, p * LANES:(p + 1) * LANES]

    bcum_r = _seg_scan(gf_ref[...], 1, jnp.add, 0.0)
    r_r = gi_ref[...] - bcum_r
    cmx_r = _seg_scan(r_r, 1, jnp.maximum, -jnp.inf)
    for p in range(ML_PAIRS):
        ls = slice(p * LANES, (p + 1) * LANES)
        rr_ref[p] = r_r[:, ls]
        cmr_ref[p] = cmx_r[:, ls]
        bcr_ref[p] = bcum_r[:, ls]

    g = gtm_ref[...]
    lane_g = lax.broadcasted_iota(jnp.int32, g.shape, 1)
    head_lane = (lane_g >= ML_HL) & (lane_g < ML_HL + MLSTM_HEADS)
    bcum_c = jnp.where(
        head_lane, pltpu.roll(_seg_scan(g, 0, jnp.add, 0.0), LANES - MLSTM_HEADS, axis=1), 0.0)
    cmx_c = _seg_scan(jnp.where(head_lane, g, 0.0) - bcum_c, 0, jnp.maximum, -jnp.inf)
    m = jnp.zeros((1, LANES), F32)
    for c in range(SEQ // ML_L):
        mfull_ref[c * ML_L:(c + 1) * ML_L, :] = jnp.broadcast_to(m, (ML_L, LANES))
        last = (c + 1) * ML_L - 1
        m = bcum_c[last:last + 1, :] + jnp.maximum(m, cmx_c[last:last + 1, :])
    mfull = mfull_ref[...]
    mx = jnp.maximum(mfull, cmx_c)
    wint_ref[...] = jnp.exp(mfull - mx)
    floor_ref[...] = jnp.exp(-(bcum_c + mx))
    mfull_ref[...] = -mx

    lane_e = lax.broadcasted_iota(jnp.int32, (LANES, MLSTM_V_WIDTH), 1) // MLSTM_V_DIM
    row_e = lax.broadcasted_iota(jnp.int32, (LANES, MLSTM_V_WIDTH), 0)
    exp_bf = jnp.where(row_e == lane_e + ML_HL, 1.0, 0.0).astype(BF16)

    def expand_tile(i, carry):
        rows = pl.ds(pl.multiple_of(i * ML_TILE, ML_TILE), ML_TILE)
        ealpha_ref[rows, :] = _expand_heads(mfull_ref[rows, :], exp_bf)
        ew_ref[rows, :] = _expand_heads(wint_ref[rows, :], exp_bf)
        return carry

    lax.fori_loop(0, SEQ // ML_TILE, expand_tile, 0)

    kb_rowh = lax.broadcasted_iota(jnp.int32, (MLSTM_HEADS * ML_L, MLSTM_QK_WIDTH), 0) // ML_L
    kb_lane = lax.broadcasted_iota(jnp.int32, (MLSTM_HEADS * ML_L, MLSTM_QK_WIDTH), 1) // MLSTM_QK_DIM
    mask_k = jnp.where(kb_rowh == kb_lane, 1.0, 0.0).astype(BF16)
    va_rowh = lax.broadcasted_iota(jnp.int32, (MLSTM_HEADS * ML_L, ML_AUG), 0) // ML_L
    va_col = lax.broadcasted_iota(jnp.int32, (MLSTM_HEADS * ML_L, ML_AUG), 1)
    mask_v = jnp.where(
        (va_col // MLSTM_V_DIM == va_rowh) | (va_col == MLSTM_V_WIDTH + ML_HL + va_rowh),
        1.0, 0.0).astype(BF16)
    c_rowh = lax.broadcasted_iota(jnp.int32, (MLSTM_QK_WIDTH, ML_AUG), 0) // MLSTM_QK_DIM
    c_col = lax.broadcasted_iota(jnp.int32, (MLSTM_QK_WIDTH, ML_AUG), 1)
    mask_c = (c_col // MLSTM_V_DIM == c_rowh) | (c_col == MLSTM_V_WIDTH + ML_HL + c_rowh)
    lane128 = lax.broadcasted_iota(jnp.int32, (ML_L, LANES), 1)
    s_idx = lax.broadcasted_iota(jnp.int32, (ML_L, MLSTM_V_WIDTH), 1) % ML_L
    t_idx = lax.broadcasted_iota(jnp.int32, (ML_L, MLSTM_V_WIDTH), 0)
    causal = s_idx <= t_idx
    ones_aug = jnp.ones((ML_L, LANES), F32)

    caug_ref[...] = jnp.zeros_like(caug_ref)

    def pair_body(cp, m_row):
        r2 = rr_ref[cp]
        cm2 = cmr_ref[cp]
        b2 = bcr_ref[cp]
        r2r = pltpu.roll(r2, ML_L, axis=1)
        kt2 = kt_ref[cp]
        for cc in range(2):
            lo = cc * ML_L
            rows = pl.ds(pl.multiple_of(cp * (2 * ML_L), 2 * ML_L) + lo, ML_L)
            qa = qc_ref[rows, :]
            ka = kc_ref[rows, :]
            va_aug = jnp.concatenate([v_ref[0, rows, :], ones_aug], axis=1).astype(BF16)

            kbd = jnp.concatenate([ka] * MLSTM_HEADS, axis=0) * mask_k
            s = lax.dot_general(qa, kbd, NT_DIMS, preferred_element_type=F32)

            cmx_last = cm2[:, lo + ML_L - 1:lo + ML_L]
            b_last = b2[:, lo + ML_L - 1:lo + ML_L]
            mx_r = jnp.maximum(m_row, cmx_last)
            decay = jnp.exp(m_row - mx_r)
            wk = jnp.exp(r2[:, lo:lo + ML_L] - mx_r)
            m_row = b_last + mx_r

            src_e, src_o = (r2, r2r) if cc == 0 else (r2r, r2)
            cols = []
            for p in range(MLSTM_HEADS // 2):
                even = jnp.broadcast_to(src_e[2 * p:2 * p + 1, :], (ML_L, LANES))
                odd = jnp.broadcast_to(src_o[2 * p + 1:2 * p + 2, :], (ML_L, LANES))
                cols.append(jnp.where(lane128 < ML_L, even, odd))
            r_all = jnp.concatenate(cols, axis=1)
            arg = jnp.where(causal, ealpha_ref[rows, :] + r_all, -jnp.inf)
            p_all = (s * jnp.exp(arg)).astype(BF16)

            vbd = jnp.concatenate([va_aug] * MLSTM_HEADS, axis=0) * mask_v
            pv = jnp.dot(p_all, vbd, preferred_element_type=F32)
            qc_state = jnp.dot(qa, caug_ref[...].astype(BF16), preferred_element_type=F32)
            o_ref[0, rows, :] = (ew_ref[rows, :] * qc_state[:, :MLSTM_V_WIDTH]
                                 + pv[:, :MLSTM_V_WIDTH])
            den_ref[rows, :] = (wint_ref[rows, :] * qc_state[:, MLSTM_V_WIDTH:]
                                + pv[:, MLSTM_V_WIDTH:])

            wk_rows = jnp.concatenate(
                [jnp.broadcast_to(wk[h:h + 1, :], (MLSTM_QK_DIM, ML_L)) for h in range(MLSTM_HEADS)],
                axis=0)
            dec_rows = jnp.concatenate(
                [jnp.broadcast_to(decay[h:h + 1, :], (MLSTM_QK_DIM, 1)) for h in range(MLSTM_HEADS)],
                axis=0)
            ktw = (kt2[:, lo:lo + ML_L] * wk_rows).astype(BF16)
            upd = jnp.dot(ktw, va_aug, preferred_element_type=F32)
            caug_ref[...] = dec_rows * caug_ref[...] + jnp.where(mask_c, upd, 0.0)
        return m_row

    lax.fori_loop(0, ML_PAIRS, pair_body, jnp.zeros((MLSTM_HEADS, 1), F32))

    ob_row = lax.broadcasted_iota(jnp.int32, (MLSTM_V_WIDTH, LANES), 0) // MLSTM_V_DIM
    ob_col = lax.broadcasted_iota(jnp.int32, (MLSTM_V_WIDTH, LANES), 1)
    ones_bd = jnp.where(ob_col == ob_row + ML_HL, 1.0, 0.0).astype(BF16)

    def norm_tile(i, carry):
        rows = pl.ds(pl.multiple_of(i * ML_TILE, ML_TILE), ML_TILE)
        num = o_ref[0, rows, :]
        dn = jnp.maximum(jnp.abs(den_ref[rows, :]), floor_ref[rows, :])
        r = 1.0 / dn
        n2_hi, n2_lo = _split_hi_lo(num * num)
        msn = (jnp.dot(n2_hi, ones_bd, preferred_element_type=F32)
               + jnp.dot(n2_lo, ones_bd, preferred_element_type=F32)) * (1.0 / MLSTM_V_DIM)
        fac = r * lax.rsqrt(r * r * msn + NORM_EPS)
        o_ref[0, rows, :] = num * _expand_heads(fac, exp_bf)
        return carry

    lax.fori_loop(0, SEQ // ML_TILE, norm_tile, 0)


def _mlstm(proj3, gt, gtm, conv_w):
    b = proj3.shape[0]
    qk_blk = (1, SEQ, MLSTM_QK_WIDTH)
    v_blk = (1, SEQ, MLSTM_V_WIDTH)
    q_col = 3 * FOX_WIDTH // MLSTM_QK_WIDTH
    v_col = (3 * FOX_WIDTH + 2 * MLSTM_QK_WIDTH) // MLSTM_V_WIDTH
    vmem = (2 * (2 * SEQ * MLSTM_QK_WIDTH * 4 + 2 * SEQ * MLSTM_V_WIDTH * 4 + 2 * SEQ * LANES * 4)
            + 2 * SEQ * MLSTM_QK_WIDTH * 2 + SEQ * MLSTM_QK_WIDTH * 4 + 2 * SEQ * MLSTM_V_WIDTH * 4
            + 3 * SEQ * LANES * 4 + (8 << 20))
    return pl.pallas_call(
        _mlstm_kernel,
        grid=(b,),
        in_specs=[
            pl.BlockSpec(qk_blk, lambda bi: (bi, 0, q_col)),
            pl.BlockSpec(qk_blk, lambda bi: (bi, 0, q_col + 1)),
            pl.BlockSpec(v_blk, lambda bi: (bi, 0, v_col)),
            pl.BlockSpec((SUBLANES, SEQ), lambda bi: (1, bi)),
            pl.BlockSpec((SUBLANES, SEQ), lambda bi: (2, bi)),
            pl.BlockSpec((SEQ, LANES), lambda bi: (bi, 0)),
            pl.BlockSpec((CONV_WIDTH, 2 * MLSTM_QK_WIDTH), lambda bi: (0, 0)),
        ],
        out_specs=pl.BlockSpec(v_blk, lambda bi: (bi, 0, 0)),
        out_shape=jax.ShapeDtypeStruct((b, SEQ, MLSTM_V_WIDTH), F32),
        scratch_shapes=[
            pltpu.VMEM((SEQ, LANES), F32),
            pltpu.VMEM((SEQ, MLSTM_QK_WIDTH), BF16),
            pltpu.VMEM((SEQ, MLSTM_QK_WIDTH), BF16),
            pltpu.VMEM((ML_PAIRS, MLSTM_QK_WIDTH, LANES), F32),
            pltpu.VMEM((ML_PAIRS, SUBLANES, LANES), F32),
            pltpu.VMEM((ML_PAIRS, SUBLANES, LANES), F32),
            pltpu.VMEM((ML_PAIRS, SUBLANES, LANES), F32),
            pltpu.VMEM((SEQ, MLSTM_V_WIDTH), F32),
            pltpu.VMEM((SEQ, MLSTM_V_WIDTH), F32),
            pltpu.VMEM((SEQ, LANES), F32),
            pltpu.VMEM((SEQ, LANES), F32),
            pltpu.VMEM((SEQ, LANES), F32),
            pltpu.VMEM((MLSTM_QK_WIDTH, ML_AUG), F32),
        ],
        compiler_params=pltpu.CompilerParams(
            dimension_semantics=("arbitrary",), vmem_limit_bytes=_vmem_limit(vmem)),
        name="mlstm",
    )(proj3, proj3, proj3, gt, gt, gtm, conv_w)


RT_TM = 512


def _out_route_kernel(x_ref, fy_ref, my_ref, mo_ref, wo_ref, mg_ref, nw_ref, wr_hi_ref, wr_lo_ref,
                      rb_ref, x1_ref, h2_ref, eidx_ref, gate_ref, rank_ref, cnt_ref, carry_ref):
    i = pl.program_id(0)

    @pl.when(i == 0)
    def _():
        carry_ref[...] = jnp.zeros_like(carry_ref)

    my = my_ref[...] * mg_ref[...] / (1.0 + jnp.exp(-mo_ref[...]))
    mixed = (jnp.dot(fy_ref[...].astype(BF16), wo_ref[:FOX_WIDTH, :], preferred_element_type=F32)
             + jnp.dot(my.astype(BF16), wo_ref[FOX_WIDTH:, :], preferred_element_type=F32))
    x1 = x_ref[...] + mixed
    x1_ref[...] = x1
    h2 = _rms(x1, nw_ref[...])
    h2_ref[...] = h2

    h_hi, h_lo = _split_hi_lo(h2)
    wr_hi, wr_lo = wr_hi_ref[...], wr_lo_ref[...]
    logit = (lax.dot_general(wr_hi, h_hi, NT_DIMS, preferred_element_type=F32)
             + lax.dot_general(wr_lo, h_hi, NT_DIMS, preferred_element_type=F32)
             + lax.dot_general(wr_hi, h_lo, NT_DIMS, preferred_element_type=F32)) + rb_ref[...]

    e_iota = lax.broadcasted_iota(jnp.int32, logit.shape, 0).astype(F32)
    vals, idxs, hots = [], [], []
    for _ in range(TOP_K):
        mk = jnp.max(logit, axis=0, keepdims=True)
        idx = jnp.min(jnp.where(logit == mk, e_iota, float(N_EXPERTS)), axis=0, keepdims=True)
        hot = e_iota == idx
        logit = jnp.where(hot, -jnp.inf, logit)
        vals.append(mk)
        idxs.append(idx.astype(jnp.int32))
        hots.append(hot)
    exps = [jnp.exp(v - vals[0]) for v in vals]
    tot = exps[0] + exps[1] + exps[2] + exps[3]
    gates = [e / tot for e in exps]

    assign = jnp.zeros(logit.shape, F32)
    for hot in hots:
        assign = assign + jnp.where(hot, 1.0, 0.0)
    tm = logit.shape[1]
    src = lax.broadcasted_iota(jnp.int32, (tm, tm), 0)
    dst = lax.broadcasted_iota(jnp.int32, (tm, tm), 1)
    upper = jnp.where(src < dst, 1.0, 0.0).astype(BF16)
    base = jnp.dot(assign.astype(BF16), upper, preferred_element_type=F32) + carry_ref[:, 0:1]
    ranks = [jnp.sum(jnp.where(hot, base, 0.0), axis=0, keepdims=True) for hot in hots]
    new_carry = carry_ref[...] + jnp.sum(assign, axis=1, keepdims=True)
    carry_ref[...] = new_carry
    cnt_ref[...] = new_carry

    zi = jnp.zeros((SUBLANES - TOP_K, tm), jnp.int32)
    eidx_ref[...] = jnp.concatenate(idxs + [zi], axis=0)
    rank_ref[...] = jnp.concatenate([r.astype(jnp.int32) for r in ranks] + [zi], axis=0)
    gate_ref[...] = jnp.concatenate(gates + [zi.astype(F32)], axis=0)


def _out_route(x2d, fox_y2d, mlstm_y2d, proj, w_out_bf, mlstm_gain, moe_norm_w, wr_hi, wr_lo, rb):
    n = x2d.shape[0]
    tm = RT_TM
    const = lambda i: (0, 0)
    mo_col = (MAIN_WIDTH - MLSTM_V_WIDTH) // MLSTM_V_WIDTH
    row_blk = lambda w: pl.BlockSpec((tm, w), lambda i: (i, 0))
    lane_blk = pl.BlockSpec((SUBLANES, tm), lambda i: (0, i))
    vmem = (2 * (tm * D_MODEL * 4 * 3 + tm * FOX_WIDTH * 4 * 3 + D_MODEL * D_MODEL * 2)
            + 6 * tm * D_MODEL * 4 + tm * tm * 6)
    return pl.pallas_call(
        _out_route_kernel,
        grid=(n // tm,),
        in_specs=[
            row_blk(D_MODEL), row_blk(FOX_WIDTH), row_blk(MLSTM_V_WIDTH),
            pl.BlockSpec((tm, MLSTM_V_WIDTH), lambda i: (i, mo_col)),
            pl.BlockSpec((D_MODEL, D_MODEL), const),
            pl.BlockSpec((1, MLSTM_V_WIDTH), const),
            pl.BlockSpec((1, D_MODEL), const),
            pl.BlockSpec((N_EXPERTS, D_MODEL), const),
            pl.BlockSpec((N_EXPERTS, D_MODEL), const),
            pl.BlockSpec((N_EXPERTS, 1), const),
        ],
        out_specs=[row_blk(D_MODEL), row_blk(D_MODEL), lane_blk, lane_blk, lane_blk,
                   pl.BlockSpec((N_EXPERTS, LANES), const)],
        out_shape=[
            jax.ShapeDtypeStruct((n, D_MODEL), F32),
            jax.ShapeDtypeStruct((n, D_MODEL), F32),
            jax.ShapeDtypeStruct((SUBLANES, n), jnp.int32),
            jax.ShapeDtypeStruct((SUBLANES, n), F32),
            jax.ShapeDtypeStruct((SUBLANES, n), jnp.int32),
            jax.ShapeDtypeStruct((N_EXPERTS, LANES), F32),
        ],
        scratch_shapes=[pltpu.VMEM((N_EXPERTS, LANES), F32)],
        compiler_params=pltpu.CompilerParams(
            dimension_semantics=("arbitrary",), vmem_limit_bytes=_vmem_limit(vmem)),
        name="out_route",
    )(x2d, fox_y2d, mlstm_y2d, proj, w_out_bf, mlstm_gain, moe_norm_w, wr_hi, wr_lo, rb)


EX_BM = 256


def _experts_kernel(be_ref, nu_ref, nv_ref, tok_ref, tokn_ref, dst_ref, h2_hbm, wgu_ref, bgu_ref,
                    wd_ref, bd_ref, y_hbm, xb_ref, yb_ref, wgu_bf_ref, wd_bf_ref, gsem, ssem):
    i = pl.program_id(0)
    nb = pl.num_programs(0)
    nu = nu_ref[0]
    slot = i % 2

    def start_gather(idx_ref, s):
        for r in range(EX_BM):
            pltpu.make_async_copy(h2_hbm.at[pl.ds(idx_ref[0, 0, r], 1), :],
                                  xb_ref.at[s, pl.ds(r, 1), :], gsem.at[s]).start()

    def scatter_row(r):
        pltpu.make_async_copy(yb_ref.at[pl.ds(r, 1), :],
                              y_hbm.at[pl.ds(dst_ref[0, 0, r], 1), :], ssem.at[0]).start()

    def wait_block(sem, buf):
        pltpu.make_async_copy(h2_hbm.at[pl.ds(0, EX_BM), :], buf, sem).wait()

    def wait_scatter(n):
        @pl.when(n == EX_BM)
        def _():
            wait_block(ssem.at[0], yb_ref)

        @pl.when(n < EX_BM)
        def _():
            def body(r, carry):
                pltpu.make_async_copy(h2_hbm.at[pl.ds(0, 1), :], yb_ref.at[pl.ds(0, 1), :],
                                      ssem.at[0]).wait()
                return carry
            lax.fori_loop(0, n, body, 0)

    @pl.when(i == 0)
    def _():
        start_gather(tok_ref, 0)

    @pl.when(i + 1 < nu)
    def _():
        start_gather(tokn_ref, 1 - slot)

    @pl.when((i > 0) & (i - 1 < nu))
    def _():
        wait_scatter(nv_ref[jnp.maximum(i - 1, 0)])

    @pl.when(i < nu)
    def _():
        prev = be_ref[jnp.maximum(i - 1, 0)]

        @pl.when((i == 0) | (be_ref[i] != prev))
        def _():
            wgu_bf_ref[...] = wgu_ref[0].astype(BF16)
            wd_bf_ref[...] = wd_ref[0].astype(BF16)

        wait_block(gsem.at[slot], xb_ref.at[slot])
        xb = xb_ref[slot].astype(BF16)
        gu = jnp.dot(xb, wgu_bf_ref[...], preferred_element_type=F32) + bgu_ref[0]
        gate = jnp.minimum(gu[:, :D_EXPERT], SWIGLU_LIMIT)
        up = jnp.clip(gu[:, D_EXPERT:], -SWIGLU_LIMIT, SWIGLU_LIMIT)
        act = (up + 1.0) * (gate / (1.0 + jnp.exp(-SWIGLU_ALPHA * gate)))
        yb_ref[...] = jnp.dot(act.astype(BF16), wd_bf_ref[...], preferred_element_type=F32) + bd_ref[0]
        nv = nv_ref[i]

        @pl.when(nv == EX_BM)
        def _():
            for r in range(EX_BM):
                scatter_row(r)

        @pl.when(nv < EX_BM)
        def _():
            def body(r, carry):
                scatter_row(r)
                return carry
            lax.fori_loop(0, nv, body, 0)

    @pl.when((i == nb - 1) & (i < nu))
    def _():
        wait_scatter(nv_ref[i])


def _experts(block_e, n_used, n_valid, buf_tok3, dst3, h2, w_gu, b_gu, w_down, b_down, n_slots):
    nb = buf_tok3.shape[0]
    idx_blk = lambda f: pl.BlockSpec((1, 1, EX_BM), f, memory_space=pltpu.SMEM)
    vmem = (2 * (D_MODEL * 2 * D_EXPERT * 4 + D_EXPERT * D_MODEL * 4)
            + D_MODEL * 2 * D_EXPERT * 2 + D_EXPERT * D_MODEL * 2
            + 3 * EX_BM * D_MODEL * 4 + 3 * EX_BM * 2 * D_EXPERT * 4)
    grid_spec = pltpu.PrefetchScalarGridSpec(
        num_scalar_prefetch=3,
        grid=(nb,),
        in_specs=[
            idx_blk(lambda i, be, nu, nv: (i, 0, 0)),
            idx_blk(lambda i, be, nu, nv: (jnp.minimum(i + 1, nb - 1), 0, 0)),
            idx_blk(lambda i, be, nu, nv: (i, 0, 0)),
            pl.BlockSpec(memory_space=pl.ANY),
            pl.BlockSpec((1, D_MODEL, 2 * D_EXPERT), lambda i, be, nu, nv: (be[i], 0, 0)),
            pl.BlockSpec((1, 1, 2 * D_EXPERT), lambda i, be, nu, nv: (be[i], 0, 0)),
            pl.BlockSpec((1, D_EXPERT, D_MODEL), lambda i, be, nu, nv: (be[i], 0, 0)),
            pl.BlockSpec((1, 1, D_MODEL), lambda i, be, nu, nv: (be[i], 0, 0)),
        ],
        out_specs=pl.BlockSpec(memory_space=pl.ANY),
        scratch_shapes=[
            pltpu.VMEM((2, EX_BM, D_MODEL), F32),
            pltpu.VMEM((EX_BM, D_MODEL), F32),
            pltpu.VMEM((D_MODEL, 2 * D_EXPERT), BF16),
            pltpu.VMEM((D_EXPERT, D_MODEL), BF16),
            pltpu.SemaphoreType.DMA((2,)),
            pltpu.SemaphoreType.DMA((1,)),
        ],
    )
    return pl.pallas_call(
        _experts_kernel,
        grid_spec=grid_spec,
        out_shape=jax.ShapeDtypeStruct((n_slots, D_MODEL), F32),
        compiler_params=pltpu.CompilerParams(
            dimension_semantics=("arbitrary",), vmem_limit_bytes=_vmem_limit(vmem)),
        name="experts",
    )(block_e, n_used, n_valid, buf_tok3, buf_tok3, dst3, h2, w_gu, b_gu[:, None, :], w_down,
      b_down[:, None, :])


CB_TM = 256


def _combine_kernel(x1_ref, y_ref, gate_ref, o_ref):
    tm = x1_ref.shape[0]
    g = jnp.concatenate([gate_ref[...], jnp.zeros((LANES - SUBLANES, tm), F32)], axis=0).T
    acc = x1_ref[...]
    for k in range(TOP_K):
        acc = acc + g[:, k:k + 1] * y_ref[:, k * D_MODEL:(k + 1) * D_MODEL]
    o_ref[...] = acc


def _combine(x1, y_slots, gates):
    n = x1.shape[0]
    tm = CB_TM
    y2 = y_slots.reshape(-1, TOP_K * D_MODEL)
    vmem = 2 * (2 * tm * D_MODEL * 4 + tm * TOP_K * D_MODEL * 4) + 4 * tm * D_MODEL * 4
    return pl.pallas_call(
        _combine_kernel,
        grid=(n // tm,),
        in_specs=[
            pl.BlockSpec((tm, D_MODEL), lambda i: (i, 0)),
            pl.BlockSpec((tm, TOP_K * D_MODEL), lambda i: (i, 0)),
            pl.BlockSpec((SUBLANES, tm), lambda i: (0, i)),
        ],
        out_specs=pl.BlockSpec((tm, D_MODEL), lambda i: (i, 0)),
        out_shape=jax.ShapeDtypeStruct((n, D_MODEL), F32),
        compiler_params=pltpu.CompilerParams(
            dimension_semantics=("arbitrary",), vmem_limit_bytes=_vmem_limit(vmem)),
        name="combine",
    )(x1, y2, gates)


def _dispatch_plan(eidx, rank, counts):
    n = eidx.shape[1]
    n_slots = n * TOP_K
    nb = n_slots // EX_BM + N_EXPERTS
    counts = counts.astype(jnp.int32)
    padded = ((counts + EX_BM - 1) // EX_BM) * EX_BM
    pad_end = jnp.cumsum(padded)
    pad_start = pad_end - padded
    e = eidx[:TOP_K]
    pos = pad_start[e] + rank[:TOP_K]
    slot_id = jnp.arange(n, dtype=jnp.int32)[None, :] * TOP_K + jnp.arange(TOP_K, dtype=jnp.int32)[:, None]
    inv = jnp.full((nb * EX_BM,), -1, jnp.int32).at[pos.reshape(-1)].set(
        slot_id.reshape(-1), unique_indices=True)
    buf_tok = jnp.where(inv >= 0, inv // TOP_K, 0)
    dst = jnp.maximum(inv, 0)
    blk_start = jnp.arange(nb, dtype=jnp.int32) * EX_BM
    block_e = jnp.minimum(jnp.sum((pad_end[None, :] <= blk_start[:, None]).astype(jnp.int32), axis=1),
                          N_EXPERTS - 1)
    n_used = (pad_end[-1] // EX_BM).astype(jnp.int32).reshape(1)
    n_valid = jnp.clip(pad_start[block_e] + counts[block_e] - blk_start, 0, EX_BM).astype(jnp.int32)
    n_valid = jnp.where(blk_start < pad_end[-1], n_valid, 0)
    return (block_e, n_used, n_valid, buf_tok.reshape(nb, 1, EX_BM), dst.reshape(nb, 1, EX_BM), n_slots)


def _prep_in_proj_weights(w_in, fox_f_bias, mlstm_i_bias, mlstm_f_bias):
    split_at = []
    acc = 0
    for wdt in SPLIT_WIDTHS[:-1]:
        acc += wdt
        split_at.append(acc)
    fq, fk, fv, ff, mq, mk, mv, mi, mf, mo = jnp.split(w_in, split_at, axis=-1)
    w_main = jnp.concatenate([fq, fk, fv, mq, mk, mv, mo], axis=-1).astype(BF16)
    w_gate = jnp.concatenate([ff, mi, mf], axis=-1)
    bias = jnp.concatenate([fox_f_bias, mlstm_i_bias, mlstm_f_bias]).astype(F32)
    n_gate = w_gate.shape[1]
    wg = jnp.pad(w_gate, ((0, 0), (0, LANES - n_gate)))
    wg_hi, wg_lo = _split_hi_lo(wg)
    wgt = jnp.pad(w_gate.T, ((0, GATE_ROWS - n_gate), (0, 0)))
    wgt_hi, wgt_lo = _split_hi_lo(wgt)
    bias_r = jnp.pad(bias, (0, LANES - n_gate))[None, :]
    bias_t = jnp.pad(bias, (0, GATE_ROWS - n_gate))[:, None]
    return w_main, wgt_hi, wgt_lo, bias_t, wg_hi, wg_lo, bias_r


def kernel(x, attn_norm_w, w_in, fox_f_bias, fox_q_norm_w, fox_k_norm_w, fox_out_norm_w, mlstm_conv_w, mlstm_i_bias, mlstm_f_bias, mlstm_out_norm_w, w_out, moe_norm_w, router_w, router_b, expert_w_gate_up, expert_b_gate_up, expert_w_down, expert_b_down):
    bsz, seq, d = x.shape
    x2d = x.reshape(bsz * seq, d)
    prep = _prep_in_proj_weights(w_in[0], fox_f_bias[0], mlstm_i_bias[0], mlstm_f_bias[0])
    proj, gt, gtm = _in_proj(x2d, attn_norm_w[0][None, :], *prep)
    proj3 = proj.reshape(bsz, seq, MAIN_WIDTH)
    fox_y = _fox(proj3, gt, gtm, fox_q_norm_w[0][None, :], fox_k_norm_w[0][None, :],
                 fox_out_norm_w[0][None, :])
    mlstm_y = _mlstm(proj3, gt, gtm, mlstm_conv_w[0])
    return _channel_mixer(x2d, fox_y.reshape(-1, FOX_WIDTH), mlstm_y.reshape(-1, MLSTM_V_WIDTH), proj,
                          mlstm_out_norm_w[0], w_out[0], moe_norm_w[0], router_w[0], router_b[0],
                          expert_w_gate_up[0], expert_b_gate_up[0], expert_w_down[0],
                          expert_b_down[0]).reshape(bsz, seq, d)


def _channel_mixer(x2d, fox_y2d, mlstm_y2d, proj, mlstm_gain, w_out, moe_norm_w, router_w, router_b,
                   w_gu, b_gu, w_down, b_down):
    wr_hi, wr_lo = _split_hi_lo(router_w.T)
    x1, h2, eidx, gates, rank, counts = _out_route(
        x2d, fox_y2d, mlstm_y2d, proj, w_out.astype(BF16), mlstm_gain[None, :], moe_norm_w[None, :],
        wr_hi, wr_lo, router_b[:, None])
    block_e, n_used, n_valid, buf_tok3, dst3, n_slots = _dispatch_plan(eidx, rank, counts[:, 0])
    y_slots = _experts(block_e, n_used, n_valid, buf_tok3, dst3, h2, w_gu, b_gu, w_down, b_down, n_slots)
    return _combine(x1, y_slots, gates)
```

```python
import functools
import math

import jax
import jax.numpy as jnp
from jax import lax
from jax.experimental import pallas as pl
from jax.experimental.pallas import tpu as pltpu

F32 = jnp.float32
BF16 = jnp.bfloat16

D_MODEL = 1024
SEQ = 2048
HEAD_DIM = 64
FOX_HEADS = 8
FOX_WIDTH = FOX_HEADS * HEAD_DIM
MLSTM_HEADS = 8
MLSTM_QK_DIM = 32
MLSTM_V_DIM = 64
MLSTM_QK_WIDTH = MLSTM_HEADS * MLSTM_QK_DIM
MLSTM_V_WIDTH = MLSTM_HEADS * MLSTM_V_DIM
CONV_WIDTH = 4
MLSTM_CHUNK = 64
SPLIT_WIDTHS = (FOX_WIDTH, FOX_WIDTH, FOX_WIDTH, FOX_HEADS,
                MLSTM_QK_WIDTH, MLSTM_QK_WIDTH, MLSTM_V_WIDTH,
                MLSTM_HEADS, MLSTM_HEADS, MLSTM_V_WIDTH)
N_EXPERTS = 32
TOP_K = 4
D_EXPERT = D_MODEL
SWIGLU_ALPHA = 1.702
SWIGLU_LIMIT = 7.0
NORM_EPS = 1e-5

LANES = 128
SUBLANES = 8
V7X_VMEM_BYTES = 64 * 1024 * 1024

MAIN_WIDTH = 3 * FOX_WIDTH + 2 * MLSTM_QK_WIDTH + 2 * MLSTM_V_WIDTH
GATE_ROWS = 32

NT_DIMS = (((1,), (1,)), ((), ()))


def _vmem_limit(nbytes):
    return int(min(nbytes + (8 << 20), V7X_VMEM_BYTES - (4 << 20)))


def _log_sigmoid(x):
    return jnp.minimum(x, 0.0) - jnp.log(1.0 + jnp.exp(-jnp.abs(x)))


def _split_hi_lo(x):
    hi = x.astype(BF16)
    lo = (x - hi.astype(F32)).astype(BF16)
    return hi, lo


IN_TM = 512


def _in_proj_kernel(x_ref, nw_ref, w_ref, wgt_hi_ref, wgt_lo_ref, bt_ref,
                    wg_hi_ref, wg_lo_ref, br_ref, proj_ref, gt_ref, gtm_ref):
    x = x_ref[...]
    ms = jnp.mean(x * x, axis=-1, keepdims=True)
    y = x * lax.rsqrt(ms + NORM_EPS) * nw_ref[...]
    h_hi, h_lo = _split_hi_lo(y)
    proj_ref[...] = jnp.dot(h_hi, w_ref[...], preferred_element_type=F32)

    wt_hi, wt_lo = wgt_hi_ref[...], wgt_lo_ref[...]
    gt = (lax.dot_general(wt_hi, h_hi, NT_DIMS, preferred_element_type=F32)
          + lax.dot_general(wt_lo, h_hi, NT_DIMS, preferred_element_type=F32)
          + lax.dot_general(wt_hi, h_lo, NT_DIMS, preferred_element_type=F32))
    gt = gt + bt_ref[...]
    row = lax.broadcasted_iota(jnp.int32, gt.shape, 0)
    is_input_gate = (row >= FOX_HEADS) & (row < FOX_HEADS + MLSTM_HEADS)
    gt_ref[...] = jnp.where(is_input_gate, gt, _log_sigmoid(gt))

    w_hi, w_lo = wg_hi_ref[...], wg_lo_ref[...]
    g = (jnp.dot(h_hi, w_hi, preferred_element_type=F32)
         + jnp.dot(h_hi, w_lo, preferred_element_type=F32)
         + jnp.dot(h_lo, w_hi, preferred_element_type=F32))
    g = g + br_ref[...]
    lane = lax.broadcasted_iota(jnp.int32, g.shape, 1)
    is_input_gate = (lane >= FOX_HEADS) & (lane < FOX_HEADS + MLSTM_HEADS)
    gtm_ref[...] = jnp.where(is_input_gate, g, _log_sigmoid(g))


def _in_proj(x2d, norm_w, w_main, wgt_hi, wgt_lo, bias_t, wg_hi, wg_lo, bias_r):
    n = x2d.shape[0]
    tm = IN_TM
    const = lambda i: (0, 0)
    vmem = 2 * (tm * D_MODEL * 4 + D_MODEL * MAIN_WIDTH * 2 + tm * MAIN_WIDTH * 4
                + GATE_ROWS * tm * 4 + tm * LANES * 4) + tm * MAIN_WIDTH * 4
    return pl.pallas_call(
        _in_proj_kernel,
        grid=(n // tm,),
        in_specs=[
            pl.BlockSpec((tm, D_MODEL), lambda i: (i, 0)),
            pl.BlockSpec((1, D_MODEL), const),
            pl.BlockSpec((D_MODEL, MAIN_WIDTH), const),
            pl.BlockSpec((GATE_ROWS, D_MODEL), const),
            pl.BlockSpec((GATE_ROWS, D_MODEL), const),
            pl.BlockSpec((GATE_ROWS, 1), const),
            pl.BlockSpec((D_MODEL, LANES), const),
            pl.BlockSpec((D_MODEL, LANES), const),
            pl.BlockSpec((1, LANES), const),
        ],
        out_specs=[
            pl.BlockSpec((tm, MAIN_WIDTH), lambda i: (i, 0)),
            pl.BlockSpec((GATE_ROWS, tm), lambda i: (0, i)),
            pl.BlockSpec((tm, LANES), lambda i: (i, 0)),
        ],
        out_shape=[
            jax.ShapeDtypeStruct((n, MAIN_WIDTH), F32),
            jax.ShapeDtypeStruct((GATE_ROWS, n), F32),
            jax.ShapeDtypeStruct((n, LANES), F32),
        ],
        compiler_params=pltpu.CompilerParams(
            dimension_semantics=("arbitrary",), vmem_limit_bytes=_vmem_limit(vmem)),
        name="in_proj",
    )(x2d, norm_w, w_main, wgt_hi, wgt_lo, bias_t, wg_hi, wg_lo, bias_r)


FOX_TQ = 256
SCAN_SHIFTS = tuple(1 << i for i in range(int(math.log2(SEQ))))


def _rms(x, w):
    return x * lax.rsqrt(jnp.mean(x * x, axis=-1, keepdims=True) + NORM_EPS) * w


def _fox_kernel(q_ref, k_ref, v_ref, gt_ref, gtm_ref, qw_ref, kw_ref, ow_ref, o_ref,
                cumr_ref, cumc_ref):
    hp = pl.program_id(1)

    @pl.when(hp == 0)
    def _():
        x = gt_ref[...]
        lane = lax.broadcasted_iota(jnp.int32, x.shape, 1)
        for s in SCAN_SHIFTS:
            x = x + jnp.where(lane >= s, pltpu.roll(x, s, axis=1), 0.0)
        cumr_ref[...] = x
        y = gtm_ref[...]
        rowi = lax.broadcasted_iota(jnp.int32, y.shape, 0)
        for s in SCAN_SHIFTS:
            y = y + jnp.where(rowi >= s, pltpu.roll(y, s, axis=0), 0.0)
        cumc_ref[...] = y

    lane_t = lax.broadcasted_iota(jnp.int32, (SEQ, LANES), 1)
    for j in range(2):
        h = 2 * hp + j
        cs = slice(j * HEAD_DIM, (j + 1) * HEAD_DIM)
        kn = _rms(k_ref[0, :, cs], kw_ref[...]).astype(BF16)
        vb = v_ref[0, :, cs].astype(BF16)
        cum_row = cumr_ref[pl.ds(h, 1), :]
        cum_col = jnp.sum(jnp.where(lane_t == h, cumc_ref[...], 0.0),
                          axis=1, keepdims=True)
        for i in range(SEQ // FOX_TQ):
            qs = slice(i * FOX_TQ, (i + 1) * FOX_TQ)
            n = (i + 1) * FOX_TQ
            qn = (_rms(q_ref[0, qs, cs], qw_ref[...]) * (HEAD_DIM ** -0.5)).astype(BF16)
            s = lax.dot_general(qn, kn[:n], NT_DIMS, preferred_element_type=F32)
            s = s + (cum_col[qs] - cum_row[:, :n])
            col = lax.broadcasted_iota(jnp.int32, s.shape, 1)
            row = lax.broadcasted_iota(jnp.int32, s.shape, 0) + i * FOX_TQ
            s = jnp.where(col <= row, s, -jnp.inf)
            m = jnp.max(s, axis=-1, keepdims=True)
            p = jnp.exp(s - m)
            l = jnp.sum(p, axis=-1, keepdims=True)
            o = jnp.dot(p.astype(BF16), vb[:n], preferred_element_type=F32) / l
            o_ref[0, qs, cs] = _rms(o, ow_ref[:, cs])


def _fox(proj3, gt, gtm, qw, kw, ow):
    b = proj3.shape[0]
    nq = FOX_WIDTH // LANES
    blk = (1, SEQ, LANES)
    vmem = 2 * 5 * SEQ * LANES * 4 + 2 * SEQ * LANES * 4 + 6 * FOX_TQ * SEQ * 4
    return pl.pallas_call(
        _fox_kernel,
        grid=(b, nq),
        in_specs=[
            pl.BlockSpec(blk, lambda bi, hp: (bi, 0, hp)),
            pl.BlockSpec(blk, lambda bi, hp: (bi, 0, nq + hp)),
            pl.BlockSpec(blk, lambda bi, hp: (bi, 0, 2 * nq + hp)),
            pl.BlockSpec((SUBLANES, SEQ), lambda bi, hp: (0, bi)),
            pl.BlockSpec((SEQ, LANES), lambda bi, hp: (bi, 0)),
            pl.BlockSpec((1, HEAD_DIM), lambda bi, hp: (0, 0)),
            pl.BlockSpec((1, HEAD_DIM), lambda bi, hp: (0, 0)),
            pl.BlockSpec((1, LANES), lambda bi, hp: (0, hp)),
        ],
        out_specs=pl.BlockSpec(blk, lambda bi, hp: (bi, 0, hp)),
        out_shape=jax.ShapeDtypeStruct((b, SEQ, FOX_WIDTH), F32),
        scratch_shapes=[pltpu.VMEM((SUBLANES, SEQ), F32), pltpu.VMEM((SEQ, LANES), F32)],
        compiler_params=pltpu.CompilerParams(
            dimension_semantics=("arbitrary", "arbitrary"), vmem_limit_bytes=_vmem_limit(vmem)),
        name="fox",
    )(proj3, proj3, proj3, gt, gtm, qw, kw, ow)


ML_L = MLSTM_CHUNK
ML_PAIRS = SEQ // (2 * ML_L)
ML_HL = FOX_HEADS
ML_AUG = MLSTM_V_WIDTH + LANES
ML_TILE = 256
SEG_SHIFTS = tuple(1 << i for i in range(int(math.log2(ML_L))))


def _split3(x):
    a = x.astype(BF16)
    r = x - a.astype(F32)
    b = r.astype(BF16)
    c = (r - b.astype(F32)).astype(BF16)
    return a, b, c


def _expand_heads(x, exp_bf):
    a, b, c = _split3(x)
    return (jnp.dot(a, exp_bf, preferred_element_type=F32)
            + jnp.dot(b, exp_bf, preferred_element_type=F32)
            + jnp.dot(c, exp_bf, preferred_element_type=F32))


def _seg_scan(x, axis, op, ident):
    idx = lax.broadcasted_iota(jnp.int32, x.shape, axis) % ML_L
    for s in SEG_SHIFTS:
        x = op(x, jnp.where(idx >= s, pltpu.roll(x, s, axis=axis), ident))
    return x


def _mlstm_kernel(q_ref, k_ref, v_ref, gi_ref, gf_ref, gtm_ref, cw_ref, o_ref, den_ref,
                  qc_ref, kc_ref, kt_ref, rr_ref, cmr_ref, bcr_ref,
                  ealpha_ref, ew_ref, wint_ref, floor_ref, mfull_ref, caug_ref):
    def conv_silu(u, w):
        rowi = lax.broadcasted_iota(jnp.int32, u.shape, 0)
        acc = u * w[CONV_WIDTH - 1:CONV_WIDTH, :]
        for d in range(1, CONV_WIDTH):
            sh = jnp.where(rowi >= d, pltpu.roll(u, d, axis=0), 0.0)
            acc = acc + sh * w[CONV_WIDTH - 1 - d:CONV_WIDTH - d, :]
        return acc / (1.0 + jnp.exp(-acc))

    cw = cw_ref[...]
    qc_ref[...] = conv_silu(q_ref[0], cw[:, :MLSTM_QK_WIDTH]).astype(BF16)
    kc = conv_silu(k_ref[0], cw[:, MLSTM_QK_WIDTH:]) * (MLSTM_QK_DIM ** -0.5)
    kc_ref[...] = kc.astype(BF16)
    kt = kc.T
    for p in range(ML_PAIRS):
        kt_ref[p] = kt[:, p * LANES:(p + 1) * LANES]

    bcum_r = _seg_scan(gf_ref[...], 1, jnp.add, 0.0)
    r_r = gi_ref[...] - bcum_r
    cmx_r = _seg_scan(r_r, 1, jnp.maximum, -jnp.inf)
    for p in range(ML_PAIRS):
        ls = slice(p * LANES, (p + 1) * LANES)
        rr_ref[p] = r_r[:, ls]
        cmr_ref[p] = cmx_r[:, ls]
        bcr_ref[p] = bcum_r[:, ls]

    g = gtm_ref[...]
    lane_g = lax.broadcasted_iota(jnp.int32, g.shape, 1)
    head_lane = (lane_g >= ML_HL) & (lane_g < ML_HL + MLSTM_HEADS)
    bcum_c = jnp.where(
        head_lane, pltpu.roll(_seg_scan(g, 0, jnp.add, 0.0), LANES - MLSTM_HEADS, axis=1), 0.0)
    cmx_c = _seg_scan(jnp.where(head_lane, g, 0.0) - bcum_c, 0, jnp.maximum, -jnp.inf)
    m = jnp.zeros((1, LANES), F32)
    for c in range(SEQ // ML_L):
        mfull_ref[c * ML_L:(c + 1) * ML_L, :] = jnp.broadcast_to(m, (ML_L, LANES))
        last = (c + 1) * ML_L - 1
        m = bcum_c[last:last + 1, :] + jnp.maximum(m, cmx_c[last:last + 1, :])
    mfull = mfull_ref[...]
    mx = jnp.maximum(mfull, cmx_c)
    wint_ref[...] = jnp.exp(mfull - mx)
    floor_ref[...] = jnp.exp(-(bcum_c + mx))
    mfull_ref[...] = -mx

    lane_e = lax.broadcasted_iota(jnp.int32, (LANES, MLSTM_V_WIDTH), 1) // MLSTM_V_DIM
    row_e = lax.broadcasted_iota(jnp.int32, (LANES, MLSTM_V_WIDTH), 0)
    exp_bf = jnp.where(row_e == lane_e + ML_HL, 1.0, 0.0).astype(BF16)

    def expand_tile(i, carry):
        rows = pl.ds(pl.multiple_of(i * ML_TILE, ML_TILE), ML_TILE)
        ealpha_ref[rows, :] = _expand_heads(mfull_ref[rows, :], exp_bf)
        ew_ref[rows, :] = _expand_heads(wint_ref[rows, :], exp_bf)
        return carry

    lax.fori_loop(0, SEQ // ML_TILE, expand_tile, 0)

    kb_rowh = lax.broadcasted_iota(jnp.int32, (MLSTM_HEADS * ML_L, MLSTM_QK_WIDTH), 0) // ML_L
    kb_lane = lax.broadcasted_iota(jnp.int32, (MLSTM_HEADS * ML_L, MLSTM_QK_WIDTH), 1) // MLSTM_QK_DIM
    mask_k = jnp.where(kb_rowh == kb_lane, 1.0, 0.0).astype(BF16)
    va_rowh = lax.broadcasted_iota(jnp.int32, (MLSTM_HEADS * ML_L, ML_AUG), 0) // ML_L
    va_col = lax.broadcasted_iota(jnp.int32, (MLSTM_HEADS * ML_L, ML_AUG), 1)
    mask_v = jnp.where(
        (va_col // MLSTM_V_DIM == va_rowh) | (va_col == MLSTM_V_WIDTH + ML_HL + va_rowh),
        1.0, 0.0).astype(BF16)
    c_rowh = lax.broadcasted_iota(jnp.int32, (MLSTM_QK_WIDTH, ML_AUG), 0) // MLSTM_QK_DIM
    c_col = lax.broadcasted_iota(jnp.int32, (MLSTM_QK_WIDTH, ML_AUG), 1)
    mask_c = (c_col // MLSTM_V_DIM == c_rowh) | (c_col == MLSTM_V_WIDTH + ML_HL + c_rowh)
    lane128 = lax.broadcasted_iota(jnp.int32, (ML_L, LANES), 1)
    s_idx = lax.broadcasted_iota(jnp.int32, (ML_L, MLSTM_V_WIDTH), 1) % ML_L
    t_idx = lax.broadcasted_iota(jnp.int32, (ML_L, MLSTM_V_WIDTH), 0)
    causal = s_idx <= t_idx
    ones_aug = jnp.ones((ML_L, LANES), F32)

    caug_ref[...] = jnp.zeros_like(caug_ref)

    def pair_body(cp, m_row):
        r2 = rr_ref[cp]
        cm2 = cmr_ref[cp]
        b2 = bcr_ref[cp]
        r2r = pltpu.roll(r2, ML_L, axis=1)
        kt2 = kt_ref[cp]
        for cc in range(2):
            lo = cc * ML_L
            rows = pl.ds(pl.multiple_of(cp * (2 * ML_L), 2 * ML_L) + lo, ML_L)
            qa = qc_ref[rows, :]
            ka = kc_ref[rows, :]
            va_aug = jnp.concatenate([v_ref[0, rows, :], ones_aug], axis=1).astype(BF16)

            kbd = jnp.concatenate([ka] * MLSTM_HEADS, axis=0) * mask_k
            s = lax.dot_general(qa, kbd, NT_DIMS, preferred_element_type=F32)

            cmx_last = cm2[:, lo + ML_L - 1:lo + ML_L]
            b_last = b2[:, lo + ML_L - 1:lo + ML_L]
            mx_r = jnp.maximum(m_row, cmx_last)
            decay = jnp.exp(m_row - mx_r)
            wk = jnp.exp(r2[:, lo:lo + ML_L] - mx_r)
            m_row = b_last + mx_r

            src_e, src_o = (r2, r2r) if cc == 0 else (r2r, r2)
            cols = []
            for p in range(MLSTM_HEADS // 2):
                even = jnp.broadcast_to(src_e[2 * p:2 * p + 1, :], (ML_L, LANES))
                odd = jnp.broadcast_to(src_o[2 * p + 1:2 * p + 2, :], (ML_L, LANES))
                cols.append(jnp.where(lane128 < ML_L, even, odd))
            r_all = jnp.concatenate(cols, axis=1)
            arg = jnp.where(causal, ealpha_ref[rows, :] + r_all, -jnp.inf)
            p_all = (s * jnp.exp(arg)).astype(BF16)

            vbd = jnp.concatenate([va_aug] * MLSTM_HEADS, axis=0) * mask_v
            pv = jnp.dot(p_all, vbd, preferred_element_type=F32)
            qc_state = jnp.dot(qa, caug_ref[...].astype(BF16), preferred_element_type=F32)
            o_ref[0, rows, :] = (ew_ref[rows, :] * qc_state[:, :MLSTM_V_WIDTH]
                                 + pv[:, :MLSTM_V_WIDTH])
            den_ref[rows, :] = (wint_ref[rows, :] * qc_state[:, MLSTM_V_WIDTH:]
                                + pv[:, MLSTM_V_WIDTH:])

            wk_rows = jnp.concatenate(
                [jnp.broadcast_to(wk[h:h + 1, :], (MLSTM_QK_DIM, ML_L)) for h in range(MLSTM_HEADS)],
                axis=0)
            dec_rows = jnp.concatenate(
                [jnp.broadcast_to(decay[h:h + 1, :], (MLSTM_QK_DIM, 1)) for h in range(MLSTM_HEADS)],
                axis=0)
            ktw = (kt2[:, lo:lo + ML_L] * wk_rows).astype(BF16)
            upd = jnp.dot(ktw, va_aug, preferred_element_type=F32)
            caug_ref[...] = dec_rows * caug_ref[...] + jnp.where(mask_c, upd, 0.0)
        return m_row

    lax.fori_loop(0, ML_PAIRS, pair_body, jnp.zeros((MLSTM_HEADS, 1), F32))

    ob_row = lax.broadcasted_iota(jnp.int32, (MLSTM_V_WIDTH, LANES), 0) // MLSTM_V_DIM
    ob_col = lax.broadcasted_iota(jnp.int32, (MLSTM_V_WIDTH, LANES), 1)
    ones_bd = jnp.where(ob_col == ob_row + ML_HL, 1.0, 0.0).astype(BF16)

    def norm_tile(i, carry):
        rows = pl.ds(pl.multiple_of(i * ML_TILE, ML_TILE), ML_TILE)
        num = o_ref[0, rows, :]
        dn = jnp.maximum(jnp.abs(den_ref[rows, :]), floor_ref[rows, :])
        r = 1.0 / dn
        n2_hi, n2_lo = _split_hi_lo(num * num)
        msn = (jnp.dot(n2_hi, ones_bd, preferred_element_type=F32)
               + jnp.dot(n2_lo, ones_bd, preferred_element_type=F32)) * (1.0 / MLSTM_V_DIM)
        fac = r * lax.rsqrt(r * r * msn + NORM_EPS)
        o_ref[0, rows, :] = num * _expand_heads(fac, exp_bf)
        return carry

    lax.fori_loop(0, SEQ // ML_TILE, norm_tile, 0)


def _mlstm(proj3, gt, gtm, conv_w):
    b = proj3.shape[0]
    qk_blk = (1, SEQ, MLSTM_QK_WIDTH)
    v_blk = (1, SEQ, MLSTM_V_WIDTH)
    q_col = 3 * FOX_WIDTH // MLSTM_QK_WIDTH
    v_col = (3 * FOX_WIDTH + 2 * MLSTM_QK_WIDTH) // MLSTM_V_WIDTH
    vmem = (2 * (2 * SEQ * MLSTM_QK_WIDTH * 4 + 2 * SEQ * MLSTM_V_WIDTH * 4 + 2 * SEQ * LANES * 4)
            + 2 * SEQ * MLSTM_QK_WIDTH * 2 + SEQ * MLSTM_QK_WIDTH * 4 + 2 * SEQ * MLSTM_V_WIDTH * 4
            + 3 * SEQ * LANES * 4 + (8 << 20))
    return pl.pallas_call(
        _mlstm_kernel,
        grid=(b,),
        in_specs=[
            pl.BlockSpec(qk_blk, lambda bi: (bi, 0, q_col)),
            pl.BlockSpec(qk_blk, lambda bi: (bi, 0, q_col + 1)),
            pl.BlockSpec(v_blk, lambda bi: (bi, 0, v_col)),
            pl.BlockSpec((SUBLANES, SEQ), lambda bi: (1, bi)),
            pl.BlockSpec((SUBLANES, SEQ), lambda bi: (2, bi)),
            pl.BlockSpec((SEQ, LANES), lambda bi: (bi, 0)),
            pl.BlockSpec((CONV_WIDTH, 2 * MLSTM_QK_WIDTH), lambda bi: (0, 0)),
        ],
        out_specs=pl.BlockSpec(v_blk, lambda bi: (bi, 0, 0)),
        out_shape=jax.ShapeDtypeStruct((b, SEQ, MLSTM_V_WIDTH), F32),
        scratch_shapes=[
            pltpu.VMEM((SEQ, LANES), F32),
            pltpu.VMEM((SEQ, MLSTM_QK_WIDTH), BF16),
            pltpu.VMEM((SEQ, MLSTM_QK_WIDTH), BF16),
            pltpu.VMEM((ML_PAIRS, MLSTM_QK_WIDTH, LANES), F32),
            pltpu.VMEM((ML_PAIRS, SUBLANES, LANES), F32),
            pltpu.VMEM((ML_PAIRS, SUBLANES, LANES), F32),
            pltpu.VMEM((ML_PAIRS, SUBLANES, LANES), F32),
            pltpu.VMEM((SEQ, MLSTM_V_WIDTH), F32),
            pltpu.VMEM((SEQ, MLSTM_V_WIDTH), F32),
            pltpu.VMEM((SEQ, LANES), F32),
            pltpu.VMEM((SEQ, LANES), F32),
            pltpu.VMEM((SEQ, LANES), F32),
            pltpu.VMEM((MLSTM_QK_WIDTH, ML_AUG), F32),
        ],
        compiler_params=pltpu.CompilerParams(
            dimension_semantics=("arbitrary",), vmem_limit_bytes=_vmem_limit(vmem)),
        name="mlstm",
    )(proj3, proj3, proj3, gt, gt, gtm, conv_w)


RT_TM = 512


def _out_route_kernel(x_ref, fy_ref, my_ref, mo_ref, wo_ref, mg_ref, nw_ref, wr_hi_ref, wr_lo_ref,
                      rb_ref, x1_ref, h2_ref, eidx_ref, gate_ref, rank_ref, cnt_ref, carry_ref):
    i = pl.program_id(0)

    @pl.when(i == 0)
    def _():
        carry_ref[...] = jnp.zeros_like(carry_ref)

    my = my_ref[...] * mg_ref[...] / (1.0 + jnp.exp(-mo_ref[...]))
    mixed = (jnp.dot(fy_ref[...].astype(BF16), wo_ref[:FOX_WIDTH, :], preferred_element_type=F32)
             + jnp.dot(my.astype(BF16), wo_ref[FOX_WIDTH:, :], preferred_element_type=F32))
    x1 = x_ref[...] + mixed
    x1_ref[...] = x1
    h2 = _rms(x1, nw_ref[...])
    h2_ref[...] = h2

    h_hi, h_lo = _split_hi_lo(h2)
    wr_hi, wr_lo = wr_hi_ref[...], wr_lo_ref[...]
    logit = (lax.dot_general(wr_hi, h_hi, NT_DIMS, preferred_element_type=F32)
             + lax.dot_general(wr_lo, h_hi, NT_DIMS, preferred_element_type=F32)
             + lax.dot_general(wr_hi, h_lo, NT_DIMS, preferred_element_type=F32)) + rb_ref[...]

    e_iota = lax.broadcasted_iota(jnp.int32, logit.shape, 0).astype(F32)
    vals, idxs, hots = [], [], []
    for _ in range(TOP_K):
        mk = jnp.max(logit, axis=0, keepdims=True)
        idx = jnp.min(jnp.where(logit == mk, e_iota, float(N_EXPERTS)), axis=0, keepdims=True)
        hot = e_iota == idx
        logit = jnp.where(hot, -jnp.inf, logit)
        vals.append(mk)
        idxs.append(idx.astype(jnp.int32))
        hots.append(hot)
    exps = [jnp.exp(v - vals[0]) for v in vals]
    tot = exps[0] + exps[1] + exps[2] + exps[3]
    gates = [e / tot for e in exps]

    assign = jnp.zeros(logit.shape, F32)
    for hot in hots:
        assign = assign + jnp.where(hot, 1.0, 0.0)
    tm = logit.shape[1]
    src = lax.broadcasted_iota(jnp.int32, (tm, tm), 0)
    dst = lax.broadcasted_iota(jnp.int32, (tm, tm), 1)
    upper = jnp.where(src < dst, 1.0, 0.0).astype(BF16)
    base = jnp.dot(assign.astype(BF16), upper, preferred_element_type=F32) + carry_ref[:, 0:1]
    ranks = [jnp.sum(jnp.where(hot, base, 0.0), axis=0, keepdims=True) for hot in hots]
    new_carry = carry_ref[...] + jnp.sum(assign, axis=1, keepdims=True)
    carry_ref[...] = new_carry
    cnt_ref[...] = new_carry

    zi = jnp.zeros((SUBLANES - TOP_K, tm), jnp.int32)
    eidx_ref[...] = jnp.concatenate(idxs + [zi], axis=0)
    rank_ref[...] = jnp.concatenate([r.astype(jnp.int32) for r in ranks] + [zi], axis=0)
    gate_ref[...] = jnp.concatenate(gates + [zi.astype(F32)], axis=0)


def _out_route(x2d, fox_y2d, mlstm_y2d, proj, w_out_bf, mlstm_gain, moe_norm_w, wr_hi, wr_lo, rb):
    n = x2d.shape[0]
    tm = RT_TM
    const = lambda i: (0, 0)
    mo_col = (MAIN_WIDTH - MLSTM_V_WIDTH) // MLSTM_V_WIDTH
    row_blk = lambda w: pl.BlockSpec((tm, w), lambda i: (i, 0))
    lane_blk = pl.BlockSpec((SUBLANES, tm), lambda i: (0, i))
    vmem = (2 * (tm * D_MODEL * 4 * 3 + tm * FOX_WIDTH * 4 * 3 + D_MODEL * D_MODEL * 2)
            + 6 * tm * D_MODEL * 4 + tm * tm * 6)
    return pl.pallas_call(
        _out_route_kernel,
        grid=(n // tm,),
        in_specs=[
            row_blk(D_MODEL), row_blk(FOX_WIDTH), row_blk(MLSTM_V_WIDTH),
            pl.BlockSpec((tm, MLSTM_V_WIDTH), lambda i: (i, mo_col)),
            pl.BlockSpec((D_MODEL, D_MODEL), const),
            pl.BlockSpec((1, MLSTM_V_WIDTH), const),
            pl.BlockSpec((1, D_MODEL), const),
            pl.BlockSpec((N_EXPERTS, D_MODEL), const),
            pl.BlockSpec((N_EXPERTS, D_MODEL), const),
            pl.BlockSpec((N_EXPERTS, 1), const),
        ],
        out_specs=[row_blk(D_MODEL), row_blk(D_MODEL), lane_blk, lane_blk, lane_blk,
                   pl.BlockSpec((N_EXPERTS, LANES), const)],
        out_shape=[
            jax.ShapeDtypeStruct((n, D_MODEL), F32),
            jax.ShapeDtypeStruct((n, D_MODEL), F32),
            jax.ShapeDtypeStruct((SUBLANES, n), jnp.int32),
            jax.ShapeDtypeStruct((SUBLANES, n), F32),
            jax.ShapeDtypeStruct((SUBLANES, n), jnp.int32),
            jax.ShapeDtypeStruct((N_EXPERTS, LANES), F32),
        ],
        scratch_shapes=[pltpu.VMEM((N_EXPERTS, LANES), F32)],
        compiler_params=pltpu.CompilerParams(
            dimension_semantics=("arbitrary",), vmem_limit_bytes=_vmem_limit(vmem)),
        name="out_route",
    )(x2d, fox_y2d, mlstm_y2d, proj, w_out_bf, mlstm_gain, moe_norm_w, wr_hi, wr_lo, rb)


EX_BM = 256


def _experts_kernel(be_ref, nu_ref, nv_ref, tok_ref, tokn_ref, dstp_ref, h2_hbm, wgu_ref, bgu_ref,
                    wd_ref, bd_ref, y_hbm, xb_ref, yb_ref, xbf_ref, wgu_bf_ref, wd_bf_ref, gsem, ssem):
    i = pl.program_id(0)
    last_blk = pl.num_programs(0) - 2
    nu = nu_ref[0]
    slot = i % 2
    oslot = 1 - slot
    used = i < nu
    cur = jnp.minimum(i, last_blk)
    nv_prev = nv_ref[jnp.clip(i - 1, 0, last_blk)]
    nv_prev2 = nv_ref[jnp.clip(i - 2, 0, last_blk)]

    def start_gather(idx_ref, s):
        for r in range(EX_BM):
            pltpu.make_async_copy(h2_hbm.at[pl.ds(idx_ref[0, 0, r], 1), :],
                                  xb_ref.at[s, pl.ds(r, 1), :], gsem.at[s]).start()

    def scatter_prev_row(r):
        pltpu.make_async_copy(yb_ref.at[oslot, pl.ds(r, 1), :],
                              y_hbm.at[pl.ds(dstp_ref[0, 0, r], 1), :], ssem.at[oslot]).start()

    def scatter_prev_partial():
        def body(r, carry):
            scatter_prev_row(r)
            return carry
        lax.fori_loop(0, nv_prev, body, 0)

    def wait_block(sem, buf):
        pltpu.make_async_copy(h2_hbm.at[pl.ds(0, EX_BM), :], buf, sem).wait()

    def wait_scatter(s, n):
        @pl.when(n == EX_BM)
        def _():
            wait_block(ssem.at[s], yb_ref.at[s])

        @pl.when(n < EX_BM)
        def _():
            def body(r, carry):
                pltpu.make_async_copy(h2_hbm.at[pl.ds(0, 1), :], yb_ref.at[s, pl.ds(0, 1), :],
                                      ssem.at[s]).wait()
                return carry
            lax.fori_loop(0, n, body, 0)

    @pl.when(i == 0)
    def _():
        start_gather(tok_ref, 0)

    @pl.when((i >= 2) & (i <= nu))
    def _():
        wait_scatter(slot, nv_prev2)

    @pl.when(used & ((i == 0) | (be_ref[cur] != be_ref[jnp.maximum(cur - 1, 0)])))
    def _():
        wgu_bf_ref[...] = wgu_ref[0].astype(BF16)
        wd_bf_ref[...] = wd_ref[0].astype(BF16)

    def step_body(scatter_prev):
        wait_block(gsem.at[slot], xb_ref.at[slot])
        xbf_ref[...] = xb_ref[slot].astype(BF16)
        if scatter_prev:
            for r in range(EX_BM):
                scatter_prev_row(r)
        start_gather(tokn_ref, oslot)
        gu = jnp.dot(xbf_ref[...], wgu_bf_ref[...], preferred_element_type=F32) + bgu_ref[0]
        gate = jnp.minimum(gu[:, :D_EXPERT], SWIGLU_LIMIT)
        up = jnp.clip(gu[:, D_EXPERT:], -SWIGLU_LIMIT, SWIGLU_LIMIT)
        act = (up + 1.0) * (gate / (1.0 + jnp.exp(-SWIGLU_ALPHA * gate)))
        yb_ref[slot] = jnp.dot(act.astype(BF16), wd_bf_ref[...], preferred_element_type=F32) + bd_ref[0]

    prev_full = (i >= 1) & (nv_prev == EX_BM)

    @pl.when(used & (i >= 1) & (nv_prev < EX_BM))
    def _():
        scatter_prev_partial()

    @pl.when(used & jnp.logical_not(prev_full))
    def _():
        step_body(False)

    @pl.when(used & prev_full)
    def _():
        step_body(True)

    @pl.when(i == nu)
    def _():
        wait_block(gsem.at[slot], xb_ref.at[slot])

        @pl.when(nv_prev == EX_BM)
        def _():
            for r in range(EX_BM):
                scatter_prev_row(r)

        @pl.when(nv_prev < EX_BM)
        def _():
            scatter_prev_partial()

        wait_scatter(oslot, nv_prev)


def _experts(block_e, n_used, n_valid, buf_tok3, dst3, h2, w_gu, b_gu, w_down, b_down, n_slots):
    nb = buf_tok3.shape[0]
    idx_blk = lambda f: pl.BlockSpec((1, 1, EX_BM), f, memory_space=pltpu.SMEM)
    vmem = (2 * (D_MODEL * 2 * D_EXPERT * 4 + D_EXPERT * D_MODEL * 4)
            + D_MODEL * 2 * D_EXPERT * 2 + D_EXPERT * D_MODEL * 2
            + 3 * EX_BM * D_MODEL * 4 + 3 * EX_BM * 2 * D_EXPERT * 4)
    blk = lambda i: jnp.minimum(i, nb - 1)
    w_map = lambda i, be, nu, nv: (be[blk(i)], 0, 0)
    grid_spec = pltpu.PrefetchScalarGridSpec(
        num_scalar_prefetch=3,
        grid=(nb + 1,),
        in_specs=[
            idx_blk(lambda i, be, nu, nv: (blk(i), 0, 0)),
            idx_blk(lambda i, be, nu, nv: (blk(i + 1), 0, 0)),
            idx_blk(lambda i, be, nu, nv: (jnp.maximum(i - 1, 0), 0, 0)),
            pl.BlockSpec(memory_space=pl.ANY),
            pl.BlockSpec((1, D_MODEL, 2 * D_EXPERT), w_map),
            pl.BlockSpec((1, 1, 2 * D_EXPERT), w_map),
            pl.BlockSpec((1, D_EXPERT, D_MODEL), w_map),
            pl.BlockSpec((1, 1, D_MODEL), w_map),
        ],
        out_specs=pl.BlockSpec(memory_space=pl.ANY),
        scratch_shapes=[
            pltpu.VMEM((2, EX_BM, D_MODEL), F32),
            pltpu.VMEM((2, EX_BM, D_MODEL), F32),
            pltpu.VMEM((EX_BM, D_MODEL), BF16),
            pltpu.VMEM((D_MODEL, 2 * D_EXPERT), BF16),
            pltpu.VMEM((D_EXPERT, D_MODEL), BF16),
            pltpu.SemaphoreType.DMA((2,)),
            pltpu.SemaphoreType.DMA((2,)),
        ],
    )
    return pl.pallas_call(
        _experts_kernel,
        grid_spec=grid_spec,
        out_shape=jax.ShapeDtypeStruct((n_slots, D_MODEL), F32),
        compiler_params=pltpu.CompilerParams(
            dimension_semantics=("arbitrary",), vmem_limit_bytes=_vmem_limit(vmem)),
        name="experts",
    )(block_e, n_used, n_valid, buf_tok3, buf_tok3, dst3, h2, w_gu, b_gu[:, None, :], w_down,
      b_down[:, None, :])


CB_TM = 256


def _combine_kernel(x1_ref, y0_ref, y1_ref, y2_ref, y3_ref, gate_ref, o_ref):
    tm = x1_ref.shape[0]
    g = jnp.concatenate([gate_ref[...], jnp.zeros((LANES - SUBLANES, tm), F32)], axis=0).T
    acc = x1_ref[...]
    for k, y_ref in enumerate((y0_ref, y1_ref, y2_ref, y3_ref)):
        acc = acc + g[:, k:k + 1] * y_ref[...]
    o_ref[...] = acc


def _combine(x1, y_slots, gates):
    n = x1.shape[0]
    tm = CB_TM
    nt = n // tm
    vmem = 2 * (2 * tm * D_MODEL * 4 + tm * TOP_K * D_MODEL * 4) + 4 * tm * D_MODEL * 4
    y_spec = lambda k: pl.BlockSpec((tm, D_MODEL), lambda i: (k * nt + i, 0))
    return pl.pallas_call(
        _combine_kernel,
        grid=(nt,),
        in_specs=[pl.BlockSpec((tm, D_MODEL), lambda i: (i, 0))]
        + [y_spec(k) for k in range(TOP_K)]
        + [pl.BlockSpec((SUBLANES, tm), lambda i: (0, i))],
        out_specs=pl.BlockSpec((tm, D_MODEL), lambda i: (i, 0)),
        out_shape=jax.ShapeDtypeStruct((n, D_MODEL), F32),
        compiler_params=pltpu.CompilerParams(
            dimension_semantics=("arbitrary",), vmem_limit_bytes=_vmem_limit(vmem)),
        name="combine",
    )(x1, y_slots, y_slots, y_slots, y_slots, gates)


def _dispatch_plan(eidx, rank, counts):
    n = eidx.shape[1]
    n_slots = n * TOP_K
    nb = n_slots // EX_BM + N_EXPERTS
    counts = counts.astype(jnp.int32)
    padded = ((counts + EX_BM - 1) // EX_BM) * EX_BM
    pad_end = jnp.cumsum(padded)
    pad_start = pad_end - padded
    e = eidx[:TOP_K]
    start_of = jnp.sum(jnp.where(e[:, :, None] == jnp.arange(N_EXPERTS, dtype=jnp.int32),
                                 pad_start[None, None, :], 0), axis=-1)
    pos = start_of + rank[:TOP_K]
    slot_id = jnp.arange(n_slots, dtype=jnp.int32)
    inv = jnp.full((nb * EX_BM,), -1, jnp.int32).at[pos.reshape(-1)].set(slot_id, unique_indices=True)
    buf_tok = jnp.where(inv >= 0, inv % n, 0)
    dst = jnp.maximum(inv, 0)
    blk_start = jnp.arange(nb, dtype=jnp.int32) * EX_BM
    block_e = jnp.minimum(jnp.sum((pad_end[None, :] <= blk_start[:, None]).astype(jnp.int32), axis=1),
                          N_EXPERTS - 1)
    n_used = (pad_end[-1] // EX_BM).astype(jnp.int32).reshape(1)
    n_valid = jnp.clip(pad_start[block_e] + counts[block_e] - blk_start, 0, EX_BM).astype(jnp.int32)
    n_valid = jnp.where(blk_start < pad_end[-1], n_valid, 0)
    return (block_e, n_used, n_valid, buf_tok.reshape(nb, 1, EX_BM), dst.reshape(nb, 1, EX_BM), n_slots)


def _prep_in_proj_weights(w_in, fox_f_bias, mlstm_i_bias, mlstm_f_bias):
    split_at = []
    acc = 0
    for wdt in SPLIT_WIDTHS[:-1]:
        acc += wdt
        split_at.append(acc)
    fq, fk, fv, ff, mq, mk, mv, mi, mf, mo = jnp.split(w_in, split_at, axis=-1)
    w_main = jnp.concatenate([fq, fk, fv, mq, mk, mv, mo], axis=-1).astype(BF16)
    w_gate = jnp.concatenate([ff, mi, mf], axis=-1)
    bias = jnp.concatenate([fox_f_bias, mlstm_i_bias, mlstm_f_bias]).astype(F32)
    n_gate = w_gate.shape[1]
    wg = jnp.pad(w_gate, ((0, 0), (0, LANES - n_gate)))
    wg_hi, wg_lo = _split_hi_lo(wg)
    wgt = jnp.pad(w_gate.T, ((0, GATE_ROWS - n_gate), (0, 0)))
    wgt_hi, wgt_lo = _split_hi_lo(wgt)
    bias_r = jnp.pad(bias, (0, LANES - n_gate))[None, :]
    bias_t = jnp.pad(bias, (0, GATE_ROWS - n_gate))[:, None]
    return w_main, wgt_hi, wgt_lo, bias_t, wg_hi, wg_lo, bias_r


def kernel(x, attn_norm_w, w_in, fox_f_bias, fox_q_norm_w, fox_k_norm_w, fox_out_norm_w, mlstm_conv_w, mlstm_i_bias, mlstm_f_bias, mlstm_out_norm_w, w_out, moe_norm_w, router_w, router_b, expert_w_gate_up, expert_b_gate_up, expert_w_down, expert_b_down):
    bsz, seq, d = x.shape
    x2d = x.reshape(bsz * seq, d)
    prep = _prep_in_proj_weights(w_in[0], fox_f_bias[0], mlstm_i_bias[0], mlstm_f_bias[0])
    proj, gt, gtm = _in_proj(x2d, attn_norm_w[0][None, :], *prep)
    proj3 = proj.reshape(bsz, seq, MAIN_WIDTH)
    fox_y = _fox(proj3, gt, gtm, fox_q_norm_w[0][None, :], fox_k_norm_w[0][None, :],
                 fox_out_norm_w[0][None, :])
    mlstm_y = _mlstm(proj3, gt, gtm, mlstm_conv_w[0])
    return _channel_mixer(x2d, fox_y.reshape(-1, FOX_WIDTH), mlstm_y.reshape(-1, MLSTM_V_WIDTH), proj,
                          mlstm_out_norm_w[0], w_out[0], moe_norm_w[0], router_w[0], router_b[0],
                          expert_w_gate_up[0], expert_b_gate_up[0], expert_w_down[0],
                          expert_b_down[0]).reshape(bsz, seq, d)


def _channel_mixer(x2d, fox_y2d, mlstm_y2d, proj, mlstm_gain, w_out, moe_norm_w, router_w, router_b,
                   w_gu, b_gu, w_down, b_down):
    wr_hi, wr_lo = _split_hi_lo(router_w.T)
    x1, h2, eidx, gates, rank, counts = _out_route(
        x2d, fox_y2d, mlstm_y2d, proj, w_out.astype(BF16), mlstm_gain[None, :], moe_norm_w[None, :],
        wr_hi, wr_lo, router_b[:, None])
    block_e, n_used, n_valid, buf_tok3, dst3, n_slots = _dispatch_plan(eidx, rank, counts[:, 0])
    y_slots = _experts(block_e, n_used, n_valid, buf_tok3, dst3, h2, w_gu, b_gu, w_down, b_down, n_slots)
    return _combine(x1, y_slots, gates)
```

```python
import functools
import math

import jax
import jax.numpy as jnp
from jax import lax
from jax.experimental import pallas as pl
from jax.experimental.pallas import tpu as pltpu

F32 = jnp.float32
BF16 = jnp.bfloat16

D_MODEL = 1024
SEQ = 2048
HEAD_DIM = 64
FOX_HEADS = 8
FOX_WIDTH = FOX_HEADS * HEAD_DIM
MLSTM_HEADS = 8
MLSTM_QK_DIM = 32
MLSTM_V_DIM = 64
MLSTM_QK_WIDTH = MLSTM_HEADS * MLSTM_QK_DIM
MLSTM_V_WIDTH = MLSTM_HEADS * MLSTM_V_DIM
CONV_WIDTH = 4
MLSTM_CHUNK = 64
SPLIT_WIDTHS = (FOX_WIDTH, FOX_WIDTH, FOX_WIDTH, FOX_HEADS,
                MLSTM_QK_WIDTH, MLSTM_QK_WIDTH, MLSTM_V_WIDTH,
                MLSTM_HEADS, MLSTM_HEADS, MLSTM_V_WIDTH)
N_EXPERTS = 32
TOP_K = 4
D_EXPERT = D_MODEL
SWIGLU_ALPHA = 1.702
SWIGLU_LIMIT = 7.0
NORM_EPS = 1e-5

LANES = 128
SUBLANES = 8
V7X_VMEM_BYTES = 64 * 1024 * 1024

MAIN_WIDTH = 3 * FOX_WIDTH + 2 * MLSTM_QK_WIDTH + 2 * MLSTM_V_WIDTH
GATE_ROWS = 32

NT_DIMS = (((1,), (1,)), ((), ()))


def _vmem_limit(nbytes):
    return int(min(nbytes + (8 << 20), V7X_VMEM_BYTES - (4 << 20)))


def _log_sigmoid(x):
    return jnp.minimum(x, 0.0) - jnp.log(1.0 + jnp.exp(-jnp.abs(x)))


def _split_hi_lo(x):
    hi = x.astype(BF16)
    lo = (x - hi.astype(F32)).astype(BF16)
    return hi, lo


TOK_ROWS = D_MODEL // LANES


def _rows_to_tiles(x, tile_ref):
    m = x.shape[0]
    for j in range(TOK_ROWS):
        tile_ref[pl.ds(j, m, stride=TOK_ROWS), :] = x[:, j * LANES:(j + 1) * LANES]


def _tiles_to_rows(tile_ref, m):
    return jnp.concatenate(
        [tile_ref[pl.ds(j, m, stride=TOK_ROWS), :] for j in range(TOK_ROWS)], axis=1)


IN_TM = 512


def _in_proj_kernel(x_ref, nw_ref, w_ref, wgt_hi_ref, wgt_lo_ref, bt_ref,
                    wg_hi_ref, wg_lo_ref, br_ref, proj_ref, gt_ref, gtm_ref):
    x = x_ref[...]
    ms = jnp.mean(x * x, axis=-1, keepdims=True)
    y = x * lax.rsqrt(ms + NORM_EPS) * nw_ref[...]
    h_hi, h_lo = _split_hi_lo(y)
    proj_ref[...] = jnp.dot(h_hi, w_ref[...], preferred_element_type=F32)

    wt_hi, wt_lo = wgt_hi_ref[...], wgt_lo_ref[...]
    gt = (lax.dot_general(wt_hi, h_hi, NT_DIMS, preferred_element_type=F32)
          + lax.dot_general(wt_lo, h_hi, NT_DIMS, preferred_element_type=F32)
          + lax.dot_general(wt_hi, h_lo, NT_DIMS, preferred_element_type=F32))
    gt = gt + bt_ref[...]
    row = lax.broadcasted_iota(jnp.int32, gt.shape, 0)
    is_input_gate = (row >= FOX_HEADS) & (row < FOX_HEADS + MLSTM_HEADS)
    gt_ref[...] = jnp.where(is_input_gate, gt, _log_sigmoid(gt))

    w_hi, w_lo = wg_hi_ref[...], wg_lo_ref[...]
    g = (jnp.dot(h_hi, w_hi, preferred_element_type=F32)
         + jnp.dot(h_hi, w_lo, preferred_element_type=F32)
         + jnp.dot(h_lo, w_hi, preferred_element_type=F32))
    g = g + br_ref[...]
    lane = lax.broadcasted_iota(jnp.int32, g.shape, 1)
    is_input_gate = (lane >= FOX_HEADS) & (lane < FOX_HEADS + MLSTM_HEADS)
    gtm_ref[...] = jnp.where(is_input_gate, g, _log_sigmoid(g))


def _in_proj(x2d, norm_w, w_main, wgt_hi, wgt_lo, bias_t, wg_hi, wg_lo, bias_r):
    n = x2d.shape[0]
    tm = IN_TM
    const = lambda i: (0, 0)
    vmem = 2 * (tm * D_MODEL * 4 + D_MODEL * MAIN_WIDTH * 2 + tm * MAIN_WIDTH * 4
                + GATE_ROWS * tm * 4 + tm * LANES * 4) + tm * MAIN_WIDTH * 4
    return pl.pallas_call(
        _in_proj_kernel,
        grid=(n // tm,),
        in_specs=[
            pl.BlockSpec((tm, D_MODEL), lambda i: (i, 0)),
            pl.BlockSpec((1, D_MODEL), const),
            pl.BlockSpec((D_MODEL, MAIN_WIDTH), const),
            pl.BlockSpec((GATE_ROWS, D_MODEL), const),
            pl.BlockSpec((GATE_ROWS, D_MODEL), const),
            pl.BlockSpec((GATE_ROWS, 1), const),
            pl.BlockSpec((D_MODEL, LANES), const),
            pl.BlockSpec((D_MODEL, LANES), const),
            pl.BlockSpec((1, LANES), const),
        ],
        out_specs=[
            pl.BlockSpec((tm, MAIN_WIDTH), lambda i: (i, 0)),
            pl.BlockSpec((GATE_ROWS, tm), lambda i: (0, i)),
            pl.BlockSpec((tm, LANES), lambda i: (i, 0)),
        ],
        out_shape=[
            jax.ShapeDtypeStruct((n, MAIN_WIDTH), F32),
            jax.ShapeDtypeStruct((GATE_ROWS, n), F32),
            jax.ShapeDtypeStruct((n, LANES), F32),
        ],
        compiler_params=pltpu.CompilerParams(
            dimension_semantics=("arbitrary",), vmem_limit_bytes=_vmem_limit(vmem)),
        name="in_proj",
    )(x2d, norm_w, w_main, wgt_hi, wgt_lo, bias_t, wg_hi, wg_lo, bias_r)


FOX_TQ = 256
SCAN_SHIFTS = tuple(1 << i for i in range(int(math.log2(SEQ))))


def _rms(x, w):
    return x * lax.rsqrt(jnp.mean(x * x, axis=-1, keepdims=True) + NORM_EPS) * w


def _fox_kernel(q_ref, k_ref, v_ref, gt_ref, gtm_ref, qw_ref, kw_ref, ow_ref, o_ref,
                cumr_ref, cumc_ref):
    hp = pl.program_id(1)

    @pl.when(hp == 0)
    def _():
        x = gt_ref[...]
        lane = lax.broadcasted_iota(jnp.int32, x.shape, 1)
        for s in SCAN_SHIFTS:
            x = x + jnp.where(lane >= s, pltpu.roll(x, s, axis=1), 0.0)
        cumr_ref[...] = x
        y = gtm_ref[...]
        rowi = lax.broadcasted_iota(jnp.int32, y.shape, 0)
        for s in SCAN_SHIFTS:
            y = y + jnp.where(rowi >= s, pltpu.roll(y, s, axis=0), 0.0)
        cumc_ref[...] = y

    lane_t = lax.broadcasted_iota(jnp.int32, (SEQ, LANES), 1)
    for j in range(2):
        h = 2 * hp + j
        cs = slice(j * HEAD_DIM, (j + 1) * HEAD_DIM)
        kn = _rms(k_ref[0, :, cs], kw_ref[...]).astype(BF16)
        vb = v_ref[0, :, cs].astype(BF16)
        cum_row = cumr_ref[pl.ds(h, 1), :]
        cum_col = jnp.sum(jnp.where(lane_t == h, cumc_ref[...], 0.0),
                          axis=1, keepdims=True)
        for i in range(SEQ // FOX_TQ):
            qs = slice(i * FOX_TQ, (i + 1) * FOX_TQ)
            n = (i + 1) * FOX_TQ
            qn = (_rms(q_ref[0, qs, cs], qw_ref[...]) * (HEAD_DIM ** -0.5)).astype(BF16)
            s = lax.dot_general(qn, kn[:n], NT_DIMS, preferred_element_type=F32)
            s = s + (cum_col[qs] - cum_row[:, :n])
            col = lax.broadcasted_iota(jnp.int32, s.shape, 1)
            row = lax.broadcasted_iota(jnp.int32, s.shape, 0) + i * FOX_TQ
            s = jnp.where(col <= row, s, -jnp.inf)
            m = jnp.max(s, axis=-1, keepdims=True)
            p = jnp.exp(s - m)
            l = jnp.sum(p, axis=-1, keepdims=True)
            o = jnp.dot(p.astype(BF16), vb[:n], preferred_element_type=F32) / l
            o_ref[0, qs, cs] = _rms(o, ow_ref[:, cs])


def _fox(proj3, gt, gtm, qw, kw, ow):
    b = proj3.shape[0]
    nq = FOX_WIDTH // LANES
    blk = (1, SEQ, LANES)
    vmem = 2 * 5 * SEQ * LANES * 4 + 2 * SEQ * LANES * 4 + 6 * FOX_TQ * SEQ * 4
    return pl.pallas_call(
        _fox_kernel,
        grid=(b, nq),
        in_specs=[
            pl.BlockSpec(blk, lambda bi, hp: (bi, 0, hp)),
            pl.BlockSpec(blk, lambda bi, hp: (bi, 0, nq + hp)),
            pl.BlockSpec(blk, lambda bi, hp: (bi, 0, 2 * nq + hp)),
            pl.BlockSpec((SUBLANES, SEQ), lambda bi, hp: (0, bi)),
            pl.BlockSpec((SEQ, LANES), lambda bi, hp: (bi, 0)),
            pl.BlockSpec((1, HEAD_DIM), lambda bi, hp: (0, 0)),
            pl.BlockSpec((1, HEAD_DIM), lambda bi, hp: (0, 0)),
            pl.BlockSpec((1, LANES), lambda bi, hp: (0, hp)),
        ],
        out_specs=pl.BlockSpec(blk, lambda bi, hp: (bi, 0, hp)),
        out_shape=jax.ShapeDtypeStruct((b, SEQ, FOX_WIDTH), F32),
        scratch_shapes=[pltpu.VMEM((SUBLANES, SEQ), F32), pltpu.VMEM((SEQ, LANES), F32)],
        compiler_params=pltpu.CompilerParams(
            dimension_semantics=("arbitrary", "arbitrary"), vmem_limit_bytes=_vmem_limit(vmem)),
        name="fox",
    )(proj3, proj3, proj3, gt, gtm, qw, kw, ow)


ML_L = MLSTM_CHUNK
ML_PAIRS = SEQ // (2 * ML_L)
ML_HL = FOX_HEADS
ML_AUG = MLSTM_V_WIDTH + LANES
ML_TILE = 256
SEG_SHIFTS = tuple(1 << i for i in range(int(math.log2(ML_L))))


def _split3(x):
    a = x.astype(BF16)
    r = x - a.astype(F32)
    b = r.astype(BF16)
    c = (r - b.astype(F32)).astype(BF16)
    return a, b, c


def _expand_heads(x, exp_bf):
    a, b, c = _split3(x)
    return (jnp.dot(a, exp_bf, preferred_element_type=F32)
            + jnp.dot(b, exp_bf, preferred_element_type=F32)
            + jnp.dot(c, exp_bf, preferred_element_type=F32))


def _seg_scan(x, axis, op, ident):
    idx = lax.broadcasted_iota(jnp.int32, x.shape, axis) % ML_L
    for s in SEG_SHIFTS:
        x = op(x, jnp.where(idx >= s, pltpu.roll(x, s, axis=axis), ident))
    return x


def _mlstm_kernel(q_ref, k_ref, v_ref, gi_ref, gf_ref, gtm_ref, cw_ref, o_ref, den_ref,
                  qc_ref, kc_ref, kt_ref, rr_ref, cmr_ref, bcr_ref,
                  ealpha_ref, ew_ref, wint_ref, floor_ref, mfull_ref, caug_ref):
    def conv_silu(u, w):
        rowi = lax.broadcasted_iota(jnp.int32, u.shape, 0)
        acc = u * w[CONV_WIDTH - 1:CONV_WIDTH, :]
        for d in range(1, CONV_WIDTH):
            sh = jnp.where(rowi >= d, pltpu.roll(u, d, axis=0), 0.0)
            acc = acc + sh * w[CONV_WIDTH - 1 - d:CONV_WIDTH - d, :]
        return acc / (1.0 + jnp.exp(-acc))

    cw = cw_ref[...]
    qc_ref[...] = conv_silu(q_ref[0], cw[:, :MLSTM_QK_WIDTH]).astype(BF16)
    kc = conv_silu(k_ref[0], cw[:, MLSTM_QK_WIDTH:]) * (MLSTM_QK_DIM ** -0.5)
    kc_ref[...] = kc.astype(BF16)
    kt = kc.T
    for p in range(ML_PAIRS):
        kt_ref[p] = kt[:, p * LANES:(p + 1) * LANES]

    bcum_r = _seg_scan(gf_ref[...], 1, jnp.add, 0.0)
    r_r = gi_ref[...] - bcum_r
    cmx_r = _seg_scan(r_r, 1, jnp.maximum, -jnp.inf)
    for p in range(ML_PAIRS):
        ls = slice(p * LANES, (p + 1) * LANES)
        rr_ref[p] = r_r[:, ls]
        cmr_ref[p] = cmx_r[:, ls]
        bcr_ref[p] = bcum_r[:, ls]

    g = gtm_ref[...]
    lane_g = lax.broadcasted_iota(jnp.int32, g.shape, 1)
    head_lane = (lane_g >= ML_HL) & (lane_g < ML_HL + MLSTM_HEADS)
    bcum_c = jnp.where(
        head_lane, pltpu.roll(_seg_scan(g, 0, jnp.add, 0.0), LANES - MLSTM_HEADS, axis=1), 0.0)
    cmx_c = _seg_scan(jnp.where(head_lane, g, 0.0) - bcum_c, 0, jnp.maximum, -jnp.inf)
    m = jnp.zeros((1, LANES), F32)
    for c in range(SEQ // ML_L):
        mfull_ref[c * ML_L:(c + 1) * ML_L, :] = jnp.broadcast_to(m, (ML_L, LANES))
        last = (c + 1) * ML_L - 1
        m = bcum_c[last:last + 1, :] + jnp.maximum(m, cmx_c[last:last + 1, :])
    mfull = mfull_ref[...]
    mx = jnp.maximum(mfull, cmx_c)
    wint_ref[...] = jnp.exp(mfull - mx)
    floor_ref[...] = jnp.exp(-(bcum_c + mx))
    mfull_ref[...] = -mx

    lane_e = lax.broadcasted_iota(jnp.int32, (LANES, MLSTM_V_WIDTH), 1) // MLSTM_V_DIM
    row_e = lax.broadcasted_iota(jnp.int32, (LANES, MLSTM_V_WIDTH), 0)
    exp_bf = jnp.where(row_e == lane_e + ML_HL, 1.0, 0.0).astype(BF16)

    def expand_tile(i, carry):
        rows = pl.ds(pl.multiple_of(i * ML_TILE, ML_TILE), ML_TILE)
        ealpha_ref[rows, :] = _expand_heads(mfull_ref[rows, :], exp_bf)
        ew_ref[rows, :] = _expand_heads(wint_ref[rows, :], exp_bf)
        return carry

    lax.fori_loop(0, SEQ // ML_TILE, expand_tile, 0)

    kb_rowh = lax.broadcasted_iota(jnp.int32, (MLSTM_HEADS * ML_L, MLSTM_QK_WIDTH), 0) // ML_L
    kb_lane = lax.broadcasted_iota(jnp.int32, (MLSTM_HEADS * ML_L, MLSTM_QK_WIDTH), 1) // MLSTM_QK_DIM
    mask_k = jnp.where(kb_rowh == kb_lane, 1.0, 0.0).astype(BF16)
    va_rowh = lax.broadcasted_iota(jnp.int32, (MLSTM_HEADS * ML_L, ML_AUG), 0) // ML_L
    va_col = lax.broadcasted_iota(jnp.int32, (MLSTM_HEADS * ML_L, ML_AUG), 1)
    mask_v = jnp.where(
        (va_col // MLSTM_V_DIM == va_rowh) | (va_col == MLSTM_V_WIDTH + ML_HL + va_rowh),
        1.0, 0.0).astype(BF16)
    c_rowh = lax.broadcasted_iota(jnp.int32, (MLSTM_QK_WIDTH, ML_AUG), 0) // MLSTM_QK_DIM
    c_col = lax.broadcasted_iota(jnp.int32, (MLSTM_QK_WIDTH, ML_AUG), 1)
    mask_c = (c_col // MLSTM_V_DIM == c_rowh) | (c_col == MLSTM_V_WIDTH + ML_HL + c_rowh)
    lane128 = lax.broadcasted_iota(jnp.int32, (ML_L, LANES), 1)
    s_idx = lax.broadcasted_iota(jnp.int32, (ML_L, MLSTM_V_WIDTH), 1) % ML_L
    t_idx = lax.broadcasted_iota(jnp.int32, (ML_L, MLSTM_V_WIDTH), 0)
    causal = s_idx <= t_idx
    ones_aug = jnp.ones((ML_L, LANES), F32)

    caug_ref[...] = jnp.zeros_like(caug_ref)

    def pair_body(cp, m_row):
        r2 = rr_ref[cp]
        cm2 = cmr_ref[cp]
        b2 = bcr_ref[cp]
        r2r = pltpu.roll(r2, ML_L, axis=1)
        kt2 = kt_ref[cp]
        for cc in range(2):
            lo = cc * ML_L
            rows = pl.ds(pl.multiple_of(cp * (2 * ML_L), 2 * ML_L) + lo, ML_L)
            qa = qc_ref[rows, :]
            ka = kc_ref[rows, :]
            va_aug = jnp.concatenate([v_ref[0, rows, :], ones_aug], axis=1).astype(BF16)

            kbd = jnp.concatenate([ka] * MLSTM_HEADS, axis=0) * mask_k
            s = lax.dot_general(qa, kbd, NT_DIMS, preferred_element_type=F32)

            cmx_last = cm2[:, lo + ML_L - 1:lo + ML_L]
            b_last = b2[:, lo + ML_L - 1:lo + ML_L]
            mx_r = jnp.maximum(m_row, cmx_last)
            decay = jnp.exp(m_row - mx_r)
            wk = jnp.exp(r2[:, lo:lo + ML_L] - mx_r)
            m_row = b_last + mx_r

            src_e, src_o = (r2, r2r) if cc == 0 else (r2r, r2)
            cols = []
            for p in range(MLSTM_HEADS // 2):
                even = jnp.broadcast_to(src_e[2 * p:2 * p + 1, :], (ML_L, LANES))
                odd = jnp.broadcast_to(src_o[2 * p + 1:2 * p + 2, :], (ML_L, LANES))
                cols.append(jnp.where(lane128 < ML_L, even, odd))
            r_all = jnp.concatenate(cols, axis=1)
            arg = jnp.where(causal, ealpha_ref[rows, :] + r_all, -jnp.inf)
            p_all = (s * jnp.exp(arg)).astype(BF16)

            vbd = jnp.concatenate([va_aug] * MLSTM_HEADS, axis=0) * mask_v
            pv = jnp.dot(p_all, vbd, preferred_element_type=F32)
            qc_state = jnp.dot(qa, caug_ref[...].astype(BF16), preferred_element_type=F32)
            o_ref[0, rows, :] = (ew_ref[rows, :] * qc_state[:, :MLSTM_V_WIDTH]
                                 + pv[:, :MLSTM_V_WIDTH])
            den_ref[rows, :] = (wint_ref[rows, :] * qc_state[:, MLSTM_V_WIDTH:]
                                + pv[:, MLSTM_V_WIDTH:])

            wk_rows = jnp.concatenate(
                [jnp.broadcast_to(wk[h:h + 1, :], (MLSTM_QK_DIM, ML_L)) for h in range(MLSTM_HEADS)],
                axis=0)
            dec_rows = jnp.concatenate(
                [jnp.broadcast_to(decay[h:h + 1, :], (MLSTM_QK_DIM, 1)) for h in range(MLSTM_HEADS)],
                axis=0)
            ktw = (kt2[:, lo:lo + ML_L] * wk_rows).astype(BF16)
            upd = jnp.dot(ktw, va_aug, preferred_element_type=F32)
            caug_ref[...] = dec_rows * caug_ref[...] + jnp.where(mask_c, upd, 0.0)
        return m_row

    lax.fori_loop(0, ML_PAIRS, pair_body, jnp.zeros((MLSTM_HEADS, 1), F32))

    ob_row = lax.broadcasted_iota(jnp.int32, (MLSTM_V_WIDTH, LANES), 0) // MLSTM_V_DIM
    ob_col = lax.broadcasted_iota(jnp.int32, (MLSTM_V_WIDTH, LANES), 1)
    ones_bd = jnp.where(ob_col == ob_row + ML_HL, 1.0, 0.0).astype(BF16)

    def norm_tile(i, carry):
        rows = pl.ds(pl.multiple_of(i * ML_TILE, ML_TILE), ML_TILE)
        num = o_ref[0, rows, :]
        dn = jnp.maximum(jnp.abs(den_ref[rows, :]), floor_ref[rows, :])
        r = 1.0 / dn
        n2_hi, n2_lo = _split_hi_lo(num * num)
        msn = (jnp.dot(n2_hi, ones_bd, preferred_element_type=F32)
               + jnp.dot(n2_lo, ones_bd, preferred_element_type=F32)) * (1.0 / MLSTM_V_DIM)
        fac = r * lax.rsqrt(r * r * msn + NORM_EPS)
        o_ref[0, rows, :] = num * _expand_heads(fac, exp_bf)
        return carry

    lax.fori_loop(0, SEQ // ML_TILE, norm_tile, 0)


def _mlstm(proj3, gt, gtm, conv_w):
    b = proj3.shape[0]
    qk_blk = (1, SEQ, MLSTM_QK_WIDTH)
    v_blk = (1, SEQ, MLSTM_V_WIDTH)
    q_col = 3 * FOX_WIDTH // MLSTM_QK_WIDTH
    v_col = (3 * FOX_WIDTH + 2 * MLSTM_QK_WIDTH) // MLSTM_V_WIDTH
    vmem = (2 * (2 * SEQ * MLSTM_QK_WIDTH * 4 + 2 * SEQ * MLSTM_V_WIDTH * 4 + 2 * SEQ * LANES * 4)
            + 2 * SEQ * MLSTM_QK_WIDTH * 2 + SEQ * MLSTM_QK_WIDTH * 4 + 2 * SEQ * MLSTM_V_WIDTH * 4
            + 3 * SEQ * LANES * 4 + (8 << 20))
    return pl.pallas_call(
        _mlstm_kernel,
        grid=(b,),
        in_specs=[
            pl.BlockSpec(qk_blk, lambda bi: (bi, 0, q_col)),
            pl.BlockSpec(qk_blk, lambda bi: (bi, 0, q_col + 1)),
            pl.BlockSpec(v_blk, lambda bi: (bi, 0, v_col)),
            pl.BlockSpec((SUBLANES, SEQ), lambda bi: (1, bi)),
            pl.BlockSpec((SUBLANES, SEQ), lambda bi: (2, bi)),
            pl.BlockSpec((SEQ, LANES), lambda bi: (bi, 0)),
            pl.BlockSpec((CONV_WIDTH, 2 * MLSTM_QK_WIDTH), lambda bi: (0, 0)),
        ],
        out_specs=pl.BlockSpec(v_blk, lambda bi: (bi, 0, 0)),
        out_shape=jax.ShapeDtypeStruct((b, SEQ, MLSTM_V_WIDTH), F32),
        scratch_shapes=[
            pltpu.VMEM((SEQ, LANES), F32),
            pltpu.VMEM((SEQ, MLSTM_QK_WIDTH), BF16),
            pltpu.VMEM((SEQ, MLSTM_QK_WIDTH), BF16),
            pltpu.VMEM((ML_PAIRS, MLSTM_QK_WIDTH, LANES), F32),
            pltpu.VMEM((ML_PAIRS, SUBLANES, LANES), F32),
            pltpu.VMEM((ML_PAIRS, SUBLANES, LANES), F32),
            pltpu.VMEM((ML_PAIRS, SUBLANES, LANES), F32),
            pltpu.VMEM((SEQ, MLSTM_V_WIDTH), F32),
            pltpu.VMEM((SEQ, MLSTM_V_WIDTH), F32),
            pltpu.VMEM((SEQ, LANES), F32),
            pltpu.VMEM((SEQ, LANES), F32),
            pltpu.VMEM((SEQ, LANES), F32),
            pltpu.VMEM((MLSTM_QK_WIDTH, ML_AUG), F32),
        ],
        compiler_params=pltpu.CompilerParams(
            dimension_semantics=("arbitrary",), vmem_limit_bytes=_vmem_limit(vmem)),
        name="mlstm",
    )(proj3, proj3, proj3, gt, gt, gtm, conv_w)


RT_TM = 512


def _out_route_kernel(x_ref, fy_ref, my_ref, mo_ref, wo_ref, mg_ref, nw_ref, wr_hi_ref, wr_lo_ref,
                      rb_ref, x1_ref, h2_ref, eidx_ref, gate_ref, rank_ref, cnt_ref, carry_ref):
    i = pl.program_id(0)

    @pl.when(i == 0)
    def _():
        carry_ref[...] = jnp.zeros_like(carry_ref)

    my = my_ref[...] * mg_ref[...] / (1.0 + jnp.exp(-mo_ref[...]))
    mixed = (jnp.dot(fy_ref[...].astype(BF16), wo_ref[:FOX_WIDTH, :], preferred_element_type=F32)
             + jnp.dot(my.astype(BF16), wo_ref[FOX_WIDTH:, :], preferred_element_type=F32))
    x1 = x_ref[...] + mixed
    x1_ref[...] = x1
    h2 = _rms(x1, nw_ref[...])
    _rows_to_tiles(h2, h2_ref)

    h_hi, h_lo = _split_hi_lo(h2)
    wr_hi, wr_lo = wr_hi_ref[...], wr_lo_ref[...]
    logit = (lax.dot_general(wr_hi, h_hi, NT_DIMS, preferred_element_type=F32)
             + lax.dot_general(wr_lo, h_hi, NT_DIMS, preferred_element_type=F32)
             + lax.dot_general(wr_hi, h_lo, NT_DIMS, preferred_element_type=F32)) + rb_ref[...]

    e_iota = lax.broadcasted_iota(jnp.int32, logit.shape, 0).astype(F32)
    vals, idxs, hots = [], [], []
    for _ in range(TOP_K):
        mk = jnp.max(logit, axis=0, keepdims=True)
        idx = jnp.min(jnp.where(logit == mk, e_iota, float(N_EXPERTS)), axis=0, keepdims=True)
        hot = e_iota == idx
        logit = jnp.where(hot, -jnp.inf, logit)
        vals.append(mk)
        idxs.append(idx.astype(jnp.int32))
        hots.append(hot)
    exps = [jnp.exp(v - vals[0]) for v in vals]
    tot = exps[0] + exps[1] + exps[2] + exps[3]
    gates = [e / tot for e in exps]

    assign = jnp.zeros(logit.shape, F32)
    for hot in hots:
        assign = assign + jnp.where(hot, 1.0, 0.0)
    tm = logit.shape[1]
    src = lax.broadcasted_iota(jnp.int32, (tm, tm), 0)
    dst = lax.broadcasted_iota(jnp.int32, (tm, tm), 1)
    upper = jnp.where(src < dst, 1.0, 0.0).astype(BF16)
    base = jnp.dot(assign.astype(BF16), upper, preferred_element_type=F32) + carry_ref[:, 0:1]
    ranks = [jnp.sum(jnp.where(hot, base, 0.0), axis=0, keepdims=True) for hot in hots]
    new_carry = carry_ref[...] + jnp.sum(assign, axis=1, keepdims=True)
    carry_ref[...] = new_carry
    cnt_ref[...] = new_carry

    zi = jnp.zeros((SUBLANES - TOP_K, tm), jnp.int32)
    eidx_ref[...] = jnp.concatenate(idxs + [zi], axis=0)
    rank_ref[...] = jnp.concatenate([r.astype(jnp.int32) for r in ranks] + [zi], axis=0)
    gate_ref[...] = jnp.concatenate(gates + [zi.astype(F32)], axis=0)


def _out_route(x2d, fox_y2d, mlstm_y2d, proj, w_out_bf, mlstm_gain, moe_norm_w, wr_hi, wr_lo, rb):
    n = x2d.shape[0]
    tm = RT_TM
    const = lambda i: (0, 0)
    mo_col = (MAIN_WIDTH - MLSTM_V_WIDTH) // MLSTM_V_WIDTH
    row_blk = lambda w: pl.BlockSpec((tm, w), lambda i: (i, 0))
    lane_blk = pl.BlockSpec((SUBLANES, tm), lambda i: (0, i))
    vmem = (2 * (tm * D_MODEL * 4 * 3 + tm * FOX_WIDTH * 4 * 3 + D_MODEL * D_MODEL * 2)
            + 6 * tm * D_MODEL * 4 + tm * tm * 6)
    return pl.pallas_call(
        _out_route_kernel,
        grid=(n // tm,),
        in_specs=[
            row_blk(D_MODEL), row_blk(FOX_WIDTH), row_blk(MLSTM_V_WIDTH),
            pl.BlockSpec((tm, MLSTM_V_WIDTH), lambda i: (i, mo_col)),
            pl.BlockSpec((D_MODEL, D_MODEL), const),
            pl.BlockSpec((1, MLSTM_V_WIDTH), const),
            pl.BlockSpec((1, D_MODEL), const),
            pl.BlockSpec((N_EXPERTS, D_MODEL), const),
            pl.BlockSpec((N_EXPERTS, D_MODEL), const),
            pl.BlockSpec((N_EXPERTS, 1), const),
        ],
        out_specs=[row_blk(D_MODEL), pl.BlockSpec((tm * TOK_ROWS, LANES), lambda i: (i, 0)),
                   lane_blk, lane_blk, lane_blk, pl.BlockSpec((N_EXPERTS, LANES), const)],
        out_shape=[
            jax.ShapeDtypeStruct((n, D_MODEL), F32),
            jax.ShapeDtypeStruct((n * TOK_ROWS, LANES), F32),
            jax.ShapeDtypeStruct((SUBLANES, n), jnp.int32),
            jax.ShapeDtypeStruct((SUBLANES, n), F32),
            jax.ShapeDtypeStruct((SUBLANES, n), jnp.int32),
            jax.ShapeDtypeStruct((N_EXPERTS, LANES), F32),
        ],
        scratch_shapes=[pltpu.VMEM((N_EXPERTS, LANES), F32)],
        compiler_params=pltpu.CompilerParams(
            dimension_semantics=("arbitrary",), vmem_limit_bytes=_vmem_limit(vmem)),
        name="out_route",
    )(x2d, fox_y2d, mlstm_y2d, proj, w_out_bf, mlstm_gain, moe_norm_w, wr_hi, wr_lo, rb)


EX_BM = 256
EX_DRAIN_STEPS = 2


def _experts_kernel(be_ref, nu_ref, nv_ref, tok_ref, tokn_ref, dst_ref, h2_hbm, wgu_ref, bgu_ref,
                    wd_ref, bd_ref, y_hbm, xt_ref, yt_ref, wgu_bf_ref, wd_bf_ref, gsem, ssem):
    i = pl.program_id(0)
    last_blk = pl.num_programs(0) - 1 - EX_DRAIN_STEPS
    nu = nu_ref[0]
    slot = i % 2
    cur = jnp.minimum(i, last_blk)
    tile_rows = EX_BM * TOK_ROWS

    def tok_tile(ref, idx):
        return ref.at[pl.ds(pl.multiple_of(idx, TOK_ROWS), TOK_ROWS), :]

    def start_gather(idx_ref, s):
        for r in range(EX_BM):
            pltpu.make_async_copy(tok_tile(h2_hbm, idx_ref[0, 0, r]),
                                  xt_ref.at[s, pl.ds(r * TOK_ROWS, TOK_ROWS), :], gsem.at[s]).start()

    def scatter_row(r):
        pltpu.make_async_copy(yt_ref.at[slot, pl.ds(r * TOK_ROWS, TOK_ROWS), :],
                              tok_tile(y_hbm, dst_ref[0, 0, r]), ssem.at[slot]).start()

    def wait_block(sem, buf):
        pltpu.make_async_copy(h2_hbm.at[pl.ds(0, tile_rows), :], buf, sem).wait()

    def wait_scatter(s, n):
        @pl.when(n == EX_BM)
        def _():
            wait_block(ssem.at[s], yt_ref.at[s])

        @pl.when(n < EX_BM)
        def _():
            def body(r, carry):
                pltpu.make_async_copy(h2_hbm.at[pl.ds(0, TOK_ROWS), :],
                                      yt_ref.at[s, pl.ds(0, TOK_ROWS), :], ssem.at[s]).wait()
                return carry
            lax.fori_loop(0, n, body, 0)

    @pl.when(i == 0)
    def _():
        start_gather(tok_ref, 0)

    @pl.when(i + 1 < nu)
    def _():
        start_gather(tokn_ref, 1 - slot)

    @pl.when((i >= 2) & (i - 2 < nu))
    def _():
        wait_scatter(slot, nv_ref[jnp.clip(i - 2, 0, last_blk)])

    @pl.when(i < nu)
    def _():
        @pl.when((i == 0) | (be_ref[cur] != be_ref[jnp.maximum(cur - 1, 0)]))
        def _():
            wgu_bf_ref[...] = wgu_ref[0].astype(BF16)
            wd_bf_ref[...] = wd_ref[0].astype(BF16)

        wait_block(gsem.at[slot], xt_ref.at[slot])
        xb = _tiles_to_rows(xt_ref.at[slot], EX_BM).astype(BF16)
        gu = jnp.dot(xb, wgu_bf_ref[...], preferred_element_type=F32) + bgu_ref[0]
        gate = jnp.minimum(gu[:, :D_EXPERT], SWIGLU_LIMIT)
        up = jnp.clip(gu[:, D_EXPERT:], -SWIGLU_LIMIT, SWIGLU_LIMIT)
        act = (up + 1.0) * (gate / (1.0 + jnp.exp(-SWIGLU_ALPHA * gate)))
        y = jnp.dot(act.astype(BF16), wd_bf_ref[...], preferred_element_type=F32) + bd_ref[0]
        _rows_to_tiles(y, yt_ref.at[slot])
        nv = nv_ref[cur]

        @pl.when(nv == EX_BM)
        def _():
            for r in range(EX_BM):
                scatter_row(r)

        @pl.when(nv < EX_BM)
        def _():
            def body(r, carry):
                pltpu.make_async_copy(
                    yt_ref.at[slot, pl.ds(pl.multiple_of(r * TOK_ROWS, TOK_ROWS), TOK_ROWS), :],
                    tok_tile(y_hbm, dst_ref[0, 0, r]), ssem.at[slot]).start()
                return carry
            lax.fori_loop(0, nv, body, 0)


def _experts(block_e, n_used, n_valid, buf_tok3, dst3, h2, w_gu, b_gu, w_down, b_down, n_slots):
    nb = buf_tok3.shape[0]
    idx_blk = lambda f: pl.BlockSpec((1, 1, EX_BM), f, memory_space=pltpu.SMEM)
    vmem = (2 * (D_MODEL * 2 * D_EXPERT * 4 + D_EXPERT * D_MODEL * 4)
            + D_MODEL * 2 * D_EXPERT * 2 + D_EXPERT * D_MODEL * 2
            + 3 * EX_BM * D_MODEL * 4 + 3 * EX_BM * 2 * D_EXPERT * 4)
    blk = lambda i: jnp.minimum(i, nb - 1)
    w_map = lambda i, be, nu, nv: (be[blk(i)], 0, 0)
    grid_spec = pltpu.PrefetchScalarGridSpec(
        num_scalar_prefetch=3,
        grid=(nb + EX_DRAIN_STEPS,),
        in_specs=[
            idx_blk(lambda i, be, nu, nv: (blk(i), 0, 0)),
            idx_blk(lambda i, be, nu, nv: (blk(i + 1), 0, 0)),
            idx_blk(lambda i, be, nu, nv: (blk(i), 0, 0)),
            pl.BlockSpec(memory_space=pl.ANY),
            pl.BlockSpec((1, D_MODEL, 2 * D_EXPERT), w_map),
            pl.BlockSpec((1, 1, 2 * D_EXPERT), w_map),
            pl.BlockSpec((1, D_EXPERT, D_MODEL), w_map),
            pl.BlockSpec((1, 1, D_MODEL), w_map),
        ],
        out_specs=pl.BlockSpec(memory_space=pl.ANY),
        scratch_shapes=[
            pltpu.VMEM((2, EX_BM * TOK_ROWS, LANES), F32),
            pltpu.VMEM((2, EX_BM * TOK_ROWS, LANES), F32),
            pltpu.VMEM((D_MODEL, 2 * D_EXPERT), BF16),
            pltpu.VMEM((D_EXPERT, D_MODEL), BF16),
            pltpu.SemaphoreType.DMA((2,)),
            pltpu.SemaphoreType.DMA((2,)),
        ],
    )
    return pl.pallas_call(
        _experts_kernel,
        grid_spec=grid_spec,
        out_shape=jax.ShapeDtypeStruct((n_slots * TOK_ROWS, LANES), F32),
        compiler_params=pltpu.CompilerParams(
            dimension_semantics=("arbitrary",), vmem_limit_bytes=_vmem_limit(vmem)),
        name="experts",
    )(block_e, n_used, n_valid, buf_tok3, buf_tok3, dst3, h2, w_gu, b_gu[:, None, :], w_down,
      b_down[:, None, :])


CB_TM = 256


def _combine_kernel(x1_ref, y0_ref, y1_ref, y2_ref, y3_ref, gate_ref, o_ref):
    tm = x1_ref.shape[0]
    g = jnp.concatenate([gate_ref[...], jnp.zeros((LANES - SUBLANES, tm), F32)], axis=0).T
    acc = x1_ref[...]
    for k, y_ref in enumerate((y0_ref, y1_ref, y2_ref, y3_ref)):
        acc = acc + g[:, k:k + 1] * _tiles_to_rows(y_ref, tm)
    o_ref[...] = acc


def _combine(x1, y_slots, gates):
    n = x1.shape[0]
    tm = CB_TM
    nt = n // tm
    vmem = 2 * (2 * tm * D_MODEL * 4 + tm * TOP_K * D_MODEL * 4) + 4 * tm * D_MODEL * 4
    y_spec = lambda k: pl.BlockSpec((tm * TOK_ROWS, LANES), lambda i: (k * nt + i, 0))
    return pl.pallas_call(
        _combine_kernel,
        grid=(nt,),
        in_specs=[pl.BlockSpec((tm, D_MODEL), lambda i: (i, 0))]
        + [y_spec(k) for k in range(TOP_K)]
        + [pl.BlockSpec((SUBLANES, tm), lambda i: (0, i))],
        out_specs=pl.BlockSpec((tm, D_MODEL), lambda i: (i, 0)),
        out_shape=jax.ShapeDtypeStruct((n, D_MODEL), F32),
        compiler_params=pltpu.CompilerParams(
            dimension_semantics=("arbitrary",), vmem_limit_bytes=_vmem_limit(vmem)),
        name="combine",
    )(x1, y_slots, y_slots, y_slots, y_slots, gates)


def _dispatch_plan(eidx, rank, counts):
    n = eidx.shape[1]
    n_slots = n * TOP_K
    nb = n_slots // EX_BM + N_EXPERTS
    counts = counts.astype(jnp.int32)
    padded = ((counts + EX_BM - 1) // EX_BM) * EX_BM
    pad_end = jnp.cumsum(padded)
    pad_start = pad_end - padded
    e = eidx[:TOP_K]
    start_of = jnp.sum(jnp.where(e[:, :, None] == jnp.arange(N_EXPERTS, dtype=jnp.int32),
                                 pad_start[None, None, :], 0), axis=-1)
    pos = start_of + rank[:TOP_K]
    slot_id = jnp.arange(n_slots, dtype=jnp.int32)
    inv = jnp.full((nb * EX_BM,), -1, jnp.int32).at[pos.reshape(-1)].set(slot_id, unique_indices=True)
    buf_tok = jnp.where(inv >= 0, inv % n, 0) * TOK_ROWS
    dst = jnp.maximum(inv, 0) * TOK_ROWS
    blk_start = jnp.arange(nb, dtype=jnp.int32) * EX_BM
    block_e = jnp.minimum(jnp.sum((pad_end[None, :] <= blk_start[:, None]).astype(jnp.int32), axis=1),
                          N_EXPERTS - 1)
    n_used = (pad_end[-1] // EX_BM).astype(jnp.int32).reshape(1)
    n_valid = jnp.clip(pad_start[block_e] + counts[block_e] - blk_start, 0, EX_BM).astype(jnp.int32)
    n_valid = jnp.where(blk_start < pad_end[-1], n_valid, 0)
    return (block_e, n_used, n_valid, buf_tok.reshape(nb, 1, EX_BM), dst.reshape(nb, 1, EX_BM), n_slots)


def _prep_in_proj_weights(w_in, fox_f_bias, mlstm_i_bias, mlstm_f_bias):
    split_at = []
    acc = 0
    for wdt in SPLIT_WIDTHS[:-1]:
        acc += wdt
        split_at.append(acc)
    fq, fk, fv, ff, mq, mk, mv, mi, mf, mo = jnp.split(w_in, split_at, axis=-1)
    w_main = jnp.concatenate([fq, fk, fv, mq, mk, mv, mo], axis=-1).astype(BF16)
    w_gate = jnp.concatenate([ff, mi, mf], axis=-1)
    bias = jnp.concatenate([fox_f_bias, mlstm_i_bias, mlstm_f_bias]).astype(F32)
    n_gate = w_gate.shape[1]
    wg = jnp.pad(w_gate, ((0, 0), (0, LANES - n_gate)))
    wg_hi, wg_lo = _split_hi_lo(wg)
    wgt = jnp.pad(w_gate.T, ((0, GATE_ROWS - n_gate), (0, 0)))
    wgt_hi, wgt_lo = _split_hi_lo(wgt)
    bias_r = jnp.pad(bias, (0, LANES - n_gate))[None, :]
    bias_t = jnp.pad(bias, (0, GATE_ROWS - n_gate))[:, None]
    return w_main, wgt_hi, wgt_lo, bias_t, wg_hi, wg_lo, bias_r


def kernel(x, attn_norm_w, w_in, fox_f_bias, fox_q_norm_w, fox_k_norm_w, fox_out_norm_w, mlstm_conv_w, mlstm_i_bias, mlstm_f_bias, mlstm_out_norm_w, w_out, moe_norm_w, router_w, router_b, expert_w_gate_up, expert_b_gate_up, expert_w_down, expert_b_down):
    bsz, seq, d = x.shape
    x2d = x.reshape(bsz * seq, d)
    prep = _prep_in_proj_weights(w_in[0], fox_f_bias[0], mlstm_i_bias[0], mlstm_f_bias[0])
    proj, gt, gtm = _in_proj(x2d, attn_norm_w[0][None, :], *prep)
    proj3 = proj.reshape(bsz, seq, MAIN_WIDTH)
    fox_y = _fox(proj3, gt, gtm, fox_q_norm_w[0][None, :], fox_k_norm_w[0][None, :],
                 fox_out_norm_w[0][None, :])
    mlstm_y = _mlstm(proj3, gt, gtm, mlstm_conv_w[0])
    return _channel_mixer(x2d, fox_y.reshape(-1, FOX_WIDTH), mlstm_y.reshape(-1, MLSTM_V_WIDTH), proj,
                          mlstm_out_norm_w[0], w_out[0], moe_norm_w[0], router_w[0], router_b[0],
                          expert_w_gate_up[0], expert_b_gate_up[0], expert_w_down[0],
                          expert_b_down[0]).reshape(bsz, seq, d)


def _channel_mixer(x2d, fox_y2d, mlstm_y2d, proj, mlstm_gain, w_out, moe_norm_w, router_w, router_b,
                   w_gu, b_gu, w_down, b_down):
    wr_hi, wr_lo = _split_hi_lo(router_w.T)
    x1, h2, eidx, gates, rank, counts = _out_route(
        x2d, fox_y2d, mlstm_y2d, proj, w_out.astype(BF16), mlstm_gain[None, :], moe_norm_w[None, :],
        wr_hi, wr_lo, router_b[:, None])
    block_e, n_used, n_valid, buf_tok3, dst3, n_slots = _dispatch_plan(eidx, rank, counts[:, 0])
    y_slots = _experts(block_e, n_used, n_valid, buf_tok3, dst3, h2, w_gu, b_gu, w_down, b_down, n_slots)
    return _combine(x1, y_slots, gates)
```

```python
import functools
import math

import jax
import jax.numpy as jnp
from jax import lax
from jax.experimental import pallas as pl
from jax.experimental.pallas import tpu as pltpu

F32 = jnp.float32
BF16 = jnp.bfloat16

D_MODEL = 1024
SEQ = 2048
HEAD_DIM = 64
FOX_HEADS = 8
FOX_WIDTH = FOX_HEADS * HEAD_DIM
MLSTM_HEADS = 8
MLSTM_QK_DIM = 32
MLSTM_V_DIM = 64
MLSTM_QK_WIDTH = MLSTM_HEADS * MLSTM_QK_DIM
MLSTM_V_WIDTH = MLSTM_HEADS * MLSTM_V_DIM
CONV_WIDTH = 4
MLSTM_CHUNK = 64
SPLIT_WIDTHS = (FOX_WIDTH, FOX_WIDTH, FOX_WIDTH, FOX_HEADS,
                MLSTM_QK_WIDTH, MLSTM_QK_WIDTH, MLSTM_V_WIDTH,
                MLSTM_HEADS, MLSTM_HEADS, MLSTM_V_WIDTH)
N_EXPERTS = 32
TOP_K = 4
D_EXPERT = D_MODEL
SWIGLU_ALPHA = 1.702
SWIGLU_LIMIT = 7.0
NORM_EPS = 1e-5

LANES = 128
SUBLANES = 8
V7X_VMEM_BYTES = 64 * 1024 * 1024

MAIN_WIDTH = 3 * FOX_WIDTH + 2 * MLSTM_QK_WIDTH + 2 * MLSTM_V_WIDTH
GATE_ROWS = 32

NT_DIMS = (((1,), (1,)), ((), ()))


def _vmem_limit(nbytes):
    return int(min(nbytes + (8 << 20), V7X_VMEM_BYTES - (4 << 20)))


def _log_sigmoid(x):
    return jnp.minimum(x, 0.0) - jnp.log(1.0 + jnp.exp(-jnp.abs(x)))


def _split_hi_lo(x):
    hi = x.astype(BF16)
    lo = (x - hi.astype(F32)).astype(BF16)
    return hi, lo


TOK_ROWS = D_MODEL // LANES


def _rows_to_tiles(x, tile_ref):
    m = x.shape[0]
    for j in range(TOK_ROWS):
        tile_ref[pl.ds(j, m, stride=TOK_ROWS), :] = x[:, j * LANES:(j + 1) * LANES]


def _tiles_to_rows(tile_ref, m):
    return jnp.concatenate(
        [tile_ref[pl.ds(j, m, stride=TOK_ROWS), :] for j in range(TOK_ROWS)], axis=1)


IN_TM = 512


def _in_proj_kernel(x_ref, nw_ref, w_ref, wgt_hi_ref, wgt_lo_ref, bt_ref,
                    wg_hi_ref, wg_lo_ref, br_ref, proj_ref, gt_ref, gtm_ref):
    x = x_ref[...]
    ms = jnp.mean(x * x, axis=-1, keepdims=True)
    y = x * lax.rsqrt(ms + NORM_EPS) * nw_ref[...]
    h_hi, h_lo = _split_hi_lo(y)
    proj_ref[...] = jnp.dot(h_hi, w_ref[...], preferred_element_type=F32)

    wt_hi, wt_lo = wgt_hi_ref[...], wgt_lo_ref[...]
    gt = (lax.dot_general(wt_hi, h_hi, NT_DIMS, preferred_element_type=F32)
          + lax.dot_general(wt_lo, h_hi, NT_DIMS, preferred_element_type=F32)
          + lax.dot_general(wt_hi, h_lo, NT_DIMS, preferred_element_type=F32))
    gt = gt + bt_ref[...]
    row = lax.broadcasted_iota(jnp.int32, gt.shape, 0)
    is_input_gate = (row >= FOX_HEADS) & (row < FOX_HEADS + MLSTM_HEADS)
    gt_ref[...] = jnp.where(is_input_gate, gt, _log_sigmoid(gt))

    w_hi, w_lo = wg_hi_ref[...], wg_lo_ref[...]
    g = (jnp.dot(h_hi, w_hi, preferred_element_type=F32)
         + jnp.dot(h_hi, w_lo, preferred_element_type=F32)
         + jnp.dot(h_lo, w_hi, preferred_element_type=F32))
    g = g + br_ref[...]
    lane = lax.broadcasted_iota(jnp.int32, g.shape, 1)
    is_input_gate = (lane >= FOX_HEADS) & (lane < FOX_HEADS + MLSTM_HEADS)
    gtm_ref[...] = jnp.where(is_input_gate, g, _log_sigmoid(g))


def _in_proj(x2d, norm_w, w_main, wgt_hi, wgt_lo, bias_t, wg_hi, wg_lo, bias_r):
    n = x2d.shape[0]
    tm = IN_TM
    const = lambda i: (0, 0)
    vmem = 2 * (tm * D_MODEL * 4 + D_MODEL * MAIN_WIDTH * 2 + tm * MAIN_WIDTH * 4
                + GATE_ROWS * tm * 4 + tm * LANES * 4) + tm * MAIN_WIDTH * 4
    return pl.pallas_call(
        _in_proj_kernel,
        grid=(n // tm,),
        in_specs=[
            pl.BlockSpec((tm, D_MODEL), lambda i: (i, 0)),
            pl.BlockSpec((1, D_MODEL), const),
            pl.BlockSpec((D_MODEL, MAIN_WIDTH), const),
            pl.BlockSpec((GATE_ROWS, D_MODEL), const),
            pl.BlockSpec((GATE_ROWS, D_MODEL), const),
            pl.BlockSpec((GATE_ROWS, 1), const),
            pl.BlockSpec((D_MODEL, LANES), const),
            pl.BlockSpec((D_MODEL, LANES), const),
            pl.BlockSpec((1, LANES), const),
        ],
        out_specs=[
            pl.BlockSpec((tm, MAIN_WIDTH), lambda i: (i, 0)),
            pl.BlockSpec((GATE_ROWS, tm), lambda i: (0, i)),
            pl.BlockSpec((tm, LANES), lambda i: (i, 0)),
        ],
        out_shape=[
            jax.ShapeDtypeStruct((n, MAIN_WIDTH), F32),
            jax.ShapeDtypeStruct((GATE_ROWS, n), F32),
            jax.ShapeDtypeStruct((n, LANES), F32),
        ],
        compiler_params=pltpu.CompilerParams(
            dimension_semantics=("arbitrary",), vmem_limit_bytes=_vmem_limit(vmem)),
        name="in_proj",
    )(x2d, norm_w, w_main, wgt_hi, wgt_lo, bias_t, wg_hi, wg_lo, bias_r)


FOX_TQ = 256
SCAN_SHIFTS = tuple(1 << i for i in range(int(math.log2(SEQ))))


def _rms(x, w):
    return x * lax.rsqrt(jnp.mean(x * x, axis=-1, keepdims=True) + NORM_EPS) * w


LOG2E = 1.4426950408889634


def _pair_rms(x, w, lo_half):
    sq = x * x
    ms_lo = jnp.sum(jnp.where(lo_half, sq, 0.0), axis=-1, keepdims=True) * (1.0 / HEAD_DIM)
    ms_hi = jnp.sum(jnp.where(lo_half, 0.0, sq), axis=-1, keepdims=True) * (1.0 / HEAD_DIM)
    inv = jnp.where(lo_half, lax.rsqrt(ms_lo + NORM_EPS), lax.rsqrt(ms_hi + NORM_EPS))
    return x * inv * w


def _fox_kernel(q_ref, k_ref, v_ref, gtm_ref, qw_ref, kw_ref, ow_ref, o_ref, cum_ref):
    hp = pl.program_id(1)

    @pl.when(hp == 0)
    def _():
        y = gtm_ref[...]
        rowi = lax.broadcasted_iota(jnp.int32, y.shape, 0)
        for s in SCAN_SHIFTS:
            y = y + jnp.where(rowi >= s, pltpu.roll(y, s, axis=0), 0.0)
        cum_ref[...] = y * LOG2E

    lane = lax.broadcasted_iota(jnp.int32, (SEQ, LANES), 1)
    lo_half = lane < HEAD_DIM
    kn_all = _pair_rms(k_ref[0], kw_ref[...], lo_half)
    qn_all = _pair_rms(q_ref[0], qw_ref[...], lo_half) * ((HEAD_DIM ** -0.5) * LOG2E)
    vb = v_ref[0].astype(BF16)
    tri = (lax.broadcasted_iota(jnp.int32, (FOX_TQ, FOX_TQ), 1)
           <= lax.broadcasted_iota(jnp.int32, (FOX_TQ, FOX_TQ), 0))
    lo_q = lax.broadcasted_iota(jnp.int32, (FOX_TQ, LANES), 1) < HEAD_DIM

    for j in range(2):
        h = 2 * hp + j
        cs = slice(j * HEAD_DIM, (j + 1) * HEAD_DIM)
        own = lo_half if j == 0 else jnp.logical_not(lo_half)
        own_q = lo_q if j == 0 else jnp.logical_not(lo_q)
        base = HEAD_DIM if j == 0 else 0
        cum = jnp.sum(jnp.where(lane == h, cum_ref[...], 0.0), axis=1, keepdims=True)
        c_hi = cum.astype(BF16).astype(F32)
        r1 = cum - c_hi
        c_mid = r1.astype(BF16).astype(F32)
        c_lo = (r1 - c_mid).astype(BF16).astype(F32)
        q_aug = jnp.zeros((SEQ, LANES), F32)
        k_aug = jnp.zeros((SEQ, LANES), F32)
        for t, piece in enumerate((c_hi, c_mid, c_lo)):
            q_aug = jnp.where(lane == base + t, piece, q_aug)
            k_aug = jnp.where(lane == base + 3 + t, -piece, k_aug)
        q_aug = jnp.where((lane >= base + 3) & (lane < base + 6), 1.0, q_aug)
        k_aug = jnp.where((lane >= base) & (lane < base + 3), 1.0, k_aug)
        qa = jnp.where(own, qn_all, q_aug).astype(BF16)
        ka = jnp.where(own, kn_all, k_aug).astype(BF16)

        for i in range(SEQ // FOX_TQ):
            qs = slice(i * FOX_TQ, (i + 1) * FOX_TQ)
            n = (i + 1) * FOX_TQ
            s = lax.dot_general(qa[qs], ka[:n], NT_DIMS, preferred_element_type=F32)
            diag = jnp.where(tri, s[:, n - FOX_TQ:], -jnp.inf)
            m = jnp.max(diag, axis=-1, keepdims=True)
            if i > 0:
                past = s[:, :n - FOX_TQ]
                m = jnp.maximum(m, jnp.max(past, axis=-1, keepdims=True))
                p = jnp.concatenate([jnp.exp2(past - m), jnp.exp2(diag - m)], axis=1)
            else:
                p = jnp.exp2(diag - m)
            l = jnp.sum(p, axis=-1, keepdims=True)
            o = jnp.dot(p.astype(BF16), vb[:n], preferred_element_type=F32) / l
            ms = jnp.sum(jnp.where(own_q, o * o, 0.0), axis=-1, keepdims=True) * (1.0 / HEAD_DIM)
            on = o * lax.rsqrt(ms + NORM_EPS) * ow_ref[...]
            o_ref[0, qs, cs] = on[:, cs]


def _fox(proj3, gtm, qw, kw, ow):
    b = proj3.shape[0]
    nq = FOX_WIDTH // LANES
    blk = (1, SEQ, LANES)
    vmem = 2 * 5 * SEQ * LANES * 4 + 8 * SEQ * LANES * 4 + 6 * FOX_TQ * SEQ * 4
    return pl.pallas_call(
        _fox_kernel,
        grid=(b, nq),
        in_specs=[
            pl.BlockSpec(blk, lambda bi, hp: (bi, 0, hp)),
            pl.BlockSpec(blk, lambda bi, hp: (bi, 0, nq + hp)),
            pl.BlockSpec(blk, lambda bi, hp: (bi, 0, 2 * nq + hp)),
            pl.BlockSpec((SEQ, LANES), lambda bi, hp: (bi, 0)),
            pl.BlockSpec((1, LANES), lambda bi, hp: (0, 0)),
            pl.BlockSpec((1, LANES), lambda bi, hp: (0, 0)),
            pl.BlockSpec((1, LANES), lambda bi, hp: (0, hp)),
        ],
        out_specs=pl.BlockSpec(blk, lambda bi, hp: (bi, 0, hp)),
        out_shape=jax.ShapeDtypeStruct((b, SEQ, FOX_WIDTH), F32),
        scratch_shapes=[pltpu.VMEM((SEQ, LANES), F32)],
        compiler_params=pltpu.CompilerParams(
            dimension_semantics=("arbitrary", "arbitrary"), vmem_limit_bytes=_vmem_limit(vmem)),
        name="fox",
    )(proj3, proj3, proj3, gtm, qw, kw, ow)


ML_L = MLSTM_CHUNK
ML_PAIRS = SEQ // (2 * ML_L)
ML_HL = FOX_HEADS
ML_AUG = MLSTM_V_WIDTH + LANES
ML_TILE = 256
SEG_SHIFTS = tuple(1 << i for i in range(int(math.log2(ML_L))))


def _split3(x):
    a = x.astype(BF16)
    r = x - a.astype(F32)
    b = r.astype(BF16)
    c = (r - b.astype(F32)).astype(BF16)
    return a, b, c


def _expand_heads(x, exp_bf):
    a, b, c = _split3(x)
    return (jnp.dot(a, exp_bf, preferred_element_type=F32)
            + jnp.dot(b, exp_bf, preferred_element_type=F32)
            + jnp.dot(c, exp_bf, preferred_element_type=F32))


def _seg_scan(x, axis, op, ident):
    idx = lax.broadcasted_iota(jnp.int32, x.shape, axis) % ML_L
    for s in SEG_SHIFTS:
        x = op(x, jnp.where(idx >= s, pltpu.roll(x, s, axis=axis), ident))
    return x


def _mlstm_kernel(q_ref, k_ref, v_ref, gi_ref, gf_ref, gtm_ref, cw_ref, o_ref, den_ref,
                  qc_ref, kc_ref, kt_ref, rr_ref, cmr_ref, bcr_ref,
                  ealpha_ref, ew_ref, wint_ref, floor_ref, mfull_ref, caug_ref):
    def conv_silu(u, w):
        rowi = lax.broadcasted_iota(jnp.int32, u.shape, 0)
        acc = u * w[CONV_WIDTH - 1:CONV_WIDTH, :]
        for d in range(1, CONV_WIDTH):
            sh = jnp.where(rowi >= d, pltpu.roll(u, d, axis=0), 0.0)
            acc = acc + sh * w[CONV_WIDTH - 1 - d:CONV_WIDTH - d, :]
        return acc / (1.0 + jnp.exp(-acc))

    cw = cw_ref[...]
    qc_ref[...] = conv_silu(q_ref[0], cw[:, :MLSTM_QK_WIDTH]).astype(BF16)
    kc = conv_silu(k_ref[0], cw[:, MLSTM_QK_WIDTH:]) * (MLSTM_QK_DIM ** -0.5)
    kc_ref[...] = kc.astype(BF16)
    kt = kc.T
    for p in range(ML_PAIRS):
        kt_ref[p] = kt[:, p * LANES:(p + 1) * LANES]

    bcum_r = _seg_scan(gf_ref[...], 1, jnp.add, 0.0)
    r_r = gi_ref[...] - bcum_r
    cmx_r = _seg_scan(r_r, 1, jnp.maximum, -jnp.inf)
    for p in range(ML_PAIRS):
        ls = slice(p * LANES, (p + 1) * LANES)
        rr_ref[p] = r_r[:, ls]
        cmr_ref[p] = cmx_r[:, ls]
        bcr_ref[p] = bcum_r[:, ls]

    g = gtm_ref[...]
    lane_g = lax.broadcasted_iota(jnp.int32, g.shape, 1)
    head_lane = (lane_g >= ML_HL) & (lane_g < ML_HL + MLSTM_HEADS)
    bcum_c = jnp.where(
        head_lane, pltpu.roll(_seg_scan(g, 0, jnp.add, 0.0), LANES - MLSTM_HEADS, axis=1), 0.0)
    cmx_c = _seg_scan(jnp.where(head_lane, g, 0.0) - bcum_c, 0, jnp.maximum, -jnp.inf)
    m = jnp.zeros((1, LANES), F32)
    for c in range(SEQ // ML_L):
        mfull_ref[c * ML_L:(c + 1) * ML_L, :] = jnp.broadcast_to(m, (ML_L, LANES))
        last = (c + 1) * ML_L - 1
        m = bcum_c[last:last + 1, :] + jnp.maximum(m, cmx_c[last:last + 1, :])
    mfull = mfull_ref[...]
    mx = jnp.maximum(mfull, cmx_c)
    wint_ref[...] = jnp.exp(mfull - mx)
    floor_ref[...] = jnp.exp(-(bcum_c + mx))
    mfull_ref[...] = -mx

    lane_e = lax.broadcasted_iota(jnp.int32, (LANES, MLSTM_V_WIDTH), 1) // MLSTM_V_DIM
    row_e = lax.broadcasted_iota(jnp.int32, (LANES, MLSTM_V_WIDTH), 0)
    exp_bf = jnp.where(row_e == lane_e + ML_HL, 1.0, 0.0).astype(BF16)

    def expand_tile(i, carry):
        rows = pl.ds(pl.multiple_of(i * ML_TILE, ML_TILE), ML_TILE)
        ealpha_ref[rows, :] = _expand_heads(mfull_ref[rows, :], exp_bf)
        ew_ref[rows, :] = _expand_heads(wint_ref[rows, :], exp_bf)
        return carry

    lax.fori_loop(0, SEQ // ML_TILE, expand_tile, 0)

    kb_rowh = lax.broadcasted_iota(jnp.int32, (MLSTM_HEADS * ML_L, MLSTM_QK_WIDTH), 0) // ML_L
    kb_lane = lax.broadcasted_iota(jnp.int32, (MLSTM_HEADS * ML_L, MLSTM_QK_WIDTH), 1) // MLSTM_QK_DIM
    mask_k = jnp.where(kb_rowh == kb_lane, 1.0, 0.0).astype(BF16)
    va_rowh = lax.broadcasted_iota(jnp.int32, (MLSTM_HEADS * ML_L, ML_AUG), 0) // ML_L
    va_col = lax.broadcasted_iota(jnp.int32, (MLSTM_HEADS * ML_L, ML_AUG), 1)
    mask_v = jnp.where(
        (va_col // MLSTM_V_DIM == va_rowh) | (va_col == MLSTM_V_WIDTH + ML_HL + va_rowh),
        1.0, 0.0).astype(BF16)
    c_rowh = lax.broadcasted_iota(jnp.int32, (MLSTM_QK_WIDTH, ML_AUG), 0) // MLSTM_QK_DIM
    c_col = lax.broadcasted_iota(jnp.int32, (MLSTM_QK_WIDTH, ML_AUG), 1)
    mask_c = (c_col // MLSTM_V_DIM == c_rowh) | (c_col == MLSTM_V_WIDTH + ML_HL + c_rowh)
    lane128 = lax.broadcasted_iota(jnp.int32, (ML_L, LANES), 1)
    s_idx = lax.broadcasted_iota(jnp.int32, (ML_L, MLSTM_V_WIDTH), 1) % ML_L
    t_idx = lax.broadcasted_iota(jnp.int32, (ML_L, MLSTM_V_WIDTH), 0)
    causal = s_idx <= t_idx
    ones_aug = jnp.ones((ML_L, LANES), F32)

    caug_ref[...] = jnp.zeros_like(caug_ref)

    def pair_body(cp, m_row):
        r2 = rr_ref[cp]
        cm2 = cmr_ref[cp]
        b2 = bcr_ref[cp]
        r2r = pltpu.roll(r2, ML_L, axis=1)
        kt2 = kt_ref[cp]
        for cc in range(2):
            lo = cc * ML_L
            rows = pl.ds(pl.multiple_of(cp * (2 * ML_L), 2 * ML_L) + lo, ML_L)
            qa = qc_ref[rows, :]
            ka = kc_ref[rows, :]
            va_aug = jnp.concatenate([v_ref[0, rows, :], ones_aug], axis=1).astype(BF16)

            kbd = jnp.concatenate([ka] * MLSTM_HEADS, axis=0) * mask_k
            s = lax.dot_general(qa, kbd, NT_DIMS, preferred_element_type=F32)

            cmx_last = cm2[:, lo + ML_L - 1:lo + ML_L]
            b_last = b2[:, lo + ML_L - 1:lo + ML_L]
            mx_r = jnp.maximum(m_row, cmx_last)
            decay = jnp.exp(m_row - mx_r)
            wk = jnp.exp(r2[:, lo:lo + ML_L] - mx_r)
            m_row = b_last + mx_r

            src_e, src_o = (r2, r2r) if cc == 0 else (r2r, r2)
            cols = []
            for p in range(MLSTM_HEADS // 2):
                even = jnp.broadcast_to(src_e[2 * p:2 * p + 1, :], (ML_L, LANES))
                odd = jnp.broadcast_to(src_o[2 * p + 1:2 * p + 2, :], (ML_L, LANES))
                cols.append(jnp.where(lane128 < ML_L, even, odd))
            r_all = jnp.concatenate(cols, axis=1)
            arg = jnp.where(causal, ealpha_ref[rows, :] + r_all, -jnp.inf)
            p_all = (s * jnp.exp(arg)).astype(BF16)

            vbd = jnp.concatenate([va_aug] * MLSTM_HEADS, axis=0) * mask_v
            pv = jnp.dot(p_all, vbd, preferred_element_type=F32)
            qc_state = jnp.dot(qa, caug_ref[...].astype(BF16), preferred_element_type=F32)
            o_ref[0, rows, :] = (ew_ref[rows, :] * qc_state[:, :MLSTM_V_WIDTH]
                                 + pv[:, :MLSTM_V_WIDTH])
            den_ref[rows, :] = (wint_ref[rows, :] * qc_state[:, MLSTM_V_WIDTH:]
                                + pv[:, MLSTM_V_WIDTH:])

            wk_rows = jnp.concatenate(
                [jnp.broadcast_to(wk[h:h + 1, :], (MLSTM_QK_DIM, ML_L)) for h in range(MLSTM_HEADS)],
                axis=0)
            dec_rows = jnp.concatenate(
                [jnp.broadcast_to(decay[h:h + 1, :], (MLSTM_QK_DIM, 1)) for h in range(MLSTM_HEADS)],
                axis=0)
            ktw = (kt2[:, lo:lo + ML_L] * wk_rows).astype(BF16)
            upd = jnp.dot(ktw, va_aug, preferred_element_type=F32)
            caug_ref[...] = dec_rows * caug_ref[...] + jnp.where(mask_c, upd, 0.0)
        return m_row

    lax.fori_loop(0, ML_PAIRS, pair_body, jnp.zeros((MLSTM_HEADS, 1), F32))

    ob_row = lax.broadcasted_iota(jnp.int32, (MLSTM_V_WIDTH, LANES), 0) // MLSTM_V_DIM
    ob_col = lax.broadcasted_iota(jnp.int32, (MLSTM_V_WIDTH, LANES), 1)
    ones_bd = jnp.where(ob_col == ob_row + ML_HL, 1.0, 0.0).astype(BF16)

    def norm_tile(i, carry):
        rows = pl.ds(pl.multiple_of(i * ML_TILE, ML_TILE), ML_TILE)
        num = o_ref[0, rows, :]
        dn = jnp.maximum(jnp.abs(den_ref[rows, :]), floor_ref[rows, :])
        r = 1.0 / dn
        n2_hi, n2_lo = _split_hi_lo(num * num)
        msn = (jnp.dot(n2_hi, ones_bd, preferred_element_type=F32)
               + jnp.dot(n2_lo, ones_bd, preferred_element_type=F32)) * (1.0 / MLSTM_V_DIM)
        fac = r * lax.rsqrt(r * r * msn + NORM_EPS)
        o_ref[0, rows, :] = num * _expand_heads(fac, exp_bf)
        return carry

    lax.fori_loop(0, SEQ // ML_TILE, norm_tile, 0)


def _mlstm(proj3, gt, gtm, conv_w):
    b = proj3.shape[0]
    qk_blk = (1, SEQ, MLSTM_QK_WIDTH)
    v_blk = (1, SEQ, MLSTM_V_WIDTH)
    q_col = 3 * FOX_WIDTH // MLSTM_QK_WIDTH
    v_col = (3 * FOX_WIDTH + 2 * MLSTM_QK_WIDTH) // MLSTM_V_WIDTH
    vmem = (2 * (2 * SEQ * MLSTM_QK_WIDTH * 4 + 2 * SEQ * MLSTM_V_WIDTH * 4 + 2 * SEQ * LANES * 4)
            + 2 * SEQ * MLSTM_QK_WIDTH * 2 + SEQ * MLSTM_QK_WIDTH * 4 + 2 * SEQ * MLSTM_V_WIDTH * 4
            + 3 * SEQ * LANES * 4 + (8 << 20))
    return pl.pallas_call(
        _mlstm_kernel,
        grid=(b,),
        in_specs=[
            pl.BlockSpec(qk_blk, lambda bi: (bi, 0, q_col)),
            pl.BlockSpec(qk_blk, lambda bi: (bi, 0, q_col + 1)),
            pl.BlockSpec(v_blk, lambda bi: (bi, 0, v_col)),
            pl.BlockSpec((SUBLANES, SEQ), lambda bi: (1, bi)),
            pl.BlockSpec((SUBLANES, SEQ), lambda bi: (2, bi)),
            pl.BlockSpec((SEQ, LANES), lambda bi: (bi, 0)),
            pl.BlockSpec((CONV_WIDTH, 2 * MLSTM_QK_WIDTH), lambda bi: (0, 0)),
        ],
        out_specs=pl.BlockSpec(v_blk, lambda bi: (bi, 0, 0)),
        out_shape=jax.ShapeDtypeStruct((b, SEQ, MLSTM_V_WIDTH), F32),
        scratch_shapes=[
            pltpu.VMEM((SEQ, LANES), F32),
            pltpu.VMEM((SEQ, MLSTM_QK_WIDTH), BF16),
            pltpu.VMEM((SEQ, MLSTM_QK_WIDTH), BF16),
            pltpu.VMEM((ML_PAIRS, MLSTM_QK_WIDTH, LANES), F32),
            pltpu.VMEM((ML_PAIRS, SUBLANES, LANES), F32),
            pltpu.VMEM((ML_PAIRS, SUBLANES, LANES), F32),
            pltpu.VMEM((ML_PAIRS, SUBLANES, LANES), F32),
            pltpu.VMEM((SEQ, MLSTM_V_WIDTH), F32),
            pltpu.VMEM((SEQ, MLSTM_V_WIDTH), F32),
            pltpu.VMEM((SEQ, LANES), F32),
            pltpu.VMEM((SEQ, LANES), F32),
            pltpu.VMEM((SEQ, LANES), F32),
            pltpu.VMEM((MLSTM_QK_WIDTH, ML_AUG), F32),
        ],
        compiler_params=pltpu.CompilerParams(
            dimension_semantics=("arbitrary",), vmem_limit_bytes=_vmem_limit(vmem)),
        name="mlstm",
    )(proj3, proj3, proj3, gt, gt, gtm, conv_w)


RT_TM = 512


def _out_route_kernel(x_ref, fy_ref, my_ref, mo_ref, wo_ref, mg_ref, nw_ref, wr_hi_ref, wr_lo_ref,
                      rb_ref, x1_ref, h2_ref, eidx_ref, gate_ref, rank_ref, cnt_ref, carry_ref):
    i = pl.program_id(0)

    @pl.when(i == 0)
    def _():
        carry_ref[...] = jnp.zeros_like(carry_ref)

    my = my_ref[...] * mg_ref[...] / (1.0 + jnp.exp(-mo_ref[...]))
    mixed = (jnp.dot(fy_ref[...].astype(BF16), wo_ref[:FOX_WIDTH, :], preferred_element_type=F32)
             + jnp.dot(my.astype(BF16), wo_ref[FOX_WIDTH:, :], preferred_element_type=F32))
    x1 = x_ref[...] + mixed
    x1_ref[...] = x1
    h2 = _rms(x1, nw_ref[...])
    _rows_to_tiles(h2, h2_ref)

    h_hi, h_lo = _split_hi_lo(h2)
    wr_hi, wr_lo = wr_hi_ref[...], wr_lo_ref[...]
    logit = (lax.dot_general(wr_hi, h_hi, NT_DIMS, preferred_element_type=F32)
             + lax.dot_general(wr_lo, h_hi, NT_DIMS, preferred_element_type=F32)
             + lax.dot_general(wr_hi, h_lo, NT_DIMS, preferred_element_type=F32)) + rb_ref[...]

    e_iota = lax.broadcasted_iota(jnp.int32, logit.shape, 0).astype(F32)
    vals, idxs, hots = [], [], []
    for _ in range(TOP_K):
        mk = jnp.max(logit, axis=0, keepdims=True)
        idx = jnp.min(jnp.where(logit == mk, e_iota, float(N_EXPERTS)), axis=0, keepdims=True)
        hot = e_iota == idx
        logit = jnp.where(hot, -jnp.inf, logit)
        vals.append(mk)
        idxs.append(idx.astype(jnp.int32))
        hots.append(hot)
    exps = [jnp.exp(v - vals[0]) for v in vals]
    tot = exps[0] + exps[1] + exps[2] + exps[3]
    gates = [e / tot for e in exps]

    assign = jnp.zeros(logit.shape, F32)
    for hot in hots:
        assign = assign + jnp.where(hot, 1.0, 0.0)
    tm = logit.shape[1]
    src = lax.broadcasted_iota(jnp.int32, (tm, tm), 0)
    dst = lax.broadcasted_iota(jnp.int32, (tm, tm), 1)
    upper = jnp.where(src < dst, 1.0, 0.0).astype(BF16)
    base = jnp.dot(assign.astype(BF16), upper, preferred_element_type=F32) + carry_ref[:, 0:1]
    ranks = [jnp.sum(jnp.where(hot, base, 0.0), axis=0, keepdims=True) for hot in hots]
    new_carry = carry_ref[...] + jnp.sum(assign, axis=1, keepdims=True)
    carry_ref[...] = new_carry
    cnt_ref[...] = new_carry

    zi = jnp.zeros((SUBLANES - TOP_K, tm), jnp.int32)
    eidx_ref[...] = jnp.concatenate(idxs + [zi], axis=0)
    rank_ref[...] = jnp.concatenate([r.astype(jnp.int32) for r in ranks] + [zi], axis=0)
    gate_ref[...] = jnp.concatenate(gates + [zi.astype(F32)], axis=0)


def _out_route(x2d, fox_y2d, mlstm_y2d, proj, w_out_bf, mlstm_gain, moe_norm_w, wr_hi, wr_lo, rb):
    n = x2d.shape[0]
    tm = RT_TM
    const = lambda i: (0, 0)
    mo_col = (MAIN_WIDTH - MLSTM_V_WIDTH) // MLSTM_V_WIDTH
    row_blk = lambda w: pl.BlockSpec((tm, w), lambda i: (i, 0))
    lane_blk = pl.BlockSpec((SUBLANES, tm), lambda i: (0, i))
    vmem = (2 * (tm * D_MODEL * 4 * 3 + tm * FOX_WIDTH * 4 * 3 + D_MODEL * D_MODEL * 2)
            + 6 * tm * D_MODEL * 4 + tm * tm * 6)
    return pl.pallas_call(
        _out_route_kernel,
        grid=(n // tm,),
        in_specs=[
            row_blk(D_MODEL), row_blk(FOX_WIDTH), row_blk(MLSTM_V_WIDTH),
            pl.BlockSpec((tm, MLSTM_V_WIDTH), lambda i: (i, mo_col)),
            pl.BlockSpec((D_MODEL, D_MODEL), const),
            pl.BlockSpec((1, MLSTM_V_WIDTH), const),
            pl.BlockSpec((1, D_MODEL), const),
            pl.BlockSpec((N_EXPERTS, D_MODEL), const),
            pl.BlockSpec((N_EXPERTS, D_MODEL), const),
            pl.BlockSpec((N_EXPERTS, 1), const),
        ],
        out_specs=[row_blk(D_MODEL), pl.BlockSpec((tm * TOK_ROWS, LANES), lambda i: (i, 0)),
                   lane_blk, lane_blk, lane_blk, pl.BlockSpec((N_EXPERTS, LANES), const)],
        out_shape=[
            jax.ShapeDtypeStruct((n, D_MODEL), F32),
            jax.ShapeDtypeStruct((n * TOK_ROWS, LANES), F32),
            jax.ShapeDtypeStruct((SUBLANES, n), jnp.int32),
            jax.ShapeDtypeStruct((SUBLANES, n), F32),
            jax.ShapeDtypeStruct((SUBLANES, n), jnp.int32),
            jax.ShapeDtypeStruct((N_EXPERTS, LANES), F32),
        ],
        scratch_shapes=[pltpu.VMEM((N_EXPERTS, LANES), F32)],
        compiler_params=pltpu.CompilerParams(
            dimension_semantics=("arbitrary",), vmem_limit_bytes=_vmem_limit(vmem)),
        name="out_route",
    )(x2d, fox_y2d, mlstm_y2d, proj, w_out_bf, mlstm_gain, moe_norm_w, wr_hi, wr_lo, rb)


EX_BM = 256
EX_DRAIN_STEPS = 2


def _experts_kernel(be_ref, nu_ref, nv_ref, tok_ref, tokn_ref, dst_ref, h2_hbm, wgu_ref, bgu_ref,
                    wd_ref, bd_ref, y_hbm, xt_ref, yt_ref, wgu_bf_ref, wd_bf_ref, gsem, ssem):
    i = pl.program_id(0)
    last_blk = pl.num_programs(0) - 1 - EX_DRAIN_STEPS
    nu = nu_ref[0]
    slot = i % 2
    cur = jnp.minimum(i, last_blk)
    tile_rows = EX_BM * TOK_ROWS

    def tok_tile(ref, idx):
        return ref.at[pl.ds(pl.multiple_of(idx, TOK_ROWS), TOK_ROWS), :]

    def start_gather(idx_ref, s):
        for r in range(EX_BM):
            pltpu.make_async_copy(tok_tile(h2_hbm, idx_ref[0, 0, r]),
                                  xt_ref.at[s, pl.ds(r * TOK_ROWS, TOK_ROWS), :],
                                  gsem.at[s]).start(priority=r % 2)

    def scatter_row(r):
        pltpu.make_async_copy(yt_ref.at[slot, pl.ds(r * TOK_ROWS, TOK_ROWS), :],
                              tok_tile(y_hbm, dst_ref[0, 0, r]), ssem.at[slot]).start(priority=r % 2)

    def wait_block(sem, buf):
        pltpu.make_async_copy(h2_hbm.at[pl.ds(0, tile_rows), :], buf, sem).wait()

    def wait_scatter(s, n):
        @pl.when(n == EX_BM)
        def _():
            wait_block(ssem.at[s], yt_ref.at[s])

        @pl.when(n < EX_BM)
        def _():
            def body(r, carry):
                pltpu.make_async_copy(h2_hbm.at[pl.ds(0, TOK_ROWS), :],
                                      yt_ref.at[s, pl.ds(0, TOK_ROWS), :], ssem.at[s]).wait()
                return carry
            lax.fori_loop(0, n, body, 0)

    @pl.when(i == 0)
    def _():
        start_gather(tok_ref, 0)

    @pl.when(i + 1 < nu)
    def _():
        start_gather(tokn_ref, 1 - slot)

    @pl.when((i >= 2) & (i - 2 < nu))
    def _():
        wait_scatter(slot, nv_ref[jnp.clip(i - 2, 0, last_blk)])

    @pl.when(i < nu)
    def _():
        @pl.when((i == 0) | (be_ref[cur] != be_ref[jnp.maximum(cur - 1, 0)]))
        def _():
            wgu_bf_ref[...] = wgu_ref[0].astype(BF16)
            wd_bf_ref[...] = wd_ref[0].astype(BF16)

        wait_block(gsem.at[slot], xt_ref.at[slot])
        xb = _tiles_to_rows(xt_ref.at[slot], EX_BM).astype(BF16)
        gu = jnp.dot(xb, wgu_bf_ref[...], preferred_element_type=F32) + bgu_ref[0]
        gate = jnp.minimum(gu[:, :D_EXPERT], SWIGLU_LIMIT)
        up = jnp.clip(gu[:, D_EXPERT:], -SWIGLU_LIMIT, SWIGLU_LIMIT)
        act = (up + 1.0) * (gate / (1.0 + jnp.exp(-SWIGLU_ALPHA * gate)))
        y = jnp.dot(act.astype(BF16), wd_bf_ref[...], preferred_element_type=F32) + bd_ref[0]
        _rows_to_tiles(y, yt_ref.at[slot])
        nv = nv_ref[cur]

        @pl.when(nv == EX_BM)
        def _():
            for r in range(EX_BM):
                scatter_row(r)

        @pl.when(nv < EX_BM)
        def _():
            def body(r, carry):
                pltpu.make_async_copy(
                    yt_ref.at[slot, pl.ds(pl.multiple_of(r * TOK_ROWS, TOK_ROWS), TOK_ROWS), :],
                    tok_tile(y_hbm, dst_ref[0, 0, r]), ssem.at[slot]).start()
                return carry
            lax.fori_loop(0, nv, body, 0)


def _experts(block_e, n_used, n_valid, buf_tok3, dst3, h2, w_gu, b_gu, w_down, b_down, n_slots):
    nb = buf_tok3.shape[0]
    idx_blk = lambda f: pl.BlockSpec((1, 1, EX_BM), f, memory_space=pltpu.SMEM)
    vmem = (2 * (D_MODEL * 2 * D_EXPERT * 4 + D_EXPERT * D_MODEL * 4)
            + D_MODEL * 2 * D_EXPERT * 2 + D_EXPERT * D_MODEL * 2
            + 3 * EX_BM * D_MODEL * 4 + 3 * EX_BM * 2 * D_EXPERT * 4)
    blk = lambda i: jnp.minimum(i, nb - 1)
    w_map = lambda i, be, nu, nv: (be[blk(i)], 0, 0)
    grid_spec = pltpu.PrefetchScalarGridSpec(
        num_scalar_prefetch=3,
        grid=(nb + EX_DRAIN_STEPS,),
        in_specs=[
            idx_blk(lambda i, be, nu, nv: (blk(i), 0, 0)),
            idx_blk(lambda i, be, nu, nv: (blk(i + 1), 0, 0)),
            idx_blk(lambda i, be, nu, nv: (blk(i), 0, 0)),
            pl.BlockSpec(memory_space=pl.ANY),
            pl.BlockSpec((1, D_MODEL, 2 * D_EXPERT), w_map),
            pl.BlockSpec((1, 1, 2 * D_EXPERT), w_map),
            pl.BlockSpec((1, D_EXPERT, D_MODEL), w_map),
            pl.BlockSpec((1, 1, D_MODEL), w_map),
        ],
        out_specs=pl.BlockSpec(memory_space=pl.ANY),
        scratch_shapes=[
            pltpu.VMEM((2, EX_BM * TOK_ROWS, LANES), F32),
            pltpu.VMEM((2, EX_BM * TOK_ROWS, LANES), F32),
            pltpu.VMEM((D_MODEL, 2 * D_EXPERT), BF16),
            pltpu.VMEM((D_EXPERT, D_MODEL), BF16),
            pltpu.SemaphoreType.DMA((2,)),
            pltpu.SemaphoreType.DMA((2,)),
        ],
    )
    return pl.pallas_call(
        _experts_kernel,
        grid_spec=grid_spec,
        out_shape=jax.ShapeDtypeStruct((n_slots * TOK_ROWS, LANES), F32),
        compiler_params=pltpu.CompilerParams(
            dimension_semantics=("arbitrary",), vmem_limit_bytes=_vmem_limit(vmem)),
        name="experts",
    )(block_e, n_used, n_valid, buf_tok3, buf_tok3, dst3, h2, w_gu, b_gu[:, None, :], w_down,
      b_down[:, None, :])


CB_TM = 256


def _combine_kernel(x1_ref, y0_ref, y1_ref, y2_ref, y3_ref, gate_ref, o_ref):
    tm = x1_ref.shape[0]
    g = jnp.concatenate([gate_ref[...], jnp.zeros((LANES - SUBLANES, tm), F32)], axis=0).T
    acc = x1_ref[...]
    for k, y_ref in enumerate((y0_ref, y1_ref, y2_ref, y3_ref)):
        acc = acc + g[:, k:k + 1] * _tiles_to_rows(y_ref, tm)
    o_ref[...] = acc


def _combine(x1, y_slots, gates):
    n = x1.shape[0]
    tm = CB_TM
    nt = n // tm
    vmem = 2 * (2 * tm * D_MODEL * 4 + tm * TOP_K * D_MODEL * 4) + 4 * tm * D_MODEL * 4
    y_spec = lambda k: pl.BlockSpec((tm * TOK_ROWS, LANES), lambda i: (k * nt + i, 0))
    return pl.pallas_call(
        _combine_kernel,
        grid=(nt,),
        in_specs=[pl.BlockSpec((tm, D_MODEL), lambda i: (i, 0))]
        + [y_spec(k) for k in range(TOP_K)]
        + [pl.BlockSpec((SUBLANES, tm), lambda i: (0, i))],
        out_specs=pl.BlockSpec((tm, D_MODEL), lambda i: (i, 0)),
        out_shape=jax.ShapeDtypeStruct((n, D_MODEL), F32),
        compiler_params=pltpu.CompilerParams(
            dimension_semantics=("arbitrary",), vmem_limit_bytes=_vmem_limit(vmem)),
        name="combine",
    )(x1, y_slots, y_slots, y_slots, y_slots, gates)


def _dispatch_plan(eidx, rank, counts):
    n = eidx.shape[1]
    n_slots = n * TOP_K
    nb = n_slots // EX_BM + N_EXPERTS
    counts = counts.astype(jnp.int32)
    padded = ((counts + EX_BM - 1) // EX_BM) * EX_BM
    pad_end = jnp.cumsum(padded)
    pad_start = pad_end - padded
    e = eidx[:TOP_K]
    start_of = jnp.sum(jnp.where(e[:, :, None] == jnp.arange(N_EXPERTS, dtype=jnp.int32),
                                 pad_start[None, None, :], 0), axis=-1)
    pos = start_of + rank[:TOP_K]
    slot_id = jnp.arange(n_slots, dtype=jnp.int32)
    inv = jnp.full((nb * EX_BM,), -1, jnp.int32).at[pos.reshape(-1)].set(slot_id, unique_indices=True)
    buf_tok = jnp.where(inv >= 0, inv % n, 0) * TOK_ROWS
    dst = jnp.maximum(inv, 0) * TOK_ROWS
    blk_start = jnp.arange(nb, dtype=jnp.int32) * EX_BM
    block_e = jnp.minimum(jnp.sum((pad_end[None, :] <= blk_start[:, None]).astype(jnp.int32), axis=1),
                          N_EXPERTS - 1)
    n_used = (pad_end[-1] // EX_BM).astype(jnp.int32).reshape(1)
    n_valid = jnp.clip(pad_start[block_e] + counts[block_e] - blk_start, 0, EX_BM).astype(jnp.int32)
    n_valid = jnp.where(blk_start < pad_end[-1], n_valid, 0)
    return (block_e, n_used, n_valid, buf_tok.reshape(nb, 1, EX_BM), dst.reshape(nb, 1, EX_BM), n_slots)


def _prep_in_proj_weights(w_in, fox_f_bias, mlstm_i_bias, mlstm_f_bias):
    split_at = []
    acc = 0
    for wdt in SPLIT_WIDTHS[:-1]:
        acc += wdt
        split_at.append(acc)
    fq, fk, fv, ff, mq, mk, mv, mi, mf, mo = jnp.split(w_in, split_at, axis=-1)
    w_main = jnp.concatenate([fq, fk, fv, mq, mk, mv, mo], axis=-1).astype(BF16)
    w_gate = jnp.concatenate([ff, mi, mf], axis=-1)
    bias = jnp.concatenate([fox_f_bias, mlstm_i_bias, mlstm_f_bias]).astype(F32)
    n_gate = w_gate.shape[1]
    wg = jnp.pad(w_gate, ((0, 0), (0, LANES - n_gate)))
    wg_hi, wg_lo = _split_hi_lo(wg)
    wgt = jnp.pad(w_gate.T, ((0, GATE_ROWS - n_gate), (0, 0)))
    wgt_hi, wgt_lo = _split_hi_lo(wgt)
    bias_r = jnp.pad(bias, (0, LANES - n_gate))[None, :]
    bias_t = jnp.pad(bias, (0, GATE_ROWS - n_gate))[:, None]
    return w_main, wgt_hi, wgt_lo, bias_t, wg_hi, wg_lo, bias_r


def kernel(x, attn_norm_w, w_in, fox_f_bias, fox_q_norm_w, fox_k_norm_w, fox_out_norm_w, mlstm_conv_w, mlstm_i_bias, mlstm_f_bias, mlstm_out_norm_w, w_out, moe_norm_w, router_w, router_b, expert_w_gate_up, expert_b_gate_up, expert_w_down, expert_b_down):
    bsz, seq, d = x.shape
    x2d = x.reshape(bsz * seq, d)
    prep = _prep_in_proj_weights(w_in[0], fox_f_bias[0], mlstm_i_bias[0], mlstm_f_bias[0])
    proj, gt, gtm = _in_proj(x2d, attn_norm_w[0][None, :], *prep)
    proj3 = proj.reshape(bsz, seq, MAIN_WIDTH)
    pair = lambda w: jnp.tile(w, LANES // HEAD_DIM)[None, :]
    fox_y = _fox(proj3, gtm, pair(fox_q_norm_w[0]), pair(fox_k_norm_w[0]), fox_out_norm_w[0][None, :])
    mlstm_y = _mlstm(proj3, gt, gtm, mlstm_conv_w[0])
    return _channel_mixer(x2d, fox_y.reshape(-1, FOX_WIDTH), mlstm_y.reshape(-1, MLSTM_V_WIDTH), proj,
                          mlstm_out_norm_w[0], w_out[0], moe_norm_w[0], router_w[0], router_b[0],
                          expert_w_gate_up[0], expert_b_gate_up[0], expert_w_down[0],
                          expert_b_down[0]).reshape(bsz, seq, d)


def _channel_mixer(x2d, fox_y2d, mlstm_y2d, proj, mlstm_gain, w_out, moe_norm_w, router_w, router_b,
                   w_gu, b_gu, w_down, b_down):
    wr_hi, wr_lo = _split_hi_lo(router_w.T)
    x1, h2, eidx, gates, rank, counts = _out_route(
        x2d, fox_y2d, mlstm_y2d, proj, w_out.astype(BF16), mlstm_gain[None, :], moe_norm_w[None, :],
        wr_hi, wr_lo, router_b[:, None])
    block_e, n_used, n_valid, buf_tok3, dst3, n_slots = _dispatch_plan(eidx, rank, counts[:, 0])
    y_slots = _experts(block_e, n_used, n_valid, buf_tok3, dst3, h2, w_gu, b_gu, w_down, b_down, n_slots)
    return _combine(x1, y_slots, gates)
```

```python
import functools
import math

import jax
import jax.numpy as jnp
from jax import lax
from jax.experimental import pallas as pl
from jax.experimental.pallas import tpu as pltpu

F32 = jnp.float32
BF16 = jnp.bfloat16

D_MODEL = 1024
SEQ = 2048
HEAD_DIM = 64
FOX_HEADS = 8
FOX_WIDTH = FOX_HEADS * HEAD_DIM
MLSTM_HEADS = 8
MLSTM_QK_DIM = 32
MLSTM_V_DIM = 64
MLSTM_QK_WIDTH = MLSTM_HEADS * MLSTM_QK_DIM
MLSTM_V_WIDTH = MLSTM_HEADS * MLSTM_V_DIM
CONV_WIDTH = 4
MLSTM_CHUNK = 64
SPLIT_WIDTHS = (FOX_WIDTH, FOX_WIDTH, FOX_WIDTH, FOX_HEADS,
                MLSTM_QK_WIDTH, MLSTM_QK_WIDTH, MLSTM_V_WIDTH,
                MLSTM_HEADS, MLSTM_HEADS, MLSTM_V_WIDTH)
N_EXPERTS = 32
TOP_K = 4
D_EXPERT = D_MODEL
SWIGLU_ALPHA = 1.702
SWIGLU_LIMIT = 7.0
NORM_EPS = 1e-5

LANES = 128
SUBLANES = 8
V7X_VMEM_BYTES = 64 * 1024 * 1024

MAIN_WIDTH = 3 * FOX_WIDTH + 2 * MLSTM_QK_WIDTH + 2 * MLSTM_V_WIDTH
GATE_ROWS = 32

NT_DIMS = (((1,), (1,)), ((), ()))


def _vmem_limit(nbytes):
    return int(min(nbytes + (8 << 20), V7X_VMEM_BYTES - (4 << 20)))


def _log_sigmoid(x):
    return jnp.minimum(x, 0.0) - jnp.log(1.0 + jnp.exp(-jnp.abs(x)))


def _split_hi_lo(x):
    hi = x.astype(BF16)
    lo = (x - hi.astype(F32)).astype(BF16)
    return hi, lo


TOK_ROWS = D_MODEL // LANES


def _rows_to_tiles(x, tile_ref):
    m = x.shape[0]
    for j in range(TOK_ROWS):
        tile_ref[pl.ds(j, m, stride=TOK_ROWS), :] = x[:, j * LANES:(j + 1) * LANES]


def _tiles_to_rows(tile_ref, m):
    return jnp.concatenate(
        [tile_ref[pl.ds(j, m, stride=TOK_ROWS), :] for j in range(TOK_ROWS)], axis=1)


IN_TM = 512


def _in_proj_kernel(x_ref, nw_ref, w_ref, wgt_hi_ref, wgt_lo_ref, bt_ref,
                    wg_hi_ref, wg_lo_ref, br_ref, proj_ref, gt_ref, gtm_ref):
    x = x_ref[...]
    ms = jnp.mean(x * x, axis=-1, keepdims=True)
    y = x * lax.rsqrt(ms + NORM_EPS) * nw_ref[...]
    h_hi, h_lo = _split_hi_lo(y)
    proj_ref[...] = jnp.dot(h_hi, w_ref[...], preferred_element_type=F32)

    wt_hi, wt_lo = wgt_hi_ref[...], wgt_lo_ref[...]
    gt = (lax.dot_general(wt_hi, h_hi, NT_DIMS, preferred_element_type=F32)
          + lax.dot_general(wt_lo, h_hi, NT_DIMS, preferred_element_type=F32)
          + lax.dot_general(wt_hi, h_lo, NT_DIMS, preferred_element_type=F32))
    gt = gt + bt_ref[...]
    row = lax.broadcasted_iota(jnp.int32, gt.shape, 0)
    is_input_gate = (row >= FOX_HEADS) & (row < FOX_HEADS + MLSTM_HEADS)
    gt_ref[...] = jnp.where(is_input_gate, gt, _log_sigmoid(gt))

    w_hi, w_lo = wg_hi_ref[...], wg_lo_ref[...]
    g = (jnp.dot(h_hi, w_hi, preferred_element_type=F32)
         + jnp.dot(h_hi, w_lo, preferred_element_type=F32)
         + jnp.dot(h_lo, w_hi, preferred_element_type=F32))
    g = g + br_ref[...]
    lane = lax.broadcasted_iota(jnp.int32, g.shape, 1)
    is_input_gate = (lane >= FOX_HEADS) & (lane < FOX_HEADS + MLSTM_HEADS)
    gtm_ref[...] = jnp.where(is_input_gate, g, _log_sigmoid(g))


def _in_proj(x2d, norm_w, w_main, wgt_hi, wgt_lo, bias_t, wg_hi, wg_lo, bias_r):
    n = x2d.shape[0]
    tm = IN_TM
    const = lambda i: (0, 0)
    vmem = 2 * (tm * D_MODEL * 4 + D_MODEL * MAIN_WIDTH * 2 + tm * MAIN_WIDTH * 4
                + GATE_ROWS * tm * 4 + tm * LANES * 4) + tm * MAIN_WIDTH * 4
    return pl.pallas_call(
        _in_proj_kernel,
        grid=(n // tm,),
        in_specs=[
            pl.BlockSpec((tm, D_MODEL), lambda i: (i, 0)),
            pl.BlockSpec((1, D_MODEL), const),
            pl.BlockSpec((D_MODEL, MAIN_WIDTH), const),
            pl.BlockSpec((GATE_ROWS, D_MODEL), const),
            pl.BlockSpec((GATE_ROWS, D_MODEL), const),
            pl.BlockSpec((GATE_ROWS, 1), const),
            pl.BlockSpec((D_MODEL, LANES), const),
            pl.BlockSpec((D_MODEL, LANES), const),
            pl.BlockSpec((1, LANES), const),
        ],
        out_specs=[
            pl.BlockSpec((tm, MAIN_WIDTH), lambda i: (i, 0)),
            pl.BlockSpec((GATE_ROWS, tm), lambda i: (0, i)),
            pl.BlockSpec((tm, LANES), lambda i: (i, 0)),
        ],
        out_shape=[
            jax.ShapeDtypeStruct((n, MAIN_WIDTH), F32),
            jax.ShapeDtypeStruct((GATE_ROWS, n), F32),
            jax.ShapeDtypeStruct((n, LANES), F32),
        ],
        compiler_params=pltpu.CompilerParams(
            dimension_semantics=("arbitrary",), vmem_limit_bytes=_vmem_limit(vmem)),
        name="in_proj",
    )(x2d, norm_w, w_main, wgt_hi, wgt_lo, bias_t, wg_hi, wg_lo, bias_r)


FOX_TQ = 256
SCAN_SHIFTS = tuple(1 << i for i in range(int(math.log2(SEQ))))


def _rms(x, w):
    return x * lax.rsqrt(jnp.mean(x * x, axis=-1, keepdims=True) + NORM_EPS) * w


LOG2E = 1.4426950408889634


def _pair_rms(x, w, lo_half):
    sq = x * x
    ms_lo = jnp.sum(jnp.where(lo_half, sq, 0.0), axis=-1, keepdims=True) * (1.0 / HEAD_DIM)
    ms_hi = jnp.sum(jnp.where(lo_half, 0.0, sq), axis=-1, keepdims=True) * (1.0 / HEAD_DIM)
    inv = jnp.where(lo_half, lax.rsqrt(ms_lo + NORM_EPS), lax.rsqrt(ms_hi + NORM_EPS))
    return x * inv * w


def _fox_kernel(q_ref, k_ref, v_ref, gtm_ref, qw_ref, kw_ref, ow_ref, o_ref, cum_ref):
    hp = pl.program_id(1)

    @pl.when(hp == 0)
    def _():
        y = gtm_ref[...]
        rowi = lax.broadcasted_iota(jnp.int32, y.shape, 0)
        for s in SCAN_SHIFTS:
            y = y + jnp.where(rowi >= s, pltpu.roll(y, s, axis=0), 0.0)
        cum_ref[...] = y * LOG2E

    lane = lax.broadcasted_iota(jnp.int32, (SEQ, LANES), 1)
    lo_half = lane < HEAD_DIM
    kn_all = _pair_rms(k_ref[0], kw_ref[...], lo_half)
    qn_all = _pair_rms(q_ref[0], qw_ref[...], lo_half) * ((HEAD_DIM ** -0.5) * LOG2E)
    vb = v_ref[0].astype(BF16)
    tri = (lax.broadcasted_iota(jnp.int32, (FOX_TQ, FOX_TQ), 1)
           <= lax.broadcasted_iota(jnp.int32, (FOX_TQ, FOX_TQ), 0))
    lo_q = lax.broadcasted_iota(jnp.int32, (FOX_TQ, LANES), 1) < HEAD_DIM

    for j in range(2):
        h = 2 * hp + j
        cs = slice(j * HEAD_DIM, (j + 1) * HEAD_DIM)
        own = lo_half if j == 0 else jnp.logical_not(lo_half)
        own_q = lo_q if j == 0 else jnp.logical_not(lo_q)
        base = HEAD_DIM if j == 0 else 0
        cum = jnp.sum(jnp.where(lane == h, cum_ref[...], 0.0), axis=1, keepdims=True)
        c_hi = cum.astype(BF16).astype(F32)
        r1 = cum - c_hi
        c_mid = r1.astype(BF16).astype(F32)
        c_lo = (r1 - c_mid).astype(BF16).astype(F32)
        q_aug = jnp.zeros((SEQ, LANES), F32)
        k_aug = jnp.zeros((SEQ, LANES), F32)
        for t, piece in enumerate((c_hi, c_mid, c_lo)):
            q_aug = jnp.where(lane == base + t, piece, q_aug)
            k_aug = jnp.where(lane == base + 3 + t, -piece, k_aug)
        q_aug = jnp.where((lane >= base + 3) & (lane < base + 6), 1.0, q_aug)
        k_aug = jnp.where((lane >= base) & (lane < base + 3), 1.0, k_aug)
        qa = jnp.where(own, qn_all, q_aug).astype(BF16)
        ka = jnp.where(own, kn_all, k_aug).astype(BF16)

        for i in range(SEQ // FOX_TQ):
            qs = slice(i * FOX_TQ, (i + 1) * FOX_TQ)
            n = (i + 1) * FOX_TQ
            s = lax.dot_general(qa[qs], ka[:n], NT_DIMS, preferred_element_type=F32)
            diag = jnp.where(tri, s[:, n - FOX_TQ:], -jnp.inf)
            m = jnp.max(diag, axis=-1, keepdims=True)
            if i > 0:
                past = s[:, :n - FOX_TQ]
                m = jnp.maximum(m, jnp.max(past, axis=-1, keepdims=True))
                p = jnp.concatenate([jnp.exp2(past - m), jnp.exp2(diag - m)], axis=1)
            else:
                p = jnp.exp2(diag - m)
            l = jnp.sum(p, axis=-1, keepdims=True)
            o = jnp.dot(p.astype(BF16), vb[:n], preferred_element_type=F32) / l
            ms = jnp.sum(jnp.where(own_q, o * o, 0.0), axis=-1, keepdims=True) * (1.0 / HEAD_DIM)
            on = o * lax.rsqrt(ms + NORM_EPS) * ow_ref[...]
            o_ref[0, qs, cs] = on[:, cs]


def _fox(proj3, gtm, qw, kw, ow):
    b = proj3.shape[0]
    nq = FOX_WIDTH // LANES
    blk = (1, SEQ, LANES)
    vmem = 2 * 5 * SEQ * LANES * 4 + 8 * SEQ * LANES * 4 + 6 * FOX_TQ * SEQ * 4
    return pl.pallas_call(
        _fox_kernel,
        grid=(b, nq),
        in_specs=[
            pl.BlockSpec(blk, lambda bi, hp: (bi, 0, hp)),
            pl.BlockSpec(blk, lambda bi, hp: (bi, 0, nq + hp)),
            pl.BlockSpec(blk, lambda bi, hp: (bi, 0, 2 * nq + hp)),
            pl.BlockSpec((SEQ, LANES), lambda bi, hp: (bi, 0)),
            pl.BlockSpec((1, LANES), lambda bi, hp: (0, 0)),
            pl.BlockSpec((1, LANES), lambda bi, hp: (0, 0)),
            pl.BlockSpec((1, LANES), lambda bi, hp: (0, hp)),
        ],
        out_specs=pl.BlockSpec(blk, lambda bi, hp: (bi, 0, hp)),
        out_shape=jax.ShapeDtypeStruct((b, SEQ, FOX_WIDTH), F32),
        scratch_shapes=[pltpu.VMEM((SEQ, LANES), F32)],
        compiler_params=pltpu.CompilerParams(
            dimension_semantics=("arbitrary", "arbitrary"), vmem_limit_bytes=_vmem_limit(vmem)),
        name="fox",
    )(proj3, proj3, proj3, gtm, qw, kw, ow)


ML_L = MLSTM_CHUNK
ML_PAIRS = SEQ // (2 * ML_L)
ML_HL = FOX_HEADS
ML_AUG = MLSTM_V_WIDTH + LANES
ML_TILE = 256
SEG_SHIFTS = tuple(1 << i for i in range(int(math.log2(ML_L))))


def _split3(x):
    a = x.astype(BF16)
    r = x - a.astype(F32)
    b = r.astype(BF16)
    c = (r - b.astype(F32)).astype(BF16)
    return a, b, c


def _expand_heads(x, exp_bf):
    a, b, c = _split3(x)
    return (jnp.dot(a, exp_bf, preferred_element_type=F32)
            + jnp.dot(b, exp_bf, preferred_element_type=F32)
            + jnp.dot(c, exp_bf, preferred_element_type=F32))


def _seg_scan(x, axis, op, ident):
    idx = lax.broadcasted_iota(jnp.int32, x.shape, axis) % ML_L
    for s in SEG_SHIFTS:
        x = op(x, jnp.where(idx >= s, pltpu.roll(x, s, axis=axis), ident))
    return x


def _mlstm_kernel(q_ref, k_ref, v_ref, gi_ref, gf_ref, gtm_ref, cw_ref, o_ref, den_ref,
                  qc_ref, kc_ref, kt_ref, rr_ref, cmr_ref, bcr_ref,
                  ealpha_ref, ew_ref, wint_ref, floor_ref, mfull_ref, caug_ref):
    def conv_silu(u, w):
        rowi = lax.broadcasted_iota(jnp.int32, u.shape, 0)
        acc = u * w[CONV_WIDTH - 1:CONV_WIDTH, :]
        for d in range(1, CONV_WIDTH):
            sh = jnp.where(rowi >= d, pltpu.roll(u, d, axis=0), 0.0)
            acc = acc + sh * w[CONV_WIDTH - 1 - d:CONV_WIDTH - d, :]
        return acc / (1.0 + jnp.exp(-acc))

    cw = cw_ref[...]
    qc_ref[...] = conv_silu(q_ref[0], cw[:, :MLSTM_QK_WIDTH]).astype(BF16)
    kc = conv_silu(k_ref[0], cw[:, MLSTM_QK_WIDTH:]) * (MLSTM_QK_DIM ** -0.5)
    kc_ref[...] = kc.astype(BF16)
    kt = kc.T
    for p in range(ML_PAIRS):
        kt_ref[p] = kt[:, p * LANES:(p + 1) * LANES]

    bcum_r = _seg_scan(gf_ref[...], 1, jnp.add, 0.0)
    r_r = gi_ref[...] - bcum_r
    cmx_r = _seg_scan(r_r, 1, jnp.maximum, -jnp.inf)
    for p in range(ML_PAIRS):
        ls = slice(p * LANES, (p + 1) * LANES)
        rr_ref[p] = r_r[:, ls]
        cmr_ref[p] = cmx_r[:, ls]
        bcr_ref[p] = bcum_r[:, ls]

    g = gtm_ref[...]
    lane_g = lax.broadcasted_iota(jnp.int32, g.shape, 1)
    head_lane = (lane_g >= ML_HL) & (lane_g < ML_HL + MLSTM_HEADS)
    bcum_c = jnp.where(
        head_lane, pltpu.roll(_seg_scan(g, 0, jnp.add, 0.0), LANES - MLSTM_HEADS, axis=1), 0.0)
    cmx_c = _seg_scan(jnp.where(head_lane, g, 0.0) - bcum_c, 0, jnp.maximum, -jnp.inf)
    m = jnp.zeros((1, LANES), F32)
    for c in range(SEQ // ML_L):
        mfull_ref[c * ML_L:(c + 1) * ML_L, :] = jnp.broadcast_to(m, (ML_L, LANES))
        last = (c + 1) * ML_L - 1
        m = bcum_c[last:last + 1, :] + jnp.maximum(m, cmx_c[last:last + 1, :])
    mfull = mfull_ref[...]
    mx = jnp.maximum(mfull, cmx_c)
    wint_ref[...] = jnp.exp(mfull - mx)
    floor_ref[...] = jnp.exp(-(bcum_c + mx))
    mfull_ref[...] = -mx

    lane_e = lax.broadcasted_iota(jnp.int32, (LANES, MLSTM_V_WIDTH), 1) // MLSTM_V_DIM
    row_e = lax.broadcasted_iota(jnp.int32, (LANES, MLSTM_V_WIDTH), 0)
    exp_bf = jnp.where(row_e == lane_e + ML_HL, 1.0, 0.0).astype(BF16)

    def expand_tile(i, carry):
        rows = pl.ds(pl.multiple_of(i * ML_TILE, ML_TILE), ML_TILE)
        ealpha_ref[rows, :] = _expand_heads(mfull_ref[rows, :], exp_bf)
        ew_ref[rows, :] = _expand_heads(wint_ref[rows, :], exp_bf)
        return carry

    lax.fori_loop(0, SEQ // ML_TILE, expand_tile, 0)

    kb_rowh = lax.broadcasted_iota(jnp.int32, (MLSTM_HEADS * ML_L, MLSTM_QK_WIDTH), 0) // ML_L
    kb_lane = lax.broadcasted_iota(jnp.int32, (MLSTM_HEADS * ML_L, MLSTM_QK_WIDTH), 1) // MLSTM_QK_DIM
    mask_k = jnp.where(kb_rowh == kb_lane, 1.0, 0.0).astype(BF16)
    va_rowh = lax.broadcasted_iota(jnp.int32, (MLSTM_HEADS * ML_L, ML_AUG), 0) // ML_L
    va_col = lax.broadcasted_iota(jnp.int32, (MLSTM_HEADS * ML_L, ML_AUG), 1)
    mask_v = jnp.where(
        (va_col // MLSTM_V_DIM == va_rowh) | (va_col == MLSTM_V_WIDTH + ML_HL + va_rowh),
        1.0, 0.0).astype(BF16)
    c_rowh = lax.broadcasted_iota(jnp.int32, (MLSTM_QK_WIDTH, ML_AUG), 0) // MLSTM_QK_DIM
    c_col = lax.broadcasted_iota(jnp.int32, (MLSTM_QK_WIDTH, ML_AUG), 1)
    mask_c = (c_col // MLSTM_V_DIM == c_rowh) | (c_col == MLSTM_V_WIDTH + ML_HL + c_rowh)
    lane128 = lax.broadcasted_iota(jnp.int32, (ML_L, LANES), 1)
    s_idx = lax.broadcasted_iota(jnp.int32, (ML_L, MLSTM_V_WIDTH), 1) % ML_L
    t_idx = lax.broadcasted_iota(jnp.int32, (ML_L, MLSTM_V_WIDTH), 0)
    causal = s_idx <= t_idx
    ones_aug = jnp.ones((ML_L, LANES), F32)

    caug_ref[...] = jnp.zeros_like(caug_ref)

    def pair_body(cp, m_row):
        r2 = rr_ref[cp]
        cm2 = cmr_ref[cp]
        b2 = bcr_ref[cp]
        r2r = pltpu.roll(r2, ML_L, axis=1)
        kt2 = kt_ref[cp]
        for cc in range(2):
            lo = cc * ML_L
            rows = pl.ds(pl.multiple_of(cp * (2 * ML_L), 2 * ML_L) + lo, ML_L)
            qa = qc_ref[rows, :]
            ka = kc_ref[rows, :]
            va_aug = jnp.concatenate([v_ref[0, rows, :], ones_aug], axis=1).astype(BF16)

            kbd = jnp.concatenate([ka] * MLSTM_HEADS, axis=0) * mask_k
            s = lax.dot_general(qa, kbd, NT_DIMS, preferred_element_type=F32)

            cmx_last = cm2[:, lo + ML_L - 1:lo + ML_L]
            b_last = b2[:, lo + ML_L - 1:lo + ML_L]
            mx_r = jnp.maximum(m_row, cmx_last)
            decay = jnp.exp(m_row - mx_r)
            wk = jnp.exp(r2[:, lo:lo + ML_L] - mx_r)
            m_row = b_last + mx_r

            src_e, src_o = (r2, r2r) if cc == 0 else (r2r, r2)
            cols = []
            for p in range(MLSTM_HEADS // 2):
                even = jnp.broadcast_to(src_e[2 * p:2 * p + 1, :], (ML_L, LANES))
                odd = jnp.broadcast_to(src_o[2 * p + 1:2 * p + 2, :], (ML_L, LANES))
                cols.append(jnp.where(lane128 < ML_L, even, odd))
            r_all = jnp.concatenate(cols, axis=1)
            arg = jnp.where(causal, ealpha_ref[rows, :] + r_all, -jnp.inf)
            p_all = (s * jnp.exp(arg)).astype(BF16)

            vbd = jnp.concatenate([va_aug] * MLSTM_HEADS, axis=0) * mask_v
            pv = jnp.dot(p_all, vbd, preferred_element_type=F32)
            qc_state = jnp.dot(qa, caug_ref[...].astype(BF16), preferred_element_type=F32)
            o_ref[0, rows, :] = (ew_ref[rows, :] * qc_state[:, :MLSTM_V_WIDTH]
                                 + pv[:, :MLSTM_V_WIDTH])
            den_ref[rows, :] = (wint_ref[rows, :] * qc_state[:, MLSTM_V_WIDTH:]
                                + pv[:, MLSTM_V_WIDTH:])

            wk_rows = jnp.concatenate(
                [jnp.broadcast_to(wk[h:h + 1, :], (MLSTM_QK_DIM, ML_L)) for h in range(MLSTM_HEADS)],
                axis=0)
            dec_rows = jnp.concatenate(
                [jnp.broadcast_to(decay[h:h + 1, :], (MLSTM_QK_DIM, 1)) for h in range(MLSTM_HEADS)],
                axis=0)
            ktw = (kt2[:, lo:lo + ML_L] * wk_rows).astype(BF16)
            upd = jnp.dot(ktw, va_aug, preferred_element_type=F32)
            caug_ref[...] = dec_rows * caug_ref[...] + jnp.where(mask_c, upd, 0.0)
        return m_row

    lax.fori_loop(0, ML_PAIRS, pair_body, jnp.zeros((MLSTM_HEADS, 1), F32))

    ob_row = lax.broadcasted_iota(jnp.int32, (MLSTM_V_WIDTH, LANES), 0) // MLSTM_V_DIM
    ob_col = lax.broadcasted_iota(jnp.int32, (MLSTM_V_WIDTH, LANES), 1)
    ones_bd = jnp.where(ob_col == ob_row + ML_HL, 1.0, 0.0).astype(BF16)

    def norm_tile(i, carry):
        rows = pl.ds(pl.multiple_of(i * ML_TILE, ML_TILE), ML_TILE)
        num = o_ref[0, rows, :]
        dn = jnp.maximum(jnp.abs(den_ref[rows, :]), floor_ref[rows, :])
        r = 1.0 / dn
        n2_hi, n2_lo = _split_hi_lo(num * num)
        msn = (jnp.dot(n2_hi, ones_bd, preferred_element_type=F32)
               + jnp.dot(n2_lo, ones_bd, preferred_element_type=F32)) * (1.0 / MLSTM_V_DIM)
        fac = r * lax.rsqrt(r * r * msn + NORM_EPS)
        o_ref[0, rows, :] = num * _expand_heads(fac, exp_bf)
        return carry

    lax.fori_loop(0, SEQ // ML_TILE, norm_tile, 0)


def _mlstm(proj3, gt, gtm, conv_w):
    b = proj3.shape[0]
    qk_blk = (1, SEQ, MLSTM_QK_WIDTH)
    v_blk = (1, SEQ, MLSTM_V_WIDTH)
    q_col = 3 * FOX_WIDTH // MLSTM_QK_WIDTH
    v_col = (3 * FOX_WIDTH + 2 * MLSTM_QK_WIDTH) // MLSTM_V_WIDTH
    vmem = (2 * (2 * SEQ * MLSTM_QK_WIDTH * 4 + 2 * SEQ * MLSTM_V_WIDTH * 4 + 2 * SEQ * LANES * 4)
            + 2 * SEQ * MLSTM_QK_WIDTH * 2 + SEQ * MLSTM_QK_WIDTH * 4 + 2 * SEQ * MLSTM_V_WIDTH * 4
            + 3 * SEQ * LANES * 4 + (8 << 20))
    return pl.pallas_call(
        _mlstm_kernel,
        grid=(b,),
        in_specs=[
            pl.BlockSpec(qk_blk, lambda bi: (bi, 0, q_col)),
            pl.BlockSpec(qk_blk, lambda bi: (bi, 0, q_col + 1)),
            pl.BlockSpec(v_blk, lambda bi: (bi, 0, v_col)),
            pl.BlockSpec((SUBLANES, SEQ), lambda bi: (1, bi)),
            pl.BlockSpec((SUBLANES, SEQ), lambda bi: (2, bi)),
            pl.BlockSpec((SEQ, LANES), lambda bi: (bi, 0)),
            pl.BlockSpec((CONV_WIDTH, 2 * MLSTM_QK_WIDTH), lambda bi: (0, 0)),
        ],
        out_specs=pl.BlockSpec(v_blk, lambda bi: (bi, 0, 0)),
        out_shape=jax.ShapeDtypeStruct((b, SEQ, MLSTM_V_WIDTH), F32),
        scratch_shapes=[
            pltpu.VMEM((SEQ, LANES), F32),
            pltpu.VMEM((SEQ, MLSTM_QK_WIDTH), BF16),
            pltpu.VMEM((SEQ, MLSTM_QK_WIDTH), BF16),
            pltpu.VMEM((ML_PAIRS, MLSTM_QK_WIDTH, LANES), F32),
            pltpu.VMEM((ML_PAIRS, SUBLANES, LANES), F32),
            pltpu.VMEM((ML_PAIRS, SUBLANES, LANES), F32),
            pltpu.VMEM((ML_PAIRS, SUBLANES, LANES), F32),
            pltpu.VMEM((SEQ, MLSTM_V_WIDTH), F32),
            pltpu.VMEM((SEQ, MLSTM_V_WIDTH), F32),
            pltpu.VMEM((SEQ, LANES), F32),
            pltpu.VMEM((SEQ, LANES), F32),
            pltpu.VMEM((SEQ, LANES), F32),
            pltpu.VMEM((MLSTM_QK_WIDTH, ML_AUG), F32),
        ],
        compiler_params=pltpu.CompilerParams(
            dimension_semantics=("arbitrary",), vmem_limit_bytes=_vmem_limit(vmem)),
        name="mlstm",
    )(proj3, proj3, proj3, gt, gt, gtm, conv_w)


RT_TM = 512


def _out_route_kernel(x_ref, fy_ref, my_ref, mo_ref, wo_ref, mg_ref, nw_ref, wr_hi_ref, wr_lo_ref,
                      rb_ref, x1_ref, h2_ref, eidx_ref, gate_ref, rank_ref, cnt_ref, carry_ref):
    i = pl.program_id(0)

    @pl.when(i == 0)
    def _():
        carry_ref[...] = jnp.zeros_like(carry_ref)

    my = my_ref[...] * mg_ref[...] / (1.0 + jnp.exp(-mo_ref[...]))
    mixed = (jnp.dot(fy_ref[...].astype(BF16), wo_ref[:FOX_WIDTH, :], preferred_element_type=F32)
             + jnp.dot(my.astype(BF16), wo_ref[FOX_WIDTH:, :], preferred_element_type=F32))
    x1 = x_ref[...] + mixed
    x1_ref[...] = x1
    h2 = _rms(x1, nw_ref[...])
    _rows_to_tiles(h2, h2_ref)

    h_hi, h_lo = _split_hi_lo(h2)
    wr_hi, wr_lo = wr_hi_ref[...], wr_lo_ref[...]
    logit = (lax.dot_general(wr_hi, h_hi, NT_DIMS, preferred_element_type=F32)
             + lax.dot_general(wr_lo, h_hi, NT_DIMS, preferred_element_type=F32)
             + lax.dot_general(wr_hi, h_lo, NT_DIMS, preferred_element_type=F32)) + rb_ref[...]

    e_iota = lax.broadcasted_iota(jnp.int32, logit.shape, 0).astype(F32)
    vals, idxs, hots = [], [], []
    for _ in range(TOP_K):
        mk = jnp.max(logit, axis=0, keepdims=True)
        idx = jnp.min(jnp.where(logit == mk, e_iota, float(N_EXPERTS)), axis=0, keepdims=True)
        hot = e_iota == idx
        logit = jnp.where(hot, -jnp.inf, logit)
        vals.append(mk)
        idxs.append(idx.astype(jnp.int32))
        hots.append(hot)
    exps = [jnp.exp(v - vals[0]) for v in vals]
    tot = exps[0] + exps[1] + exps[2] + exps[3]
    gates = [e / tot for e in exps]

    assign = jnp.zeros(logit.shape, F32)
    for hot in hots:
        assign = assign + jnp.where(hot, 1.0, 0.0)
    tm = logit.shape[1]
    src = lax.broadcasted_iota(jnp.int32, (tm, tm), 0)
    dst = lax.broadcasted_iota(jnp.int32, (tm, tm), 1)
    upper = jnp.where(src < dst, 1.0, 0.0).astype(BF16)
    base = jnp.dot(assign.astype(BF16), upper, preferred_element_type=F32) + carry_ref[:, 0:1]
    ranks = [jnp.sum(jnp.where(hot, base, 0.0), axis=0, keepdims=True) for hot in hots]
    new_carry = carry_ref[...] + jnp.sum(assign, axis=1, keepdims=True)
    carry_ref[...] = new_carry
    cnt_ref[...] = new_carry

    zi = jnp.zeros((SUBLANES - TOP_K, tm), jnp.int32)
    eidx_ref[...] = jnp.concatenate(idxs + [zi], axis=0)
    rank_ref[...] = jnp.concatenate([r.astype(jnp.int32) for r in ranks] + [zi], axis=0)
    gate_ref[...] = jnp.concatenate(gates + [zi.astype(F32)], axis=0)


def _out_route(x2d, fox_y2d, mlstm_y2d, proj, w_out_bf, mlstm_gain, moe_norm_w, wr_hi, wr_lo, rb):
    n = x2d.shape[0]
    tm = RT_TM
    const = lambda i: (0, 0)
    mo_col = (MAIN_WIDTH - MLSTM_V_WIDTH) // MLSTM_V_WIDTH
    row_blk = lambda w: pl.BlockSpec((tm, w), lambda i: (i, 0))
    lane_blk = pl.BlockSpec((SUBLANES, tm), lambda i: (0, i))
    vmem = (2 * (tm * D_MODEL * 4 * 3 + tm * FOX_WIDTH * 4 * 3 + D_MODEL * D_MODEL * 2)
            + 6 * tm * D_MODEL * 4 + tm * tm * 6)
    return pl.pallas_call(
        _out_route_kernel,
        grid=(n // tm,),
        in_specs=[
            row_blk(D_MODEL), row_blk(FOX_WIDTH), row_blk(MLSTM_V_WIDTH),
            pl.BlockSpec((tm, MLSTM_V_WIDTH), lambda i: (i, mo_col)),
            pl.BlockSpec((D_MODEL, D_MODEL), const),
            pl.BlockSpec((1, MLSTM_V_WIDTH), const),
            pl.BlockSpec((1, D_MODEL), const),
            pl.BlockSpec((N_EXPERTS, D_MODEL), const),
            pl.BlockSpec((N_EXPERTS, D_MODEL), const),
            pl.BlockSpec((N_EXPERTS, 1), const),
        ],
        out_specs=[row_blk(D_MODEL), pl.BlockSpec((tm * TOK_ROWS, LANES), lambda i: (i, 0)),
                   lane_blk, lane_blk, lane_blk, pl.BlockSpec((N_EXPERTS, LANES), const)],
        out_shape=[
            jax.ShapeDtypeStruct((n, D_MODEL), F32),
            jax.ShapeDtypeStruct((n * TOK_ROWS, LANES), F32),
            jax.ShapeDtypeStruct((SUBLANES, n), jnp.int32),
            jax.ShapeDtypeStruct((SUBLANES, n), F32),
            jax.ShapeDtypeStruct((SUBLANES, n), jnp.int32),
            jax.ShapeDtypeStruct((N_EXPERTS, LANES), F32),
        ],
        scratch_shapes=[pltpu.VMEM((N_EXPERTS, LANES), F32)],
        compiler_params=pltpu.CompilerParams(
            dimension_semantics=("arbitrary",), vmem_limit_bytes=_vmem_limit(vmem)),
        name="out_route",
    )(x2d, fox_y2d, mlstm_y2d, proj, w_out_bf, mlstm_gain, moe_norm_w, wr_hi, wr_lo, rb)


EX_BM = 256
EX_DRAIN_STEPS = 1
EX_CHUNKS = 4
EX_M1_GATHER = (86, 85, 85)
EX_M1_SCATTER = (48, 48, 48)
EX_M2_SCATTER = (28, 28, 28, 28)
assert sum(EX_M1_GATHER) == EX_BM and sum(EX_M1_SCATTER) + sum(EX_M2_SCATTER) == EX_BM


def _experts_kernel(be_ref, nu_ref, nv_ref, tok_ref, tokn_ref, dstp_ref, h2_hbm, wgu_ref, bgu_ref,
                    wd_ref, bd_ref, y_hbm, tb_ref, wgu_bf_ref, wd_bf_ref, gsem, ssem):
    i = pl.program_id(0)
    last_blk = pl.num_programs(0) - 1 - EX_DRAIN_STEPS
    nu = nu_ref[0]
    slot = i % 2
    oslot = 1 - slot
    used = i < nu
    cur = jnp.minimum(i, last_blk)
    nv_prev = nv_ref[jnp.clip(i - 1, 0, last_blk)]
    nv_prev2 = nv_ref[jnp.clip(i - 2, 0, last_blk)]
    tile_rows = EX_BM * TOK_ROWS
    grp = EX_BM // EX_CHUNKS
    kc = D_MODEL // EX_CHUNKS
    tr = kc // LANES

    def tok_tile(ref, idx):
        return ref.at[pl.ds(pl.multiple_of(idx, TOK_ROWS), TOK_ROWS), :]

    def gather_next_row(idx_ref, s, r):
        pltpu.make_async_copy(tok_tile(h2_hbm, idx_ref[0, 0, r]),
                              tb_ref.at[s, pl.ds(r * TOK_ROWS, TOK_ROWS), :], gsem.at[s]).start()

    def scatter_prev_row(r):
        pltpu.make_async_copy(
            tb_ref.at[2 + oslot, pl.ds(pl.multiple_of(r * TOK_ROWS, TOK_ROWS), TOK_ROWS), :],
            tok_tile(y_hbm, dstp_ref[0, 0, r]), ssem.at[oslot]).start()

    def scatter_prev_partial():
        def body(r, carry):
            scatter_prev_row(r)
            return carry
        lax.fori_loop(0, nv_prev, body, 0)

    def wait_block(sem, buf):
        pltpu.make_async_copy(h2_hbm.at[pl.ds(0, tile_rows), :], buf, sem).wait()

    def wait_scatter(s, n):
        @pl.when(n == EX_BM)
        def _():
            wait_block(ssem.at[s], tb_ref.at[2 + s])

        @pl.when(n < EX_BM)
        def _():
            def body(r, carry):
                pltpu.make_async_copy(h2_hbm.at[pl.ds(0, TOK_ROWS), :],
                                      tb_ref.at[2 + s, pl.ds(0, TOK_ROWS), :], ssem.at[s]).wait()
                return carry
            lax.fori_loop(0, n, body, 0)

    @pl.when(i == 0)
    def _():
        for r in range(EX_BM):
            gather_next_row(tok_ref, 0, r)

    @pl.when((i >= 2) & (i <= nu))
    def _():
        wait_scatter(slot, nv_prev2)

    @pl.when(used & ((i == 0) | (be_ref[cur] != be_ref[jnp.maximum(cur - 1, 0)])))
    def _():
        wgu_bf_ref[...] = wgu_ref[0].astype(BF16)
        wd_bf_ref[...] = wd_ref[0].astype(BF16)

    def step_body(scatter_prev):
        wait_block(gsem.at[slot], tb_ref.at[slot])
        g_next = s_next = 0
        gu = None
        for c in range(EX_CHUNKS):
            xk = jnp.concatenate(
                [tb_ref[slot, pl.ds(c * tr + t, EX_BM, stride=TOK_ROWS), :] for t in range(tr)],
                axis=1).astype(BF16)
            part = jnp.dot(xk, wgu_bf_ref[c * kc:(c + 1) * kc, :], preferred_element_type=F32)
            gu = part if gu is None else gu + part
            if c < len(EX_M1_GATHER):
                for r in range(g_next, g_next + EX_M1_GATHER[c]):
                    gather_next_row(tokn_ref, oslot, r)
                g_next += EX_M1_GATHER[c]
                if scatter_prev:
                    for r in range(s_next, s_next + EX_M1_SCATTER[c]):
                        scatter_prev_row(r)
                s_next += EX_M1_SCATTER[c]
        gu = gu + bgu_ref[0]
        gate = jnp.minimum(gu[:, :D_EXPERT], SWIGLU_LIMIT)
        up = jnp.clip(gu[:, D_EXPERT:], -SWIGLU_LIMIT, SWIGLU_LIMIT)
        act = ((up + 1.0) * (gate / (1.0 + jnp.exp(-SWIGLU_ALPHA * gate)))).astype(BF16)
        for c in range(EX_CHUNKS):
            if scatter_prev:
                for r in range(s_next, s_next + EX_M2_SCATTER[c]):
                    scatter_prev_row(r)
            s_next += EX_M2_SCATTER[c]
            yc = (jnp.dot(act, wd_bf_ref[:, c * kc:(c + 1) * kc], preferred_element_type=F32)
                  + bd_ref[0, :, c * kc:(c + 1) * kc])
            for t in range(tr):
                tb_ref[2 + slot, pl.ds(c * tr + t, EX_BM, stride=TOK_ROWS), :] = (
                    yc[:, t * LANES:(t + 1) * LANES])
        assert g_next == EX_BM and s_next == EX_BM

    prev_full = (i >= 1) & (nv_prev == EX_BM)

    @pl.when(used & (i >= 1) & (nv_prev < EX_BM))
    def _():
        scatter_prev_partial()

    @pl.when(used & jnp.logical_not(prev_full))
    def _():
        step_body(False)

    @pl.when(used & prev_full)
    def _():
        step_body(True)

    @pl.when(i == nu)
    def _():
        wait_block(gsem.at[slot], tb_ref.at[slot])

        @pl.when(nv_prev == EX_BM)
        def _():
            for r in range(EX_BM):
                scatter_prev_row(r)

        @pl.when(nv_prev < EX_BM)
        def _():
            scatter_prev_partial()

        wait_scatter(oslot, nv_prev)


def _experts(block_e, n_used, n_valid, buf_tok3, dst3, h2, w_gu, b_gu, w_down, b_down, n_slots):
    nb = buf_tok3.shape[0]
    idx_blk = lambda f: pl.BlockSpec((1, 1, EX_BM), f, memory_space=pltpu.SMEM)
    vmem = (2 * (D_MODEL * 2 * D_EXPERT * 4 + D_EXPERT * D_MODEL * 4)
            + D_MODEL * 2 * D_EXPERT * 2 + D_EXPERT * D_MODEL * 2
            + 3 * EX_BM * D_MODEL * 4 + 3 * EX_BM * 2 * D_EXPERT * 4)
    blk = lambda i: jnp.minimum(i, nb - 1)
    w_map = lambda i, be, nu, nv: (be[blk(i)], 0, 0)
    grid_spec = pltpu.PrefetchScalarGridSpec(
        num_scalar_prefetch=3,
        grid=(nb + EX_DRAIN_STEPS,),
        in_specs=[
            idx_blk(lambda i, be, nu, nv: (blk(i), 0, 0)),
            idx_blk(lambda i, be, nu, nv: (blk(i + 1), 0, 0)),
            idx_blk(lambda i, be, nu, nv: (jnp.maximum(i - 1, 0), 0, 0)),
            pl.BlockSpec(memory_space=pl.ANY),
            pl.BlockSpec((1, D_MODEL, 2 * D_EXPERT), w_map),
            pl.BlockSpec((1, 1, 2 * D_EXPERT), w_map),
            pl.BlockSpec((1, D_EXPERT, D_MODEL), w_map),
            pl.BlockSpec((1, 1, D_MODEL), w_map),
        ],
        out_specs=pl.BlockSpec(memory_space=pl.ANY),
        scratch_shapes=[
            pltpu.VMEM((4, EX_BM * TOK_ROWS, LANES), F32),
            pltpu.VMEM((D_MODEL, 2 * D_EXPERT), BF16),
            pltpu.VMEM((D_EXPERT, D_MODEL), BF16),
            pltpu.SemaphoreType.DMA((2,)),
            pltpu.SemaphoreType.DMA((2,)),
        ],
    )
    return pl.pallas_call(
        _experts_kernel,
        grid_spec=grid_spec,
        out_shape=jax.ShapeDtypeStruct((n_slots * TOK_ROWS, LANES), F32),
        compiler_params=pltpu.CompilerParams(
            dimension_semantics=("arbitrary",), vmem_limit_bytes=_vmem_limit(vmem)),
        name="experts",
    )(block_e, n_used, n_valid, buf_tok3, buf_tok3, dst3, h2, w_gu, b_gu[:, None, :], w_down,
      b_down[:, None, :])


CB_TM = 256


def _combine_kernel(x1_ref, y0_ref, y1_ref, y2_ref, y3_ref, gate_ref, o_ref):
    tm = x1_ref.shape[0]
    g = jnp.concatenate([gate_ref[...], jnp.zeros((LANES - SUBLANES, tm), F32)], axis=0).T
    acc = x1_ref[...]
    for k, y_ref in enumerate((y0_ref, y1_ref, y2_ref, y3_ref)):
        acc = acc + g[:, k:k + 1] * _tiles_to_rows(y_ref, tm)
    o_ref[...] = acc


def _combine(x1, y_slots, gates):
    n = x1.shape[0]
    tm = CB_TM
    nt = n // tm
    vmem = 2 * (2 * tm * D_MODEL * 4 + tm * TOP_K * D_MODEL * 4) + 4 * tm * D_MODEL * 4
    y_spec = lambda k: pl.BlockSpec((tm * TOK_ROWS, LANES), lambda i: (k * nt + i, 0))
    return pl.pallas_call(
        _combine_kernel,
        grid=(nt,),
        in_specs=[pl.BlockSpec((tm, D_MODEL), lambda i: (i, 0))]
        + [y_spec(k) for k in range(TOP_K)]
        + [pl.BlockSpec((SUBLANES, tm), lambda i: (0, i))],
        out_specs=pl.BlockSpec((tm, D_MODEL), lambda i: (i, 0)),
        out_shape=jax.ShapeDtypeStruct((n, D_MODEL), F32),
        compiler_params=pltpu.CompilerParams(
            dimension_semantics=("arbitrary",), vmem_limit_bytes=_vmem_limit(vmem)),
        name="combine",
    )(x1, y_slots, y_slots, y_slots, y_slots, gates)


def _dispatch_plan(eidx, rank, counts):
    n = eidx.shape[1]
    n_slots = n * TOP_K
    nb = n_slots // EX_BM + N_EXPERTS
    counts = counts.astype(jnp.int32)
    padded = ((counts + EX_BM - 1) // EX_BM) * EX_BM
    pad_end = jnp.cumsum(padded)
    pad_start = pad_end - padded
    e = eidx[:TOP_K]
    start_of = jnp.sum(jnp.where(e[:, :, None] == jnp.arange(N_EXPERTS, dtype=jnp.int32),
                                 pad_start[None, None, :], 0), axis=-1)
    pos = start_of + rank[:TOP_K]
    slot_id = jnp.arange(n_slots, dtype=jnp.int32)
    inv = jnp.full((nb * EX_BM,), -1, jnp.int32).at[pos.reshape(-1)].set(slot_id, unique_indices=True)
    buf_tok = jnp.where(inv >= 0, inv % n, 0) * TOK_ROWS
    dst = jnp.maximum(inv, 0) * TOK_ROWS
    blk_start = jnp.arange(nb, dtype=jnp.int32) * EX_BM
    block_e = jnp.minimum(jnp.sum((pad_end[None, :] <= blk_start[:, None]).astype(jnp.int32), axis=1),
                          N_EXPERTS - 1)
    n_used = (pad_end[-1] // EX_BM).astype(jnp.int32).reshape(1)
    n_valid = jnp.clip(pad_start[block_e] + counts[block_e] - blk_start, 0, EX_BM).astype(jnp.int32)
    n_valid = jnp.where(blk_start < pad_end[-1], n_valid, 0)
    return (block_e, n_used, n_valid, buf_tok.reshape(nb, 1, EX_BM), dst.reshape(nb, 1, EX_BM), n_slots)


def _prep_in_proj_weights(w_in, fox_f_bias, mlstm_i_bias, mlstm_f_bias):
    split_at = []
    acc = 0
    for wdt in SPLIT_WIDTHS[:-1]:
        acc += wdt
        split_at.append(acc)
    fq, fk, fv, ff, mq, mk, mv, mi, mf, mo = jnp.split(w_in, split_at, axis=-1)
    w_main = jnp.concatenate([fq, fk, fv, mq, mk, mv, mo], axis=-1).astype(BF16)
    w_gate = jnp.concatenate([ff, mi, mf], axis=-1)
    bias = jnp.concatenate([fox_f_bias, mlstm_i_bias, mlstm_f_bias]).astype(F32)
    n_gate = w_gate.shape[1]
    wg = jnp.pad(w_gate, ((0, 0), (0, LANES - n_gate)))
    wg_hi, wg_lo = _split_hi_lo(wg)
    wgt = jnp.pad(w_gate.T, ((0, GATE_ROWS - n_gate), (0, 0)))
    wgt_hi, wgt_lo = _split_hi_lo(wgt)
    bias_r = jnp.pad(bias, (0, LANES - n_gate))[None, :]
    bias_t = jnp.pad(bias, (0, GATE_ROWS - n_gate))[:, None]
    return w_main, wgt_hi, wgt_lo, bias_t, wg_hi, wg_lo, bias_r


def kernel(x, attn_norm_w, w_in, fox_f_bias, fox_q_norm_w, fox_k_norm_w, fox_out_norm_w, mlstm_conv_w, mlstm_i_bias, mlstm_f_bias, mlstm_out_norm_w, w_out, moe_norm_w, router_w, router_b, expert_w_gate_up, expert_b_gate_up, expert_w_down, expert_b_down):
    bsz, seq, d = x.shape
    x2d = x.reshape(bsz * seq, d)
    prep = _prep_in_proj_weights(w_in[0], fox_f_bias[0], mlstm_i_bias[0], mlstm_f_bias[0])
    proj, gt, gtm = _in_proj(x2d, attn_norm_w[0][None, :], *prep)
    proj3 = proj.reshape(bsz, seq, MAIN_WIDTH)
    pair = lambda w: jnp.tile(w, LANES // HEAD_DIM)[None, :]
    fox_y = _fox(proj3, gtm, pair(fox_q_norm_w[0]), pair(fox_k_norm_w[0]), fox_out_norm_w[0][None, :])
    mlstm_y = _mlstm(proj3, gt, gtm, mlstm_conv_w[0])
    return _channel_mixer(x2d, fox_y.reshape(-1, FOX_WIDTH), mlstm_y.reshape(-1, MLSTM_V_WIDTH), proj,
                          mlstm_out_norm_w[0], w_out[0], moe_norm_w[0], router_w[0], router_b[0],
                          expert_w_gate_up[0], expert_b_gate_up[0], expert_w_down[0],
                          expert_b_down[0]).reshape(bsz, seq, d)


def _channel_mixer(x2d, fox_y2d, mlstm_y2d, proj, mlstm_gain, w_out, moe_norm_w, router_w, router_b,
                   w_gu, b_gu, w_down, b_down):
    wr_hi, wr_lo = _split_hi_lo(router_w.T)
    x1, h2, eidx, gates, rank, counts = _out_route(
        x2d, fox_y2d, mlstm_y2d, proj, w_out.astype(BF16), mlstm_gain[None, :], moe_norm_w[None, :],
        wr_hi, wr_lo, router_b[:, None])
    block_e, n_used, n_valid, buf_tok3, dst3, n_slots = _dispatch_plan(eidx, rank, counts[:, 0])
    y_slots = _experts(block_e, n_used, n_valid, buf_tok3, dst3, h2, w_gu, b_gu, w_down, b_down, n_slots)
    return _combine(x1, y_slots, gates)
```

```python
import functools
import math

import jax
import jax.numpy as jnp
from jax import lax
from jax.experimental import pallas as pl
from jax.experimental.pallas import tpu as pltpu

F32 = jnp.float32
BF16 = jnp.bfloat16

D_MODEL = 1024
SEQ = 2048
HEAD_DIM = 64
FOX_HEADS = 8
FOX_WIDTH = FOX_HEADS * HEAD_DIM
MLSTM_HEADS = 8
MLSTM_QK_DIM = 32
MLSTM_V_DIM = 64
MLSTM_QK_WIDTH = MLSTM_HEADS * MLSTM_QK_DIM
MLSTM_V_WIDTH = MLSTM_HEADS * MLSTM_V_DIM
CONV_WIDTH = 4
MLSTM_CHUNK = 64
SPLIT_WIDTHS = (FOX_WIDTH, FOX_WIDTH, FOX_WIDTH, FOX_HEADS,
                MLSTM_QK_WIDTH, MLSTM_QK_WIDTH, MLSTM_V_WIDTH,
                MLSTM_HEADS, MLSTM_HEADS, MLSTM_V_WIDTH)
N_EXPERTS = 32
TOP_K = 4
D_EXPERT = D_MODEL
SWIGLU_ALPHA = 1.702
SWIGLU_LIMIT = 7.0
NORM_EPS = 1e-5
LOG2E = 1.4426950408889634

LANES = 128
SUBLANES = 8
V7X_VMEM_BYTES = 64 * 1024 * 1024

MAIN_WIDTH = 3 * FOX_WIDTH + 2 * MLSTM_QK_WIDTH + 2 * MLSTM_V_WIDTH
MLSTM_COLS = MAIN_WIDTH - 3 * FOX_WIDTH
GATE_ROWS = 32
FOX_PAIR_WIDTH = FOX_HEADS * LANES

NT_DIMS = (((1,), (1,)), ((), ()))


def _vmem_limit(nbytes):
    return int(min(nbytes + (8 << 20), V7X_VMEM_BYTES - (4 << 20)))


def _log_sigmoid(x):
    return jnp.minimum(x, 0.0) - jnp.log(1.0 + jnp.exp(-jnp.abs(x)))


def _split_hi_lo(x):
    hi = x.astype(BF16)
    lo = (x - hi.astype(F32)).astype(BF16)
    return hi, lo


def _rms(x, w):
    return x * lax.rsqrt(jnp.mean(x * x, axis=-1, keepdims=True) + NORM_EPS) * w


TOK_ROWS = D_MODEL // LANES


def _rows_to_tiles(x, tile_ref):
    m = x.shape[0]
    for j in range(TOK_ROWS):
        tile_ref[pl.ds(j, m, stride=TOK_ROWS), :] = x[:, j * LANES:(j + 1) * LANES]


def _tiles_to_rows(tile_ref, m):
    return jnp.concatenate(
        [tile_ref[pl.ds(j, m, stride=TOK_ROWS), :] for j in range(TOK_ROWS)], axis=1)


IN_TM = 512
IN_TILES_PER_SEQ = SEQ // IN_TM
IN_SCAN_SHIFTS = tuple(1 << i for i in range(int(math.log2(IN_TM))))


def _pair_rms(x, w, lo_half):
    sq = x * x
    ms_lo = jnp.sum(jnp.where(lo_half, sq, 0.0), axis=-1, keepdims=True) * (1.0 / HEAD_DIM)
    ms_hi = jnp.sum(jnp.where(lo_half, 0.0, sq), axis=-1, keepdims=True) * (1.0 / HEAD_DIM)
    inv = jnp.where(lo_half, lax.rsqrt(ms_lo + NORM_EPS), lax.rsqrt(ms_hi + NORM_EPS))
    return x * inv * w


def _in_proj_kernel(x_ref, nw_ref, w_ref, wg_hi_ref, wg_lo_ref, br_ref, qw_ref, kw_ref,
                    qa_ref, ka_ref, vb_ref, projm_ref, gt_ref, gtm_ref, carry_ref):
    i = pl.program_id(0)
    x = x_ref[...]
    ms = jnp.mean(x * x, axis=-1, keepdims=True)
    y = x * lax.rsqrt(ms + NORM_EPS) * nw_ref[...]
    h_hi, h_lo = _split_hi_lo(y)
    main = jnp.dot(h_hi, w_ref[...], preferred_element_type=F32)

    w_hi, w_lo = wg_hi_ref[...], wg_lo_ref[...]
    g = (jnp.dot(h_hi, w_hi, preferred_element_type=F32)
         + jnp.dot(h_hi, w_lo, preferred_element_type=F32)
         + jnp.dot(h_lo, w_hi, preferred_element_type=F32))
    g = g + br_ref[...]
    lane = lax.broadcasted_iota(jnp.int32, g.shape, 1)
    is_input_gate = (lane >= FOX_HEADS) & (lane < FOX_HEADS + MLSTM_HEADS)
    gates = jnp.where(is_input_gate, g, _log_sigmoid(g))
    gtm_ref[...] = gates
    gt_ref[...] = gates.T[:GATE_ROWS, :]

    @pl.when(i % IN_TILES_PER_SEQ == 0)
    def _():
        carry_ref[...] = jnp.zeros_like(carry_ref)

    rowi = lax.broadcasted_iota(jnp.int32, gates.shape, 0)
    c = gates
    for s in IN_SCAN_SHIFTS:
        c = c + jnp.where(rowi >= s, pltpu.roll(c, s, axis=0), 0.0)
    c = c + carry_ref[...]
    carry_ref[...] = c[IN_TM - 1:IN_TM, :]
    cum2 = c * LOG2E

    lo_half = lane < HEAD_DIM
    hi_half = jnp.logical_not(lo_half)
    q_scale = (HEAD_DIM ** -0.5) * LOG2E
    for p in range(FOX_HEADS // 2):
        qn = _pair_rms(main[:, p * LANES:(p + 1) * LANES], qw_ref[...], lo_half) * q_scale
        kn = _pair_rms(main[:, FOX_WIDTH + p * LANES:FOX_WIDTH + (p + 1) * LANES], kw_ref[...], lo_half)
        for j in range(2):
            h = 2 * p + j
            own = lo_half if j == 0 else hi_half
            base = HEAD_DIM if j == 0 else 0
            cum = jnp.sum(jnp.where(lane == h, cum2, 0.0), axis=1, keepdims=True)
            c_hi = cum.astype(BF16).astype(F32)
            r1 = cum - c_hi
            c_mid = r1.astype(BF16).astype(F32)
            c_lo = (r1 - c_mid).astype(BF16).astype(F32)
            q_aug = jnp.zeros_like(qn)
            k_aug = jnp.zeros_like(kn)
            for t, piece in enumerate((c_hi, c_mid, c_lo)):
                q_aug = jnp.where(lane == base + t, piece, q_aug)
                k_aug = jnp.where(lane == base + 3 + t, -piece, k_aug)
            q_aug = jnp.where((lane >= base + 3) & (lane < base + 6), 1.0, q_aug)
            k_aug = jnp.where((lane >= base) & (lane < base + 3), 1.0, k_aug)
            qa_ref[:, h * LANES:(h + 1) * LANES] = jnp.where(own, qn, q_aug).astype(BF16)
            ka_ref[:, h * LANES:(h + 1) * LANES] = jnp.where(own, kn, k_aug).astype(BF16)

    vb_ref[...] = main[:, 2 * FOX_WIDTH:3 * FOX_WIDTH].astype(BF16)
    projm_ref[...] = main[:, 3 * FOX_WIDTH:]


def _in_proj(x2d, norm_w, w_main, wg_hi, wg_lo, bias_r, qw, kw):
    n = x2d.shape[0]
    tm = IN_TM
    const = lambda i: (0, 0)
    row = lambda w: pl.BlockSpec((tm, w), lambda i: (i, 0))
    vmem = (2 * (tm * D_MODEL * 4 + D_MODEL * MAIN_WIDTH * 2 + 2 * tm * FOX_PAIR_WIDTH * 2
                 + tm * FOX_WIDTH * 2 + tm * MLSTM_COLS * 4 + GATE_ROWS * tm * 4 + tm * LANES * 4)
            + 2 * tm * MAIN_WIDTH * 4)
    return pl.pallas_call(
        _in_proj_kernel,
        grid=(n // tm,),
        in_specs=[
            row(D_MODEL),
            pl.BlockSpec((1, D_MODEL), const),
            pl.BlockSpec((D_MODEL, MAIN_WIDTH), const),
            pl.BlockSpec((D_MODEL, LANES), const),
            pl.BlockSpec((D_MODEL, LANES), const),
            pl.BlockSpec((1, LANES), const),
            pl.BlockSpec((1, LANES), const),
            pl.BlockSpec((1, LANES), const),
        ],
        out_specs=[
            row(FOX_PAIR_WIDTH), row(FOX_PAIR_WIDTH), row(FOX_WIDTH), row(MLSTM_COLS),
            pl.BlockSpec((GATE_ROWS, tm), lambda i: (0, i)),
            row(LANES),
        ],
        out_shape=[
            jax.ShapeDtypeStruct((n, FOX_PAIR_WIDTH), BF16),
            jax.ShapeDtypeStruct((n, FOX_PAIR_WIDTH), BF16),
            jax.ShapeDtypeStruct((n, FOX_WIDTH), BF16),
            jax.ShapeDtypeStruct((n, MLSTM_COLS), F32),
            jax.ShapeDtypeStruct((GATE_ROWS, n), F32),
            jax.ShapeDtypeStruct((n, LANES), F32),
        ],
        scratch_shapes=[pltpu.VMEM((1, LANES), F32)],
        compiler_params=pltpu.CompilerParams(
            dimension_semantics=("arbitrary",), vmem_limit_bytes=_vmem_limit(vmem)),
        name="in_proj",
    )(x2d, norm_w, w_main, wg_hi, wg_lo, bias_r, qw, kw)


FOX_TQ = 256


def _fox_kernel(qa_ref, ka_ref, v_ref, ow_ref, o_ref):
    tri = (lax.broadcasted_iota(jnp.int32, (FOX_TQ, FOX_TQ), 1)
           <= lax.broadcasted_iota(jnp.int32, (FOX_TQ, FOX_TQ), 0))
    lo_q = lax.broadcasted_iota(jnp.int32, (FOX_TQ, LANES), 1) < HEAD_DIM
    for j in range(2):
        cs = slice(j * HEAD_DIM, (j + 1) * HEAD_DIM)
        hs = slice(j * LANES, (j + 1) * LANES)
        own_q = lo_q if j == 0 else jnp.logical_not(lo_q)
        for i in range(SEQ // FOX_TQ):
            qs = slice(i * FOX_TQ, (i + 1) * FOX_TQ)
            n = (i + 1) * FOX_TQ
            s = lax.dot_general(qa_ref[0, qs, hs], ka_ref[0, :n, hs], NT_DIMS,
                                preferred_element_type=F32)
            diag = jnp.where(tri, s[:, n - FOX_TQ:], -jnp.inf)
            m = jnp.max(diag, axis=-1, keepdims=True)
            if i > 0:
                past = s[:, :n - FOX_TQ]
                m = jnp.maximum(m, jnp.max(past, axis=-1, keepdims=True))
                p = jnp.concatenate([jnp.exp2(past - m), jnp.exp2(diag - m)], axis=1)
            else:
                p = jnp.exp2(diag - m)
            l = jnp.sum(p, axis=-1, keepdims=True)
            o = jnp.dot(p.astype(BF16), v_ref[0, :n, :], preferred_element_type=F32) / l
            ms = jnp.sum(jnp.where(own_q, o * o, 0.0), axis=-1, keepdims=True) * (1.0 / HEAD_DIM)
            on = o * lax.rsqrt(ms + NORM_EPS) * ow_ref[...]
            o_ref[0, qs, cs] = on[:, cs]


def _fox(qa3, ka3, v3, ow):
    b = qa3.shape[0]
    nq = FOX_WIDTH // LANES
    pair_blk = (1, SEQ, 2 * LANES)
    blk = (1, SEQ, LANES)
    vmem = 2 * (2 * SEQ * 2 * LANES * 2 + SEQ * LANES * 2 + SEQ * LANES * 4) + 8 * FOX_TQ * SEQ * 4
    return pl.pallas_call(
        _fox_kernel,
        grid=(b, nq),
        in_specs=[
            pl.BlockSpec(pair_blk, lambda bi, hp: (bi, 0, hp)),
            pl.BlockSpec(pair_blk, lambda bi, hp: (bi, 0, hp)),
            pl.BlockSpec(blk, lambda bi, hp: (bi, 0, hp)),
            pl.BlockSpec((1, LANES), lambda bi, hp: (0, hp)),
        ],
        out_specs=pl.BlockSpec(blk, lambda bi, hp: (bi, 0, hp)),
        out_shape=jax.ShapeDtypeStruct((b, SEQ, FOX_WIDTH), F32),
        compiler_params=pltpu.CompilerParams(
            dimension_semantics=("arbitrary", "arbitrary"), vmem_limit_bytes=_vmem_limit(vmem)),
        name="fox",
    )(qa3, ka3, v3, ow)


ML_L = MLSTM_CHUNK
ML_PAIRS = SEQ // (2 * ML_L)
ML_HL = FOX_HEADS
ML_AUG = MLSTM_V_WIDTH + LANES
ML_TILE = 256
SEG_SHIFTS = tuple(1 << i for i in range(int(math.log2(ML_L))))


def _split3(x):
    a = x.astype(BF16)
    r = x - a.astype(F32)
    b = r.astype(BF16)
    c = (r - b.astype(F32)).astype(BF16)
    return a, b, c


def _expand_heads(x, exp_bf):
    a, b, c = _split3(x)
    return (jnp.dot(a, exp_bf, preferred_element_type=F32)
            + jnp.dot(b, exp_bf, preferred_element_type=F32)
            + jnp.dot(c, exp_bf, preferred_element_type=F32))


def _seg_scan(x, axis, op, ident):
    idx = lax.broadcasted_iota(jnp.int32, x.shape, axis) % ML_L
    for s in SEG_SHIFTS:
        x = op(x, jnp.where(idx >= s, pltpu.roll(x, s, axis=axis), ident))
    return x


def _mlstm_kernel(q_ref, k_ref, v_ref, gi_ref, gf_ref, gtm_ref, cw_ref, o_ref, den_ref,
                  qc_ref, kc_ref, kt_ref, rr_ref, cmr_ref, bcr_ref,
                  ealpha_ref, ew_ref, wint_ref, floor_ref, mfull_ref, caug_ref):
    def conv_silu(u, w):
        rowi = lax.broadcasted_iota(jnp.int32, u.shape, 0)
        acc = u * w[CONV_WIDTH - 1:CONV_WIDTH, :]
        for d in range(1, CONV_WIDTH):
            sh = jnp.where(rowi >= d, pltpu.roll(u, d, axis=0), 0.0)
            acc = acc + sh * w[CONV_WIDTH - 1 - d:CONV_WIDTH - d, :]
        return acc / (1.0 + jnp.exp(-acc))

    cw = cw_ref[...]
    qc_ref[...] = conv_silu(q_ref[0], cw[:, :MLSTM_QK_WIDTH]).astype(BF16)
    kc = conv_silu(k_ref[0], cw[:, MLSTM_QK_WIDTH:]) * (MLSTM_QK_DIM ** -0.5)
    kc_ref[...] = kc.astype(BF16)
    kt = kc.T
    for p in range(ML_PAIRS):
        kt_ref[p] = kt[:, p * LANES:(p + 1) * LANES]

    bcum_r = _seg_scan(gf_ref[...], 1, jnp.add, 0.0)
    r_r = gi_ref[...] - bcum_r
    cmx_r = _seg_scan(r_r, 1, jnp.maximum, -jnp.inf)
    for p in range(ML_PAIRS):
        ls = slice(p * LANES, (p + 1) * LANES)
        rr_ref[p] = r_r[:, ls]
        cmr_ref[p] = cmx_r[:, ls]
        bcr_ref[p] = bcum_r[:, ls]

    g = gtm_ref[...]
    lane_g = lax.broadcasted_iota(jnp.int32, g.shape, 1)
    head_lane = (lane_g >= ML_HL) & (lane_g < ML_HL + MLSTM_HEADS)
    bcum_c = jnp.where(
        head_lane, pltpu.roll(_seg_scan(g, 0, jnp.add, 0.0), LANES - MLSTM_HEADS, axis=1), 0.0)
    cmx_c = _seg_scan(jnp.where(head_lane, g, 0.0) - bcum_c, 0, jnp.maximum, -jnp.inf)
    m = jnp.zeros((1, LANES), F32)
    for c in range(SEQ // ML_L):
        mfull_ref[c * ML_L:(c + 1) * ML_L, :] = jnp.broadcast_to(m, (ML_L, LANES))
        last = (c + 1) * ML_L - 1
        m = bcum_c[last:last + 1, :] + jnp.maximum(m, cmx_c[last:last + 1, :])
    mfull = mfull_ref[...]
    mx = jnp.maximum(mfull, cmx_c)
    wint_ref[...] = jnp.exp(mfull - mx)
    floor_ref[...] = jnp.exp(-(bcum_c + mx))
    mfull_ref[...] = -mx

    lane_e = lax.broadcasted_iota(jnp.int32, (LANES, MLSTM_V_WIDTH), 1) // MLSTM_V_DIM
    row_e = lax.broadcasted_iota(jnp.int32, (LANES, MLSTM_V_WIDTH), 0)
    exp_bf = jnp.where(row_e == lane_e + ML_HL, 1.0, 0.0).astype(BF16)

    def expand_tile(i, carry):
        rows = pl.ds(pl.multiple_of(i * ML_TILE, ML_TILE), ML_TILE)
        ealpha_ref[rows, :] = _expand_heads(mfull_ref[rows, :], exp_bf)
        ew_ref[rows, :] = _expand_heads(wint_ref[rows, :], exp_bf)
        return carry

    lax.fori_loop(0, SEQ // ML_TILE, expand_tile, 0)

    kb_rowh = lax.broadcasted_iota(jnp.int32, (MLSTM_HEADS * ML_L, MLSTM_QK_WIDTH), 0) // ML_L
    kb_lane = lax.broadcasted_iota(jnp.int32, (MLSTM_HEADS * ML_L, MLSTM_QK_WIDTH), 1) // MLSTM_QK_DIM
    mask_k = jnp.where(kb_rowh == kb_lane, 1.0, 0.0).astype(BF16)
    va_rowh = lax.broadcasted_iota(jnp.int32, (MLSTM_HEADS * ML_L, ML_AUG), 0) // ML_L
    va_col = lax.broadcasted_iota(jnp.int32, (MLSTM_HEADS * ML_L, ML_AUG), 1)
    mask_v = jnp.where(
        (va_col // MLSTM_V_DIM == va_rowh) | (va_col == MLSTM_V_WIDTH + ML_HL + va_rowh),
        1.0, 0.0).astype(BF16)
    c_rowh = lax.broadcasted_iota(jnp.int32, (MLSTM_QK_WIDTH, ML_AUG), 0) // MLSTM_QK_DIM
    c_col = lax.broadcasted_iota(jnp.int32, (MLSTM_QK_WIDTH, ML_AUG), 1)
    mask_c = (c_col // MLSTM_V_DIM == c_rowh) | (c_col == MLSTM_V_WIDTH + ML_HL + c_rowh)
    lane128 = lax.broadcasted_iota(jnp.int32, (ML_L, LANES), 1)
    s_idx = lax.broadcasted_iota(jnp.int32, (ML_L, MLSTM_V_WIDTH), 1) % ML_L
    t_idx = lax.broadcasted_iota(jnp.int32, (ML_L, MLSTM_V_WIDTH), 0)
    causal = s_idx <= t_idx
    ones_aug = jnp.ones((ML_L, LANES), F32)

    caug_ref[...] = jnp.zeros_like(caug_ref)

    def pair_body(cp, m_row):
        r2 = rr_ref[cp]
        cm2 = cmr_ref[cp]
        b2 = bcr_ref[cp]
        r2r = pltpu.roll(r2, ML_L, axis=1)
        kt2 = kt_ref[cp]
        for cc in range(2):
            lo = cc * ML_L
            rows = pl.ds(pl.multiple_of(cp * (2 * ML_L), 2 * ML_L) + lo, ML_L)
            qa = qc_ref[rows, :]
            ka = kc_ref[rows, :]
            va_aug = jnp.concatenate([v_ref[0, rows, :], ones_aug], axis=1).astype(BF16)

            kbd = jnp.concatenate([ka] * MLSTM_HEADS, axis=0) * mask_k
            s = lax.dot_general(qa, kbd, NT_DIMS, preferred_element_type=F32)

            cmx_last = cm2[:, lo + ML_L - 1:lo + ML_L]
            b_last = b2[:, lo + ML_L - 1:lo + ML_L]
            mx_r = jnp.maximum(m_row, cmx_last)
            decay = jnp.exp(m_row - mx_r)
            wk = jnp.exp(r2[:, lo:lo + ML_L] - mx_r)
            m_row = b_last + mx_r

            src_e, src_o = (r2, r2r) if cc == 0 else (r2r, r2)
            cols = []
            for p in range(MLSTM_HEADS // 2):
                even = jnp.broadcast_to(src_e[2 * p:2 * p + 1, :], (ML_L, LANES))
                odd = jnp.broadcast_to(src_o[2 * p + 1:2 * p + 2, :], (ML_L, LANES))
                cols.append(jnp.where(lane128 < ML_L, even, odd))
            r_all = jnp.concatenate(cols, axis=1)
            arg = jnp.where(causal, ealpha_ref[rows, :] + r_all, -jnp.inf)
            p_all = (s * jnp.exp(arg)).astype(BF16)

            vbd = jnp.concatenate([va_aug] * MLSTM_HEADS, axis=0) * mask_v
            pv = jnp.dot(p_all, vbd, preferred_element_type=F32)
            qc_state = jnp.dot(qa, caug_ref[...].astype(BF16), preferred_element_type=F32)
            o_ref[0, rows, :] = (ew_ref[rows, :] * qc_state[:, :MLSTM_V_WIDTH]
                                 + pv[:, :MLSTM_V_WIDTH])
            den_ref[rows, :] = (wint_ref[rows, :] * qc_state[:, MLSTM_V_WIDTH:]
                                + pv[:, MLSTM_V_WIDTH:])

            wk_rows = jnp.concatenate(
                [jnp.broadcast_to(wk[h:h + 1, :], (MLSTM_QK_DIM, ML_L)) for h in range(MLSTM_HEADS)],
                axis=0)
            dec_rows = jnp.concatenate(
                [jnp.broadcast_to(decay[h:h + 1, :], (MLSTM_QK_DIM, 1)) for h in range(MLSTM_HEADS)],
                axis=0)
            ktw = (kt2[:, lo:lo + ML_L] * wk_rows).astype(BF16)
            upd = jnp.dot(ktw, va_aug, preferred_element_type=F32)
            caug_ref[...] = dec_rows * caug_ref[...] + jnp.where(mask_c, upd, 0.0)
        return m_row

    lax.fori_loop(0, ML_PAIRS, pair_body, jnp.zeros((MLSTM_HEADS, 1), F32))

    ob_row = lax.broadcasted_iota(jnp.int32, (MLSTM_V_WIDTH, LANES), 0) // MLSTM_V_DIM
    ob_col = lax.broadcasted_iota(jnp.int32, (MLSTM_V_WIDTH, LANES), 1)
    ones_bd = jnp.where(ob_col == ob_row + ML_HL, 1.0, 0.0).astype(BF16)

    def norm_tile(i, carry):
        rows = pl.ds(pl.multiple_of(i * ML_TILE, ML_TILE), ML_TILE)
        num = o_ref[0, rows, :]
        dn = jnp.maximum(jnp.abs(den_ref[rows, :]), floor_ref[rows, :])
        r = 1.0 / dn
        n2_hi, n2_lo = _split_hi_lo(num * num)
        msn = (jnp.dot(n2_hi, ones_bd, preferred_element_type=F32)
               + jnp.dot(n2_lo, ones_bd, preferred_element_type=F32)) * (1.0 / MLSTM_V_DIM)
        fac = r * lax.rsqrt(r * r * msn + NORM_EPS)
        o_ref[0, rows, :] = num * _expand_heads(fac, exp_bf)
        return carry

    lax.fori_loop(0, SEQ // ML_TILE, norm_tile, 0)


def _mlstm(projm3, gt, gtm, conv_w):
    b = projm3.shape[0]
    qk_blk = (1, SEQ, MLSTM_QK_WIDTH)
    v_blk = (1, SEQ, MLSTM_V_WIDTH)
    v_col = 2 * MLSTM_QK_WIDTH // MLSTM_V_WIDTH
    vmem = (2 * (2 * SEQ * MLSTM_QK_WIDTH * 4 + 2 * SEQ * MLSTM_V_WIDTH * 4 + 2 * SEQ * LANES * 4)
            + 2 * SEQ * MLSTM_QK_WIDTH * 2 + SEQ * MLSTM_QK_WIDTH * 4 + 2 * SEQ * MLSTM_V_WIDTH * 4
            + 3 * SEQ * LANES * 4 + (8 << 20))
    return pl.pallas_call(
        _mlstm_kernel,
        grid=(b,),
        in_specs=[
            pl.BlockSpec(qk_blk, lambda bi: (bi, 0, 0)),
            pl.BlockSpec(qk_blk, lambda bi: (bi, 0, 1)),
            pl.BlockSpec(v_blk, lambda bi: (bi, 0, v_col)),
            pl.BlockSpec((SUBLANES, SEQ), lambda bi: (1, bi)),
            pl.BlockSpec((SUBLANES, SEQ), lambda bi: (2, bi)),
            pl.BlockSpec((SEQ, LANES), lambda bi: (bi, 0)),
            pl.BlockSpec((CONV_WIDTH, 2 * MLSTM_QK_WIDTH), lambda bi: (0, 0)),
        ],
        out_specs=pl.BlockSpec(v_blk, lambda bi: (bi, 0, 0)),
        out_shape=jax.ShapeDtypeStruct((b, SEQ, MLSTM_V_WIDTH), F32),
        scratch_shapes=[
            pltpu.VMEM((SEQ, LANES), F32),
            pltpu.VMEM((SEQ, MLSTM_QK_WIDTH), BF16),
            pltpu.VMEM((SEQ, MLSTM_QK_WIDTH), BF16),
            pltpu.VMEM((ML_PAIRS, MLSTM_QK_WIDTH, LANES), F32),
            pltpu.VMEM((ML_PAIRS, SUBLANES, LANES), F32),
            pltpu.VMEM((ML_PAIRS, SUBLANES, LANES), F32),
            pltpu.VMEM((ML_PAIRS, SUBLANES, LANES), F32),
            pltpu.VMEM((SEQ, MLSTM_V_WIDTH), F32),
            pltpu.VMEM((SEQ, MLSTM_V_WIDTH), F32),
            pltpu.VMEM((SEQ, LANES), F32),
            pltpu.VMEM((SEQ, LANES), F32),
            pltpu.VMEM((SEQ, LANES), F32),
            pltpu.VMEM((MLSTM_QK_WIDTH, ML_AUG), F32),
        ],
        compiler_params=pltpu.CompilerParams(
            dimension_semantics=("arbitrary",), vmem_limit_bytes=_vmem_limit(vmem)),
        name="mlstm",
    )(projm3, projm3, projm3, gt, gt, gtm, conv_w)


RT_TM = 512


def _out_route_kernel(x_ref, fy_ref, my_ref, mo_ref, wo_ref, mg_ref, nw_ref, wr_hi_ref, wr_lo_ref,
                      rb_ref, x1_ref, h2_ref, eidx_ref, gate_ref, rank_ref, cnt_ref, carry_ref):
    i = pl.program_id(0)

    @pl.when(i == 0)
    def _():
        carry_ref[...] = jnp.zeros_like(carry_ref)

    my = my_ref[...] * mg_ref[...] / (1.0 + jnp.exp(-mo_ref[...]))
    mixed = (jnp.dot(fy_ref[...].astype(BF16), wo_ref[:FOX_WIDTH, :], preferred_element_type=F32)
             + jnp.dot(my.astype(BF16), wo_ref[FOX_WIDTH:, :], preferred_element_type=F32))
    x1 = x_ref[...] + mixed
    x1_ref[...] = x1
    h2 = _rms(x1, nw_ref[...])
    _rows_to_tiles(h2, h2_ref)

    h_hi, h_lo = _split_hi_lo(h2)
    wr_hi, wr_lo = wr_hi_ref[...], wr_lo_ref[...]
    logit = (lax.dot_general(wr_hi, h_hi, NT_DIMS, preferred_element_type=F32)
             + lax.dot_general(wr_lo, h_hi, NT_DIMS, preferred_element_type=F32)
             + lax.dot_general(wr_hi, h_lo, NT_DIMS, preferred_element_type=F32)) + rb_ref[...]

    e_iota = lax.broadcasted_iota(jnp.int32, logit.shape, 0).astype(F32)
    vals, idxs, hots = [], [], []
    for _ in range(TOP_K):
        mk = jnp.max(logit, axis=0, keepdims=True)
        idx = jnp.min(jnp.where(logit == mk, e_iota, float(N_EXPERTS)), axis=0, keepdims=True)
        hot = e_iota == idx
        logit = jnp.where(hot, -jnp.inf, logit)
        vals.append(mk)
        idxs.append(idx.astype(jnp.int32))
        hots.append(hot)
    exps = [jnp.exp(v - vals[0]) for v in vals]
    tot = exps[0] + exps[1] + exps[2] + exps[3]
    gates = [e / tot for e in exps]

    assign = jnp.zeros(logit.shape, F32)
    for hot in hots:
        assign = assign + jnp.where(hot, 1.0, 0.0)
    tm = logit.shape[1]
    src = lax.broadcasted_iota(jnp.int32, (tm, tm), 0)
    dst = lax.broadcasted_iota(jnp.int32, (tm, tm), 1)
    upper = jnp.where(src < dst, 1.0, 0.0).astype(BF16)
    base = jnp.dot(assign.astype(BF16), upper, preferred_element_type=F32) + carry_ref[:, 0:1]
    ranks = [jnp.sum(jnp.where(hot, base, 0.0), axis=0, keepdims=True) for hot in hots]
    new_carry = carry_ref[...] + jnp.sum(assign, axis=1, keepdims=True)
    carry_ref[...] = new_carry
    cnt_ref[...] = new_carry

    zi = jnp.zeros((SUBLANES - TOP_K, tm), jnp.int32)
    eidx_ref[...] = jnp.concatenate(idxs + [zi], axis=0)
    rank_ref[...] = jnp.concatenate([r.astype(jnp.int32) for r in ranks] + [zi], axis=0)
    gate_ref[...] = jnp.concatenate(gates + [zi.astype(F32)], axis=0)


def _out_route(x2d, fox_y2d, mlstm_y2d, projm, w_out_bf, mlstm_gain, moe_norm_w, wr_hi, wr_lo, rb):
    n = x2d.shape[0]
    tm = RT_TM
    const = lambda i: (0, 0)
    mo_col = (MLSTM_COLS - MLSTM_V_WIDTH) // MLSTM_V_WIDTH
    row_blk = lambda w: pl.BlockSpec((tm, w), lambda i: (i, 0))
    lane_blk = pl.BlockSpec((SUBLANES, tm), lambda i: (0, i))
    vmem = (2 * (tm * D_MODEL * 4 * 3 + tm * FOX_WIDTH * 4 * 3 + D_MODEL * D_MODEL * 2)
            + 6 * tm * D_MODEL * 4 + tm * tm * 6)
    return pl.pallas_call(
        _out_route_kernel,
        grid=(n // tm,),
        in_specs=[
            row_blk(D_MODEL), row_blk(FOX_WIDTH), row_blk(MLSTM_V_WIDTH),
            pl.BlockSpec((tm, MLSTM_V_WIDTH), lambda i: (i, mo_col)),
            pl.BlockSpec((D_MODEL, D_MODEL), const),
            pl.BlockSpec((1, MLSTM_V_WIDTH), const),
            pl.BlockSpec((1, D_MODEL), const),
            pl.BlockSpec((N_EXPERTS, D_MODEL), const),
            pl.BlockSpec((N_EXPERTS, D_MODEL), const),
            pl.BlockSpec((N_EXPERTS, 1), const),
        ],
        out_specs=[row_blk(D_MODEL), pl.BlockSpec((tm * TOK_ROWS, LANES), lambda i: (i, 0)),
                   lane_blk, lane_blk, lane_blk, pl.BlockSpec((N_EXPERTS, LANES), const)],
        out_shape=[
            jax.ShapeDtypeStruct((n, D_MODEL), F32),
            jax.ShapeDtypeStruct((n * TOK_ROWS, LANES), F32),
            jax.ShapeDtypeStruct((SUBLANES, n), jnp.int32),
            jax.ShapeDtypeStruct((SUBLANES, n), F32),
            jax.ShapeDtypeStruct((SUBLANES, n), jnp.int32),
            jax.ShapeDtypeStruct((N_EXPERTS, LANES), F32),
        ],
        scratch_shapes=[pltpu.VMEM((N_EXPERTS, LANES), F32)],
        compiler_params=pltpu.CompilerParams(
            dimension_semantics=("arbitrary",), vmem_limit_bytes=_vmem_limit(vmem)),
        name="out_route",
    )(x2d, fox_y2d, mlstm_y2d, projm, w_out_bf, mlstm_gain, moe_norm_w, wr_hi, wr_lo, rb)


INV_CHUNK = 8192
INV_UNROLL = 8


def _invert_kernel(pos_ref, zeros_hbm, inv_ref, sem):
    i = pl.program_id(0)

    @pl.when(i == 0)
    def _():
        cp = pltpu.make_async_copy(zeros_hbm, inv_ref, sem.at[0])
        cp.start()
        cp.wait()

    base = i * INV_CHUNK

    def body(j, carry):
        inv_ref[pos_ref[0, 0, j]] = base + j
        return carry

    lax.fori_loop(0, INV_CHUNK, body, 0, unroll=INV_UNROLL)


def _invert(pos_flat, n_rows):
    n_slots = pos_flat.shape[0]
    steps = n_slots // INV_CHUNK
    return pl.pallas_call(
        _invert_kernel,
        grid=(steps,),
        in_specs=[
            pl.BlockSpec((1, 1, INV_CHUNK), lambda i: (i, 0, 0), memory_space=pltpu.SMEM),
            pl.BlockSpec(memory_space=pl.ANY),
        ],
        out_specs=pl.BlockSpec(memory_space=pltpu.SMEM),
        out_shape=jax.ShapeDtypeStruct((n_rows,), jnp.int32),
        scratch_shapes=[pltpu.SemaphoreType.DMA((1,))],
        compiler_params=pltpu.CompilerParams(dimension_semantics=("arbitrary",)),
        name="invert",
    )(pos_flat.reshape(steps, 1, INV_CHUNK), jnp.zeros((n_rows,), jnp.int32))


EX_BM = 256
EX_DRAIN_STEPS = 2


def _experts_kernel(be_ref, nu_ref, nv_ref, tok_ref, tokn_ref, dst_ref, h2_hbm, wgu_ref, bgu_ref,
                    wd_ref, bd_ref, y_hbm, xt_ref, yt_ref, wgu_bf_ref, wd_bf_ref, gsem, ssem):
    i = pl.program_id(0)
    last_blk = pl.num_programs(0) - 1 - EX_DRAIN_STEPS
    nu = nu_ref[0]
    slot = i % 2
    cur = jnp.minimum(i, last_blk)
    tile_rows = EX_BM * TOK_ROWS

    def tok_tile(ref, idx):
        return ref.at[pl.ds(pl.multiple_of(idx, TOK_ROWS), TOK_ROWS), :]

    def start_gather(idx_ref, s):
        for r in range(EX_BM):
            pltpu.make_async_copy(tok_tile(h2_hbm, idx_ref[0, 0, r]),
                                  xt_ref.at[s, pl.ds(r * TOK_ROWS, TOK_ROWS), :], gsem.at[s]).start()

    def scatter_row(r):
        pltpu.make_async_copy(yt_ref.at[slot, pl.ds(r * TOK_ROWS, TOK_ROWS), :],
                              tok_tile(y_hbm, dst_ref[0, 0, r]), ssem.at[slot]).start()

    def wait_block(sem, buf):
        pltpu.make_async_copy(h2_hbm.at[pl.ds(0, tile_rows), :], buf, sem).wait()

    def wait_scatter(s, n):
        @pl.when(n == EX_BM)
        def _():
            wait_block(ssem.at[s], yt_ref.at[s])

        @pl.when(n < EX_BM)
        def _():
            def body(r, carry):
                pltpu.make_async_copy(h2_hbm.at[pl.ds(0, TOK_ROWS), :],
                                      yt_ref.at[s, pl.ds(0, TOK_ROWS), :], ssem.at[s]).wait()
                return carry
            lax.fori_loop(0, n, body, 0)

    @pl.when(i == 0)
    def _():
        start_gather(tok_ref, 0)

    @pl.when(i + 1 < nu)
    def _():
        start_gather(tokn_ref, 1 - slot)

    @pl.when((i >= 2) & (i - 2 < nu))
    def _():
        wait_scatter(slot, nv_ref[jnp.clip(i - 2, 0, last_blk)])

    @pl.when(i < nu)
    def _():
        @pl.when((i == 0) | (be_ref[cur] != be_ref[jnp.maximum(cur - 1, 0)]))
        def _():
            wgu_bf_ref[...] = wgu_ref[0].astype(BF16)
            wd_bf_ref[...] = wd_ref[0].astype(BF16)

        wait_block(gsem.at[slot], xt_ref.at[slot])
        xb = _tiles_to_rows(xt_ref.at[slot], EX_BM).astype(BF16)
        gu = jnp.dot(xb, wgu_bf_ref[...], preferred_element_type=F32) + bgu_ref[0]
        gate = jnp.minimum(gu[:, :D_EXPERT], SWIGLU_LIMIT)
        up = jnp.clip(gu[:, D_EXPERT:], -SWIGLU_LIMIT, SWIGLU_LIMIT)
        act = (up + 1.0) * (gate / (1.0 + jnp.exp(-SWIGLU_ALPHA * gate)))
        y = jnp.dot(act.astype(BF16), wd_bf_ref[...], preferred_element_type=F32) + bd_ref[0]
        _rows_to_tiles(y, yt_ref.at[slot])
        nv = nv_ref[cur]

        @pl.when(nv == EX_BM)
        def _():
            for r in range(EX_BM):
                scatter_row(r)

        @pl.when(nv < EX_BM)
        def _():
            def body(r, carry):
                pltpu.make_async_copy(
                    yt_ref.at[slot, pl.ds(pl.multiple_of(r * TOK_ROWS, TOK_ROWS), TOK_ROWS), :],
                    tok_tile(y_hbm, dst_ref[0, 0, r]), ssem.at[slot]).start()
                return carry
            lax.fori_loop(0, nv, body, 0)


def _experts(block_e, n_used, n_valid, buf_tok3, dst3, h2, w_gu, b_gu, w_down, b_down, n_slots):
    nb = buf_tok3.shape[0]
    idx_blk = lambda f: pl.BlockSpec((1, 1, EX_BM), f, memory_space=pltpu.SMEM)
    vmem = (2 * (D_MODEL * 2 * D_EXPERT * 4 + D_EXPERT * D_MODEL * 4)
            + D_MODEL * 2 * D_EXPERT * 2 + D_EXPERT * D_MODEL * 2
            + 3 * EX_BM * D_MODEL * 4 + 3 * EX_BM * 2 * D_EXPERT * 4)
    blk = lambda i: jnp.minimum(i, nb - 1)
    w_map = lambda i, be, nu, nv: (be[blk(i)], 0, 0)
    grid_spec = pltpu.PrefetchScalarGridSpec(
        num_scalar_prefetch=3,
        grid=(nb + EX_DRAIN_STEPS,),
        in_specs=[
            idx_blk(lambda i, be, nu, nv: (blk(i), 0, 0)),
            idx_blk(lambda i, be, nu, nv: (blk(i + 1), 0, 0)),
            idx_blk(lambda i, be, nu, nv: (blk(i), 0, 0)),
            pl.BlockSpec(memory_space=pl.ANY),
            pl.BlockSpec((1, D_MODEL, 2 * D_EXPERT), w_map),
            pl.BlockSpec((1, 1, 2 * D_EXPERT), w_map),
            pl.BlockSpec((1, D_EXPERT, D_MODEL), w_map),
            pl.BlockSpec((1, 1, D_MODEL), w_map),
        ],
        out_specs=pl.BlockSpec(memory_space=pl.ANY),
        scratch_shapes=[
            pltpu.VMEM((2, EX_BM * TOK_ROWS, LANES), F32),
            pltpu.VMEM((2, EX_BM * TOK_ROWS, LANES), F32),
            pltpu.VMEM((D_MODEL, 2 * D_EXPERT), BF16),
            pltpu.VMEM((D_EXPERT, D_MODEL), BF16),
            pltpu.SemaphoreType.DMA((2,)),
            pltpu.SemaphoreType.DMA((2,)),
        ],
    )
    return pl.pallas_call(
        _experts_kernel,
        grid_spec=grid_spec,
        out_shape=jax.ShapeDtypeStruct((n_slots * TOK_ROWS, LANES), F32),
        compiler_params=pltpu.CompilerParams(
            dimension_semantics=("arbitrary",), vmem_limit_bytes=_vmem_limit(vmem)),
        name="experts",
    )(block_e, n_used, n_valid, buf_tok3, buf_tok3, dst3, h2, w_gu, b_gu[:, None, :], w_down,
      b_down[:, None, :])


CB_TM = 256


def _combine_kernel(x1_ref, y0_ref, y1_ref, y2_ref, y3_ref, gate_ref, o_ref):
    tm = x1_ref.shape[0]
    g = jnp.concatenate([gate_ref[...], jnp.zeros((LANES - SUBLANES, tm), F32)], axis=0).T
    acc = x1_ref[...]
    for k, y_ref in enumerate((y0_ref, y1_ref, y2_ref, y3_ref)):
        acc = acc + g[:, k:k + 1] * _tiles_to_rows(y_ref, tm)
    o_ref[...] = acc


def _combine(x1, y_slots, gates):
    n = x1.shape[0]
    tm = CB_TM
    nt = n // tm
    vmem = 2 * (2 * tm * D_MODEL * 4 + tm * TOP_K * D_MODEL * 4) + 4 * tm * D_MODEL * 4
    y_spec = lambda k: pl.BlockSpec((tm * TOK_ROWS, LANES), lambda i: (k * nt + i, 0))
    return pl.pallas_call(
        _combine_kernel,
        grid=(nt,),
        in_specs=[pl.BlockSpec((tm, D_MODEL), lambda i: (i, 0))]
        + [y_spec(k) for k in range(TOP_K)]
        + [pl.BlockSpec((SUBLANES, tm), lambda i: (0, i))],
        out_specs=pl.BlockSpec((tm, D_MODEL), lambda i: (i, 0)),
        out_shape=jax.ShapeDtypeStruct((n, D_MODEL), F32),
        compiler_params=pltpu.CompilerParams(
            dimension_semantics=("arbitrary",), vmem_limit_bytes=_vmem_limit(vmem)),
        name="combine",
    )(x1, y_slots, y_slots, y_slots, y_slots, gates)


def _dispatch_plan(eidx, rank, counts):
    n = eidx.shape[1]
    n_slots = n * TOP_K
    nb = n_slots // EX_BM + N_EXPERTS
    counts = counts.astype(jnp.int32)
    padded = ((counts + EX_BM - 1) // EX_BM) * EX_BM
    pad_end = jnp.cumsum(padded)
    pad_start = pad_end - padded
    e = eidx[:TOP_K]
    start_of = jnp.sum(jnp.where(e[:, :, None] == jnp.arange(N_EXPERTS, dtype=jnp.int32),
                                 pad_start[None, None, :], 0), axis=-1)
    pos = start_of + rank[:TOP_K]
    inv = _invert(pos.reshape(-1), nb * EX_BM)
    buf_tok = (inv % n) * TOK_ROWS
    dst = inv * TOK_ROWS
    blk_start = jnp.arange(nb, dtype=jnp.int32) * EX_BM
    block_e = jnp.minimum(jnp.sum((pad_end[None, :] <= blk_start[:, None]).astype(jnp.int32), axis=1),
                          N_EXPERTS - 1)
    n_used = (pad_end[-1] // EX_BM).astype(jnp.int32).reshape(1)
    n_valid = jnp.clip(pad_start[block_e] + counts[block_e] - blk_start, 0, EX_BM).astype(jnp.int32)
    n_valid = jnp.where(blk_start < pad_end[-1], n_valid, 0)
    return (block_e, n_used, n_valid, buf_tok.reshape(nb, 1, EX_BM), dst.reshape(nb, 1, EX_BM), n_slots)


def _prep_in_proj_weights(w_in, fox_f_bias, mlstm_i_bias, mlstm_f_bias):
    split_at = []
    acc = 0
    for wdt in SPLIT_WIDTHS[:-1]:
        acc += wdt
        split_at.append(acc)
    fq, fk, fv, ff, mq, mk, mv, mi, mf, mo = jnp.split(w_in, split_at, axis=-1)
    w_main = jnp.concatenate([fq, fk, fv, mq, mk, mv, mo], axis=-1).astype(BF16)
    w_gate = jnp.concatenate([ff, mi, mf], axis=-1)
    bias = jnp.concatenate([fox_f_bias, mlstm_i_bias, mlstm_f_bias]).astype(F32)
    n_gate = w_gate.shape[1]
    wg_hi, wg_lo = _split_hi_lo(jnp.pad(w_gate, ((0, 0), (0, LANES - n_gate))))
    bias_r = jnp.pad(bias, (0, LANES - n_gate))[None, :]
    return w_main, wg_hi, wg_lo, bias_r


def kernel(x, attn_norm_w, w_in, fox_f_bias, fox_q_norm_w, fox_k_norm_w, fox_out_norm_w, mlstm_conv_w, mlstm_i_bias, mlstm_f_bias, mlstm_out_norm_w, w_out, moe_norm_w, router_w, router_b, expert_w_gate_up, expert_b_gate_up, expert_w_down, expert_b_down):
    bsz, seq, d = x.shape
    x2d = x.reshape(bsz * seq, d)
    prep = _prep_in_proj_weights(w_in[0], fox_f_bias[0], mlstm_i_bias[0], mlstm_f_bias[0])
    pair = lambda w: jnp.tile(w, LANES // HEAD_DIM)[None, :]
    qa, ka, vb, projm, gt, gtm = _in_proj(x2d, attn_norm_w[0][None, :], *prep,
                                          pair(fox_q_norm_w[0]), pair(fox_k_norm_w[0]))
    b3 = lambda a: a.reshape(bsz, seq, a.shape[-1])
    fox_y = _fox(b3(qa), b3(ka), b3(vb), fox_out_norm_w[0][None, :])
    mlstm_y = _mlstm(b3(projm), gt, gtm, mlstm_conv_w[0])
    return _channel_mixer(x2d, fox_y.reshape(-1, FOX_WIDTH), mlstm_y.reshape(-1, MLSTM_V_WIDTH), projm,
                          mlstm_out_norm_w[0], w_out[0], moe_norm_w[0], router_w[0], router_b[0],
                          expert_w_gate_up[0], expert_b_gate_up[0], expert_w_down[0],
                          expert_b_down[0]).reshape(bsz, seq, d)


def _channel_mixer(x2d, fox_y2d, mlstm_y2d, projm, mlstm_gain, w_out, moe_norm_w, router_w, router_b,
                   w_gu, b_gu, w_down, b_down):
    wr_hi, wr_lo = _split_hi_lo(router_w.T)
    x1, h2, eidx, gates, rank, counts = _out_route(
        x2d, fox_y2d, mlstm_y2d, projm, w_out.astype(BF16), mlstm_gain[None, :], moe_norm_w[None, :],
        wr_hi, wr_lo, router_b[:, None])
    block_e, n_used, n_valid, buf_tok3, dst3, n_slots = _dispatch_plan(eidx, rank, counts[:, 0])
    y_slots = _experts(block_e, n_used, n_valid, buf_tok3, dst3, h2, w_gu, b_gu, w_down, b_down, n_slots)
    return _combine(x1, y_slots, gates)
```

```python
import functools
import math

import jax
import jax.numpy as jnp
from jax import lax
from jax.experimental import pallas as pl
from jax.experimental.pallas import tpu as pltpu

F32 = jnp.float32
BF16 = jnp.bfloat16

D_MODEL = 1024
SEQ = 2048
HEAD_DIM = 64
FOX_HEADS = 8
FOX_WIDTH = FOX_HEADS * HEAD_DIM
MLSTM_HEADS = 8
MLSTM_QK_DIM = 32
MLSTM_V_DIM = 64
MLSTM_QK_WIDTH = MLSTM_HEADS * MLSTM_QK_DIM
MLSTM_V_WIDTH = MLSTM_HEADS * MLSTM_V_DIM
CONV_WIDTH = 4
MLSTM_CHUNK = 64
SPLIT_WIDTHS = (FOX_WIDTH, FOX_WIDTH, FOX_WIDTH, FOX_HEADS,
                MLSTM_QK_WIDTH, MLSTM_QK_WIDTH, MLSTM_V_WIDTH,
                MLSTM_HEADS, MLSTM_HEADS, MLSTM_V_WIDTH)
N_EXPERTS = 32
TOP_K = 4
D_EXPERT = D_MODEL
SWIGLU_ALPHA = 1.702
SWIGLU_LIMIT = 7.0
NORM_EPS = 1e-5
LOG2E = 1.4426950408889634

LANES = 128
SUBLANES = 8
V7X_VMEM_BYTES = 64 * 1024 * 1024

MAIN_WIDTH = 3 * FOX_WIDTH + 2 * MLSTM_QK_WIDTH + 2 * MLSTM_V_WIDTH
MLSTM_COLS = MAIN_WIDTH - 3 * FOX_WIDTH
GATE_ROWS = 32

NT_DIMS = (((1,), (1,)), ((), ()))


def _vmem_limit(nbytes):
    return int(min(nbytes + (8 << 20), V7X_VMEM_BYTES - (4 << 20)))


def _log_sigmoid(x):
    return jnp.minimum(x, 0.0) - jnp.log(1.0 + jnp.exp(-jnp.abs(x)))


def _split_hi_lo(x):
    hi = x.astype(BF16)
    lo = (x - hi.astype(F32)).astype(BF16)
    return hi, lo


def _rms(x, w):
    return x * lax.rsqrt(jnp.mean(x * x, axis=-1, keepdims=True) + NORM_EPS) * w


TOK_ROWS = D_MODEL // LANES


def _rows_to_tiles(x, tile_ref):
    m = x.shape[0]
    for j in range(TOK_ROWS):
        tile_ref[pl.ds(j, m, stride=TOK_ROWS), :] = x[:, j * LANES:(j + 1) * LANES]


def _tiles_to_rows(tile_ref, m):
    return jnp.concatenate(
        [tile_ref[pl.ds(j, m, stride=TOK_ROWS), :] for j in range(TOK_ROWS)], axis=1)


IN_TM = 512
IN_TILES_PER_SEQ = SEQ // IN_TM
IN_SCAN_SHIFTS = tuple(1 << i for i in range(int(math.log2(IN_TM))))


def _pair_rms(x, w, lo_half):
    sq = x * x
    ms_lo = jnp.sum(jnp.where(lo_half, sq, 0.0), axis=-1, keepdims=True) * (1.0 / HEAD_DIM)
    ms_hi = jnp.sum(jnp.where(lo_half, 0.0, sq), axis=-1, keepdims=True) * (1.0 / HEAD_DIM)
    inv = jnp.where(lo_half, lax.rsqrt(ms_lo + NORM_EPS), lax.rsqrt(ms_hi + NORM_EPS))
    return x * inv * w


def _in_proj_kernel(x_ref, nw_ref, w_ref, wg_hi_ref, wg_lo_ref, br_ref, qw_ref, kw_ref,
                    qn_ref, kn_ref, aq_ref, ak_ref, vb_ref, projm_ref, gt_ref, gtm_ref, carry_ref):
    i = pl.program_id(0)
    x = x_ref[...]
    ms = jnp.mean(x * x, axis=-1, keepdims=True)
    y = x * lax.rsqrt(ms + NORM_EPS) * nw_ref[...]
    h_hi, h_lo = _split_hi_lo(y)
    main = jnp.dot(h_hi, w_ref[...], preferred_element_type=F32)

    w_hi, w_lo = wg_hi_ref[...], wg_lo_ref[...]
    g = (jnp.dot(h_hi, w_hi, preferred_element_type=F32)
         + jnp.dot(h_hi, w_lo, preferred_element_type=F32)
         + jnp.dot(h_lo, w_hi, preferred_element_type=F32))
    g = g + br_ref[...]
    lane = lax.broadcasted_iota(jnp.int32, g.shape, 1)
    is_input_gate = (lane >= FOX_HEADS) & (lane < FOX_HEADS + MLSTM_HEADS)
    gates = jnp.where(is_input_gate, g, _log_sigmoid(g))
    gtm_ref[...] = gates
    gt_ref[...] = gates.T[:GATE_ROWS, :]

    @pl.when(i % IN_TILES_PER_SEQ == 0)
    def _():
        carry_ref[...] = jnp.zeros_like(carry_ref)

    rowi = lax.broadcasted_iota(jnp.int32, gates.shape, 0)
    c = gates
    for s in IN_SCAN_SHIFTS:
        c = c + jnp.where(rowi >= s, pltpu.roll(c, s, axis=0), 0.0)
    c = c + carry_ref[...]
    carry_ref[...] = c[IN_TM - 1:IN_TM, :]
    cum2 = c * LOG2E

    c8 = jnp.where(lane < FOX_HEADS, cum2, 0.0)
    c_hi = c8.astype(BF16).astype(F32)
    r1 = c8 - c_hi
    c_mid = r1.astype(BF16).astype(F32)
    c_lo = (r1 - c_mid).astype(BF16).astype(F32)
    ones_q = jnp.where((lane >= 3 * FOX_HEADS) & (lane < 6 * FOX_HEADS), 1.0, 0.0)
    ones_k = jnp.where(lane < 3 * FOX_HEADS, 1.0, 0.0)
    aq = c_hi + pltpu.roll(c_mid, FOX_HEADS, axis=1) + pltpu.roll(c_lo, 2 * FOX_HEADS, axis=1) + ones_q
    ak = ones_k - (pltpu.roll(c_hi, 3 * FOX_HEADS, axis=1) + pltpu.roll(c_mid, 4 * FOX_HEADS, axis=1)
                   + pltpu.roll(c_lo, 5 * FOX_HEADS, axis=1))
    aq_ref[...] = aq.astype(BF16)
    ak_ref[...] = ak.astype(BF16)

    lo_half = lane < HEAD_DIM
    q_scale = (HEAD_DIM ** -0.5) * LOG2E
    for p in range(FOX_HEADS // 2):
        ps = slice(p * LANES, (p + 1) * LANES)
        qn_ref[:, ps] = (_pair_rms(main[:, ps], qw_ref[...], lo_half) * q_scale).astype(BF16)
        kn_ref[:, ps] = _pair_rms(main[:, FOX_WIDTH + p * LANES:FOX_WIDTH + (p + 1) * LANES],
                                  kw_ref[...], lo_half).astype(BF16)

    vb_ref[...] = main[:, 2 * FOX_WIDTH:3 * FOX_WIDTH].astype(BF16)
    projm_ref[...] = main[:, 3 * FOX_WIDTH:]


def _in_proj(x2d, norm_w, w_main, wg_hi, wg_lo, bias_r, qw, kw):
    n = x2d.shape[0]
    tm = IN_TM
    const = lambda i: (0, 0)
    row = lambda w: pl.BlockSpec((tm, w), lambda i: (i, 0))
    vmem = (2 * (tm * D_MODEL * 4 + D_MODEL * MAIN_WIDTH * 2 + 3 * tm * FOX_WIDTH * 2 + 2 * tm * LANES * 2
                 + tm * MLSTM_COLS * 4 + GATE_ROWS * tm * 4 + tm * LANES * 4)
            + 2 * tm * MAIN_WIDTH * 4)
    return pl.pallas_call(
        _in_proj_kernel,
        grid=(n // tm,),
        in_specs=[
            row(D_MODEL),
            pl.BlockSpec((1, D_MODEL), const),
            pl.BlockSpec((D_MODEL, MAIN_WIDTH), const),
            pl.BlockSpec((D_MODEL, LANES), const),
            pl.BlockSpec((D_MODEL, LANES), const),
            pl.BlockSpec((1, LANES), const),
            pl.BlockSpec((1, LANES), const),
            pl.BlockSpec((1, LANES), const),
        ],
        out_specs=[
            row(FOX_WIDTH), row(FOX_WIDTH), row(LANES), row(LANES), row(FOX_WIDTH), row(MLSTM_COLS),
            pl.BlockSpec((GATE_ROWS, tm), lambda i: (0, i)),
            row(LANES),
        ],
        out_shape=[
            jax.ShapeDtypeStruct((n, FOX_WIDTH), BF16),
            jax.ShapeDtypeStruct((n, FOX_WIDTH), BF16),
            jax.ShapeDtypeStruct((n, LANES), BF16),
            jax.ShapeDtypeStruct((n, LANES), BF16),
            jax.ShapeDtypeStruct((n, FOX_WIDTH), BF16),
            jax.ShapeDtypeStruct((n, MLSTM_COLS), F32),
            jax.ShapeDtypeStruct((GATE_ROWS, n), F32),
            jax.ShapeDtypeStruct((n, LANES), F32),
        ],
        scratch_shapes=[pltpu.VMEM((1, LANES), F32)],
        compiler_params=pltpu.CompilerParams(
            dimension_semantics=("arbitrary",), vmem_limit_bytes=_vmem_limit(vmem)),
        name="in_proj",
    )(x2d, norm_w, w_main, wg_hi, wg_lo, bias_r, qw, kw)


FOX_TQ = 256


def _fox_kernel(qn_ref, kn_ref, aq_ref, ak_ref, v_ref, ow_ref, o_ref, qa_ref, ka_ref):
    hp = pl.program_id(1)
    tri = (lax.broadcasted_iota(jnp.int32, (FOX_TQ, FOX_TQ), 1)
           <= lax.broadcasted_iota(jnp.int32, (FOX_TQ, FOX_TQ), 0))
    lo_q = lax.broadcasted_iota(jnp.int32, (FOX_TQ, LANES), 1) < HEAD_DIM
    lane = lax.broadcasted_iota(jnp.int32, (1, LANES), 1)
    ka_ref[:, :LANES] = kn_ref[0]
    ka_ref[:, LANES:] = ak_ref[0]
    for j in range(2):
        cs = slice(j * HEAD_DIM, (j + 1) * HEAD_DIM)
        own_q = lo_q if j == 0 else jnp.logical_not(lo_q)
        h = 2 * hp + j
        own_mask = jnp.where((lane < HEAD_DIM) if j == 0 else (lane >= HEAD_DIM), 1.0, 0.0).astype(BF16)
        bias_mask = jnp.where((lane < 6 * FOX_HEADS) & (lane % FOX_HEADS == h), 1.0, 0.0).astype(BF16)
        qa_ref[j, :, :LANES] = qn_ref[0] * own_mask
        qa_ref[j, :, LANES:] = aq_ref[0] * bias_mask
        for i in range(SEQ // FOX_TQ):
            qs = slice(i * FOX_TQ, (i + 1) * FOX_TQ)
            n = (i + 1) * FOX_TQ
            s = lax.dot_general(qa_ref[j, qs, :], ka_ref[:n, :], NT_DIMS,
                                preferred_element_type=F32)
            diag = jnp.where(tri, s[:, n - FOX_TQ:], -jnp.inf)
            m = jnp.max(diag, axis=-1, keepdims=True)
            if i > 0:
                past = s[:, :n - FOX_TQ]
                m = jnp.maximum(m, jnp.max(past, axis=-1, keepdims=True))
                p = jnp.concatenate([jnp.exp2(past - m), jnp.exp2(diag - m)], axis=1)
            else:
                p = jnp.exp2(diag - m)
            l = jnp.sum(p, axis=-1, keepdims=True)
            o = jnp.dot(p.astype(BF16), v_ref[0, :n, :], preferred_element_type=F32) / l
            ms = jnp.sum(jnp.where(own_q, o * o, 0.0), axis=-1, keepdims=True) * (1.0 / HEAD_DIM)
            on = o * lax.rsqrt(ms + NORM_EPS) * ow_ref[...]
            o_ref[0, qs, cs] = on[:, cs]


def _fox(qn3, kn3, aq3, ak3, v3, ow):
    b = qn3.shape[0]
    nq = FOX_WIDTH // LANES
    blk = (1, SEQ, LANES)
    pair = pl.BlockSpec(blk, lambda bi, hp: (bi, 0, hp))
    shared = pl.BlockSpec(blk, lambda bi, hp: (bi, 0, 0))
    vmem = 2 * (5 * SEQ * LANES * 2 + SEQ * LANES * 4) + 3 * SEQ * 2 * LANES * 2 + 8 * FOX_TQ * SEQ * 4
    return pl.pallas_call(
        _fox_kernel,
        grid=(b, nq),
        in_specs=[pair, pair, shared, shared, pair, pl.BlockSpec((1, LANES), lambda bi, hp: (0, hp))],
        out_specs=pl.BlockSpec(blk, lambda bi, hp: (bi, 0, hp)),
        out_shape=jax.ShapeDtypeStruct((b, SEQ, FOX_WIDTH), F32),
        scratch_shapes=[pltpu.VMEM((2, SEQ, 2 * LANES), BF16), pltpu.VMEM((SEQ, 2 * LANES), BF16)],
        compiler_params=pltpu.CompilerParams(
            dimension_semantics=("arbitrary", "arbitrary"), vmem_limit_bytes=_vmem_limit(vmem)),
        name="fox",
    )(qn3, kn3, aq3, ak3, v3, ow)


ML_L = MLSTM_CHUNK
ML_PAIRS = SEQ // (2 * ML_L)
ML_HL = FOX_HEADS
ML_AUG = MLSTM_V_WIDTH + LANES
ML_TILE = 256
SEG_SHIFTS = tuple(1 << i for i in range(int(math.log2(ML_L))))


def _split3(x):
    a = x.astype(BF16)
    r = x - a.astype(F32)
    b = r.astype(BF16)
    c = (r - b.astype(F32)).astype(BF16)
    return a, b, c


def _expand_heads(x, exp_bf):
    a, b, c = _split3(x)
    return (jnp.dot(a, exp_bf, preferred_element_type=F32)
            + jnp.dot(b, exp_bf, preferred_element_type=F32)
            + jnp.dot(c, exp_bf, preferred_element_type=F32))


def _seg_scan(x, axis, op, ident):
    idx = lax.broadcasted_iota(jnp.int32, x.shape, axis) % ML_L
    for s in SEG_SHIFTS:
        x = op(x, jnp.where(idx >= s, pltpu.roll(x, s, axis=axis), ident))
    return x


def _mlstm_kernel(q_ref, k_ref, v_ref, gi_ref, gf_ref, gtm_ref, cw_ref, o_ref, den_ref,
                  qc_ref, kc_ref, kt_ref, rr_ref, cmr_ref, bcr_ref,
                  ealpha_ref, ew_ref, wint_ref, floor_ref, mfull_ref, caug_ref):
    def conv_silu(u, w):
        rowi = lax.broadcasted_iota(jnp.int32, u.shape, 0)
        acc = u * w[CONV_WIDTH - 1:CONV_WIDTH, :]
        for d in range(1, CONV_WIDTH):
            sh = jnp.where(rowi >= d, pltpu.roll(u, d, axis=0), 0.0)
            acc = acc + sh * w[CONV_WIDTH - 1 - d:CONV_WIDTH - d, :]
        return acc / (1.0 + jnp.exp(-acc))

    cw = cw_ref[...]
    qc_ref[...] = conv_silu(q_ref[0], cw[:, :MLSTM_QK_WIDTH]).astype(BF16)
    kc = conv_silu(k_ref[0], cw[:, MLSTM_QK_WIDTH:]) * (MLSTM_QK_DIM ** -0.5)
    kc_ref[...] = kc.astype(BF16)
    kt = kc.T
    for p in range(ML_PAIRS):
        kt_ref[p] = kt[:, p * LANES:(p + 1) * LANES]

    bcum_r = _seg_scan(gf_ref[...], 1, jnp.add, 0.0)
    r_r = gi_ref[...] - bcum_r
    cmx_r = _seg_scan(r_r, 1, jnp.maximum, -jnp.inf)
    for p in range(ML_PAIRS):
        ls = slice(p * LANES, (p + 1) * LANES)
        rr_ref[p] = r_r[:, ls]
        cmr_ref[p] = cmx_r[:, ls]
        bcr_ref[p] = bcum_r[:, ls]

    g = gtm_ref[...]
    lane_g = lax.broadcasted_iota(jnp.int32, g.shape, 1)
    head_lane = (lane_g >= ML_HL) & (lane_g < ML_HL + MLSTM_HEADS)
    bcum_c = jnp.where(
        head_lane, pltpu.roll(_seg_scan(g, 0, jnp.add, 0.0), LANES - MLSTM_HEADS, axis=1), 0.0)
    cmx_c = _seg_scan(jnp.where(head_lane, g, 0.0) - bcum_c, 0, jnp.maximum, -jnp.inf)
    m = jnp.zeros((1, LANES), F32)
    for c in range(SEQ // ML_L):
        mfull_ref[c * ML_L:(c + 1) * ML_L, :] = jnp.broadcast_to(m, (ML_L, LANES))
        last = (c + 1) * ML_L - 1
        m = bcum_c[last:last + 1, :] + jnp.maximum(m, cmx_c[last:last + 1, :])
    mfull = mfull_ref[...]
    mx = jnp.maximum(mfull, cmx_c)
    wint_ref[...] = jnp.exp(mfull - mx)
    floor_ref[...] = jnp.exp(-(bcum_c + mx))
    mfull_ref[...] = -mx

    lane_e = lax.broadcasted_iota(jnp.int32, (LANES, MLSTM_V_WIDTH), 1) // MLSTM_V_DIM
    row_e = lax.broadcasted_iota(jnp.int32, (LANES, MLSTM_V_WIDTH), 0)
    exp_bf = jnp.where(row_e == lane_e + ML_HL, 1.0, 0.0).astype(BF16)

    def expand_tile(i, carry):
        rows = pl.ds(pl.multiple_of(i * ML_TILE, ML_TILE), ML_TILE)
        ealpha_ref[rows, :] = _expand_heads(mfull_ref[rows, :], exp_bf)
        ew_ref[rows, :] = _expand_heads(wint_ref[rows, :], exp_bf)
        return carry

    lax.fori_loop(0, SEQ // ML_TILE, expand_tile, 0)

    kb_rowh = lax.broadcasted_iota(jnp.int32, (MLSTM_HEADS * ML_L, MLSTM_QK_WIDTH), 0) // ML_L
    kb_lane = lax.broadcasted_iota(jnp.int32, (MLSTM_HEADS * ML_L, MLSTM_QK_WIDTH), 1) // MLSTM_QK_DIM
    mask_k = jnp.where(kb_rowh == kb_lane, 1.0, 0.0).astype(BF16)
    va_rowh = lax.broadcasted_iota(jnp.int32, (MLSTM_HEADS * ML_L, ML_AUG), 0) // ML_L
    va_col = lax.broadcasted_iota(jnp.int32, (MLSTM_HEADS * ML_L, ML_AUG), 1)
    mask_v = jnp.where(
        (va_col // MLSTM_V_DIM == va_rowh) | (va_col == MLSTM_V_WIDTH + ML_HL + va_rowh),
        1.0, 0.0).astype(BF16)
    c_rowh = lax.broadcasted_iota(jnp.int32, (MLSTM_QK_WIDTH, ML_AUG), 0) // MLSTM_QK_DIM
    c_col = lax.broadcasted_iota(jnp.int32, (MLSTM_QK_WIDTH, ML_AUG), 1)
    mask_c = (c_col // MLSTM_V_DIM == c_rowh) | (c_col == MLSTM_V_WIDTH + ML_HL + c_rowh)
    lane128 = lax.broadcasted_iota(jnp.int32, (ML_L, LANES), 1)
    s_idx = lax.broadcasted_iota(jnp.int32, (ML_L, MLSTM_V_WIDTH), 1) % ML_L
    t_idx = lax.broadcasted_iota(jnp.int32, (ML_L, MLSTM_V_WIDTH), 0)
    causal = s_idx <= t_idx
    ones_aug = jnp.ones((ML_L, LANES), F32)

    caug_ref[...] = jnp.zeros_like(caug_ref)

    def pair_body(cp, m_row):
        r2 = rr_ref[cp]
        cm2 = cmr_ref[cp]
        b2 = bcr_ref[cp]
        r2r = pltpu.roll(r2, ML_L, axis=1)
        kt2 = kt_ref[cp]
        for cc in range(2):
            lo = cc * ML_L
            rows = pl.ds(pl.multiple_of(cp * (2 * ML_L), 2 * ML_L) + lo, ML_L)
            qa = qc_ref[rows, :]
            ka = kc_ref[rows, :]
            va_aug = jnp.concatenate([v_ref[0, rows, :], ones_aug], axis=1).astype(BF16)

            kbd = jnp.concatenate([ka] * MLSTM_HEADS, axis=0) * mask_k
            s = lax.dot_general(qa, kbd, NT_DIMS, preferred_element_type=F32)

            cmx_last = cm2[:, lo + ML_L - 1:lo + ML_L]
            b_last = b2[:, lo + ML_L - 1:lo + ML_L]
            mx_r = jnp.maximum(m_row, cmx_last)
            decay = jnp.exp(m_row - mx_r)
            wk = jnp.exp(r2[:, lo:lo + ML_L] - mx_r)
            m_row = b_last + mx_r

            src_e, src_o = (r2, r2r) if cc == 0 else (r2r, r2)
            cols = []
            for p in range(MLSTM_HEADS // 2):
                even = jnp.broadcast_to(src_e[2 * p:2 * p + 1, :], (ML_L, LANES))
                odd = jnp.broadcast_to(src_o[2 * p + 1:2 * p + 2, :], (ML_L, LANES))
                cols.append(jnp.where(lane128 < ML_L, even, odd))
            r_all = jnp.concatenate(cols, axis=1)
            arg = jnp.where(causal, ealpha_ref[rows, :] + r_all, -jnp.inf)
            p_all = (s * jnp.exp(arg)).astype(BF16)

            vbd = jnp.concatenate([va_aug] * MLSTM_HEADS, axis=0) * mask_v
            pv = jnp.dot(p_all, vbd, preferred_element_type=F32)
            qc_state = jnp.dot(qa, caug_ref[...].astype(BF16), preferred_element_type=F32)
            o_ref[0, rows, :] = (ew_ref[rows, :] * qc_state[:, :MLSTM_V_WIDTH]
                                 + pv[:, :MLSTM_V_WIDTH])
            den_ref[rows, :] = (wint_ref[rows, :] * qc_state[:, MLSTM_V_WIDTH:]
                                + pv[:, MLSTM_V_WIDTH:])

            wk_rows = jnp.concatenate(
                [jnp.broadcast_to(wk[h:h + 1, :], (MLSTM_QK_DIM, ML_L)) for h in range(MLSTM_HEADS)],
                axis=0)
            dec_rows = jnp.concatenate(
                [jnp.broadcast_to(decay[h:h + 1, :], (MLSTM_QK_DIM, 1)) for h in range(MLSTM_HEADS)],
                axis=0)
            ktw = (kt2[:, lo:lo + ML_L] * wk_rows).astype(BF16)
            upd = jnp.dot(ktw, va_aug, preferred_element_type=F32)
            caug_ref[...] = dec_rows * caug_ref[...] + jnp.where(mask_c, upd, 0.0)
        return m_row

    lax.fori_loop(0, ML_PAIRS, pair_body, jnp.zeros((MLSTM_HEADS, 1), F32))

    ob_row = lax.broadcasted_iota(jnp.int32, (MLSTM_V_WIDTH, LANES), 0) // MLSTM_V_DIM
    ob_col = lax.broadcasted_iota(jnp.int32, (MLSTM_V_WIDTH, LANES), 1)
    ones_bd = jnp.where(ob_col == ob_row + ML_HL, 1.0, 0.0).astype(BF16)

    def norm_tile(i, carry):
        rows = pl.ds(pl.multiple_of(i * ML_TILE, ML_TILE), ML_TILE)
        num = o_ref[0, rows, :]
        dn = jnp.maximum(jnp.abs(den_ref[rows, :]), floor_ref[rows, :])
        r = 1.0 / dn
        n2_hi, n2_lo = _split_hi_lo(num * num)
        msn = (jnp.dot(n2_hi, ones_bd, preferred_element_type=F32)
               + jnp.dot(n2_lo, ones_bd, preferred_element_type=F32)) * (1.0 / MLSTM_V_DIM)
        fac = r * lax.rsqrt(r * r * msn + NORM_EPS)
        o_ref[0, rows, :] = num * _expand_heads(fac, exp_bf)
        return carry

    lax.fori_loop(0, SEQ // ML_TILE, norm_tile, 0)


def _mlstm(projm3, gt, gtm, conv_w):
    b = projm3.shape[0]
    qk_blk = (1, SEQ, MLSTM_QK_WIDTH)
    v_blk = (1, SEQ, MLSTM_V_WIDTH)
    v_col = 2 * MLSTM_QK_WIDTH // MLSTM_V_WIDTH
    vmem = (2 * (2 * SEQ * MLSTM_QK_WIDTH * 4 + 2 * SEQ * MLSTM_V_WIDTH * 4 + 2 * SEQ * LANES * 4)
            + 2 * SEQ * MLSTM_QK_WIDTH * 2 + SEQ * MLSTM_QK_WIDTH * 4 + 2 * SEQ * MLSTM_V_WIDTH * 4
            + 3 * SEQ * LANES * 4 + (8 << 20))
    return pl.pallas_call(
        _mlstm_kernel,
        grid=(b,),
        in_specs=[
            pl.BlockSpec(qk_blk, lambda bi: (bi, 0, 0)),
            pl.BlockSpec(qk_blk, lambda bi: (bi, 0, 1)),
            pl.BlockSpec(v_blk, lambda bi: (bi, 0, v_col)),
            pl.BlockSpec((SUBLANES, SEQ), lambda bi: (1, bi)),
            pl.BlockSpec((SUBLANES, SEQ), lambda bi: (2, bi)),
            pl.BlockSpec((SEQ, LANES), lambda bi: (bi, 0)),
            pl.BlockSpec((CONV_WIDTH, 2 * MLSTM_QK_WIDTH), lambda bi: (0, 0)),
        ],
        out_specs=pl.BlockSpec(v_blk, lambda bi: (bi, 0, 0)),
        out_shape=jax.ShapeDtypeStruct((b, SEQ, MLSTM_V_WIDTH), F32),
        scratch_shapes=[
            pltpu.VMEM((SEQ, LANES), F32),
            pltpu.VMEM((SEQ, MLSTM_QK_WIDTH), BF16),
            pltpu.VMEM((SEQ, MLSTM_QK_WIDTH), BF16),
            pltpu.VMEM((ML_PAIRS, MLSTM_QK_WIDTH, LANES), F32),
            pltpu.VMEM((ML_PAIRS, SUBLANES, LANES), F32),
            pltpu.VMEM((ML_PAIRS, SUBLANES, LANES), F32),
            pltpu.VMEM((ML_PAIRS, SUBLANES, LANES), F32),
            pltpu.VMEM((SEQ, MLSTM_V_WIDTH), F32),
            pltpu.VMEM((SEQ, MLSTM_V_WIDTH), F32),
            pltpu.VMEM((SEQ, LANES), F32),
            pltpu.VMEM((SEQ, LANES), F32),
            pltpu.VMEM((SEQ, LANES), F32),
            pltpu.VMEM((MLSTM_QK_WIDTH, ML_AUG), F32),
        ],
        compiler_params=pltpu.CompilerParams(
            dimension_semantics=("arbitrary",), vmem_limit_bytes=_vmem_limit(vmem)),
        name="mlstm",
    )(projm3, projm3, projm3, gt, gt, gtm, conv_w)


RT_TM = 512


def _out_route_kernel(x_ref, fy_ref, my_ref, mo_ref, wo_ref, mg_ref, nw_ref, wr_hi_ref, wr_lo_ref,
                      rb_ref, x1_ref, h2_ref, eidx_ref, gate_ref, rank_ref, cnt_ref, carry_ref):
    i = pl.program_id(0)

    @pl.when(i == 0)
    def _():
        carry_ref[...] = jnp.zeros_like(carry_ref)

    my = my_ref[...] * mg_ref[...] / (1.0 + jnp.exp(-mo_ref[...]))
    mixed = (jnp.dot(fy_ref[...].astype(BF16), wo_ref[:FOX_WIDTH, :], preferred_element_type=F32)
             + jnp.dot(my.astype(BF16), wo_ref[FOX_WIDTH:, :], preferred_element_type=F32))
    x1 = x_ref[...] + mixed
    x1_ref[...] = x1
    h2 = _rms(x1, nw_ref[...])
    _rows_to_tiles(h2, h2_ref)

    h_hi, h_lo = _split_hi_lo(h2)
    wr_hi, wr_lo = wr_hi_ref[...], wr_lo_ref[...]
    logit = (lax.dot_general(wr_hi, h_hi, NT_DIMS, preferred_element_type=F32)
             + lax.dot_general(wr_lo, h_hi, NT_DIMS, preferred_element_type=F32)
             + lax.dot_general(wr_hi, h_lo, NT_DIMS, preferred_element_type=F32)) + rb_ref[...]

    e_iota = lax.broadcasted_iota(jnp.int32, logit.shape, 0).astype(F32)
    vals, idxs, hots = [], [], []
    for _ in range(TOP_K):
        mk = jnp.max(logit, axis=0, keepdims=True)
        idx = jnp.min(jnp.where(logit == mk, e_iota, float(N_EXPERTS)), axis=0, keepdims=True)
        hot = e_iota == idx
        logit = jnp.where(hot, -jnp.inf, logit)
        vals.append(mk)
        idxs.append(idx.astype(jnp.int32))
        hots.append(hot)
    exps = [jnp.exp(v - vals[0]) for v in vals]
    tot = exps[0] + exps[1] + exps[2] + exps[3]
    gates = [e / tot for e in exps]

    assign = jnp.zeros(logit.shape, F32)
    for hot in hots:
        assign = assign + jnp.where(hot, 1.0, 0.0)
    tm = logit.shape[1]
    src = lax.broadcasted_iota(jnp.int32, (tm, tm), 0)
    dst = lax.broadcasted_iota(jnp.int32, (tm, tm), 1)
    upper = jnp.where(src < dst, 1.0, 0.0).astype(BF16)
    base = jnp.dot(assign.astype(BF16), upper, preferred_element_type=F32) + carry_ref[:, 0:1]
    ranks = [jnp.sum(jnp.where(hot, base, 0.0), axis=0, keepdims=True) for hot in hots]
    new_carry = carry_ref[...] + jnp.sum(assign, axis=1, keepdims=True)
    carry_ref[...] = new_carry
    cnt_ref[...] = new_carry

    zi = jnp.zeros((SUBLANES - TOP_K, tm), jnp.int32)
    eidx_ref[...] = jnp.concatenate(idxs + [zi], axis=0)
    rank_ref[...] = jnp.concatenate([r.astype(jnp.int32) for r in ranks] + [zi], axis=0)
    gate_ref[...] = jnp.concatenate(gates + [zi.astype(F32)], axis=0)


def _out_route(x2d, fox_y2d, mlstm_y2d, projm, w_out_bf, mlstm_gain, moe_norm_w, wr_hi, wr_lo, rb):
    n = x2d.shape[0]
    tm = RT_TM
    const = lambda i: (0, 0)
    mo_col = (MLSTM_COLS - MLSTM_V_WIDTH) // MLSTM_V_WIDTH
    row_blk = lambda w: pl.BlockSpec((tm, w), lambda i: (i, 0))
    lane_blk = pl.BlockSpec((SUBLANES, tm), lambda i: (0, i))
    vmem = (2 * (tm * D_MODEL * 4 * 3 + tm * FOX_WIDTH * 4 * 3 + D_MODEL * D_MODEL * 2)
            + 6 * tm * D_MODEL * 4 + tm * tm * 6)
    return pl.pallas_call(
        _out_route_kernel,
        grid=(n // tm,),
        in_specs=[
            row_blk(D_MODEL), row_blk(FOX_WIDTH), row_blk(MLSTM_V_WIDTH),
            pl.BlockSpec((tm, MLSTM_V_WIDTH), lambda i: (i, mo_col)),
            pl.BlockSpec((D_MODEL, D_MODEL), const),
            pl.BlockSpec((1, MLSTM_V_WIDTH), const),
            pl.BlockSpec((1, D_MODEL), const),
            pl.BlockSpec((N_EXPERTS, D_MODEL), const),
            pl.BlockSpec((N_EXPERTS, D_MODEL), const),
            pl.BlockSpec((N_EXPERTS, 1), const),
        ],
        out_specs=[row_blk(D_MODEL), pl.BlockSpec((tm * TOK_ROWS, LANES), lambda i: (i, 0)),
                   lane_blk, lane_blk, lane_blk, pl.BlockSpec((N_EXPERTS, LANES), const)],
        out_shape=[
            jax.ShapeDtypeStruct((n, D_MODEL), F32),
            jax.ShapeDtypeStruct((n * TOK_ROWS, LANES), F32),
            jax.ShapeDtypeStruct((SUBLANES, n), jnp.int32),
            jax.ShapeDtypeStruct((SUBLANES, n), F32),
            jax.ShapeDtypeStruct((SUBLANES, n), jnp.int32),
            jax.ShapeDtypeStruct((N_EXPERTS, LANES), F32),
        ],
        scratch_shapes=[pltpu.VMEM((N_EXPERTS, LANES), F32)],
        compiler_params=pltpu.CompilerParams(
            dimension_semantics=("arbitrary",), vmem_limit_bytes=_vmem_limit(vmem)),
        name="out_route",
    )(x2d, fox_y2d, mlstm_y2d, projm, w_out_bf, mlstm_gain, moe_norm_w, wr_hi, wr_lo, rb)


INV_CHUNK = 8192
INV_UNROLL = 8


def _invert_kernel(pos_ref, zeros_hbm, inv_ref, sem):
    i = pl.program_id(0)

    @pl.when(i == 0)
    def _():
        cp = pltpu.make_async_copy(zeros_hbm, inv_ref, sem.at[0])
        cp.start()
        cp.wait()

    base = i * INV_CHUNK

    def body(j, carry):
        inv_ref[pos_ref[0, 0, j]] = base + j
        return carry

    lax.fori_loop(0, INV_CHUNK, body, 0, unroll=INV_UNROLL)


def _invert(pos_flat, n_rows):
    n_slots = pos_flat.shape[0]
    steps = n_slots // INV_CHUNK
    return pl.pallas_call(
        _invert_kernel,
        grid=(steps,),
        in_specs=[
            pl.BlockSpec((1, 1, INV_CHUNK), lambda i: (i, 0, 0), memory_space=pltpu.SMEM),
            pl.BlockSpec(memory_space=pl.ANY),
        ],
        out_specs=pl.BlockSpec(memory_space=pltpu.SMEM),
        out_shape=jax.ShapeDtypeStruct((n_rows,), jnp.int32),
        scratch_shapes=[pltpu.SemaphoreType.DMA((1,))],
        compiler_params=pltpu.CompilerParams(dimension_semantics=("arbitrary",)),
        name="invert",
    )(pos_flat.reshape(steps, 1, INV_CHUNK), jnp.zeros((n_rows,), jnp.int32))


EX_BM = 256
EX_DRAIN_STEPS = 2
EX_CHUNKS = 4


def _experts_kernel(be_ref, nu_ref, nv_ref, tok_ref, tokn_ref, dst_ref, h2_hbm, wgu_ref, bgu_ref,
                    wd_ref, bd_ref, y_hbm, xt_ref, yt_ref, wgu_bf_ref, wd_bf_ref, gsem, ssem):
    i = pl.program_id(0)
    last_blk = pl.num_programs(0) - 1 - EX_DRAIN_STEPS
    nu = nu_ref[0]
    slot = i % 2
    cur = jnp.minimum(i, last_blk)
    tile_rows = EX_BM * TOK_ROWS

    def tok_tile(ref, idx):
        return ref.at[pl.ds(pl.multiple_of(idx, TOK_ROWS), TOK_ROWS), :]

    def start_gather(idx_ref, s):
        for r in range(EX_BM):
            pltpu.make_async_copy(tok_tile(h2_hbm, idx_ref[0, 0, r]),
                                  xt_ref.at[s, pl.ds(r * TOK_ROWS, TOK_ROWS), :], gsem.at[s]).start()

    def scatter_row(r):
        pltpu.make_async_copy(yt_ref.at[slot, pl.ds(r * TOK_ROWS, TOK_ROWS), :],
                              tok_tile(y_hbm, dst_ref[0, 0, r]), ssem.at[slot]).start()

    def wait_block(sem, buf):
        pltpu.make_async_copy(h2_hbm.at[pl.ds(0, tile_rows), :], buf, sem).wait()

    def wait_scatter(s, n):
        @pl.when(n == EX_BM)
        def _():
            wait_block(ssem.at[s], yt_ref.at[s])

        @pl.when(n < EX_BM)
        def _():
            def body(r, carry):
                pltpu.make_async_copy(h2_hbm.at[pl.ds(0, TOK_ROWS), :],
                                      yt_ref.at[s, pl.ds(0, TOK_ROWS), :], ssem.at[s]).wait()
                return carry
            lax.fori_loop(0, n, body, 0)

    @pl.when(i == 0)
    def _():
        start_gather(tok_ref, 0)

    @pl.when(i + 1 < nu)
    def _():
        start_gather(tokn_ref, 1 - slot)

    @pl.when((i >= 2) & (i - 2 < nu))
    def _():
        wait_scatter(slot, nv_ref[jnp.clip(i - 2, 0, last_blk)])

    @pl.when(i < nu)
    def _():
        @pl.when((i == 0) | (be_ref[cur] != be_ref[jnp.maximum(cur - 1, 0)]))
        def _():
            wgu_bf_ref[...] = wgu_ref[0].astype(BF16)
            wd_bf_ref[...] = wd_ref[0].astype(BF16)

        wait_block(gsem.at[slot], xt_ref.at[slot])
        kc = D_MODEL // EX_CHUNKS
        tr = kc // LANES
        gu = bgu_ref[0]
        for c in range(EX_CHUNKS):
            xk = jnp.concatenate(
                [xt_ref[slot, pl.ds(c * tr + t, EX_BM, stride=TOK_ROWS), :] for t in range(tr)],
                axis=1).astype(BF16)
            gu = gu + jnp.dot(xk, wgu_bf_ref[c * kc:(c + 1) * kc, :], preferred_element_type=F32)
        gate = jnp.minimum(gu[:, :D_EXPERT], SWIGLU_LIMIT)
        up = jnp.clip(gu[:, D_EXPERT:], -SWIGLU_LIMIT, SWIGLU_LIMIT)
        act = ((up + 1.0) * (gate / (1.0 + jnp.exp(-SWIGLU_ALPHA * gate)))).astype(BF16)
        for c in range(EX_CHUNKS):
            yc = (jnp.dot(act, wd_bf_ref[:, c * kc:(c + 1) * kc], preferred_element_type=F32)
                  + bd_ref[0, :, c * kc:(c + 1) * kc])
            for t in range(tr):
                yt_ref[slot, pl.ds(c * tr + t, EX_BM, stride=TOK_ROWS), :] = (
                    yc[:, t * LANES:(t + 1) * LANES])
        nv = nv_ref[cur]

        @pl.when(nv == EX_BM)
        def _():
            for r in range(EX_BM):
                scatter_row(r)

        @pl.when(nv < EX_BM)
        def _():
            def body(r, carry):
                pltpu.make_async_copy(
                    yt_ref.at[slot, pl.ds(pl.multiple_of(r * TOK_ROWS, TOK_ROWS), TOK_ROWS), :],
                    tok_tile(y_hbm, dst_ref[0, 0, r]), ssem.at[slot]).start()
                return carry
            lax.fori_loop(0, nv, body, 0)


def _experts(block_e, n_used, n_valid, buf_tok3, dst3, h2, w_gu, b_gu, w_down, b_down, n_slots):
    nb = buf_tok3.shape[0]
    idx_blk = lambda f: pl.BlockSpec((1, 1, EX_BM), f, memory_space=pltpu.SMEM)
    vmem = (2 * (D_MODEL * 2 * D_EXPERT * 4 + D_EXPERT * D_MODEL * 4)
            + D_MODEL * 2 * D_EXPERT * 2 + D_EXPERT * D_MODEL * 2
            + 3 * EX_BM * D_MODEL * 4 + 3 * EX_BM * 2 * D_EXPERT * 4)
    blk = lambda i: jnp.minimum(i, nb - 1)
    w_map = lambda i, be, nu, nv: (be[blk(i)], 0, 0)
    grid_spec = pltpu.PrefetchScalarGridSpec(
        num_scalar_prefetch=3,
        grid=(nb + EX_DRAIN_STEPS,),
        in_specs=[
            idx_blk(lambda i, be, nu, nv: (blk(i), 0, 0)),
            idx_blk(lambda i, be, nu, nv: (blk(i + 1), 0, 0)),
            idx_blk(lambda i, be, nu, nv: (blk(i), 0, 0)),
            pl.BlockSpec(memory_space=pl.ANY),
            pl.BlockSpec((1, D_MODEL, 2 * D_EXPERT), w_map),
            pl.BlockSpec((1, 1, 2 * D_EXPERT), w_map),
            pl.BlockSpec((1, D_EXPERT, D_MODEL), w_map),
            pl.BlockSpec((1, 1, D_MODEL), w_map),
        ],
        out_specs=pl.BlockSpec(memory_space=pl.ANY),
        scratch_shapes=[
            pltpu.VMEM((2, EX_BM * TOK_ROWS, LANES), F32),
            pltpu.VMEM((2, EX_BM * TOK_ROWS, LANES), F32),
            pltpu.VMEM((D_MODEL, 2 * D_EXPERT), BF16),
            pltpu.VMEM((D_EXPERT, D_MODEL), BF16),
            pltpu.SemaphoreType.DMA((2,)),
            pltpu.SemaphoreType.DMA((2,)),
        ],
    )
    return pl.pallas_call(
        _experts_kernel,
        grid_spec=grid_spec,
        out_shape=jax.ShapeDtypeStruct((n_slots * TOK_ROWS, LANES), F32),
        compiler_params=pltpu.CompilerParams(
            dimension_semantics=("arbitrary",), vmem_limit_bytes=_vmem_limit(vmem)),
        name="experts",
    )(block_e, n_used, n_valid, buf_tok3, buf_tok3, dst3, h2, w_gu, b_gu[:, None, :], w_down,
      b_down[:, None, :])


CB_TM = 256


def _combine_kernel(x1_ref, y0_ref, y1_ref, y2_ref, y3_ref, gate_ref, o_ref):
    tm = x1_ref.shape[0]
    g = jnp.concatenate([gate_ref[...], jnp.zeros((LANES - SUBLANES, tm), F32)], axis=0).T
    acc = x1_ref[...]
    for k, y_ref in enumerate((y0_ref, y1_ref, y2_ref, y3_ref)):
        acc = acc + g[:, k:k + 1] * _tiles_to_rows(y_ref, tm)
    o_ref[...] = acc


def _combine(x1, y_slots, gates):
    n = x1.shape[0]
    tm = CB_TM
    nt = n // tm
    vmem = 2 * (2 * tm * D_MODEL * 4 + tm * TOP_K * D_MODEL * 4) + 4 * tm * D_MODEL * 4
    y_spec = lambda k: pl.BlockSpec((tm * TOK_ROWS, LANES), lambda i: (k * nt + i, 0))
    return pl.pallas_call(
        _combine_kernel,
        grid=(nt,),
        in_specs=[pl.BlockSpec((tm, D_MODEL), lambda i: (i, 0))]
        + [y_spec(k) for k in range(TOP_K)]
        + [pl.BlockSpec((SUBLANES, tm), lambda i: (0, i))],
        out_specs=pl.BlockSpec((tm, D_MODEL), lambda i: (i, 0)),
        out_shape=jax.ShapeDtypeStruct((n, D_MODEL), F32),
        compiler_params=pltpu.CompilerParams(
            dimension_semantics=("arbitrary",), vmem_limit_bytes=_vmem_limit(vmem)),
        name="combine",
    )(x1, y_slots, y_slots, y_slots, y_slots, gates)


def _dispatch_plan(eidx, rank, counts):
    n = eidx.shape[1]
    n_slots = n * TOP_K
    nb = n_slots // EX_BM + N_EXPERTS
    counts = counts.astype(jnp.int32)
    padded = ((counts + EX_BM - 1) // EX_BM) * EX_BM
    pad_end = jnp.cumsum(padded)
    pad_start = pad_end - padded
    e = eidx[:TOP_K]
    start_of = jnp.sum(jnp.where(e[:, :, None] == jnp.arange(N_EXPERTS, dtype=jnp.int32),
                                 pad_start[None, None, :], 0), axis=-1)
    pos = start_of + rank[:TOP_K]
    inv = _invert(pos.reshape(-1), nb * EX_BM)
    buf_tok = (inv % n) * TOK_ROWS
    dst = inv * TOK_ROWS
    blk_start = jnp.arange(nb, dtype=jnp.int32) * EX_BM
    block_e = jnp.minimum(jnp.sum((pad_end[None, :] <= blk_start[:, None]).astype(jnp.int32), axis=1),
                          N_EXPERTS - 1)
    n_used = (pad_end[-1] // EX_BM).astype(jnp.int32).reshape(1)
    n_valid = jnp.clip(pad_start[block_e] + counts[block_e] - blk_start, 0, EX_BM).astype(jnp.int32)
    n_valid = jnp.where(blk_start < pad_end[-1], n_valid, 0)
    return (block_e, n_used, n_valid, buf_tok.reshape(nb, 1, EX_BM), dst.reshape(nb, 1, EX_BM), n_slots)


def _prep_in_proj_weights(w_in, fox_f_bias, mlstm_i_bias, mlstm_f_bias):
    split_at = []
    acc = 0
    for wdt in SPLIT_WIDTHS[:-1]:
        acc += wdt
        split_at.append(acc)
    fq, fk, fv, ff, mq, mk, mv, mi, mf, mo = jnp.split(w_in, split_at, axis=-1)
    w_main = jnp.concatenate([fq, fk, fv, mq, mk, mv, mo], axis=-1).astype(BF16)
    w_gate = jnp.concatenate([ff, mi, mf], axis=-1)
    bias = jnp.concatenate([fox_f_bias, mlstm_i_bias, mlstm_f_bias]).astype(F32)
    n_gate = w_gate.shape[1]
    wg_hi, wg_lo = _split_hi_lo(jnp.pad(w_gate, ((0, 0), (0, LANES - n_gate))))
    bias_r = jnp.pad(bias, (0, LANES - n_gate))[None, :]
    return w_main, wg_hi, wg_lo, bias_r


def kernel(x, attn_norm_w, w_in, fox_f_bias, fox_q_norm_w, fox_k_norm_w, fox_out_norm_w, mlstm_conv_w, mlstm_i_bias, mlstm_f_bias, mlstm_out_norm_w, w_out, moe_norm_w, router_w, router_b, expert_w_gate_up, expert_b_gate_up, expert_w_down, expert_b_down):
    bsz, seq, d = x.shape
    x2d = x.reshape(bsz * seq, d)
    prep = _prep_in_proj_weights(w_in[0], fox_f_bias[0], mlstm_i_bias[0], mlstm_f_bias[0])
    pair = lambda w: jnp.tile(w, LANES // HEAD_DIM)[None, :]
    qn, kn, aq, ak, vb, projm, gt, gtm = _in_proj(x2d, attn_norm_w[0][None, :], *prep,
                                                  pair(fox_q_norm_w[0]), pair(fox_k_norm_w[0]))
    b3 = lambda a: a.reshape(bsz, seq, a.shape[-1])
    fox_y = _fox(b3(qn), b3(kn), b3(aq), b3(ak), b3(vb), fox_out_norm_w[0][None, :])
    mlstm_y = _mlstm(b3(projm), gt, gtm, mlstm_conv_w[0])
    return _channel_mixer(x2d, fox_y.reshape(-1, FOX_WIDTH), mlstm_y.reshape(-1, MLSTM_V_WIDTH), projm,
                          mlstm_out_norm_w[0], w_out[0], moe_norm_w[0], router_w[0], router_b[0],
                          expert_w_gate_up[0], expert_b_gate_up[0], expert_w_down[0],
                          expert_b_down[0]).reshape(bsz, seq, d)


def _channel_mixer(x2d, fox_y2d, mlstm_y2d, projm, mlstm_gain, w_out, moe_norm_w, router_w, router_b,
                   w_gu, b_gu, w_down, b_down):
    wr_hi, wr_lo = _split_hi_lo(router_w.T)
    x1, h2, eidx, gates, rank, counts = _out_route(
        x2d, fox_y2d, mlstm_y2d, projm, w_out.astype(BF16), mlstm_gain[None, :], moe_norm_w[None, :],
        wr_hi, wr_lo, router_b[:, None])
    block_e, n_used, n_valid, buf_tok3, dst3, n_slots = _dispatch_plan(eidx, rank, counts[:, 0])
    y_slots = _experts(block_e, n_used, n_valid, buf_tok3, dst3, h2, w_gu, b_gu, w_down, b_down, n_slots)
    return _combine(x1, y_slots, gates)
```

```python
import functools
import math

import jax
import jax.numpy as jnp
from jax import lax
from jax.experimental import pallas as pl
from jax.experimental.pallas import tpu as pltpu

F32 = jnp.float32
BF16 = jnp.bfloat16

D_MODEL = 1024
SEQ = 2048
HEAD_DIM = 64
FOX_HEADS = 8
FOX_WIDTH = FOX_HEADS * HEAD_DIM
MLSTM_HEADS = 8
MLSTM_QK_DIM = 32
MLSTM_V_DIM = 64
MLSTM_QK_WIDTH = MLSTM_HEADS * MLSTM_QK_DIM
MLSTM_V_WIDTH = MLSTM_HEADS * MLSTM_V_DIM
CONV_WIDTH = 4
MLSTM_CHUNK = 64
SPLIT_WIDTHS = (FOX_WIDTH, FOX_WIDTH, FOX_WIDTH, FOX_HEADS,
                MLSTM_QK_WIDTH, MLSTM_QK_WIDTH, MLSTM_V_WIDTH,
                MLSTM_HEADS, MLSTM_HEADS, MLSTM_V_WIDTH)
N_EXPERTS = 32
TOP_K = 4
D_EXPERT = D_MODEL
SWIGLU_ALPHA = 1.702
SWIGLU_LIMIT = 7.0
NORM_EPS = 1e-5
LOG2E = 1.4426950408889634

LANES = 128
SUBLANES = 8
V7X_VMEM_BYTES = 64 * 1024 * 1024

MAIN_WIDTH = 3 * FOX_WIDTH + 2 * MLSTM_QK_WIDTH + 2 * MLSTM_V_WIDTH
MLSTM_COLS = MAIN_WIDTH - 3 * FOX_WIDTH
GATE_ROWS = 32

NT_DIMS = (((1,), (1,)), ((), ()))


def _vmem_limit(nbytes):
    return int(min(nbytes + (8 << 20), V7X_VMEM_BYTES - (4 << 20)))


def _log_sigmoid(x):
    return jnp.minimum(x, 0.0) - jnp.log(1.0 + jnp.exp(-jnp.abs(x)))


def _split_hi_lo(x):
    hi = x.astype(BF16)
    lo = (x - hi.astype(F32)).astype(BF16)
    return hi, lo


def _rms(x, w):
    return x * lax.rsqrt(jnp.mean(x * x, axis=-1, keepdims=True) + NORM_EPS) * w


TOK_ROWS = D_MODEL // LANES


def _rows_to_tiles(x, tile_ref):
    m = x.shape[0]
    for j in range(TOK_ROWS):
        tile_ref[pl.ds(j, m, stride=TOK_ROWS), :] = x[:, j * LANES:(j + 1) * LANES]


def _tiles_to_rows(tile_ref, m):
    return jnp.concatenate(
        [tile_ref[pl.ds(j, m, stride=TOK_ROWS), :] for j in range(TOK_ROWS)], axis=1)


IN_TM = 512
IN_TILES_PER_SEQ = SEQ // IN_TM
IN_PARTS = 2
IN_PM = IN_TM // IN_PARTS
IN_SCAN_SHIFTS = tuple(1 << i for i in range(int(math.log2(IN_PM))))


def _pair_rms(x, w, lo_half):
    sq = x * x
    ms_lo = jnp.sum(jnp.where(lo_half, sq, 0.0), axis=-1, keepdims=True) * (1.0 / HEAD_DIM)
    ms_hi = jnp.sum(jnp.where(lo_half, 0.0, sq), axis=-1, keepdims=True) * (1.0 / HEAD_DIM)
    inv = jnp.where(lo_half, lax.rsqrt(ms_lo + NORM_EPS), lax.rsqrt(ms_hi + NORM_EPS))
    return x * inv * w


def _in_proj_kernel(x_ref, nw_ref, w_ref, wg_hi_ref, wg_lo_ref, br_ref, qw_ref, kw_ref,
                    qn_ref, kn_ref, aq_ref, ak_ref, vb_ref, projm_ref, gt_ref, gtm_ref, carry_ref):
    i = pl.program_id(0)

    @pl.when(i % IN_TILES_PER_SEQ == 0)
    def _():
        carry_ref[...] = jnp.zeros_like(carry_ref)

    for part in range(IN_PARTS):
        rows = slice(part * IN_PM, (part + 1) * IN_PM)
        x = x_ref[rows, :]
        ms = jnp.mean(x * x, axis=-1, keepdims=True)
        y = x * lax.rsqrt(ms + NORM_EPS) * nw_ref[...]
        h_hi, h_lo = _split_hi_lo(y)
        main = jnp.dot(h_hi, w_ref[...], preferred_element_type=F32)

        w_hi, w_lo = wg_hi_ref[...], wg_lo_ref[...]
        g = (jnp.dot(h_hi, w_hi, preferred_element_type=F32)
             + jnp.dot(h_hi, w_lo, preferred_element_type=F32)
             + jnp.dot(h_lo, w_hi, preferred_element_type=F32))
        g = g + br_ref[...]
        lane = lax.broadcasted_iota(jnp.int32, g.shape, 1)
        is_input_gate = (lane >= FOX_HEADS) & (lane < FOX_HEADS + MLSTM_HEADS)
        gates = jnp.where(is_input_gate, g, _log_sigmoid(g))
        gtm_ref[rows, :] = gates
        gt_ref[:, rows] = gates.T[:GATE_ROWS, :]

        rowi = lax.broadcasted_iota(jnp.int32, gates.shape, 0)
        c = gates
        for s in IN_SCAN_SHIFTS:
            c = c + jnp.where(rowi >= s, pltpu.roll(c, s, axis=0), 0.0)
        c = c + carry_ref[...]
        carry_ref[...] = c[IN_PM - 1:IN_PM, :]
        cum2 = c * LOG2E

        c8 = jnp.where(lane < FOX_HEADS, cum2, 0.0)
        c_hi = c8.astype(BF16).astype(F32)
        r1 = c8 - c_hi
        c_mid = r1.astype(BF16).astype(F32)
        c_lo = (r1 - c_mid).astype(BF16).astype(F32)
        ones_q = jnp.where((lane >= 3 * FOX_HEADS) & (lane < 6 * FOX_HEADS), 1.0, 0.0)
        ones_k = jnp.where(lane < 3 * FOX_HEADS, 1.0, 0.0)
        aq = (c_hi + pltpu.roll(c_mid, FOX_HEADS, axis=1) + pltpu.roll(c_lo, 2 * FOX_HEADS, axis=1)
              + ones_q)
        ak = ones_k - (pltpu.roll(c_hi, 3 * FOX_HEADS, axis=1) + pltpu.roll(c_mid, 4 * FOX_HEADS, axis=1)
                       + pltpu.roll(c_lo, 5 * FOX_HEADS, axis=1))
        aq_ref[rows, :] = aq.astype(BF16)
        ak_ref[rows, :] = ak.astype(BF16)

        lo_half = lane < HEAD_DIM
        q_scale = (HEAD_DIM ** -0.5) * LOG2E
        for p in range(FOX_HEADS // 2):
            ps = slice(p * LANES, (p + 1) * LANES)
            qn_ref[rows, ps] = (_pair_rms(main[:, ps], qw_ref[...], lo_half) * q_scale).astype(BF16)
            kn_ref[rows, ps] = _pair_rms(main[:, FOX_WIDTH + p * LANES:FOX_WIDTH + (p + 1) * LANES],
                                         kw_ref[...], lo_half).astype(BF16)

        vb_ref[rows, :] = main[:, 2 * FOX_WIDTH:3 * FOX_WIDTH].astype(BF16)
        projm_ref[rows, :] = main[:, 3 * FOX_WIDTH:]


def _in_proj(x2d, norm_w, w_main, wg_hi, wg_lo, bias_r, qw, kw):
    n = x2d.shape[0]
    tm = IN_TM
    const = lambda i: (0, 0)
    row = lambda w: pl.BlockSpec((tm, w), lambda i: (i, 0))
    vmem = (2 * (tm * D_MODEL * 4 + D_MODEL * MAIN_WIDTH * 2 + 3 * tm * FOX_WIDTH * 2 + 2 * tm * LANES * 2
                 + tm * MLSTM_COLS * 4 + GATE_ROWS * tm * 4 + tm * LANES * 4)
            + 2 * tm * MAIN_WIDTH * 4)
    return pl.pallas_call(
        _in_proj_kernel,
        grid=(n // tm,),
        in_specs=[
            row(D_MODEL),
            pl.BlockSpec((1, D_MODEL), const),
            pl.BlockSpec((D_MODEL, MAIN_WIDTH), const),
            pl.BlockSpec((D_MODEL, LANES), const),
            pl.BlockSpec((D_MODEL, LANES), const),
            pl.BlockSpec((1, LANES), const),
            pl.BlockSpec((1, LANES), const),
            pl.BlockSpec((1, LANES), const),
        ],
        out_specs=[
            row(FOX_WIDTH), row(FOX_WIDTH), row(LANES), row(LANES), row(FOX_WIDTH), row(MLSTM_COLS),
            pl.BlockSpec((GATE_ROWS, tm), lambda i: (0, i)),
            row(LANES),
        ],
        out_shape=[
            jax.ShapeDtypeStruct((n, FOX_WIDTH), BF16),
            jax.ShapeDtypeStruct((n, FOX_WIDTH), BF16),
            jax.ShapeDtypeStruct((n, LANES), BF16),
            jax.ShapeDtypeStruct((n, LANES), BF16),
            jax.ShapeDtypeStruct((n, FOX_WIDTH), BF16),
            jax.ShapeDtypeStruct((n, MLSTM_COLS), F32),
            jax.ShapeDtypeStruct((GATE_ROWS, n), F32),
            jax.ShapeDtypeStruct((n, LANES), F32),
        ],
        scratch_shapes=[pltpu.VMEM((1, LANES), F32)],
        compiler_params=pltpu.CompilerParams(
            dimension_semantics=("arbitrary",), vmem_limit_bytes=_vmem_limit(vmem)),
        name="in_proj",
    )(x2d, norm_w, w_main, wg_hi, wg_lo, bias_r, qw, kw)


FOX_TQ = 256


def _fox_kernel(qn_ref, kn_ref, aq_ref, ak_ref, v_ref, ow_ref, o_ref, qa_ref, ka_ref, va_ref):
    hp = pl.program_id(1)
    tri = (lax.broadcasted_iota(jnp.int32, (FOX_TQ, FOX_TQ), 1)
           <= lax.broadcasted_iota(jnp.int32, (FOX_TQ, FOX_TQ), 0))
    lo_q = lax.broadcasted_iota(jnp.int32, (FOX_TQ, LANES), 1) < HEAD_DIM
    lane = lax.broadcasted_iota(jnp.int32, (1, LANES), 1)
    ka_ref[:, :LANES] = kn_ref[0]
    ka_ref[:, LANES:] = ak_ref[0]
    for j in range(2):
        cs = slice(j * HEAD_DIM, (j + 1) * HEAD_DIM)
        own_q = lo_q if j == 0 else jnp.logical_not(lo_q)
        h = 2 * hp + j
        own_mask = jnp.where((lane < HEAD_DIM) if j == 0 else (lane >= HEAD_DIM), 1.0, 0.0).astype(BF16)
        bias_mask = jnp.where((lane < 6 * FOX_HEADS) & (lane % FOX_HEADS == h), 1.0, 0.0).astype(BF16)
        qa_ref[j, :, :LANES] = qn_ref[0] * own_mask
        qa_ref[j, :, LANES:] = aq_ref[0] * bias_mask
        sum_lane = HEAD_DIM if j == 0 else 0
        va_ref[j] = v_ref[0] * own_mask + jnp.where(lane == sum_lane, 1.0, 0.0).astype(BF16)
        for i in range(SEQ // FOX_TQ):
            qs = slice(i * FOX_TQ, (i + 1) * FOX_TQ)
            n = (i + 1) * FOX_TQ
            s = lax.dot_general(qa_ref[j, qs, :], ka_ref[:n, :], NT_DIMS,
                                preferred_element_type=F32)
            diag = jnp.where(tri, s[:, n - FOX_TQ:], -jnp.inf)
            m = jnp.max(diag, axis=-1, keepdims=True)
            if i > 0:
                past = s[:, :n - FOX_TQ]
                m = jnp.maximum(m, jnp.max(past, axis=-1, keepdims=True))
                p = jnp.concatenate([jnp.exp2(past - m), jnp.exp2(diag - m)], axis=1)
            else:
                p = jnp.exp2(diag - m)
            o = jnp.dot(p.astype(BF16), va_ref[j, :n, :], preferred_element_type=F32)
            o = o / o[:, sum_lane:sum_lane + 1]
            ms = jnp.sum(jnp.where(own_q, o * o, 0.0), axis=-1, keepdims=True) * (1.0 / HEAD_DIM)
            on = o * lax.rsqrt(ms + NORM_EPS) * ow_ref[...]
            o_ref[0, qs, cs] = on[:, cs]


def _fox(qn3, kn3, aq3, ak3, v3, ow):
    b = qn3.shape[0]
    nq = FOX_WIDTH // LANES
    blk = (1, SEQ, LANES)
    pair = pl.BlockSpec(blk, lambda bi, hp: (bi, 0, hp))
    shared = pl.BlockSpec(blk, lambda bi, hp: (bi, 0, 0))
    vmem = 2 * (5 * SEQ * LANES * 2 + SEQ * LANES * 4) + 3 * SEQ * 2 * LANES * 2 + 8 * FOX_TQ * SEQ * 4
    return pl.pallas_call(
        _fox_kernel,
        grid=(b, nq),
        in_specs=[pair, pair, shared, shared, pair, pl.BlockSpec((1, LANES), lambda bi, hp: (0, hp))],
        out_specs=pl.BlockSpec(blk, lambda bi, hp: (bi, 0, hp)),
        out_shape=jax.ShapeDtypeStruct((b, SEQ, FOX_WIDTH), F32),
        scratch_shapes=[pltpu.VMEM((2, SEQ, 2 * LANES), BF16), pltpu.VMEM((SEQ, 2 * LANES), BF16),
                        pltpu.VMEM((2, SEQ, LANES), BF16)],
        compiler_params=pltpu.CompilerParams(
            dimension_semantics=("arbitrary", "arbitrary"), vmem_limit_bytes=_vmem_limit(vmem)),
        name="fox",
    )(qn3, kn3, aq3, ak3, v3, ow)


ML_L = MLSTM_CHUNK
ML_PAIRS = SEQ // (2 * ML_L)
ML_HL = FOX_HEADS
ML_AUG = MLSTM_V_WIDTH + LANES
ML_TILE = 256
SEG_SHIFTS = tuple(1 << i for i in range(int(math.log2(ML_L))))


def _split3(x):
    a = x.astype(BF16)
    r = x - a.astype(F32)
    b = r.astype(BF16)
    c = (r - b.astype(F32)).astype(BF16)
    return a, b, c


def _expand_heads(x, exp_bf):
    a, b, c = _split3(x)
    return (jnp.dot(a, exp_bf, preferred_element_type=F32)
            + jnp.dot(b, exp_bf, preferred_element_type=F32)
            + jnp.dot(c, exp_bf, preferred_element_type=F32))


def _seg_scan(x, axis, op, ident):
    idx = lax.broadcasted_iota(jnp.int32, x.shape, axis) % ML_L
    for s in SEG_SHIFTS:
        x = op(x, jnp.where(idx >= s, pltpu.roll(x, s, axis=axis), ident))
    return x


def _mlstm_kernel(q_ref, k_ref, v_ref, gi_ref, gf_ref, gtm_ref, cw_ref, o_ref, den_ref,
                  qc_ref, kc_ref, kt_ref, rr_ref, cmr_ref, bcr_ref,
                  ealpha_ref, ew_ref, wint_ref, floor_ref, mfull_ref, caug_ref):
    def conv_silu(u, w):
        rowi = lax.broadcasted_iota(jnp.int32, u.shape, 0)
        acc = u * w[CONV_WIDTH - 1:CONV_WIDTH, :]
        for d in range(1, CONV_WIDTH):
            sh = jnp.where(rowi >= d, pltpu.roll(u, d, axis=0), 0.0)
            acc = acc + sh * w[CONV_WIDTH - 1 - d:CONV_WIDTH - d, :]
        return acc / (1.0 + jnp.exp(-acc))

    cw = cw_ref[...]
    qc_ref[...] = conv_silu(q_ref[0], cw[:, :MLSTM_QK_WIDTH]).astype(BF16)
    kc = conv_silu(k_ref[0], cw[:, MLSTM_QK_WIDTH:]) * (MLSTM_QK_DIM ** -0.5)
    kc_ref[...] = kc.astype(BF16)
    kt = kc.T
    for p in range(ML_PAIRS):
        kt_ref[p] = kt[:, p * LANES:(p + 1) * LANES]

    bcum_r = _seg_scan(gf_ref[...], 1, jnp.add, 0.0)
    r_r = gi_ref[...] - bcum_r
    cmx_r = _seg_scan(r_r, 1, jnp.maximum, -jnp.inf)
    for p in range(ML_PAIRS):
        ls = slice(p * LANES, (p + 1) * LANES)
        rr_ref[p] = r_r[:, ls]
        cmr_ref[p] = cmx_r[:, ls]
        bcr_ref[p] = bcum_r[:, ls]

    g = gtm_ref[...]
    lane_g = lax.broadcasted_iota(jnp.int32, g.shape, 1)
    head_lane = (lane_g >= ML_HL) & (lane_g < ML_HL + MLSTM_HEADS)
    bcum_c = jnp.where(
        head_lane, pltpu.roll(_seg_scan(g, 0, jnp.add, 0.0), LANES - MLSTM_HEADS, axis=1), 0.0)
    cmx_c = _seg_scan(jnp.where(head_lane, g, 0.0) - bcum_c, 0, jnp.maximum, -jnp.inf)
    m = jnp.zeros((1, LANES), F32)
    for c in range(SEQ // ML_L):
        mfull_ref[c * ML_L:(c + 1) * ML_L, :] = jnp.broadcast_to(m, (ML_L, LANES))
        last = (c + 1) * ML_L - 1
        m = bcum_c[last:last + 1, :] + jnp.maximum(m, cmx_c[last:last + 1, :])
    mfull = mfull_ref[...]
    mx = jnp.maximum(mfull, cmx_c)
    wint_ref[...] = jnp.exp(mfull - mx)
    floor_ref[...] = jnp.exp(-(bcum_c + mx))
    mfull_ref[...] = -mx

    lane_e = lax.broadcasted_iota(jnp.int32, (LANES, MLSTM_V_WIDTH), 1) // MLSTM_V_DIM
    row_e = lax.broadcasted_iota(jnp.int32, (LANES, MLSTM_V_WIDTH), 0)
    exp_bf = jnp.where(row_e == lane_e + ML_HL, 1.0, 0.0).astype(BF16)

    def expand_tile(i, carry):
        rows = pl.ds(pl.multiple_of(i * ML_TILE, ML_TILE), ML_TILE)
        ealpha_ref[rows, :] = _expand_heads(mfull_ref[rows, :], exp_bf)
        ew_ref[rows, :] = _expand_heads(wint_ref[rows, :], exp_bf)
        return carry

    lax.fori_loop(0, SEQ // ML_TILE, expand_tile, 0)

    kb_rowh = lax.broadcasted_iota(jnp.int32, (MLSTM_HEADS * ML_L, MLSTM_QK_WIDTH), 0) // ML_L
    kb_lane = lax.broadcasted_iota(jnp.int32, (MLSTM_HEADS * ML_L, MLSTM_QK_WIDTH), 1) // MLSTM_QK_DIM
    mask_k = jnp.where(kb_rowh == kb_lane, 1.0, 0.0).astype(BF16)
    va_rowh = lax.broadcasted_iota(jnp.int32, (MLSTM_HEADS * ML_L, ML_AUG), 0) // ML_L
    va_col = lax.broadcasted_iota(jnp.int32, (MLSTM_HEADS * ML_L, ML_AUG), 1)
    mask_v = jnp.where(
        (va_col // MLSTM_V_DIM == va_rowh) | (va_col == MLSTM_V_WIDTH + ML_HL + va_rowh),
        1.0, 0.0).astype(BF16)
    c_rowh = lax.broadcasted_iota(jnp.int32, (MLSTM_QK_WIDTH, ML_AUG), 0) // MLSTM_QK_DIM
    c_col = lax.broadcasted_iota(jnp.int32, (MLSTM_QK_WIDTH, ML_AUG), 1)
    mask_c = (c_col // MLSTM_V_DIM == c_rowh) | (c_col == MLSTM_V_WIDTH + ML_HL + c_rowh)
    lane128 = lax.broadcasted_iota(jnp.int32, (ML_L, LANES), 1)
    s_idx = lax.broadcasted_iota(jnp.int32, (ML_L, MLSTM_V_WIDTH), 1) % ML_L
    t_idx = lax.broadcasted_iota(jnp.int32, (ML_L, MLSTM_V_WIDTH), 0)
    causal = s_idx <= t_idx
    ones_aug = jnp.ones((ML_L, LANES), F32)

    caug_ref[...] = jnp.zeros_like(caug_ref)

    def pair_body(cp, m_row):
        r2 = rr_ref[cp]
        cm2 = cmr_ref[cp]
        b2 = bcr_ref[cp]
        r2r = pltpu.roll(r2, ML_L, axis=1)
        kt2 = kt_ref[cp]
        for cc in range(2):
            lo = cc * ML_L
            rows = pl.ds(pl.multiple_of(cp * (2 * ML_L), 2 * ML_L) + lo, ML_L)
            qa = qc_ref[rows, :]
            ka = kc_ref[rows, :]
            va_aug = jnp.concatenate([v_ref[0, rows, :], ones_aug], axis=1).astype(BF16)

            kbd = jnp.concatenate([ka] * MLSTM_HEADS, axis=0) * mask_k
            s = lax.dot_general(qa, kbd, NT_DIMS, preferred_element_type=F32)

            cmx_last = cm2[:, lo + ML_L - 1:lo + ML_L]
            b_last = b2[:, lo + ML_L - 1:lo + ML_L]
            mx_r = jnp.maximum(m_row, cmx_last)
            decay = jnp.exp(m_row - mx_r)
            wk = jnp.exp(r2[:, lo:lo + ML_L] - mx_r)
            m_row = b_last + mx_r

            src_e, src_o = (r2, r2r) if cc == 0 else (r2r, r2)
            cols = []
            for p in range(MLSTM_HEADS // 2):
                even = jnp.broadcast_to(src_e[2 * p:2 * p + 1, :], (ML_L, LANES))
                odd = jnp.broadcast_to(src_o[2 * p + 1:2 * p + 2, :], (ML_L, LANES))
                cols.append(jnp.where(lane128 < ML_L, even, odd))
            r_all = jnp.concatenate(cols, axis=1)
            arg = jnp.where(causal, ealpha_ref[rows, :] + r_all, -jnp.inf)
            p_all = (s * jnp.exp(arg)).astype(BF16)

            vbd = jnp.concatenate([va_aug] * MLSTM_HEADS, axis=0) * mask_v
            pv = jnp.dot(p_all, vbd, preferred_element_type=F32)
            qc_state = jnp.dot(qa, caug_ref[...].astype(BF16), preferred_element_type=F32)
            o_ref[0, rows, :] = (ew_ref[rows, :] * qc_state[:, :MLSTM_V_WIDTH]
                                 + pv[:, :MLSTM_V_WIDTH])
            den_ref[rows, :] = (wint_ref[rows, :] * qc_state[:, MLSTM_V_WIDTH:]
                                + pv[:, MLSTM_V_WIDTH:])

            wk_rows = jnp.concatenate(
                [jnp.broadcast_to(wk[h:h + 1, :], (MLSTM_QK_DIM, ML_L)) for h in range(MLSTM_HEADS)],
                axis=0)
            dec_rows = jnp.concatenate(
                [jnp.broadcast_to(decay[h:h + 1, :], (MLSTM_QK_DIM, 1)) for h in range(MLSTM_HEADS)],
                axis=0)
            ktw = (kt2[:, lo:lo + ML_L] * wk_rows).astype(BF16)
            upd = jnp.dot(ktw, va_aug, preferred_element_type=F32)
            caug_ref[...] = dec_rows * caug_ref[...] + jnp.where(mask_c, upd, 0.0)
        return m_row

    lax.fori_loop(0, ML_PAIRS, pair_body, jnp.zeros((MLSTM_HEADS, 1), F32))

    ob_row = lax.broadcasted_iota(jnp.int32, (MLSTM_V_WIDTH, LANES), 0) // MLSTM_V_DIM
    ob_col = lax.broadcasted_iota(jnp.int32, (MLSTM_V_WIDTH, LANES), 1)
    ones_bd = jnp.where(ob_col == ob_row + ML_HL, 1.0, 0.0).astype(BF16)

    def norm_tile(i, carry):
        rows = pl.ds(pl.multiple_of(i * ML_TILE, ML_TILE), ML_TILE)
        num = o_ref[0, rows, :]
        dn = jnp.maximum(jnp.abs(den_ref[rows, :]), floor_ref[rows, :])
        r = 1.0 / dn
        n2_hi, n2_lo = _split_hi_lo(num * num)
        msn = (jnp.dot(n2_hi, ones_bd, preferred_element_type=F32)
               + jnp.dot(n2_lo, ones_bd, preferred_element_type=F32)) * (1.0 / MLSTM_V_DIM)
        fac = r * lax.rsqrt(r * r * msn + NORM_EPS)
        o_ref[0, rows, :] = num * _expand_heads(fac, exp_bf)
        return carry

    lax.fori_loop(0, SEQ // ML_TILE, norm_tile, 0)


def _mlstm(projm3, gt, gtm, conv_w):
    b = projm3.shape[0]
    qk_blk = (1, SEQ, MLSTM_QK_WIDTH)
    v_blk = (1, SEQ, MLSTM_V_WIDTH)
    v_col = 2 * MLSTM_QK_WIDTH // MLSTM_V_WIDTH
    vmem = (2 * (2 * SEQ * MLSTM_QK_WIDTH * 4 + 2 * SEQ * MLSTM_V_WIDTH * 4 + 2 * SEQ * LANES * 4)
            + 2 * SEQ * MLSTM_QK_WIDTH * 2 + SEQ * MLSTM_QK_WIDTH * 4 + 2 * SEQ * MLSTM_V_WIDTH * 4
            + 3 * SEQ * LANES * 4 + (8 << 20))
    return pl.pallas_call(
        _mlstm_kernel,
        grid=(b,),
        in_specs=[
            pl.BlockSpec(qk_blk, lambda bi: (bi, 0, 0)),
            pl.BlockSpec(qk_blk, lambda bi: (bi, 0, 1)),
            pl.BlockSpec(v_blk, lambda bi: (bi, 0, v_col)),
            pl.BlockSpec((SUBLANES, SEQ), lambda bi: (1, bi)),
            pl.BlockSpec((SUBLANES, SEQ), lambda bi: (2, bi)),
            pl.BlockSpec((SEQ, LANES), lambda bi: (bi, 0)),
            pl.BlockSpec((CONV_WIDTH, 2 * MLSTM_QK_WIDTH), lambda bi: (0, 0)),
        ],
        out_specs=pl.BlockSpec(v_blk, lambda bi: (bi, 0, 0)),
        out_shape=jax.ShapeDtypeStruct((b, SEQ, MLSTM_V_WIDTH), F32),
        scratch_shapes=[
            pltpu.VMEM((SEQ, LANES), F32),
            pltpu.VMEM((SEQ, MLSTM_QK_WIDTH), BF16),
            pltpu.VMEM((SEQ, MLSTM_QK_WIDTH), BF16),
            pltpu.VMEM((ML_PAIRS, MLSTM_QK_WIDTH, LANES), F32),
            pltpu.VMEM((ML_PAIRS, SUBLANES, LANES), F32),
            pltpu.VMEM((ML_PAIRS, SUBLANES, LANES), F32),
            pltpu.VMEM((ML_PAIRS, SUBLANES, LANES), F32),
            pltpu.VMEM((SEQ, MLSTM_V_WIDTH), F32),
            pltpu.VMEM((SEQ, MLSTM_V_WIDTH), F32),
            pltpu.VMEM((SEQ, LANES), F32),
            pltpu.VMEM((SEQ, LANES), F32),
            pltpu.VMEM((SEQ, LANES), F32),
            pltpu.VMEM((MLSTM_QK_WIDTH, ML_AUG), F32),
        ],
        compiler_params=pltpu.CompilerParams(
            dimension_semantics=("arbitrary",), vmem_limit_bytes=_vmem_limit(vmem)),
        name="mlstm",
    )(projm3, projm3, projm3, gt, gt, gtm, conv_w)


RT_TM = 512


def _out_route_kernel(x_ref, fy_ref, my_ref, mo_ref, wo_ref, mg_ref, nw_ref, wr_hi_ref, wr_lo_ref,
                      rb_ref, x1_ref, h2_ref, eidx_ref, gate_ref, rank_ref, cnt_ref, carry_ref):
    i = pl.program_id(0)

    @pl.when(i == 0)
    def _():
        carry_ref[...] = jnp.zeros_like(carry_ref)

    my = my_ref[...] * mg_ref[...] / (1.0 + jnp.exp(-mo_ref[...]))
    mixed = (jnp.dot(fy_ref[...].astype(BF16), wo_ref[:FOX_WIDTH, :], preferred_element_type=F32)
             + jnp.dot(my.astype(BF16), wo_ref[FOX_WIDTH:, :], preferred_element_type=F32))
    x1 = x_ref[...] + mixed
    x1_ref[...] = x1
    h2 = _rms(x1, nw_ref[...])
    _rows_to_tiles(h2, h2_ref)

    h_hi, h_lo = _split_hi_lo(h2)
    wr_hi, wr_lo = wr_hi_ref[...], wr_lo_ref[...]
    logit = (lax.dot_general(wr_hi, h_hi, NT_DIMS, preferred_element_type=F32)
             + lax.dot_general(wr_lo, h_hi, NT_DIMS, preferred_element_type=F32)
             + lax.dot_general(wr_hi, h_lo, NT_DIMS, preferred_element_type=F32)) + rb_ref[...]

    e_iota = lax.broadcasted_iota(jnp.int32, logit.shape, 0).astype(F32)
    vals, idxs, hots = [], [], []
    for _ in range(TOP_K):
        mk = jnp.max(logit, axis=0, keepdims=True)
        idx = jnp.min(jnp.where(logit == mk, e_iota, float(N_EXPERTS)), axis=0, keepdims=True)
        hot = e_iota == idx
        logit = jnp.where(hot, -jnp.inf, logit)
        vals.append(mk)
        idxs.append(idx.astype(jnp.int32))
        hots.append(hot)
    exps = [jnp.exp(v - vals[0]) for v in vals]
    tot = exps[0] + exps[1] + exps[2] + exps[3]
    gates = [e / tot for e in exps]

    assign = jnp.zeros(logit.shape, F32)
    for hot in hots:
        assign = assign + jnp.where(hot, 1.0, 0.0)
    tm = logit.shape[1]
    src = lax.broadcasted_iota(jnp.int32, (tm, tm), 0)
    dst = lax.broadcasted_iota(jnp.int32, (tm, tm), 1)
    upper = jnp.where(src < dst, 1.0, 0.0).astype(BF16)
    base = jnp.dot(assign.astype(BF16), upper, preferred_element_type=F32) + carry_ref[:, 0:1]
    ranks = [jnp.sum(jnp.where(hot, base, 0.0), axis=0, keepdims=True) for hot in hots]
    new_carry = carry_ref[...] + jnp.sum(assign, axis=1, keepdims=True)
    carry_ref[...] = new_carry
    cnt_ref[...] = new_carry

    zi = jnp.zeros((SUBLANES - TOP_K, tm), jnp.int32)
    eidx_ref[...] = jnp.concatenate(idxs + [zi], axis=0)
    rank_ref[...] = jnp.concatenate([r.astype(jnp.int32) for r in ranks] + [zi], axis=0)
    gate_ref[...] = jnp.concatenate(gates + [zi.astype(F32)], axis=0)


def _out_route(x2d, fox_y2d, mlstm_y2d, projm, w_out_bf, mlstm_gain, moe_norm_w, wr_hi, wr_lo, rb):
    n = x2d.shape[0]
    tm = RT_TM
    const = lambda i: (0, 0)
    mo_col = (MLSTM_COLS - MLSTM_V_WIDTH) // MLSTM_V_WIDTH
    row_blk = lambda w: pl.BlockSpec((tm, w), lambda i: (i, 0))
    lane_blk = pl.BlockSpec((SUBLANES, tm), lambda i: (0, i))
    vmem = (2 * (tm * D_MODEL * 4 * 3 + tm * FOX_WIDTH * 4 * 3 + D_MODEL * D_MODEL * 2)
            + 6 * tm * D_MODEL * 4 + tm * tm * 6)
    return pl.pallas_call(
        _out_route_kernel,
        grid=(n // tm,),
        in_specs=[
            row_blk(D_MODEL), row_blk(FOX_WIDTH), row_blk(MLSTM_V_WIDTH),
            pl.BlockSpec((tm, MLSTM_V_WIDTH), lambda i: (i, mo_col)),
            pl.BlockSpec((D_MODEL, D_MODEL), const),
            pl.BlockSpec((1, MLSTM_V_WIDTH), const),
            pl.BlockSpec((1, D_MODEL), const),
            pl.BlockSpec((N_EXPERTS, D_MODEL), const),
            pl.BlockSpec((N_EXPERTS, D_MODEL), const),
            pl.BlockSpec((N_EXPERTS, 1), const),
        ],
        out_specs=[row_blk(D_MODEL), pl.BlockSpec((tm * TOK_ROWS, LANES), lambda i: (i, 0)),
                   lane_blk, lane_blk, lane_blk, pl.BlockSpec((N_EXPERTS, LANES), const)],
        out_shape=[
            jax.ShapeDtypeStruct((n, D_MODEL), F32),
            jax.ShapeDtypeStruct((n * TOK_ROWS, LANES), F32),
            jax.ShapeDtypeStruct((SUBLANES, n), jnp.int32),
            jax.ShapeDtypeStruct((SUBLANES, n), F32),
            jax.ShapeDtypeStruct((SUBLANES, n), jnp.int32),
            jax.ShapeDtypeStruct((N_EXPERTS, LANES), F32),
        ],
        scratch_shapes=[pltpu.VMEM((N_EXPERTS, LANES), F32)],
        compiler_params=pltpu.CompilerParams(
            dimension_semantics=("arbitrary",), vmem_limit_bytes=_vmem_limit(vmem)),
        name="out_route",
    )(x2d, fox_y2d, mlstm_y2d, projm, w_out_bf, mlstm_gain, moe_norm_w, wr_hi, wr_lo, rb)


INV_CHUNK = 8192
INV_UNROLL = 8


def _invert_kernel(pos_ref, zeros_hbm, inv_ref, sem):
    i = pl.program_id(0)

    @pl.when(i == 0)
    def _():
        cp = pltpu.make_async_copy(zeros_hbm, inv_ref, sem.at[0])
        cp.start()
        cp.wait()

    base = i * INV_CHUNK

    def body(j, carry):
        inv_ref[pos_ref[0, 0, j]] = base + j
        return carry

    lax.fori_loop(0, INV_CHUNK, body, 0, unroll=INV_UNROLL)


def _invert(pos_flat, n_rows):
    n_slots = pos_flat.shape[0]
    steps = n_slots // INV_CHUNK
    return pl.pallas_call(
        _invert_kernel,
        grid=(steps,),
        in_specs=[
            pl.BlockSpec((1, 1, INV_CHUNK), lambda i: (i, 0, 0), memory_space=pltpu.SMEM),
            pl.BlockSpec(memory_space=pl.ANY),
        ],
        out_specs=pl.BlockSpec(memory_space=pltpu.SMEM),
        out_shape=jax.ShapeDtypeStruct((n_rows,), jnp.int32),
        scratch_shapes=[pltpu.SemaphoreType.DMA((1,))],
        compiler_params=pltpu.CompilerParams(dimension_semantics=("arbitrary",)),
        name="invert",
    )(pos_flat.reshape(steps, 1, INV_CHUNK), jnp.zeros((n_rows,), jnp.int32))


EX_BM = 256
EX_DRAIN_STEPS = 2


def _experts_kernel(be_ref, nu_ref, nv_ref, tok_ref, tokn_ref, dst_ref, h2_hbm, wgu_ref, bgu_ref,
                    wd_ref, bd_ref, y_hbm, xt_ref, yt_ref, wgu_bf_ref, wd_bf_ref, gsem, ssem):
    i = pl.program_id(0)
    last_blk = pl.num_programs(0) - 1 - EX_DRAIN_STEPS
    nu = nu_ref[0]
    slot = i % 2
    cur = jnp.minimum(i, last_blk)
    tile_rows = EX_BM * TOK_ROWS

    def tok_tile(ref, idx):
        return ref.at[pl.ds(pl.multiple_of(idx, TOK_ROWS), TOK_ROWS), :]

    def start_gather(idx_ref, s):
        for r in range(EX_BM):
            pltpu.make_async_copy(tok_tile(h2_hbm, idx_ref[0, 0, r]),
                                  xt_ref.at[s, pl.ds(r * TOK_ROWS, TOK_ROWS), :], gsem.at[s]).start()

    def scatter_row(r):
        pltpu.make_async_copy(yt_ref.at[slot, pl.ds(r * TOK_ROWS, TOK_ROWS), :],
                              tok_tile(y_hbm, dst_ref[0, 0, r]), ssem.at[slot]).start()

    def wait_block(sem, buf):
        pltpu.make_async_copy(h2_hbm.at[pl.ds(0, tile_rows), :], buf, sem).wait()

    def wait_scatter(s, n):
        @pl.when(n == EX_BM)
        def _():
            wait_block(ssem.at[s], yt_ref.at[s])

        @pl.when(n < EX_BM)
        def _():
            def body(r, carry):
                pltpu.make_async_copy(h2_hbm.at[pl.ds(0, TOK_ROWS), :],
                                      yt_ref.at[s, pl.ds(0, TOK_ROWS), :], ssem.at[s]).wait()
                return carry
            lax.fori_loop(0, n, body, 0)

    @pl.when(i == 0)
    def _():
        start_gather(tok_ref, 0)

    @pl.when(i + 1 < nu)
    def _():
        start_gather(tokn_ref, 1 - slot)

    @pl.when((i >= 2) & (i - 2 < nu))
    def _():
        wait_scatter(slot, nv_ref[jnp.clip(i - 2, 0, last_blk)])

    @pl.when(i < nu)
    def _():
        @pl.when((i == 0) | (be_ref[cur] != be_ref[jnp.maximum(cur - 1, 0)]))
        def _():
            wgu_bf_ref[...] = wgu_ref[0].astype(BF16)
            wd_bf_ref[...] = wd_ref[0].astype(BF16)

        wait_block(gsem.at[slot], xt_ref.at[slot])
        xb = _tiles_to_rows(xt_ref.at[slot], EX_BM).astype(BF16)
        gu = jnp.dot(xb, wgu_bf_ref[...], preferred_element_type=F32) + bgu_ref[0]
        gate = jnp.minimum(gu[:, :D_EXPERT], SWIGLU_LIMIT)
        up = jnp.clip(gu[:, D_EXPERT:], -SWIGLU_LIMIT, SWIGLU_LIMIT)
        act = (up + 1.0) * (gate / (1.0 + jnp.exp(-SWIGLU_ALPHA * gate)))
        y = jnp.dot(act.astype(BF16), wd_bf_ref[...], preferred_element_type=F32) + bd_ref[0]
        _rows_to_tiles(y, yt_ref.at[slot])
        nv = nv_ref[cur]

        @pl.when(nv == EX_BM)
        def _():
            for r in range(EX_BM):
                scatter_row(r)

        @pl.when(nv < EX_BM)
        def _():
            def body(r, carry):
                pltpu.make_async_copy(
                    yt_ref.at[slot, pl.ds(pl.multiple_of(r * TOK_ROWS, TOK_ROWS), TOK_ROWS), :],
                    tok_tile(y_hbm, dst_ref[0, 0, r]), ssem.at[slot]).start()
                return carry
            lax.fori_loop(0, nv, body, 0)


def _experts(block_e, n_used, n_valid, buf_tok3, dst3, h2, w_gu, b_gu, w_down, b_down, n_slots):
    nb = buf_tok3.shape[0]
    idx_blk = lambda f: pl.BlockSpec((1, 1, EX_BM), f, memory_space=pltpu.SMEM)
    vmem = (2 * (D_MODEL * 2 * D_EXPERT * 4 + D_EXPERT * D_MODEL * 4)
            + D_MODEL * 2 * D_EXPERT * 2 + D_EXPERT * D_MODEL * 2
            + 3 * EX_BM * D_MODEL * 4 + 3 * EX_BM * 2 * D_EXPERT * 4)
    blk = lambda i: jnp.minimum(i, nb - 1)
    w_map = lambda i, be, nu, nv: (be[blk(i)], 0, 0)
    grid_spec = pltpu.PrefetchScalarGridSpec(
        num_scalar_prefetch=3,
        grid=(nb + EX_DRAIN_STEPS,),
        in_specs=[
            idx_blk(lambda i, be, nu, nv: (blk(i), 0, 0)),
            idx_blk(lambda i, be, nu, nv: (blk(i + 1), 0, 0)),
            idx_blk(lambda i, be, nu, nv: (blk(i), 0, 0)),
            pl.BlockSpec(memory_space=pl.ANY),
            pl.BlockSpec((1, D_MODEL, 2 * D_EXPERT), w_map),
            pl.BlockSpec((1, 1, 2 * D_EXPERT), w_map),
            pl.BlockSpec((1, D_EXPERT, D_MODEL), w_map),
            pl.BlockSpec((1, 1, D_MODEL), w_map),
        ],
        out_specs=pl.BlockSpec(memory_space=pl.ANY),
        scratch_shapes=[
            pltpu.VMEM((2, EX_BM * TOK_ROWS, LANES), F32),
            pltpu.VMEM((2, EX_BM * TOK_ROWS, LANES), F32),
            pltpu.VMEM((D_MODEL, 2 * D_EXPERT), BF16),
            pltpu.VMEM((D_EXPERT, D_MODEL), BF16),
            pltpu.SemaphoreType.DMA((2,)),
            pltpu.SemaphoreType.DMA((2,)),
        ],
    )
    return pl.pallas_call(
        _experts_kernel,
        grid_spec=grid_spec,
        out_shape=jax.ShapeDtypeStruct((n_slots * TOK_ROWS, LANES), F32),
        compiler_params=pltpu.CompilerParams(
            dimension_semantics=("arbitrary",), vmem_limit_bytes=_vmem_limit(vmem)),
        name="experts",
    )(block_e, n_used, n_valid, buf_tok3, buf_tok3, dst3, h2, w_gu, b_gu[:, None, :], w_down,
      b_down[:, None, :])


CB_TM = 256


def _combine_kernel(x1_ref, y0_ref, y1_ref, y2_ref, y3_ref, gate_ref, o_ref):
    tm = x1_ref.shape[0]
    g = jnp.concatenate([gate_ref[...], jnp.zeros((LANES - SUBLANES, tm), F32)], axis=0).T
    acc = x1_ref[...]
    for k, y_ref in enumerate((y0_ref, y1_ref, y2_ref, y3_ref)):
        acc = acc + g[:, k:k + 1] * _tiles_to_rows(y_ref, tm)
    o_ref[...] = acc


def _combine(x1, y_slots, gates):
    n = x1.shape[0]
    tm = CB_TM
    nt = n // tm
    vmem = 2 * (2 * tm * D_MODEL * 4 + tm * TOP_K * D_MODEL * 4) + 4 * tm * D_MODEL * 4
    y_spec = lambda k: pl.BlockSpec((tm * TOK_ROWS, LANES), lambda i: (k * nt + i, 0))
    return pl.pallas_call(
        _combine_kernel,
        grid=(nt,),
        in_specs=[pl.BlockSpec((tm, D_MODEL), lambda i: (i, 0))]
        + [y_spec(k) for k in range(TOP_K)]
        + [pl.BlockSpec((SUBLANES, tm), lambda i: (0, i))],
        out_specs=pl.BlockSpec((tm, D_MODEL), lambda i: (i, 0)),
        out_shape=jax.ShapeDtypeStruct((n, D_MODEL), F32),
        compiler_params=pltpu.CompilerParams(
            dimension_semantics=("arbitrary",), vmem_limit_bytes=_vmem_limit(vmem)),
        name="combine",
    )(x1, y_slots, y_slots, y_slots, y_slots, gates)


def _dispatch_plan(eidx, rank, counts):
    n = eidx.shape[1]
    n_slots = n * TOP_K
    nb = n_slots // EX_BM + N_EXPERTS
    counts = counts.astype(jnp.int32)
    padded = ((counts + EX_BM - 1) // EX_BM) * EX_BM
    pad_end = jnp.cumsum(padded)
    pad_start = pad_end - padded
    e = eidx[:TOP_K]
    start_of = jnp.sum(jnp.where(e[:, :, None] == jnp.arange(N_EXPERTS, dtype=jnp.int32),
                                 pad_start[None, None, :], 0), axis=-1)
    pos = start_of + rank[:TOP_K]
    inv = _invert(pos.reshape(-1), nb * EX_BM)
    buf_tok = (inv % n) * TOK_ROWS
    dst = inv * TOK_ROWS
    blk_start = jnp.arange(nb, dtype=jnp.int32) * EX_BM
    block_e = jnp.minimum(jnp.sum((pad_end[None, :] <= blk_start[:, None]).astype(jnp.int32), axis=1),
                          N_EXPERTS - 1)
    n_used = (pad_end[-1] // EX_BM).astype(jnp.int32).reshape(1)
    n_valid = jnp.clip(pad_start[block_e] + counts[block_e] - blk_start, 0, EX_BM).astype(jnp.int32)
    n_valid = jnp.where(blk_start < pad_end[-1], n_valid, 0)
    return (block_e, n_used, n_valid, buf_tok.reshape(nb, 1, EX_BM), dst.reshape(nb, 1, EX_BM), n_slots)


def _prep_in_proj_weights(w_in, fox_f_bias, mlstm_i_bias, mlstm_f_bias):
    split_at = []
    acc = 0
    for wdt in SPLIT_WIDTHS[:-1]:
        acc += wdt
        split_at.append(acc)
    fq, fk, fv, ff, mq, mk, mv, mi, mf, mo = jnp.split(w_in, split_at, axis=-1)
    w_main = jnp.concatenate([fq, fk, fv, mq, mk, mv, mo], axis=-1).astype(BF16)
    w_gate = jnp.concatenate([ff, mi, mf], axis=-1)
    bias = jnp.concatenate([fox_f_bias, mlstm_i_bias, mlstm_f_bias]).astype(F32)
    n_gate = w_gate.shape[1]
    wg_hi, wg_lo = _split_hi_lo(jnp.pad(w_gate, ((0, 0), (0, LANES - n_gate))))
    bias_r = jnp.pad(bias, (0, LANES - n_gate))[None, :]
    return w_main, wg_hi, wg_lo, bias_r


def kernel(x, attn_norm_w, w_in, fox_f_bias, fox_q_norm_w, fox_k_norm_w, fox_out_norm_w, mlstm_conv_w, mlstm_i_bias, mlstm_f_bias, mlstm_out_norm_w, w_out, moe_norm_w, router_w, router_b, expert_w_gate_up, expert_b_gate_up, expert_w_down, expert_b_down):
    bsz, seq, d = x.shape
    x2d = x.reshape(bsz * seq, d)
    prep = _prep_in_proj_weights(w_in[0], fox_f_bias[0], mlstm_i_bias[0], mlstm_f_bias[0])
    pair = lambda w: jnp.tile(w, LANES // HEAD_DIM)[None, :]
    qn, kn, aq, ak, vb, projm, gt, gtm = _in_proj(x2d, attn_norm_w[0][None, :], *prep,
                                                  pair(fox_q_norm_w[0]), pair(fox_k_norm_w[0]))
    b3 = lambda a: a.reshape(bsz, seq, a.shape[-1])
    fox_y = _fox(b3(qn), b3(kn), b3(aq), b3(ak), b3(vb), fox_out_norm_w[0][None, :])
    mlstm_y = _mlstm(b3(projm), gt, gtm, mlstm_conv_w[0])
    return _channel_mixer(x2d, fox_y.reshape(-1, FOX_WIDTH), mlstm_y.reshape(-1, MLSTM_V_WIDTH), projm,
                          mlstm_out_norm_w[0], w_out[0], moe_norm_w[0], router_w[0], router_b[0],
                          expert_w_gate_up[0], expert_b_gate_up[0], expert_w_down[0],
                          expert_b_down[0]).reshape(bsz, seq, d)


def _channel_mixer(x2d, fox_y2d, mlstm_y2d, projm, mlstm_gain, w_out, moe_norm_w, router_w, router_b,
                   w_gu, b_gu, w_down, b_down):
    wr_hi, wr_lo = _split_hi_lo(router_w.T)
    x1, h2, eidx, gates, rank, counts = _out_route(
        x2d, fox_y2d, mlstm_y2d, projm, w_out.astype(BF16), mlstm_gain[None, :], moe_norm_w[None, :],
        wr_hi, wr_lo, router_b[:, None])
    block_e, n_used, n_valid, buf_tok3, dst3, n_slots = _dispatch_plan(eidx, rank, counts[:, 0])
    y_slots = _experts(block_e, n_used, n_valid, buf_tok3, dst3, h2, w_gu, b_gu, w_down, b_down, n_slots)
    return _combine(x1, y_slots, gates)
```

```python
import functools
import math

import jax
import jax.numpy as jnp
from jax import lax
from jax.experimental import pallas as pl
from jax.experimental.pallas import tpu as pltpu

F32 = jnp.float32
BF16 = jnp.bfloat16

D_MODEL = 1024
SEQ = 2048
HEAD_DIM = 64
FOX_HEADS = 8
FOX_WIDTH = FOX_HEADS * HEAD_DIM
MLSTM_HEADS = 8
MLSTM_QK_DIM = 32
MLSTM_V_DIM = 64
MLSTM_QK_WIDTH = MLSTM_HEADS * MLSTM_QK_DIM
MLSTM_V_WIDTH = MLSTM_HEADS * MLSTM_V_DIM
CONV_WIDTH = 4
MLSTM_CHUNK = 64
SPLIT_WIDTHS = (FOX_WIDTH, FOX_WIDTH, FOX_WIDTH, FOX_HEADS,
                MLSTM_QK_WIDTH, MLSTM_QK_WIDTH, MLSTM_V_WIDTH,
                MLSTM_HEADS, MLSTM_HEADS, MLSTM_V_WIDTH)
N_EXPERTS = 32
TOP_K = 4
D_EXPERT = D_MODEL
SWIGLU_ALPHA = 1.702
SWIGLU_LIMIT = 7.0
NORM_EPS = 1e-5
LOG2E = 1.4426950408889634

LANES = 128
SUBLANES = 8
V7X_VMEM_BYTES = 64 * 1024 * 1024

MAIN_WIDTH = 3 * FOX_WIDTH + 2 * MLSTM_QK_WIDTH + 2 * MLSTM_V_WIDTH
MLSTM_COLS = MAIN_WIDTH - 3 * FOX_WIDTH
GATE_ROWS = 32

NT_DIMS = (((1,), (1,)), ((), ()))


def _vmem_limit(nbytes):
    return int(min(nbytes + (8 << 20), V7X_VMEM_BYTES - (4 << 20)))


def _log_sigmoid(x):
    return jnp.minimum(x, 0.0) - jnp.log(1.0 + jnp.exp(-jnp.abs(x)))


def _split_hi_lo(x):
    hi = x.astype(BF16)
    lo = (x - hi.astype(F32)).astype(BF16)
    return hi, lo


def _rms(x, w):
    return x * lax.rsqrt(jnp.mean(x * x, axis=-1, keepdims=True) + NORM_EPS) * w


TOK_ROWS = D_MODEL // LANES


def _rows_to_tiles(x, tile_ref):
    m = x.shape[0]
    for j in range(TOK_ROWS):
        tile_ref[pl.ds(j, m, stride=TOK_ROWS), :] = x[:, j * LANES:(j + 1) * LANES]


def _tiles_to_rows(tile_ref, m):
    return jnp.concatenate(
        [tile_ref[pl.ds(j, m, stride=TOK_ROWS), :] for j in range(TOK_ROWS)], axis=1)


IN_TM = 512
IN_TILES_PER_SEQ = SEQ // IN_TM
IN_PARTS = 2
IN_PM = IN_TM // IN_PARTS
IN_SCAN_SHIFTS = tuple(1 << i for i in range(int(math.log2(IN_PM))))


def _pair_rms(x, w, lo_half):
    sq = x * x
    ms_lo = jnp.sum(jnp.where(lo_half, sq, 0.0), axis=-1, keepdims=True) * (1.0 / HEAD_DIM)
    ms_hi = jnp.sum(jnp.where(lo_half, 0.0, sq), axis=-1, keepdims=True) * (1.0 / HEAD_DIM)
    inv = jnp.where(lo_half, lax.rsqrt(ms_lo + NORM_EPS), lax.rsqrt(ms_hi + NORM_EPS))
    return x * inv * w


def _in_proj_kernel(x_ref, nw_ref, w_ref, wg_hi_ref, wg_lo_ref, br_ref, qw_ref, kw_ref,
                    qn_ref, kn_ref, aq_ref, ak_ref, vb_ref, projm_ref, gt_ref, gtm_ref, carry_ref):
    i = pl.program_id(0)

    @pl.when(i % IN_TILES_PER_SEQ == 0)
    def _():
        carry_ref[...] = jnp.zeros_like(carry_ref)

    for part in range(IN_PARTS):
        rows = slice(part * IN_PM, (part + 1) * IN_PM)
        x = x_ref[rows, :]
        ms = jnp.mean(x * x, axis=-1, keepdims=True)
        y = x * lax.rsqrt(ms + NORM_EPS) * nw_ref[...]
        h_hi, h_lo = _split_hi_lo(y)
        main = jnp.dot(h_hi, w_ref[...], preferred_element_type=F32)

        w_hi, w_lo = wg_hi_ref[...], wg_lo_ref[...]
        g = (jnp.dot(h_hi, w_hi, preferred_element_type=F32)
             + jnp.dot(h_hi, w_lo, preferred_element_type=F32)
             + jnp.dot(h_lo, w_hi, preferred_element_type=F32))
        g = g + br_ref[...]
        lane = lax.broadcasted_iota(jnp.int32, g.shape, 1)
        is_input_gate = (lane >= FOX_HEADS) & (lane < FOX_HEADS + MLSTM_HEADS)
        gates = jnp.where(is_input_gate, g, _log_sigmoid(g))
        gtm_ref[rows, :] = gates
        gt_ref[:, rows] = gates.T[:GATE_ROWS, :]

        rowi = lax.broadcasted_iota(jnp.int32, gates.shape, 0)
        c = gates
        for s in IN_SCAN_SHIFTS:
            c = c + jnp.where(rowi >= s, pltpu.roll(c, s, axis=0), 0.0)
        c = c + carry_ref[...]
        carry_ref[...] = c[IN_PM - 1:IN_PM, :]
        cum2 = c * LOG2E

        c8 = jnp.where(lane < FOX_HEADS, cum2, 0.0)
        c_hi = c8.astype(BF16).astype(F32)
        r1 = c8 - c_hi
        c_mid = r1.astype(BF16).astype(F32)
        c_lo = (r1 - c_mid).astype(BF16).astype(F32)
        ones_q = jnp.where((lane >= 3 * FOX_HEADS) & (lane < 6 * FOX_HEADS), 1.0, 0.0)
        ones_k = jnp.where(lane < 3 * FOX_HEADS, 1.0, 0.0)
        aq = (c_hi + pltpu.roll(c_mid, FOX_HEADS, axis=1) + pltpu.roll(c_lo, 2 * FOX_HEADS, axis=1)
              + ones_q)
        ak = ones_k - (pltpu.roll(c_hi, 3 * FOX_HEADS, axis=1) + pltpu.roll(c_mid, 4 * FOX_HEADS, axis=1)
                       + pltpu.roll(c_lo, 5 * FOX_HEADS, axis=1))
        aq_ref[rows, :] = aq.astype(BF16)
        ak_ref[rows, :] = ak.astype(BF16)

        lo_half = lane < HEAD_DIM
        q_scale = (HEAD_DIM ** -0.5) * LOG2E
        for p in range(FOX_HEADS // 2):
            ps = slice(p * LANES, (p + 1) * LANES)
            qn_ref[rows, ps] = (_pair_rms(main[:, ps], qw_ref[...], lo_half) * q_scale).astype(BF16)
            kn_ref[rows, ps] = _pair_rms(main[:, FOX_WIDTH + p * LANES:FOX_WIDTH + (p + 1) * LANES],
                                         kw_ref[...], lo_half).astype(BF16)

        vb_ref[rows, :] = main[:, 2 * FOX_WIDTH:3 * FOX_WIDTH].astype(BF16)
        projm_ref[rows, :] = main[:, 3 * FOX_WIDTH:]


def _in_proj(x2d, norm_w, w_main, wg_hi, wg_lo, bias_r, qw, kw):
    n = x2d.shape[0]
    tm = IN_TM
    const = lambda i: (0, 0)
    row = lambda w: pl.BlockSpec((tm, w), lambda i: (i, 0))
    vmem = (2 * (tm * D_MODEL * 4 + D_MODEL * MAIN_WIDTH * 2 + 3 * tm * FOX_WIDTH * 2 + 2 * tm * LANES * 2
                 + tm * MLSTM_COLS * 4 + GATE_ROWS * tm * 4 + tm * LANES * 4)
            + 2 * tm * MAIN_WIDTH * 4)
    return pl.pallas_call(
        _in_proj_kernel,
        grid=(n // tm,),
        in_specs=[
            row(D_MODEL),
            pl.BlockSpec((1, D_MODEL), const),
            pl.BlockSpec((D_MODEL, MAIN_WIDTH), const),
            pl.BlockSpec((D_MODEL, LANES), const),
            pl.BlockSpec((D_MODEL, LANES), const),
            pl.BlockSpec((1, LANES), const),
            pl.BlockSpec((1, LANES), const),
            pl.BlockSpec((1, LANES), const),
        ],
        out_specs=[
            row(FOX_WIDTH), row(FOX_WIDTH), row(LANES), row(LANES), row(FOX_WIDTH), row(MLSTM_COLS),
            pl.BlockSpec((GATE_ROWS, tm), lambda i: (0, i)),
            row(LANES),
        ],
        out_shape=[
            jax.ShapeDtypeStruct((n, FOX_WIDTH), BF16),
            jax.ShapeDtypeStruct((n, FOX_WIDTH), BF16),
            jax.ShapeDtypeStruct((n, LANES), BF16),
            jax.ShapeDtypeStruct((n, LANES), BF16),
            jax.ShapeDtypeStruct((n, FOX_WIDTH), BF16),
            jax.ShapeDtypeStruct((n, MLSTM_COLS), F32),
            jax.ShapeDtypeStruct((GATE_ROWS, n), F32),
            jax.ShapeDtypeStruct((n, LANES), F32),
        ],
        scratch_shapes=[pltpu.VMEM((1, LANES), F32)],
        compiler_params=pltpu.CompilerParams(
            dimension_semantics=("arbitrary",), vmem_limit_bytes=_vmem_limit(vmem)),
        name="in_proj",
    )(x2d, norm_w, w_main, wg_hi, wg_lo, bias_r, qw, kw)


FOX_TQ = 512


def _fox_kernel(qn_ref, kn_ref, aq_ref, ak_ref, v_ref, ow_ref, o_ref, qa_ref, ka_ref, va_ref):
    hp = pl.program_id(1)
    tri = (lax.broadcasted_iota(jnp.int32, (FOX_TQ, FOX_TQ), 1)
           <= lax.broadcasted_iota(jnp.int32, (FOX_TQ, FOX_TQ), 0))
    lo_q = lax.broadcasted_iota(jnp.int32, (FOX_TQ, LANES), 1) < HEAD_DIM
    lane = lax.broadcasted_iota(jnp.int32, (1, LANES), 1)
    ka_ref[:, :LANES] = kn_ref[0]
    ka_ref[:, LANES:] = ak_ref[0]
    for j in range(2):
        cs = slice(j * HEAD_DIM, (j + 1) * HEAD_DIM)
        own_q = lo_q if j == 0 else jnp.logical_not(lo_q)
        h = 2 * hp + j
        own_mask = jnp.where((lane < HEAD_DIM) if j == 0 else (lane >= HEAD_DIM), 1.0, 0.0).astype(BF16)
        bias_mask = jnp.where((lane < 6 * FOX_HEADS) & (lane % FOX_HEADS == h), 1.0, 0.0).astype(BF16)
        qa_ref[j, :, :LANES] = qn_ref[0] * own_mask
        qa_ref[j, :, LANES:] = aq_ref[0] * bias_mask
        sum_lane = HEAD_DIM if j == 0 else 0
        va_ref[j] = v_ref[0] * own_mask + jnp.where(lane == sum_lane, 1.0, 0.0).astype(BF16)
        for i in range(SEQ // FOX_TQ):
            qs = slice(i * FOX_TQ, (i + 1) * FOX_TQ)
            n = (i + 1) * FOX_TQ
            s = lax.dot_general(qa_ref[j, qs, :], ka_ref[:n, :], NT_DIMS,
                                preferred_element_type=F32)
            diag = jnp.where(tri, s[:, n - FOX_TQ:], -jnp.inf)
            m = jnp.max(diag, axis=-1, keepdims=True)
            if i > 0:
                past = s[:, :n - FOX_TQ]
                m = jnp.maximum(m, jnp.max(past, axis=-1, keepdims=True))
                p = jnp.concatenate([jnp.exp2(past - m), jnp.exp2(diag - m)], axis=1)
            else:
                p = jnp.exp2(diag - m)
            o = jnp.dot(p.astype(BF16), va_ref[j, :n, :], preferred_element_type=F32)
            o = o / o[:, sum_lane:sum_lane + 1]
            ms = jnp.sum(jnp.where(own_q, o * o, 0.0), axis=-1, keepdims=True) * (1.0 / HEAD_DIM)
            on = o * lax.rsqrt(ms + NORM_EPS) * ow_ref[...]
            o_ref[0, qs, cs] = on[:, cs]


def _fox(qn3, kn3, aq3, ak3, v3, ow):
    b = qn3.shape[0]
    nq = FOX_WIDTH // LANES
    blk = (1, SEQ, LANES)
    pair = pl.BlockSpec(blk, lambda bi, hp: (bi, 0, hp))
    shared = pl.BlockSpec(blk, lambda bi, hp: (bi, 0, 0))
    vmem = 2 * (5 * SEQ * LANES * 2 + SEQ * LANES * 4) + 3 * SEQ * 2 * LANES * 2 + 8 * FOX_TQ * SEQ * 4
    return pl.pallas_call(
        _fox_kernel,
        grid=(b, nq),
        in_specs=[pair, pair, shared, shared, pair, pl.BlockSpec((1, LANES), lambda bi, hp: (0, hp))],
        out_specs=pl.BlockSpec(blk, lambda bi, hp: (bi, 0, hp)),
        out_shape=jax.ShapeDtypeStruct((b, SEQ, FOX_WIDTH), F32),
        scratch_shapes=[pltpu.VMEM((2, SEQ, 2 * LANES), BF16), pltpu.VMEM((SEQ, 2 * LANES), BF16),
                        pltpu.VMEM((2, SEQ, LANES), BF16)],
        compiler_params=pltpu.CompilerParams(
            dimension_semantics=("arbitrary", "arbitrary"), vmem_limit_bytes=_vmem_limit(vmem)),
        name="fox",
    )(qn3, kn3, aq3, ak3, v3, ow)


ML_L = MLSTM_CHUNK
ML_PAIRS = SEQ // (2 * ML_L)
ML_HL = FOX_HEADS
ML_AUG = MLSTM_V_WIDTH + LANES
ML_TILE = 256
SEG_SHIFTS = tuple(1 << i for i in range(int(math.log2(ML_L))))


def _split3(x):
    a = x.astype(BF16)
    r = x - a.astype(F32)
    b = r.astype(BF16)
    c = (r - b.astype(F32)).astype(BF16)
    return a, b, c


def _expand_heads(x, exp_bf):
    a, b, c = _split3(x)
    return (jnp.dot(a, exp_bf, preferred_element_type=F32)
            + jnp.dot(b, exp_bf, preferred_element_type=F32)
            + jnp.dot(c, exp_bf, preferred_element_type=F32))


def _seg_scan(x, axis, op, ident):
    idx = lax.broadcasted_iota(jnp.int32, x.shape, axis) % ML_L
    for s in SEG_SHIFTS:
        x = op(x, jnp.where(idx >= s, pltpu.roll(x, s, axis=axis), ident))
    return x


def _mlstm_kernel(q_ref, k_ref, v_ref, gi_ref, gf_ref, gtm_ref, cw_ref, o_ref, den_ref,
                  qc_ref, kc_ref, kt_ref, rr_ref, cmr_ref, bcr_ref,
                  ealpha_ref, ew_ref, wint_ref, floor_ref, mfull_ref, caug_ref):
    def conv_silu(u, w):
        rowi = lax.broadcasted_iota(jnp.int32, u.shape, 0)
        acc = u * w[CONV_WIDTH - 1:CONV_WIDTH, :]
        for d in range(1, CONV_WIDTH):
            sh = jnp.where(rowi >= d, pltpu.roll(u, d, axis=0), 0.0)
            acc = acc + sh * w[CONV_WIDTH - 1 - d:CONV_WIDTH - d, :]
        return acc / (1.0 + jnp.exp(-acc))

    cw = cw_ref[...]
    qc_ref[...] = conv_silu(q_ref[0], cw[:, :MLSTM_QK_WIDTH]).astype(BF16)
    kc = conv_silu(k_ref[0], cw[:, MLSTM_QK_WIDTH:]) * (MLSTM_QK_DIM ** -0.5)
    kc_ref[...] = kc.astype(BF16)
    kt = kc.T
    for p in range(ML_PAIRS):
        kt_ref[p] = kt[:, p * LANES:(p + 1) * LANES]

    bcum_r = _seg_scan(gf_ref[...], 1, jnp.add, 0.0)
    r_r = gi_ref[...] - bcum_r
    cmx_r = _seg_scan(r_r, 1, jnp.maximum, -jnp.inf)
    for p in range(ML_PAIRS):
        ls = slice(p * LANES, (p + 1) * LANES)
        rr_ref[p] = r_r[:, ls]
        cmr_ref[p] = cmx_r[:, ls]
        bcr_ref[p] = bcum_r[:, ls]

    g = gtm_ref[...]
    lane_g = lax.broadcasted_iota(jnp.int32, g.shape, 1)
    head_lane = (lane_g >= ML_HL) & (lane_g < ML_HL + MLSTM_HEADS)
    bcum_c = jnp.where(
        head_lane, pltpu.roll(_seg_scan(g, 0, jnp.add, 0.0), LANES - MLSTM_HEADS, axis=1), 0.0)
    cmx_c = _seg_scan(jnp.where(head_lane, g, 0.0) - bcum_c, 0, jnp.maximum, -jnp.inf)
    m = jnp.zeros((1, LANES), F32)
    for c in range(SEQ // ML_L):
        mfull_ref[c * ML_L:(c + 1) * ML_L, :] = jnp.broadcast_to(m, (ML_L, LANES))
        last = (c + 1) * ML_L - 1
        m = bcum_c[last:last + 1, :] + jnp.maximum(m, cmx_c[last:last + 1, :])
    mfull = mfull_ref[...]
    mx = jnp.maximum(mfull, cmx_c)
    wint_ref[...] = jnp.exp(mfull - mx)
    floor_ref[...] = jnp.exp(-(bcum_c + mx))
    mfull_ref[...] = -mx

    lane_e = lax.broadcasted_iota(jnp.int32, (LANES, MLSTM_V_WIDTH), 1) // MLSTM_V_DIM
    row_e = lax.broadcasted_iota(jnp.int32, (LANES, MLSTM_V_WIDTH), 0)
    exp_bf = jnp.where(row_e == lane_e + ML_HL, 1.0, 0.0).astype(BF16)

    def expand_tile(i, carry):
        rows = pl.ds(pl.multiple_of(i * ML_TILE, ML_TILE), ML_TILE)
        ealpha_ref[rows, :] = _expand_heads(mfull_ref[rows, :], exp_bf)
        ew_ref[rows, :] = _expand_heads(wint_ref[rows, :], exp_bf)
        return carry

    lax.fori_loop(0, SEQ // ML_TILE, expand_tile, 0)

    kb_rowh = lax.broadcasted_iota(jnp.int32, (MLSTM_HEADS * ML_L, MLSTM_QK_WIDTH), 0) // ML_L
    kb_lane = lax.broadcasted_iota(jnp.int32, (MLSTM_HEADS * ML_L, MLSTM_QK_WIDTH), 1) // MLSTM_QK_DIM
    mask_k = jnp.where(kb_rowh == kb_lane, 1.0, 0.0).astype(BF16)
    va_rowh = lax.broadcasted_iota(jnp.int32, (MLSTM_HEADS * ML_L, ML_AUG), 0) // ML_L
    va_col = lax.broadcasted_iota(jnp.int32, (MLSTM_HEADS * ML_L, ML_AUG), 1)
    mask_v = jnp.where(
        (va_col // MLSTM_V_DIM == va_rowh) | (va_col == MLSTM_V_WIDTH + ML_HL + va_rowh),
        1.0, 0.0).astype(BF16)
    c_rowh = lax.broadcasted_iota(jnp.int32, (MLSTM_QK_WIDTH, ML_AUG), 0) // MLSTM_QK_DIM
    c_col = lax.broadcasted_iota(jnp.int32, (MLSTM_QK_WIDTH, ML_AUG), 1)
    mask_c = (c_col // MLSTM_V_DIM == c_rowh) | (c_col == MLSTM_V_WIDTH + ML_HL + c_rowh)
    lane128 = lax.broadcasted_iota(jnp.int32, (ML_L, LANES), 1)
    s_idx = lax.broadcasted_iota(jnp.int32, (ML_L, MLSTM_V_WIDTH), 1) % ML_L
    t_idx = lax.broadcasted_iota(jnp.int32, (ML_L, MLSTM_V_WIDTH), 0)
    causal = s_idx <= t_idx
    ones_aug = jnp.ones((ML_L, LANES), F32)

    caug_ref[...] = jnp.zeros_like(caug_ref)

    def pair_body(cp, m_row):
        r2 = rr_ref[cp]
        cm2 = cmr_ref[cp]
        b2 = bcr_ref[cp]
        r2r = pltpu.roll(r2, ML_L, axis=1)
        kt2 = kt_ref[cp]
        for cc in range(2):
            lo = cc * ML_L
            rows = pl.ds(pl.multiple_of(cp * (2 * ML_L), 2 * ML_L) + lo, ML_L)
            qa = qc_ref[rows, :]
            ka = kc_ref[rows, :]
            va_aug = jnp.concatenate([v_ref[0, rows, :], ones_aug], axis=1).astype(BF16)

            kbd = jnp.concatenate([ka] * MLSTM_HEADS, axis=0) * mask_k
            s = lax.dot_general(qa, kbd, NT_DIMS, preferred_element_type=F32)

            cmx_last = cm2[:, lo + ML_L - 1:lo + ML_L]
            b_last = b2[:, lo + ML_L - 1:lo + ML_L]
            mx_r = jnp.maximum(m_row, cmx_last)
            decay = jnp.exp(m_row - mx_r)
            wk = jnp.exp(r2[:, lo:lo + ML_L] - mx_r)
            m_row = b_last + mx_r

            src_e, src_o = (r2, r2r) if cc == 0 else (r2r, r2)
            cols = []
            for p in range(MLSTM_HEADS // 2):
                even = jnp.broadcast_to(src_e[2 * p:2 * p + 1, :], (ML_L, LANES))
                odd = jnp.broadcast_to(src_o[2 * p + 1:2 * p + 2, :], (ML_L, LANES))
                cols.append(jnp.where(lane128 < ML_L, even, odd))
            r_all = jnp.concatenate(cols, axis=1)
            arg = jnp.where(causal, ealpha_ref[rows, :] + r_all, -jnp.inf)
            p_all = (s * jnp.exp(arg)).astype(BF16)

            vbd = jnp.concatenate([va_aug] * MLSTM_HEADS, axis=0) * mask_v
            pv = jnp.dot(p_all, vbd, preferred_element_type=F32)
            qc_state = jnp.dot(qa, caug_ref[...].astype(BF16), preferred_element_type=F32)
            o_ref[0, rows, :] = (ew_ref[rows, :] * qc_state[:, :MLSTM_V_WIDTH]
                                 + pv[:, :MLSTM_V_WIDTH])
            den_ref[rows, :] = (wint_ref[rows, :] * qc_state[:, MLSTM_V_WIDTH:]
                                + pv[:, MLSTM_V_WIDTH:])

            wk_rows = jnp.concatenate(
                [jnp.broadcast_to(wk[h:h + 1, :], (MLSTM_QK_DIM, ML_L)) for h in range(MLSTM_HEADS)],
                axis=0)
            dec_rows = jnp.concatenate(
                [jnp.broadcast_to(decay[h:h + 1, :], (MLSTM_QK_DIM, 1)) for h in range(MLSTM_HEADS)],
                axis=0)
            ktw = (kt2[:, lo:lo + ML_L] * wk_rows).astype(BF16)
            upd = jnp.dot(ktw, va_aug, preferred_element_type=F32)
            caug_ref[...] = dec_rows * caug_ref[...] + jnp.where(mask_c, upd, 0.0)
        return m_row

    lax.fori_loop(0, ML_PAIRS, pair_body, jnp.zeros((MLSTM_HEADS, 1), F32))

    ob_row = lax.broadcasted_iota(jnp.int32, (MLSTM_V_WIDTH, LANES), 0) // MLSTM_V_DIM
    ob_col = lax.broadcasted_iota(jnp.int32, (MLSTM_V_WIDTH, LANES), 1)
    ones_bd = jnp.where(ob_col == ob_row + ML_HL, 1.0, 0.0).astype(BF16)

    def norm_tile(i, carry):
        rows = pl.ds(pl.multiple_of(i * ML_TILE, ML_TILE), ML_TILE)
        num = o_ref[0, rows, :]
        dn = jnp.maximum(jnp.abs(den_ref[rows, :]), floor_ref[rows, :])
        r = 1.0 / dn
        n2_hi, n2_lo = _split_hi_lo(num * num)
        msn = (jnp.dot(n2_hi, ones_bd, preferred_element_type=F32)
               + jnp.dot(n2_lo, ones_bd, preferred_element_type=F32)) * (1.0 / MLSTM_V_DIM)
        fac = r * lax.rsqrt(r * r * msn + NORM_EPS)
        o_ref[0, rows, :] = num * _expand_heads(fac, exp_bf)
        return carry

    lax.fori_loop(0, SEQ // ML_TILE, norm_tile, 0)


def _mlstm(projm3, gt, gtm, conv_w):
    b = projm3.shape[0]
    qk_blk = (1, SEQ, MLSTM_QK_WIDTH)
    v_blk = (1, SEQ, MLSTM_V_WIDTH)
    v_col = 2 * MLSTM_QK_WIDTH // MLSTM_V_WIDTH
    vmem = (2 * (2 * SEQ * MLSTM_QK_WIDTH * 4 + 2 * SEQ * MLSTM_V_WIDTH * 4 + 2 * SEQ * LANES * 4)
            + 2 * SEQ * MLSTM_QK_WIDTH * 2 + SEQ * MLSTM_QK_WIDTH * 4 + 2 * SEQ * MLSTM_V_WIDTH * 4
            + 3 * SEQ * LANES * 4 + (8 << 20))
    return pl.pallas_call(
        _mlstm_kernel,
        grid=(b,),
        in_specs=[
            pl.BlockSpec(qk_blk, lambda bi: (bi, 0, 0)),
            pl.BlockSpec(qk_blk, lambda bi: (bi, 0, 1)),
            pl.BlockSpec(v_blk, lambda bi: (bi, 0, v_col)),
            pl.BlockSpec((SUBLANES, SEQ), lambda bi: (1, bi)),
            pl.BlockSpec((SUBLANES, SEQ), lambda bi: (2, bi)),
            pl.BlockSpec((SEQ, LANES), lambda bi: (bi, 0)),
            pl.BlockSpec((CONV_WIDTH, 2 * MLSTM_QK_WIDTH), lambda bi: (0, 0)),
        ],
        out_specs=pl.BlockSpec(v_blk, lambda bi: (bi, 0, 0)),
        out_shape=jax.ShapeDtypeStruct((b, SEQ, MLSTM_V_WIDTH), F32),
        scratch_shapes=[
            pltpu.VMEM((SEQ, LANES), F32),
            pltpu.VMEM((SEQ, MLSTM_QK_WIDTH), BF16),
            pltpu.VMEM((SEQ, MLSTM_QK_WIDTH), BF16),
            pltpu.VMEM((ML_PAIRS, MLSTM_QK_WIDTH, LANES), F32),
            pltpu.VMEM((ML_PAIRS, SUBLANES, LANES), F32),
            pltpu.VMEM((ML_PAIRS, SUBLANES, LANES), F32),
            pltpu.VMEM((ML_PAIRS, SUBLANES, LANES), F32),
            pltpu.VMEM((SEQ, MLSTM_V_WIDTH), F32),
            pltpu.VMEM((SEQ, MLSTM_V_WIDTH), F32),
            pltpu.VMEM((SEQ, LANES), F32),
            pltpu.VMEM((SEQ, LANES), F32),
            pltpu.VMEM((SEQ, LANES), F32),
            pltpu.VMEM((MLSTM_QK_WIDTH, ML_AUG), F32),
        ],
        compiler_params=pltpu.CompilerParams(
            dimension_semantics=("arbitrary",), vmem_limit_bytes=_vmem_limit(vmem)),
        name="mlstm",
    )(projm3, projm3, projm3, gt, gt, gtm, conv_w)


RT_TM = 512


def _out_route_kernel(x_ref, fy_ref, my_ref, mo_ref, wo_ref, mg_ref, nw_ref, wr_hi_ref, wr_lo_ref,
                      rb_ref, x1_ref, h2_ref, eidx_ref, gate_ref, rank_ref, cnt_ref, carry_ref):
    i = pl.program_id(0)

    @pl.when(i == 0)
    def _():
        carry_ref[...] = jnp.zeros_like(carry_ref)

    my = my_ref[...] * mg_ref[...] / (1.0 + jnp.exp(-mo_ref[...]))
    mixed = (jnp.dot(fy_ref[...].astype(BF16), wo_ref[:FOX_WIDTH, :], preferred_element_type=F32)
             + jnp.dot(my.astype(BF16), wo_ref[FOX_WIDTH:, :], preferred_element_type=F32))
    x1 = x_ref[...] + mixed
    x1_ref[...] = x1
    h2 = _rms(x1, nw_ref[...])
    _rows_to_tiles(h2, h2_ref)

    h_hi, h_lo = _split_hi_lo(h2)
    wr_hi, wr_lo = wr_hi_ref[...], wr_lo_ref[...]
    logit = (lax.dot_general(wr_hi, h_hi, NT_DIMS, preferred_element_type=F32)
             + lax.dot_general(wr_lo, h_hi, NT_DIMS, preferred_element_type=F32)
             + lax.dot_general(wr_hi, h_lo, NT_DIMS, preferred_element_type=F32)) + rb_ref[...]

    e_iota = lax.broadcasted_iota(jnp.int32, logit.shape, 0).astype(F32)
    vals, idxs, hots = [], [], []
    for _ in range(TOP_K):
        mk = jnp.max(logit, axis=0, keepdims=True)
        idx = jnp.min(jnp.where(logit == mk, e_iota, float(N_EXPERTS)), axis=0, keepdims=True)
        hot = e_iota == idx
        logit = jnp.where(hot, -jnp.inf, logit)
        vals.append(mk)
        idxs.append(idx.astype(jnp.int32))
        hots.append(hot)
    exps = [jnp.exp(v - vals[0]) for v in vals]
    tot = exps[0] + exps[1] + exps[2] + exps[3]
    gates = [e / tot for e in exps]

    assign = jnp.zeros(logit.shape, F32)
    for hot in hots:
        assign = assign + jnp.where(hot, 1.0, 0.0)
    tm = logit.shape[1]
    src = lax.broadcasted_iota(jnp.int32, (tm, tm), 0)
    dst = lax.broadcasted_iota(jnp.int32, (tm, tm), 1)
    upper = jnp.where(src < dst, 1.0, 0.0).astype(BF16)
    base = jnp.dot(assign.astype(BF16), upper, preferred_element_type=F32) + carry_ref[:, 0:1]
    ranks = [jnp.sum(jnp.where(hot, base, 0.0), axis=0, keepdims=True) for hot in hots]
    new_carry = carry_ref[...] + jnp.sum(assign, axis=1, keepdims=True)
    carry_ref[...] = new_carry
    cnt_ref[...] = new_carry

    zi = jnp.zeros((SUBLANES - TOP_K, tm), jnp.int32)
    eidx_ref[...] = jnp.concatenate(idxs + [zi], axis=0)
    rank_ref[...] = jnp.concatenate([r.astype(jnp.int32) for r in ranks] + [zi], axis=0)
    gate_ref[...] = jnp.concatenate(gates + [zi.astype(F32)], axis=0)


def _out_route(x2d, fox_y2d, mlstm_y2d, projm, w_out_bf, mlstm_gain, moe_norm_w, wr_hi, wr_lo, rb):
    n = x2d.shape[0]
    tm = RT_TM
    const = lambda i: (0, 0)
    mo_col = (MLSTM_COLS - MLSTM_V_WIDTH) // MLSTM_V_WIDTH
    row_blk = lambda w: pl.BlockSpec((tm, w), lambda i: (i, 0))
    lane_blk = pl.BlockSpec((SUBLANES, tm), lambda i: (0, i))
    vmem = (2 * (tm * D_MODEL * 4 * 3 + tm * FOX_WIDTH * 4 * 3 + D_MODEL * D_MODEL * 2)
            + 6 * tm * D_MODEL * 4 + tm * tm * 6)
    return pl.pallas_call(
        _out_route_kernel,
        grid=(n // tm,),
        in_specs=[
            row_blk(D_MODEL), row_blk(FOX_WIDTH), row_blk(MLSTM_V_WIDTH),
            pl.BlockSpec((tm, MLSTM_V_WIDTH), lambda i: (i, mo_col)),
            pl.BlockSpec((D_MODEL, D_MODEL), const),
            pl.BlockSpec((1, MLSTM_V_WIDTH), const),
            pl.BlockSpec((1, D_MODEL), const),
            pl.BlockSpec((N_EXPERTS, D_MODEL), const),
            pl.BlockSpec((N_EXPERTS, D_MODEL), const),
            pl.BlockSpec((N_EXPERTS, 1), const),
        ],
        out_specs=[row_blk(D_MODEL), pl.BlockSpec((tm * TOK_ROWS, LANES), lambda i: (i, 0)),
                   lane_blk, lane_blk, lane_blk, pl.BlockSpec((N_EXPERTS, LANES), const)],
        out_shape=[
            jax.ShapeDtypeStruct((n, D_MODEL), F32),
            jax.ShapeDtypeStruct((n * TOK_ROWS, LANES), F32),
            jax.ShapeDtypeStruct((SUBLANES, n), jnp.int32),
            jax.ShapeDtypeStruct((SUBLANES, n), F32),
            jax.ShapeDtypeStruct((SUBLANES, n), jnp.int32),
            jax.ShapeDtypeStruct((N_EXPERTS, LANES), F32),
        ],
        scratch_shapes=[pltpu.VMEM((N_EXPERTS, LANES), F32)],
        compiler_params=pltpu.CompilerParams(
            dimension_semantics=("arbitrary",), vmem_limit_bytes=_vmem_limit(vmem)),
        name="out_route",
    )(x2d, fox_y2d, mlstm_y2d, projm, w_out_bf, mlstm_gain, moe_norm_w, wr_hi, wr_lo, rb)


INV_CHUNK = 8192
INV_UNROLL = 8


def _invert_kernel(pos_ref, zeros_hbm, inv_ref, sem):
    i = pl.program_id(0)

    @pl.when(i == 0)
    def _():
        cp = pltpu.make_async_copy(zeros_hbm, inv_ref, sem.at[0])
        cp.start()
        cp.wait()

    base = i * INV_CHUNK

    def body(j, carry):
        inv_ref[pos_ref[0, 0, j]] = base + j
        return carry

    lax.fori_loop(0, INV_CHUNK, body, 0, unroll=INV_UNROLL)


def _invert(pos_flat, n_rows):
    n_slots = pos_flat.shape[0]
    steps = n_slots // INV_CHUNK
    return pl.pallas_call(
        _invert_kernel,
        grid=(steps,),
        in_specs=[
            pl.BlockSpec((1, 1, INV_CHUNK), lambda i: (i, 0, 0), memory_space=pltpu.SMEM),
            pl.BlockSpec(memory_space=pl.ANY),
        ],
        out_specs=pl.BlockSpec(memory_space=pltpu.SMEM),
        out_shape=jax.ShapeDtypeStruct((n_rows,), jnp.int32),
        scratch_shapes=[pltpu.SemaphoreType.DMA((1,))],
        compiler_params=pltpu.CompilerParams(dimension_semantics=("arbitrary",)),
        name="invert",
    )(pos_flat.reshape(steps, 1, INV_CHUNK), jnp.zeros((n_rows,), jnp.int32))


EX_BM = 256
EX_DRAIN_STEPS = 2


def _experts_kernel(be_ref, nu_ref, nv_ref, tok_ref, tokn_ref, dst_ref, h2_hbm, wgu_ref, bgu_ref,
                    wd_ref, bd_ref, y_hbm, xt_ref, yt_ref, wgu_bf_ref, wd_bf_ref, gsem, ssem):
    i = pl.program_id(0)
    last_blk = pl.num_programs(0) - 1 - EX_DRAIN_STEPS
    nu = nu_ref[0]
    slot = i % 2
    cur = jnp.minimum(i, last_blk)
    tile_rows = EX_BM * TOK_ROWS

    def tok_tile(ref, idx):
        return ref.at[pl.ds(pl.multiple_of(idx, TOK_ROWS), TOK_ROWS), :]

    def start_gather(idx_ref, s, n):
        @pl.when(n == EX_BM)
        def _():
            for r in range(EX_BM):
                pltpu.make_async_copy(tok_tile(h2_hbm, idx_ref[0, 0, r]),
                                      xt_ref.at[s, pl.ds(r * TOK_ROWS, TOK_ROWS), :], gsem.at[s]).start()

        @pl.when(n < EX_BM)
        def _():
            def body(r, carry):
                pltpu.make_async_copy(
                    tok_tile(h2_hbm, idx_ref[0, 0, r]),
                    xt_ref.at[s, pl.ds(pl.multiple_of(r * TOK_ROWS, TOK_ROWS), TOK_ROWS), :],
                    gsem.at[s]).start()
                return carry
            lax.fori_loop(0, n, body, 0)

    def scatter_row(r):
        pltpu.make_async_copy(yt_ref.at[slot, pl.ds(r * TOK_ROWS, TOK_ROWS), :],
                              tok_tile(y_hbm, dst_ref[0, 0, r]), ssem.at[slot]).start()

    def wait_block(sem, buf):
        pltpu.make_async_copy(h2_hbm.at[pl.ds(0, tile_rows), :], buf, sem).wait()

    def wait_tokens(sem, buf, n):
        @pl.when(n == EX_BM)
        def _():
            wait_block(sem, buf)

        @pl.when(n < EX_BM)
        def _():
            def body(r, carry):
                pltpu.make_async_copy(h2_hbm.at[pl.ds(0, TOK_ROWS), :],
                                      buf.at[pl.ds(0, TOK_ROWS), :], sem).wait()
                return carry
            lax.fori_loop(0, n, body, 0)

    @pl.when(i == 0)
    def _():
        xt_ref[...] = jnp.zeros_like(xt_ref)
        start_gather(tok_ref, 0, nv_ref[0])

    @pl.when(i + 1 < nu)
    def _():
        start_gather(tokn_ref, 1 - slot, nv_ref[jnp.minimum(i + 1, last_blk)])

    @pl.when((i >= 2) & (i - 2 < nu))
    def _():
        wait_tokens(ssem.at[slot], yt_ref.at[slot], nv_ref[jnp.clip(i - 2, 0, last_blk)])

    @pl.when(i < nu)
    def _():
        @pl.when((i == 0) | (be_ref[cur] != be_ref[jnp.maximum(cur - 1, 0)]))
        def _():
            wgu_bf_ref[...] = wgu_ref[0].astype(BF16)
            wd_bf_ref[...] = wd_ref[0].astype(BF16)

        wait_tokens(gsem.at[slot], xt_ref.at[slot], nv_ref[cur])
        xb = _tiles_to_rows(xt_ref.at[slot], EX_BM).astype(BF16)
        gu = jnp.dot(xb, wgu_bf_ref[...], preferred_element_type=F32) + bgu_ref[0]
        gate = jnp.minimum(gu[:, :D_EXPERT], SWIGLU_LIMIT)
        up = jnp.clip(gu[:, D_EXPERT:], -SWIGLU_LIMIT, SWIGLU_LIMIT)
        act = (up + 1.0) * (gate / (1.0 + jnp.exp(-SWIGLU_ALPHA * gate)))
        y = jnp.dot(act.astype(BF16), wd_bf_ref[...], preferred_element_type=F32) + bd_ref[0]
        _rows_to_tiles(y, yt_ref.at[slot])
        nv = nv_ref[cur]

        @pl.when(nv == EX_BM)
        def _():
            for r in range(EX_BM):
                scatter_row(r)

        @pl.when(nv < EX_BM)
        def _():
            def body(r, carry):
                pltpu.make_async_copy(
                    yt_ref.at[slot, pl.ds(pl.multiple_of(r * TOK_ROWS, TOK_ROWS), TOK_ROWS), :],
                    tok_tile(y_hbm, dst_ref[0, 0, r]), ssem.at[slot]).start()
                return carry
            lax.fori_loop(0, nv, body, 0)


def _experts(block_e, n_used, n_valid, buf_tok3, dst3, h2, w_gu, b_gu, w_down, b_down, n_slots):
    nb = buf_tok3.shape[0]
    idx_blk = lambda f: pl.BlockSpec((1, 1, EX_BM), f, memory_space=pltpu.SMEM)
    vmem = (2 * (D_MODEL * 2 * D_EXPERT * 4 + D_EXPERT * D_MODEL * 4)
            + D_MODEL * 2 * D_EXPERT * 2 + D_EXPERT * D_MODEL * 2
            + 3 * EX_BM * D_MODEL * 4 + 3 * EX_BM * 2 * D_EXPERT * 4)
    blk = lambda i: jnp.minimum(i, nb - 1)
    w_map = lambda i, be, nu, nv: (be[blk(i)], 0, 0)
    grid_spec = pltpu.PrefetchScalarGridSpec(
        num_scalar_prefetch=3,
        grid=(nb + EX_DRAIN_STEPS,),
        in_specs=[
            idx_blk(lambda i, be, nu, nv: (blk(i), 0, 0)),
            idx_blk(lambda i, be, nu, nv: (blk(i + 1), 0, 0)),
            idx_blk(lambda i, be, nu, nv: (blk(i), 0, 0)),
            pl.BlockSpec(memory_space=pl.ANY),
            pl.BlockSpec((1, D_MODEL, 2 * D_EXPERT), w_map),
            pl.BlockSpec((1, 1, 2 * D_EXPERT), w_map),
            pl.BlockSpec((1, D_EXPERT, D_MODEL), w_map),
            pl.BlockSpec((1, 1, D_MODEL), w_map),
        ],
        out_specs=pl.BlockSpec(memory_space=pl.ANY),
        scratch_shapes=[
            pltpu.VMEM((2, EX_BM * TOK_ROWS, LANES), F32),
            pltpu.VMEM((2, EX_BM * TOK_ROWS, LANES), F32),
            pltpu.VMEM((D_MODEL, 2 * D_EXPERT), BF16),
            pltpu.VMEM((D_EXPERT, D_MODEL), BF16),
            pltpu.SemaphoreType.DMA((2,)),
            pltpu.SemaphoreType.DMA((2,)),
        ],
    )
    return pl.pallas_call(
        _experts_kernel,
        grid_spec=grid_spec,
        out_shape=jax.ShapeDtypeStruct((n_slots * TOK_ROWS, LANES), F32),
        compiler_params=pltpu.CompilerParams(
            dimension_semantics=("arbitrary",), vmem_limit_bytes=_vmem_limit(vmem)),
        name="experts",
    )(block_e, n_used, n_valid, buf_tok3, buf_tok3, dst3, h2, w_gu, b_gu[:, None, :], w_down,
      b_down[:, None, :])


CB_TM = 256


def _combine_kernel(x1_ref, y0_ref, y1_ref, y2_ref, y3_ref, gate_ref, o_ref):
    tm = x1_ref.shape[0]
    g = jnp.concatenate([gate_ref[...], jnp.zeros((LANES - SUBLANES, tm), F32)], axis=0).T
    acc = x1_ref[...]
    for k, y_ref in enumerate((y0_ref, y1_ref, y2_ref, y3_ref)):
        acc = acc + g[:, k:k + 1] * _tiles_to_rows(y_ref, tm)
    o_ref[...] = acc


def _combine(x1, y_slots, gates):
    n = x1.shape[0]
    tm = CB_TM
    nt = n // tm
    vmem = 2 * (2 * tm * D_MODEL * 4 + tm * TOP_K * D_MODEL * 4) + 4 * tm * D_MODEL * 4
    y_spec = lambda k: pl.BlockSpec((tm * TOK_ROWS, LANES), lambda i: (k * nt + i, 0))
    return pl.pallas_call(
        _combine_kernel,
        grid=(nt,),
        in_specs=[pl.BlockSpec((tm, D_MODEL), lambda i: (i, 0))]
        + [y_spec(k) for k in range(TOP_K)]
        + [pl.BlockSpec((SUBLANES, tm), lambda i: (0, i))],
        out_specs=pl.BlockSpec((tm, D_MODEL), lambda i: (i, 0)),
        out_shape=jax.ShapeDtypeStruct((n, D_MODEL), F32),
        compiler_params=pltpu.CompilerParams(
            dimension_semantics=("arbitrary",), vmem_limit_bytes=_vmem_limit(vmem)),
        name="combine",
    )(x1, y_slots, y_slots, y_slots, y_slots, gates)


def _dispatch_plan(eidx, rank, counts):
    n = eidx.shape[1]
    n_slots = n * TOP_K
    nb = n_slots // EX_BM + N_EXPERTS
    counts = counts.astype(jnp.int32)
    padded = ((counts + EX_BM - 1) // EX_BM) * EX_BM
    pad_end = jnp.cumsum(padded)
    pad_start = pad_end - padded
    e = eidx[:TOP_K]
    start_of = jnp.sum(jnp.where(e[:, :, None] == jnp.arange(N_EXPERTS, dtype=jnp.int32),
                                 pad_start[None, None, :], 0), axis=-1)
    pos = start_of + rank[:TOP_K]
    inv = _invert(pos.reshape(-1), nb * EX_BM)
    buf_tok = (inv % n) * TOK_ROWS
    dst = inv * TOK_ROWS
    blk_start = jnp.arange(nb, dtype=jnp.int32) * EX_BM
    block_e = jnp.minimum(jnp.sum((pad_end[None, :] <= blk_start[:, None]).astype(jnp.int32), axis=1),
                          N_EXPERTS - 1)
    n_used = (pad_end[-1] // EX_BM).astype(jnp.int32).reshape(1)
    n_valid = jnp.clip(pad_start[block_e] + counts[block_e] - blk_start, 0, EX_BM).astype(jnp.int32)
    n_valid = jnp.where(blk_start < pad_end[-1], n_valid, 0)
    return (block_e, n_used, n_valid, buf_tok.reshape(nb, 1, EX_BM), dst.reshape(nb, 1, EX_BM), n_slots)


def _prep_in_proj_weights(w_in, fox_f_bias, mlstm_i_bias, mlstm_f_bias):
    split_at = []
    acc = 0
    for wdt in SPLIT_WIDTHS[:-1]:
        acc += wdt
        split_at.append(acc)
    fq, fk, fv, ff, mq, mk, mv, mi, mf, mo = jnp.split(w_in, split_at, axis=-1)
    w_main = jnp.concatenate([fq, fk, fv, mq, mk, mv, mo], axis=-1).astype(BF16)
    w_gate = jnp.concatenate([ff, mi, mf], axis=-1)
    bias = jnp.concatenate([fox_f_bias, mlstm_i_bias, mlstm_f_bias]).astype(F32)
    n_gate = w_gate.shape[1]
    wg_hi, wg_lo = _split_hi_lo(jnp.pad(w_gate, ((0, 0), (0, LANES - n_gate))))
    bias_r = jnp.pad(bias, (0, LANES - n_gate))[None, :]
    return w_main, wg_hi, wg_lo, bias_r


def kernel(x, attn_norm_w, w_in, fox_f_bias, fox_q_norm_w, fox_k_norm_w, fox_out_norm_w, mlstm_conv_w, mlstm_i_bias, mlstm_f_bias, mlstm_out_norm_w, w_out, moe_norm_w, router_w, router_b, expert_w_gate_up, expert_b_gate_up, expert_w_down, expert_b_down):
    bsz, seq, d = x.shape
    x2d = x.reshape(bsz * seq, d)
    prep = _prep_in_proj_weights(w_in[0], fox_f_bias[0], mlstm_i_bias[0], mlstm_f_bias[0])
    pair = lambda w: jnp.tile(w, LANES // HEAD_DIM)[None, :]
    qn, kn, aq, ak, vb, projm, gt, gtm = _in_proj(x2d, attn_norm_w[0][None, :], *prep,
                                                  pair(fox_q_norm_w[0]), pair(fox_k_norm_w[0]))
    b3 = lambda a: a.reshape(bsz, seq, a.shape[-1])
    fox_y = _fox(b3(qn), b3(kn), b3(aq), b3(ak), b3(vb), fox_out_norm_w[0][None, :])
    mlstm_y = _mlstm(b3(projm), gt, gtm, mlstm_conv_w[0])
    return _channel_mixer(x2d, fox_y.reshape(-1, FOX_WIDTH), mlstm_y.reshape(-1, MLSTM_V_WIDTH), projm,
                          mlstm_out_norm_w[0], w_out[0], moe_norm_w[0], router_w[0], router_b[0],
                          expert_w_gate_up[0], expert_b_gate_up[0], expert_w_down[0],
                          expert_b_down[0]).reshape(bsz, seq, d)


def _channel_mixer(x2d, fox_y2d, mlstm_y2d, projm, mlstm_gain, w_out, moe_norm_w, router_w, router_b,
                   w_gu, b_gu, w_down, b_down):
    wr_hi, wr_lo = _split_hi_lo(router_w.T)
    x1, h2, eidx, gates, rank, counts = _out_route(
        x2d, fox_y2d, mlstm_y2d, projm, w_out.astype(BF16), mlstm_gain[None, :], moe_norm_w[None, :],
        wr_hi, wr_lo, router_b[:, None])
    block_e, n_used, n_valid, buf_tok3, dst3, n_slots = _dispatch_plan(eidx, rank, counts[:, 0])
    y_slots = _experts(block_e, n_used, n_valid, buf_tok3, dst3, h2, w_gu, b_gu, w_down, b_down, n_slots)
    return _combine(x1, y_slots, gates)
```

```python
import functools
import math

import jax
import jax.numpy as jnp
from jax import lax
from jax.experimental import pallas as pl
from jax.experimental.pallas import tpu as pltpu

F32 = jnp.float32
BF16 = jnp.bfloat16

D_MODEL = 1024
SEQ = 2048
HEAD_DIM = 64
FOX_HEADS = 8
FOX_WIDTH = FOX_HEADS * HEAD_DIM
MLSTM_HEADS = 8
MLSTM_QK_DIM = 32
MLSTM_V_DIM = 64
MLSTM_QK_WIDTH = MLSTM_HEADS * MLSTM_QK_DIM
MLSTM_V_WIDTH = MLSTM_HEADS * MLSTM_V_DIM
CONV_WIDTH = 4
MLSTM_CHUNK = 64
SPLIT_WIDTHS = (FOX_WIDTH, FOX_WIDTH, FOX_WIDTH, FOX_HEADS,
                MLSTM_QK_WIDTH, MLSTM_QK_WIDTH, MLSTM_V_WIDTH,
                MLSTM_HEADS, MLSTM_HEADS, MLSTM_V_WIDTH)
N_EXPERTS = 32
TOP_K = 4
D_EXPERT = D_MODEL
SWIGLU_ALPHA = 1.702
SWIGLU_LIMIT = 7.0
NORM_EPS = 1e-5
LOG2E = 1.4426950408889634

LANES = 128
SUBLANES = 8
V7X_VMEM_BYTES = 64 * 1024 * 1024

MAIN_WIDTH = 3 * FOX_WIDTH + 2 * MLSTM_QK_WIDTH + 2 * MLSTM_V_WIDTH
MLSTM_COLS = MAIN_WIDTH - 3 * FOX_WIDTH
GATE_ROWS = 32

NT_DIMS = (((1,), (1,)), ((), ()))


def _vmem_limit(nbytes):
    return int(min(nbytes + (8 << 20), V7X_VMEM_BYTES - (4 << 20)))


def _log_sigmoid(x):
    return jnp.minimum(x, 0.0) - jnp.log(1.0 + jnp.exp(-jnp.abs(x)))


def _split_hi_lo(x):
    hi = x.astype(BF16)
    lo = (x - hi.astype(F32)).astype(BF16)
    return hi, lo


def _rms(x, w):
    return x * lax.rsqrt(jnp.mean(x * x, axis=-1, keepdims=True) + NORM_EPS) * w


TOK_ROWS = D_MODEL // LANES


def _rows_to_tiles(x, tile_ref):
    m = x.shape[0]
    for j in range(TOK_ROWS):
        tile_ref[pl.ds(j, m, stride=TOK_ROWS), :] = x[:, j * LANES:(j + 1) * LANES]


def _tiles_to_rows(tile_ref, m):
    return jnp.concatenate(
        [tile_ref[pl.ds(j, m, stride=TOK_ROWS), :] for j in range(TOK_ROWS)], axis=1)


IN_TM = 512
IN_TILES_PER_SEQ = SEQ // IN_TM
IN_PARTS = 2
IN_PM = IN_TM // IN_PARTS
IN_SCAN_SHIFTS = tuple(1 << i for i in range(int(math.log2(IN_PM))))


def _pair_rms(x, w, lo_half):
    sq = x * x
    ms_lo = jnp.sum(jnp.where(lo_half, sq, 0.0), axis=-1, keepdims=True) * (1.0 / HEAD_DIM)
    ms_hi = jnp.sum(jnp.where(lo_half, 0.0, sq), axis=-1, keepdims=True) * (1.0 / HEAD_DIM)
    inv = jnp.where(lo_half, lax.rsqrt(ms_lo + NORM_EPS), lax.rsqrt(ms_hi + NORM_EPS))
    return x * inv * w


def _in_proj_kernel(x_ref, nw_ref, w_ref, wg_hi_ref, wg_lo_ref, br_ref, qw_ref, kw_ref,
                    qn_ref, kn_ref, aq_ref, ak_ref, vb_ref, projm_ref, gt_ref, carry_ref):
    i = pl.program_id(0)

    @pl.when(i % IN_TILES_PER_SEQ == 0)
    def _():
        carry_ref[...] = jnp.zeros_like(carry_ref)

    for part in range(IN_PARTS):
        rows = slice(part * IN_PM, (part + 1) * IN_PM)
        x = x_ref[rows, :]
        ms = jnp.mean(x * x, axis=-1, keepdims=True)
        y = x * lax.rsqrt(ms + NORM_EPS) * nw_ref[...]
        h_hi, h_lo = _split_hi_lo(y)
        main = jnp.dot(h_hi, w_ref[...], preferred_element_type=F32)

        w_hi, w_lo = wg_hi_ref[...], wg_lo_ref[...]
        g = (jnp.dot(h_hi, w_hi, preferred_element_type=F32)
             + jnp.dot(h_hi, w_lo, preferred_element_type=F32)
             + jnp.dot(h_lo, w_hi, preferred_element_type=F32))
        g = g + br_ref[...]
        lane = lax.broadcasted_iota(jnp.int32, g.shape, 1)
        is_input_gate = (lane >= FOX_HEADS) & (lane < FOX_HEADS + MLSTM_HEADS)
        gates = jnp.where(is_input_gate, g, _log_sigmoid(g))
        gt_ref[:, rows] = gates.T[:GATE_ROWS, :]

        rowi = lax.broadcasted_iota(jnp.int32, gates.shape, 0)
        c = gates
        for s in IN_SCAN_SHIFTS:
            c = c + jnp.where(rowi >= s, pltpu.roll(c, s, axis=0), 0.0)
        c = c + carry_ref[...]
        carry_ref[...] = c[IN_PM - 1:IN_PM, :]
        cum2 = c * LOG2E

        c8 = jnp.where(lane < FOX_HEADS, cum2, 0.0)
        c_hi = c8.astype(BF16).astype(F32)
        r1 = c8 - c_hi
        c_mid = r1.astype(BF16).astype(F32)
        c_lo = (r1 - c_mid).astype(BF16).astype(F32)
        ones_q = jnp.where((lane >= 3 * FOX_HEADS) & (lane < 6 * FOX_HEADS), 1.0, 0.0)
        ones_k = jnp.where(lane < 3 * FOX_HEADS, 1.0, 0.0)
        aq = (c_hi + pltpu.roll(c_mid, FOX_HEADS, axis=1) + pltpu.roll(c_lo, 2 * FOX_HEADS, axis=1)
              + ones_q)
        ak = ones_k - (pltpu.roll(c_hi, 3 * FOX_HEADS, axis=1) + pltpu.roll(c_mid, 4 * FOX_HEADS, axis=1)
                       + pltpu.roll(c_lo, 5 * FOX_HEADS, axis=1))
        aq_ref[rows, :] = aq.astype(BF16)
        ak_ref[rows, :] = ak.astype(BF16)

        lo_half = lane < HEAD_DIM
        q_scale = (HEAD_DIM ** -0.5) * LOG2E
        for p in range(FOX_HEADS // 2):
            ps = slice(p * LANES, (p + 1) * LANES)
            qn_ref[rows, ps] = (_pair_rms(main[:, ps], qw_ref[...], lo_half) * q_scale).astype(BF16)
            kn_ref[rows, ps] = _pair_rms(main[:, FOX_WIDTH + p * LANES:FOX_WIDTH + (p + 1) * LANES],
                                         kw_ref[...], lo_half).astype(BF16)

        vb_ref[rows, :] = main[:, 2 * FOX_WIDTH:3 * FOX_WIDTH].astype(BF16)
        projm_ref[rows, :] = main[:, 3 * FOX_WIDTH:]


def _in_proj(x2d, norm_w, w_main, wg_hi, wg_lo, bias_r, qw, kw):
    n = x2d.shape[0]
    tm = IN_TM
    const = lambda i: (0, 0)
    row = lambda w: pl.BlockSpec((tm, w), lambda i: (i, 0))
    vmem = (2 * (tm * D_MODEL * 4 + D_MODEL * MAIN_WIDTH * 2 + 3 * tm * FOX_WIDTH * 2 + 2 * tm * LANES * 2
                 + tm * MLSTM_COLS * 4 + GATE_ROWS * tm * 4 + tm * LANES * 4)
            + 2 * tm * MAIN_WIDTH * 4)
    return pl.pallas_call(
        _in_proj_kernel,
        grid=(n // tm,),
        in_specs=[
            row(D_MODEL),
            pl.BlockSpec((1, D_MODEL), const),
            pl.BlockSpec((D_MODEL, MAIN_WIDTH), const),
            pl.BlockSpec((D_MODEL, LANES), const),
            pl.BlockSpec((D_MODEL, LANES), const),
            pl.BlockSpec((1, LANES), const),
            pl.BlockSpec((1, LANES), const),
            pl.BlockSpec((1, LANES), const),
        ],
        out_specs=[
            row(FOX_WIDTH), row(FOX_WIDTH), row(LANES), row(LANES), row(FOX_WIDTH), row(MLSTM_COLS),
            pl.BlockSpec((GATE_ROWS, tm), lambda i: (0, i)),
        ],
        out_shape=[
            jax.ShapeDtypeStruct((n, FOX_WIDTH), BF16),
            jax.ShapeDtypeStruct((n, FOX_WIDTH), BF16),
            jax.ShapeDtypeStruct((n, LANES), BF16),
            jax.ShapeDtypeStruct((n, LANES), BF16),
            jax.ShapeDtypeStruct((n, FOX_WIDTH), BF16),
            jax.ShapeDtypeStruct((n, MLSTM_COLS), F32),
            jax.ShapeDtypeStruct((GATE_ROWS, n), F32),
        ],
        scratch_shapes=[pltpu.VMEM((1, LANES), F32)],
        compiler_params=pltpu.CompilerParams(
            dimension_semantics=("arbitrary",), vmem_limit_bytes=_vmem_limit(vmem)),
        name="in_proj",
    )(x2d, norm_w, w_main, wg_hi, wg_lo, bias_r, qw, kw)


FOX_TQ = 512


def _fox_kernel(qn_ref, kn_ref, aq_ref, ak_ref, v_ref, ow_ref, o_ref, qa_ref, ka_ref, va_ref):
    hp = pl.program_id(1)
    tri = (lax.broadcasted_iota(jnp.int32, (FOX_TQ, FOX_TQ), 1)
           <= lax.broadcasted_iota(jnp.int32, (FOX_TQ, FOX_TQ), 0))
    lo_q = lax.broadcasted_iota(jnp.int32, (FOX_TQ, LANES), 1) < HEAD_DIM
    lane = lax.broadcasted_iota(jnp.int32, (1, LANES), 1)
    ka_ref[:, :LANES] = kn_ref[0]
    ka_ref[:, LANES:] = ak_ref[0]
    for j in range(2):
        cs = slice(j * HEAD_DIM, (j + 1) * HEAD_DIM)
        own_q = lo_q if j == 0 else jnp.logical_not(lo_q)
        h = 2 * hp + j
        own_mask = jnp.where((lane < HEAD_DIM) if j == 0 else (lane >= HEAD_DIM), 1.0, 0.0).astype(BF16)
        bias_mask = jnp.where((lane < 6 * FOX_HEADS) & (lane % FOX_HEADS == h), 1.0, 0.0).astype(BF16)
        qa_ref[j, :, :LANES] = qn_ref[0] * own_mask
        qa_ref[j, :, LANES:] = aq_ref[0] * bias_mask
        sum_lane = HEAD_DIM if j == 0 else 0
        va_ref[j] = v_ref[0] * own_mask + jnp.where(lane == sum_lane, 1.0, 0.0).astype(BF16)
        for i in range(SEQ // FOX_TQ):
            qs = slice(i * FOX_TQ, (i + 1) * FOX_TQ)
            n = (i + 1) * FOX_TQ
            s = lax.dot_general(qa_ref[j, qs, :], ka_ref[:n, :], NT_DIMS,
                                preferred_element_type=F32)
            diag = jnp.where(tri, s[:, n - FOX_TQ:], -jnp.inf)
            m = jnp.max(diag, axis=-1, keepdims=True)
            if i > 0:
                past = s[:, :n - FOX_TQ]
                m = jnp.maximum(m, jnp.max(past, axis=-1, keepdims=True))
                p = jnp.concatenate([jnp.exp2(past - m), jnp.exp2(diag - m)], axis=1)
            else:
                p = jnp.exp2(diag - m)
            o = jnp.dot(p.astype(BF16), va_ref[j, :n, :], preferred_element_type=F32)
            o = o / o[:, sum_lane:sum_lane + 1]
            ms = jnp.sum(jnp.where(own_q, o * o, 0.0), axis=-1, keepdims=True) * (1.0 / HEAD_DIM)
            on = o * lax.rsqrt(ms + NORM_EPS) * ow_ref[...]
            o_ref[0, qs, cs] = on[:, cs]


def _fox(qn3, kn3, aq3, ak3, v3, ow):
    b = qn3.shape[0]
    nq = FOX_WIDTH // LANES
    blk = (1, SEQ, LANES)
    pair = pl.BlockSpec(blk, lambda bi, hp: (bi, 0, hp))
    shared = pl.BlockSpec(blk, lambda bi, hp: (bi, 0, 0))
    vmem = 2 * (5 * SEQ * LANES * 2 + SEQ * LANES * 4) + 3 * SEQ * 2 * LANES * 2 + 8 * FOX_TQ * SEQ * 4
    return pl.pallas_call(
        _fox_kernel,
        grid=(b, nq),
        in_specs=[pair, pair, shared, shared, pair, pl.BlockSpec((1, LANES), lambda bi, hp: (0, hp))],
        out_specs=pl.BlockSpec(blk, lambda bi, hp: (bi, 0, hp)),
        out_shape=jax.ShapeDtypeStruct((b, SEQ, FOX_WIDTH), F32),
        scratch_shapes=[pltpu.VMEM((2, SEQ, 2 * LANES), BF16), pltpu.VMEM((SEQ, 2 * LANES), BF16),
                        pltpu.VMEM((2, SEQ, LANES), BF16)],
        compiler_params=pltpu.CompilerParams(
            dimension_semantics=("arbitrary", "arbitrary"), vmem_limit_bytes=_vmem_limit(vmem)),
        name="fox",
    )(qn3, kn3, aq3, ak3, v3, ow)


ML_L = MLSTM_CHUNK
ML_PAIRS = SEQ // (2 * ML_L)
ML_HL = FOX_HEADS
ML_AUG = MLSTM_V_WIDTH + LANES
ML_TILE = 256
SEG_SHIFTS = tuple(1 << i for i in range(int(math.log2(ML_L))))


def _split3(x):
    a = x.astype(BF16)
    r = x - a.astype(F32)
    b = r.astype(BF16)
    c = (r - b.astype(F32)).astype(BF16)
    return a, b, c


def _expand_heads(x, exp_bf):
    a, b, c = _split3(x)
    return (jnp.dot(a, exp_bf, preferred_element_type=F32)
            + jnp.dot(b, exp_bf, preferred_element_type=F32)
            + jnp.dot(c, exp_bf, preferred_element_type=F32))


def _seg_scan(x, axis, op, ident):
    idx = lax.broadcasted_iota(jnp.int32, x.shape, axis) % ML_L
    for s in SEG_SHIFTS:
        x = op(x, jnp.where(idx >= s, pltpu.roll(x, s, axis=axis), ident))
    return x


def _mlstm_kernel(q_ref, k_ref, v_ref, gi_ref, gf_ref, cw_ref, o_ref, den_ref,
                  qc_ref, kc_ref, kt_ref, rr_ref, cmr_ref, bcr_ref,
                  ealpha_ref, ew_ref, wint_ref, floor_ref, mfull_ref, caug_ref):
    def conv_silu(u, w):
        rowi = lax.broadcasted_iota(jnp.int32, u.shape, 0)
        acc = u * w[CONV_WIDTH - 1:CONV_WIDTH, :]
        for d in range(1, CONV_WIDTH):
            sh = jnp.where(rowi >= d, pltpu.roll(u, d, axis=0), 0.0)
            acc = acc + sh * w[CONV_WIDTH - 1 - d:CONV_WIDTH - d, :]
        return acc / (1.0 + jnp.exp(-acc))

    cw = cw_ref[...]
    qc_ref[...] = conv_silu(q_ref[0], cw[:, :MLSTM_QK_WIDTH]).astype(BF16)
    kc = conv_silu(k_ref[0], cw[:, MLSTM_QK_WIDTH:]) * (MLSTM_QK_DIM ** -0.5)
    kc_ref[...] = kc.astype(BF16)
    kt = kc.T
    for p in range(ML_PAIRS):
        kt_ref[p] = kt[:, p * LANES:(p + 1) * LANES]

    bcum_r = _seg_scan(gf_ref[...], 1, jnp.add, 0.0)
    r_r = gi_ref[...] - bcum_r
    cmx_r = _seg_scan(r_r, 1, jnp.maximum, -jnp.inf)
    for p in range(ML_PAIRS):
        ls = slice(p * LANES, (p + 1) * LANES)
        rr_ref[p] = r_r[:, ls]
        cmr_ref[p] = cmx_r[:, ls]
        bcr_ref[p] = bcum_r[:, ls]

    def to_columns(rows8):
        pad_lo = jnp.zeros((ML_HL, SEQ), F32)
        pad_hi = jnp.zeros((LANES - ML_HL - MLSTM_HEADS, SEQ), F32)
        return jnp.concatenate([pad_lo, rows8, pad_hi], axis=0).T

    bcum_c = to_columns(bcum_r)
    cmx_c = to_columns(cmx_r)
    m = jnp.zeros((1, LANES), F32)
    for c in range(SEQ // ML_L):
        mfull_ref[c * ML_L:(c + 1) * ML_L, :] = jnp.broadcast_to(m, (ML_L, LANES))
        last = (c + 1) * ML_L - 1
        m = bcum_c[last:last + 1, :] + jnp.maximum(m, cmx_c[last:last + 1, :])
    mfull = mfull_ref[...]
    mx = jnp.maximum(mfull, cmx_c)
    wint_ref[...] = jnp.exp(mfull - mx)
    floor_ref[...] = jnp.exp(-(bcum_c + mx))
    mfull_ref[...] = -mx

    lane_e = lax.broadcasted_iota(jnp.int32, (LANES, MLSTM_V_WIDTH), 1) // MLSTM_V_DIM
    row_e = lax.broadcasted_iota(jnp.int32, (LANES, MLSTM_V_WIDTH), 0)
    exp_bf = jnp.where(row_e == lane_e + ML_HL, 1.0, 0.0).astype(BF16)

    def expand_tile(i, carry):
        rows = pl.ds(pl.multiple_of(i * ML_TILE, ML_TILE), ML_TILE)
        ealpha_ref[rows, :] = _expand_heads(mfull_ref[rows, :], exp_bf)
        ew_ref[rows, :] = _expand_heads(wint_ref[rows, :], exp_bf)
        return carry

    lax.fori_loop(0, SEQ // ML_TILE, expand_tile, 0)

    kb_rowh = lax.broadcasted_iota(jnp.int32, (MLSTM_HEADS * ML_L, MLSTM_QK_WIDTH), 0) // ML_L
    kb_lane = lax.broadcasted_iota(jnp.int32, (MLSTM_HEADS * ML_L, MLSTM_QK_WIDTH), 1) // MLSTM_QK_DIM
    mask_k = jnp.where(kb_rowh == kb_lane, 1.0, 0.0).astype(BF16)
    va_rowh = lax.broadcasted_iota(jnp.int32, (MLSTM_HEADS * ML_L, ML_AUG), 0) // ML_L
    va_col = lax.broadcasted_iota(jnp.int32, (MLSTM_HEADS * ML_L, ML_AUG), 1)
    mask_v = jnp.where(
        (va_col // MLSTM_V_DIM == va_rowh) | (va_col == MLSTM_V_WIDTH + ML_HL + va_rowh),
        1.0, 0.0).astype(BF16)
    c_rowh = lax.broadcasted_iota(jnp.int32, (MLSTM_QK_WIDTH, ML_AUG), 0) // MLSTM_QK_DIM
    c_col = lax.broadcasted_iota(jnp.int32, (MLSTM_QK_WIDTH, ML_AUG), 1)
    mask_c = (c_col // MLSTM_V_DIM == c_rowh) | (c_col == MLSTM_V_WIDTH + ML_HL + c_rowh)
    lane128 = lax.broadcasted_iota(jnp.int32, (ML_L, LANES), 1)
    s_idx = lax.broadcasted_iota(jnp.int32, (ML_L, MLSTM_V_WIDTH), 1) % ML_L
    t_idx = lax.broadcasted_iota(jnp.int32, (ML_L, MLSTM_V_WIDTH), 0)
    causal = s_idx <= t_idx
    ones_aug = jnp.ones((ML_L, LANES), F32)

    caug_ref[...] = jnp.zeros_like(caug_ref)

    def pair_body(cp, m_row):
        r2 = rr_ref[cp]
        cm2 = cmr_ref[cp]
        b2 = bcr_ref[cp]
        r2r = pltpu.roll(r2, ML_L, axis=1)
        kt2 = kt_ref[cp]
        for cc in range(2):
            lo = cc * ML_L
            rows = pl.ds(pl.multiple_of(cp * (2 * ML_L), 2 * ML_L) + lo, ML_L)
            qa = qc_ref[rows, :]
            ka = kc_ref[rows, :]
            va_aug = jnp.concatenate([v_ref[0, rows, :], ones_aug], axis=1).astype(BF16)

            kbd = jnp.concatenate([ka] * MLSTM_HEADS, axis=0) * mask_k
            s = lax.dot_general(qa, kbd, NT_DIMS, preferred_element_type=F32)

            cmx_last = cm2[:, lo + ML_L - 1:lo + ML_L]
            b_last = b2[:, lo + ML_L - 1:lo + ML_L]
            mx_r = jnp.maximum(m_row, cmx_last)
            decay = jnp.exp(m_row - mx_r)
            wk = jnp.exp(r2[:, lo:lo + ML_L] - mx_r)
            m_row = b_last + mx_r

            src_e, src_o = (r2, r2r) if cc == 0 else (r2r, r2)
            cols = []
            for p in range(MLSTM_HEADS // 2):
                even = jnp.broadcast_to(src_e[2 * p:2 * p + 1, :], (ML_L, LANES))
                odd = jnp.broadcast_to(src_o[2 * p + 1:2 * p + 2, :], (ML_L, LANES))
                cols.append(jnp.where(lane128 < ML_L, even, odd))
            r_all = jnp.concatenate(cols, axis=1)
            arg = jnp.where(causal, ealpha_ref[rows, :] + r_all, -jnp.inf)
            p_all = (s * jnp.exp(arg)).astype(BF16)

            vbd = jnp.concatenate([va_aug] * MLSTM_HEADS, axis=0) * mask_v
            pv = jnp.dot(p_all, vbd, preferred_element_type=F32)
            qc_state = jnp.dot(qa, caug_ref[...].astype(BF16), preferred_element_type=F32)
            o_ref[0, rows, :] = (ew_ref[rows, :] * qc_state[:, :MLSTM_V_WIDTH]
                                 + pv[:, :MLSTM_V_WIDTH])
            den_ref[rows, :] = (wint_ref[rows, :] * qc_state[:, MLSTM_V_WIDTH:]
                                + pv[:, MLSTM_V_WIDTH:])

            wk_rows = jnp.concatenate(
                [jnp.broadcast_to(wk[h:h + 1, :], (MLSTM_QK_DIM, ML_L)) for h in range(MLSTM_HEADS)],
                axis=0)
            dec_rows = jnp.concatenate(
                [jnp.broadcast_to(decay[h:h + 1, :], (MLSTM_QK_DIM, 1)) for h in range(MLSTM_HEADS)],
                axis=0)
            ktw = (kt2[:, lo:lo + ML_L] * wk_rows).astype(BF16)
            upd = jnp.dot(ktw, va_aug, preferred_element_type=F32)
            caug_ref[...] = dec_rows * caug_ref[...] + jnp.where(mask_c, upd, 0.0)
        return m_row

    lax.fori_loop(0, ML_PAIRS, pair_body, jnp.zeros((MLSTM_HEADS, 1), F32), unroll=4)

    ob_row = lax.broadcasted_iota(jnp.int32, (MLSTM_V_WIDTH, LANES), 0) // MLSTM_V_DIM
    ob_col = lax.broadcasted_iota(jnp.int32, (MLSTM_V_WIDTH, LANES), 1)
    ones_bd = jnp.where(ob_col == ob_row + ML_HL, 1.0, 0.0).astype(BF16)

    def norm_tile(i, carry):
        rows = pl.ds(pl.multiple_of(i * ML_TILE, ML_TILE), ML_TILE)
        num = o_ref[0, rows, :]
        dn = jnp.maximum(jnp.abs(den_ref[rows, :]), floor_ref[rows, :])
        r = 1.0 / dn
        n2_hi, n2_lo = _split_hi_lo(num * num)
        msn = (jnp.dot(n2_hi, ones_bd, preferred_element_type=F32)
               + jnp.dot(n2_lo, ones_bd, preferred_element_type=F32)) * (1.0 / MLSTM_V_DIM)
        fac = r * lax.rsqrt(r * r * msn + NORM_EPS)
        o_ref[0, rows, :] = num * _expand_heads(fac, exp_bf)
        return carry

    lax.fori_loop(0, SEQ // ML_TILE, norm_tile, 0)


def _mlstm(projm3, gt, conv_w):
    b = projm3.shape[0]
    qk_blk = (1, SEQ, MLSTM_QK_WIDTH)
    v_blk = (1, SEQ, MLSTM_V_WIDTH)
    v_col = 2 * MLSTM_QK_WIDTH // MLSTM_V_WIDTH
    vmem = (2 * (2 * SEQ * MLSTM_QK_WIDTH * 4 + 2 * SEQ * MLSTM_V_WIDTH * 4 + 2 * SEQ * LANES * 4)
            + 2 * SEQ * MLSTM_QK_WIDTH * 2 + SEQ * MLSTM_QK_WIDTH * 4 + 2 * SEQ * MLSTM_V_WIDTH * 4
            + 3 * SEQ * LANES * 4 + (8 << 20))
    return pl.pallas_call(
        _mlstm_kernel,
        grid=(b,),
        in_specs=[
            pl.BlockSpec(qk_blk, lambda bi: (bi, 0, 0)),
            pl.BlockSpec(qk_blk, lambda bi: (bi, 0, 1)),
            pl.BlockSpec(v_blk, lambda bi: (bi, 0, v_col)),
            pl.BlockSpec((SUBLANES, SEQ), lambda bi: (1, bi)),
            pl.BlockSpec((SUBLANES, SEQ), lambda bi: (2, bi)),
            pl.BlockSpec((CONV_WIDTH, 2 * MLSTM_QK_WIDTH), lambda bi: (0, 0)),
        ],
        out_specs=pl.BlockSpec(v_blk, lambda bi: (bi, 0, 0)),
        out_shape=jax.ShapeDtypeStruct((b, SEQ, MLSTM_V_WIDTH), F32),
        scratch_shapes=[
            pltpu.VMEM((SEQ, LANES), F32),
            pltpu.VMEM((SEQ, MLSTM_QK_WIDTH), BF16),
            pltpu.VMEM((SEQ, MLSTM_QK_WIDTH), BF16),
            pltpu.VMEM((ML_PAIRS, MLSTM_QK_WIDTH, LANES), F32),
            pltpu.VMEM((ML_PAIRS, SUBLANES, LANES), F32),
            pltpu.VMEM((ML_PAIRS, SUBLANES, LANES), F32),
            pltpu.VMEM((ML_PAIRS, SUBLANES, LANES), F32),
            pltpu.VMEM((SEQ, MLSTM_V_WIDTH), F32),
            pltpu.VMEM((SEQ, MLSTM_V_WIDTH), F32),
            pltpu.VMEM((SEQ, LANES), F32),
            pltpu.VMEM((SEQ, LANES), F32),
            pltpu.VMEM((SEQ, LANES), F32),
            pltpu.VMEM((MLSTM_QK_WIDTH, ML_AUG), F32),
        ],
        compiler_params=pltpu.CompilerParams(
            dimension_semantics=("arbitrary",), vmem_limit_bytes=_vmem_limit(vmem)),
        name="mlstm",
    )(projm3, projm3, projm3, gt, gt, conv_w)


RT_TM = 512


def _out_route_kernel(x_ref, fy_ref, my_ref, mo_ref, wo_ref, mg_ref, nw_ref, wr_hi_ref, wr_lo_ref,
                      rb_ref, x1_ref, h2_ref, eidx_ref, gate_ref, rank_ref, cnt_ref, carry_ref):
    i = pl.program_id(0)

    @pl.when(i == 0)
    def _():
        carry_ref[...] = jnp.zeros_like(carry_ref)

    my = my_ref[...] * mg_ref[...] / (1.0 + jnp.exp(-mo_ref[...]))
    mixed = (jnp.dot(fy_ref[...].astype(BF16), wo_ref[:FOX_WIDTH, :], preferred_element_type=F32)
             + jnp.dot(my.astype(BF16), wo_ref[FOX_WIDTH:, :], preferred_element_type=F32))
    x1 = x_ref[...] + mixed
    x1_ref[...] = x1
    h2 = _rms(x1, nw_ref[...])
    _rows_to_tiles(h2, h2_ref)

    h_hi, h_lo = _split_hi_lo(h2)
    wr_hi, wr_lo = wr_hi_ref[...], wr_lo_ref[...]
    logit = (lax.dot_general(wr_hi, h_hi, NT_DIMS, preferred_element_type=F32)
             + lax.dot_general(wr_lo, h_hi, NT_DIMS, preferred_element_type=F32)
             + lax.dot_general(wr_hi, h_lo, NT_DIMS, preferred_element_type=F32)) + rb_ref[...]

    e_iota = lax.broadcasted_iota(jnp.int32, logit.shape, 0).astype(F32)
    vals, idxs, hots = [], [], []
    for _ in range(TOP_K):
        mk = jnp.max(logit, axis=0, keepdims=True)
        idx = jnp.min(jnp.where(logit == mk, e_iota, float(N_EXPERTS)), axis=0, keepdims=True)
        hot = e_iota == idx
        logit = jnp.where(hot, -jnp.inf, logit)
        vals.append(mk)
        idxs.append(idx.astype(jnp.int32))
        hots.append(hot)
    exps = [jnp.exp(v - vals[0]) for v in vals]
    tot = exps[0] + exps[1] + exps[2] + exps[3]
    gates = [e / tot for e in exps]

    assign = jnp.zeros(logit.shape, F32)
    for hot in hots:
        assign = assign + jnp.where(hot, 1.0, 0.0)
    tm = logit.shape[1]
    src = lax.broadcasted_iota(jnp.int32, (tm, tm), 0)
    dst = lax.broadcasted_iota(jnp.int32, (tm, tm), 1)
    upper = jnp.where(src < dst, 1.0, 0.0).astype(BF16)
    base = jnp.dot(assign.astype(BF16), upper, preferred_element_type=F32) + carry_ref[:, 0:1]
    ranks = [jnp.sum(jnp.where(hot, base, 0.0), axis=0, keepdims=True) for hot in hots]
    new_carry = carry_ref[...] + jnp.sum(assign, axis=1, keepdims=True)
    carry_ref[...] = new_carry
    cnt_ref[...] = new_carry

    zi = jnp.zeros((SUBLANES - TOP_K, tm), jnp.int32)
    eidx_ref[...] = jnp.concatenate(idxs + [zi], axis=0)
    rank_ref[...] = jnp.concatenate([r.astype(jnp.int32) for r in ranks] + [zi], axis=0)
    gate_ref[...] = jnp.concatenate(gates + [zi.astype(F32)], axis=0)


def _out_route(x2d, fox_y2d, mlstm_y2d, projm, w_out_bf, mlstm_gain, moe_norm_w, wr_hi, wr_lo, rb):
    n = x2d.shape[0]
    tm = RT_TM
    const = lambda i: (0, 0)
    mo_col = (MLSTM_COLS - MLSTM_V_WIDTH) // MLSTM_V_WIDTH
    row_blk = lambda w: pl.BlockSpec((tm, w), lambda i: (i, 0))
    lane_blk = pl.BlockSpec((SUBLANES, tm), lambda i: (0, i))
    vmem = (2 * (tm * D_MODEL * 4 * 3 + tm * FOX_WIDTH * 4 * 3 + D_MODEL * D_MODEL * 2)
            + 6 * tm * D_MODEL * 4 + tm * tm * 6)
    return pl.pallas_call(
        _out_route_kernel,
        grid=(n // tm,),
        in_specs=[
            row_blk(D_MODEL), row_blk(FOX_WIDTH), row_blk(MLSTM_V_WIDTH),
            pl.BlockSpec((tm, MLSTM_V_WIDTH), lambda i: (i, mo_col)),
            pl.BlockSpec((D_MODEL, D_MODEL), const),
            pl.BlockSpec((1, MLSTM_V_WIDTH), const),
            pl.BlockSpec((1, D_MODEL), const),
            pl.BlockSpec((N_EXPERTS, D_MODEL), const),
            pl.BlockSpec((N_EXPERTS, D_MODEL), const),
            pl.BlockSpec((N_EXPERTS, 1), const),
        ],
        out_specs=[row_blk(D_MODEL), pl.BlockSpec((tm * TOK_ROWS, LANES), lambda i: (i, 0)),
                   lane_blk, lane_blk, lane_blk, pl.BlockSpec((N_EXPERTS, LANES), const)],
        out_shape=[
            jax.ShapeDtypeStruct((n, D_MODEL), F32),
            jax.ShapeDtypeStruct((n * TOK_ROWS, LANES), F32),
            jax.ShapeDtypeStruct((SUBLANES, n), jnp.int32),
            jax.ShapeDtypeStruct((SUBLANES, n), F32),
            jax.ShapeDtypeStruct((SUBLANES, n), jnp.int32),
            jax.ShapeDtypeStruct((N_EXPERTS, LANES), F32),
        ],
        scratch_shapes=[pltpu.VMEM((N_EXPERTS, LANES), F32)],
        compiler_params=pltpu.CompilerParams(
            dimension_semantics=("arbitrary",), vmem_limit_bytes=_vmem_limit(vmem)),
        name="out_route",
    )(x2d, fox_y2d, mlstm_y2d, projm, w_out_bf, mlstm_gain, moe_norm_w, wr_hi, wr_lo, rb)


INV_CHUNK = 8192
INV_UNROLL = 16


def _invert_kernel(pos_ref, zeros_hbm, inv_ref, sem):
    i = pl.program_id(0)

    @pl.when(i == 0)
    def _():
        cp = pltpu.make_async_copy(zeros_hbm, inv_ref, sem.at[0])
        cp.start()
        cp.wait()

    base = i * INV_CHUNK

    def body(j, carry):
        inv_ref[pos_ref[0, 0, j]] = base + j
        return carry

    lax.fori_loop(0, INV_CHUNK, body, 0, unroll=INV_UNROLL)


def _invert(pos_flat, n_rows):
    n_slots = pos_flat.shape[0]
    steps = n_slots // INV_CHUNK
    return pl.pallas_call(
        _invert_kernel,
        grid=(steps,),
        in_specs=[
            pl.BlockSpec((1, 1, INV_CHUNK), lambda i: (i, 0, 0), memory_space=pltpu.SMEM),
            pl.BlockSpec(memory_space=pl.ANY),
        ],
        out_specs=pl.BlockSpec(memory_space=pltpu.SMEM),
        out_shape=jax.ShapeDtypeStruct((n_rows,), jnp.int32),
        scratch_shapes=[pltpu.SemaphoreType.DMA((1,))],
        compiler_params=pltpu.CompilerParams(dimension_semantics=("arbitrary",)),
        name="invert",
    )(pos_flat.reshape(steps, 1, INV_CHUNK), jnp.zeros((n_rows,), jnp.int32))


EX_BM = 256
EX_DRAIN_STEPS = 2


def _experts_kernel(be_ref, nu_ref, nv_ref, tok_ref, tokn_ref, dst_ref, h2_hbm, wgu_ref, bgu_ref,
                    wd_ref, bd_ref, y_hbm, xt_ref, yt_ref, wgu_bf_ref, wd_bf_ref, gsem, ssem):
    i = pl.program_id(0)
    last_blk = pl.num_programs(0) - 1 - EX_DRAIN_STEPS
    nu = nu_ref[0]
    slot = i % 2
    cur = jnp.minimum(i, last_blk)
    tile_rows = EX_BM * TOK_ROWS

    def tok_tile(ref, idx):
        return ref.at[pl.ds(pl.multiple_of(idx, TOK_ROWS), TOK_ROWS), :]

    def start_gather(idx_ref, s, n):
        @pl.when(n == EX_BM)
        def _():
            for r in range(EX_BM):
                pltpu.make_async_copy(tok_tile(h2_hbm, idx_ref[0, 0, r]),
                                      xt_ref.at[s, pl.ds(r * TOK_ROWS, TOK_ROWS), :], gsem.at[s]).start()

        @pl.when(n < EX_BM)
        def _():
            def body(r, carry):
                pltpu.make_async_copy(
                    tok_tile(h2_hbm, idx_ref[0, 0, r]),
                    xt_ref.at[s, pl.ds(pl.multiple_of(r * TOK_ROWS, TOK_ROWS), TOK_ROWS), :],
                    gsem.at[s]).start()
                return carry
            lax.fori_loop(0, n, body, 0)

    def scatter_row(r):
        pltpu.make_async_copy(yt_ref.at[slot, pl.ds(r * TOK_ROWS, TOK_ROWS), :],
                              tok_tile(y_hbm, dst_ref[0, 0, r]), ssem.at[slot]).start()

    def wait_block(sem, buf):
        pltpu.make_async_copy(h2_hbm.at[pl.ds(0, tile_rows), :], buf, sem).wait()

    def wait_tokens(sem, buf, n):
        @pl.when(n == EX_BM)
        def _():
            wait_block(sem, buf)

        @pl.when(n < EX_BM)
        def _():
            def body(r, carry):
                pltpu.make_async_copy(h2_hbm.at[pl.ds(0, TOK_ROWS), :],
                                      buf.at[pl.ds(0, TOK_ROWS), :], sem).wait()
                return carry
            lax.fori_loop(0, n, body, 0)

    @pl.when(i == 0)
    def _():
        xt_ref[...] = jnp.zeros_like(xt_ref)
        start_gather(tok_ref, 0, nv_ref[0])

    @pl.when(i + 1 < nu)
    def _():
        start_gather(tokn_ref, 1 - slot, nv_ref[jnp.minimum(i + 1, last_blk)])

    @pl.when((i >= 2) & (i - 2 < nu))
    def _():
        wait_tokens(ssem.at[slot], yt_ref.at[slot], nv_ref[jnp.clip(i - 2, 0, last_blk)])

    @pl.when(i < nu)
    def _():
        @pl.when((i == 0) | (be_ref[cur] != be_ref[jnp.maximum(cur - 1, 0)]))
        def _():
            wgu_bf_ref[...] = wgu_ref[0].astype(BF16)
            wd_bf_ref[...] = wd_ref[0].astype(BF16)

        wait_tokens(gsem.at[slot], xt_ref.at[slot], nv_ref[cur])
        xb = _tiles_to_rows(xt_ref.at[slot], EX_BM).astype(BF16)
        gu = jnp.dot(xb, wgu_bf_ref[...], preferred_element_type=F32) + bgu_ref[0]
        gate = jnp.minimum(gu[:, :D_EXPERT], SWIGLU_LIMIT)
        up = jnp.clip(gu[:, D_EXPERT:], -SWIGLU_LIMIT, SWIGLU_LIMIT)
        act = (up + 1.0) * (gate / (1.0 + jnp.exp(-SWIGLU_ALPHA * gate)))
        y = jnp.dot(act.astype(BF16), wd_bf_ref[...], preferred_element_type=F32) + bd_ref[0]
        _rows_to_tiles(y, yt_ref.at[slot])
        nv = nv_ref[cur]

        @pl.when(nv == EX_BM)
        def _():
            for r in range(EX_BM):
                scatter_row(r)

        @pl.when(nv < EX_BM)
        def _():
            def body(r, carry):
                pltpu.make_async_copy(
                    yt_ref.at[slot, pl.ds(pl.multiple_of(r * TOK_ROWS, TOK_ROWS), TOK_ROWS), :],
                    tok_tile(y_hbm, dst_ref[0, 0, r]), ssem.at[slot]).start()
                return carry
            lax.fori_loop(0, nv, body, 0)


def _experts(block_e, n_used, n_valid, buf_tok3, dst3, h2, w_gu, b_gu, w_down, b_down, n_slots):
    nb = buf_tok3.shape[0]
    idx_blk = lambda f: pl.BlockSpec((1, 1, EX_BM), f, memory_space=pltpu.SMEM)
    vmem = (2 * (D_MODEL * 2 * D_EXPERT * 4 + D_EXPERT * D_MODEL * 4)
            + D_MODEL * 2 * D_EXPERT * 2 + D_EXPERT * D_MODEL * 2
            + 3 * EX_BM * D_MODEL * 4 + 3 * EX_BM * 2 * D_EXPERT * 4)
    blk = lambda i: jnp.minimum(i, nb - 1)
    w_map = lambda i, be, nu, nv: (be[blk(i)], 0, 0)
    grid_spec = pltpu.PrefetchScalarGridSpec(
        num_scalar_prefetch=3,
        grid=(nb + EX_DRAIN_STEPS,),
        in_specs=[
            idx_blk(lambda i, be, nu, nv: (blk(i), 0, 0)),
            idx_blk(lambda i, be, nu, nv: (blk(i + 1), 0, 0)),
            idx_blk(lambda i, be, nu, nv: (blk(i), 0, 0)),
            pl.BlockSpec(memory_space=pl.ANY),
            pl.BlockSpec((1, D_MODEL, 2 * D_EXPERT), w_map),
            pl.BlockSpec((1, 1, 2 * D_EXPERT), w_map),
            pl.BlockSpec((1, D_EXPERT, D_MODEL), w_map),
            pl.BlockSpec((1, 1, D_MODEL), w_map),
        ],
        out_specs=pl.BlockSpec(memory_space=pl.ANY),
        scratch_shapes=[
            pltpu.VMEM((2, EX_BM * TOK_ROWS, LANES), F32),
            pltpu.VMEM((2, EX_BM * TOK_ROWS, LANES), F32),
            pltpu.VMEM((D_MODEL, 2 * D_EXPERT), BF16),
            pltpu.VMEM((D_EXPERT, D_MODEL), BF16),
            pltpu.SemaphoreType.DMA((2,)),
            pltpu.SemaphoreType.DMA((2,)),
        ],
    )
    return pl.pallas_call(
        _experts_kernel,
        grid_spec=grid_spec,
        out_shape=jax.ShapeDtypeStruct((n_slots * TOK_ROWS, LANES), F32),
        compiler_params=pltpu.CompilerParams(
            dimension_semantics=("arbitrary",), vmem_limit_bytes=_vmem_limit(vmem)),
        name="experts",
    )(block_e, n_used, n_valid, buf_tok3, buf_tok3, dst3, h2, w_gu, b_gu[:, None, :], w_down,
      b_down[:, None, :])


CB_TM = 512


def _combine_kernel(x1_ref, y0_ref, y1_ref, y2_ref, y3_ref, gate_ref, o_ref):
    tm = x1_ref.shape[0]
    g = jnp.concatenate([gate_ref[...], jnp.zeros((LANES - SUBLANES, tm), F32)], axis=0).T
    acc = x1_ref[...]
    for k, y_ref in enumerate((y0_ref, y1_ref, y2_ref, y3_ref)):
        acc = acc + g[:, k:k + 1] * _tiles_to_rows(y_ref, tm)
    o_ref[...] = acc


def _combine(x1, y_slots, gates):
    n = x1.shape[0]
    tm = CB_TM
    nt = n // tm
    vmem = 2 * (2 * tm * D_MODEL * 4 + tm * TOP_K * D_MODEL * 4) + 4 * tm * D_MODEL * 4
    y_spec = lambda k: pl.BlockSpec((tm * TOK_ROWS, LANES), lambda i: (k * nt + i, 0))
    return pl.pallas_call(
        _combine_kernel,
        grid=(nt,),
        in_specs=[pl.BlockSpec((tm, D_MODEL), lambda i: (i, 0))]
        + [y_spec(k) for k in range(TOP_K)]
        + [pl.BlockSpec((SUBLANES, tm), lambda i: (0, i))],
        out_specs=pl.BlockSpec((tm, D_MODEL), lambda i: (i, 0)),
        out_shape=jax.ShapeDtypeStruct((n, D_MODEL), F32),
        compiler_params=pltpu.CompilerParams(
            dimension_semantics=("arbitrary",), vmem_limit_bytes=_vmem_limit(vmem)),
        name="combine",
    )(x1, y_slots, y_slots, y_slots, y_slots, gates)


def _dispatch_plan(eidx, rank, counts):
    n = eidx.shape[1]
    n_slots = n * TOP_K
    nb = n_slots // EX_BM + N_EXPERTS
    counts = counts.astype(jnp.int32)
    padded = ((counts + EX_BM - 1) // EX_BM) * EX_BM
    pad_end = jnp.cumsum(padded)
    pad_start = pad_end - padded
    e = eidx[:TOP_K]
    start_of = jnp.sum(jnp.where(e[:, :, None] == jnp.arange(N_EXPERTS, dtype=jnp.int32),
                                 pad_start[None, None, :], 0), axis=-1)
    pos = start_of + rank[:TOP_K]
    inv = _invert(pos.reshape(-1), nb * EX_BM)
    buf_tok = (inv % n) * TOK_ROWS
    dst = inv * TOK_ROWS
    blk_start = jnp.arange(nb, dtype=jnp.int32) * EX_BM
    block_e = jnp.minimum(jnp.sum((pad_end[None, :] <= blk_start[:, None]).astype(jnp.int32), axis=1),
                          N_EXPERTS - 1)
    n_used = (pad_end[-1] // EX_BM).astype(jnp.int32).reshape(1)
    n_valid = jnp.clip(pad_start[block_e] + counts[block_e] - blk_start, 0, EX_BM).astype(jnp.int32)
    n_valid = jnp.where(blk_start < pad_end[-1], n_valid, 0)
    return (block_e, n_used, n_valid, buf_tok.reshape(nb, 1, EX_BM), dst.reshape(nb, 1, EX_BM), n_slots)


def _prep_in_proj_weights(w_in, fox_f_bias, mlstm_i_bias, mlstm_f_bias):
    split_at = []
    acc = 0
    for wdt in SPLIT_WIDTHS[:-1]:
        acc += wdt
        split_at.append(acc)
    fq, fk, fv, ff, mq, mk, mv, mi, mf, mo = jnp.split(w_in, split_at, axis=-1)
    w_main = jnp.concatenate([fq, fk, fv, mq, mk, mv, mo], axis=-1).astype(BF16)
    w_gate = jnp.concatenate([ff, mi, mf], axis=-1)
    bias = jnp.concatenate([fox_f_bias, mlstm_i_bias, mlstm_f_bias]).astype(F32)
    n_gate = w_gate.shape[1]
    wg_hi, wg_lo = _split_hi_lo(jnp.pad(w_gate, ((0, 0), (0, LANES - n_gate))))
    bias_r = jnp.pad(bias, (0, LANES - n_gate))[None, :]
    return w_main, wg_hi, wg_lo, bias_r


def kernel(x, attn_norm_w, w_in, fox_f_bias, fox_q_norm_w, fox_k_norm_w, fox_out_norm_w, mlstm_conv_w, mlstm_i_bias, mlstm_f_bias, mlstm_out_norm_w, w_out, moe_norm_w, router_w, router_b, expert_w_gate_up, expert_b_gate_up, expert_w_down, expert_b_down):
    bsz, seq, d = x.shape
    x2d = x.reshape(bsz * seq, d)
    prep = _prep_in_proj_weights(w_in[0], fox_f_bias[0], mlstm_i_bias[0], mlstm_f_bias[0])
    pair = lambda w: jnp.tile(w, LANES // HEAD_DIM)[None, :]
    qn, kn, aq, ak, vb, projm, gt = _in_proj(x2d, attn_norm_w[0][None, :], *prep,
                                             pair(fox_q_norm_w[0]), pair(fox_k_norm_w[0]))
    b3 = lambda a: a.reshape(bsz, seq, a.shape[-1])
    fox_y = _fox(b3(qn), b3(kn), b3(aq), b3(ak), b3(vb), fox_out_norm_w[0][None, :])
    mlstm_y = _mlstm(b3(projm), gt, mlstm_conv_w[0])
    return _channel_mixer(x2d, fox_y.reshape(-1, FOX_WIDTH), mlstm_y.reshape(-1, MLSTM_V_WIDTH), projm,
                          mlstm_out_norm_w[0], w_out[0], moe_norm_w[0], router_w[0], router_b[0],
                          expert_w_gate_up[0], expert_b_gate_up[0], expert_w_down[0],
                          expert_b_down[0]).reshape(bsz, seq, d)


def _channel_mixer(x2d, fox_y2d, mlstm_y2d, projm, mlstm_gain, w_out, moe_norm_w, router_w, router_b,
                   w_gu, b_gu, w_down, b_down):
    wr_hi, wr_lo = _split_hi_lo(router_w.T)
    x1, h2, eidx, gates, rank, counts = _out_route(
        x2d, fox_y2d, mlstm_y2d, projm, w_out.astype(BF16), mlstm_gain[None, :], moe_norm_w[None, :],
        wr_hi, wr_lo, router_b[:, None])
    block_e, n_used, n_valid, buf_tok3, dst3, n_slots = _dispatch_plan(eidx, rank, counts[:, 0])
    y_slots = _experts(block_e, n_used, n_valid, buf_tok3, dst3, h2, w_gu, b_gu, w_down, b_down, n_slots)
    return _combine(x1, y_slots, gates)
```

```python
import functools
import math

import jax
import jax.numpy as jnp
from jax import lax
from jax.experimental import pallas as pl
from jax.experimental.pallas import tpu as pltpu

F32 = jnp.float32
BF16 = jnp.bfloat16

D_MODEL = 1024
SEQ = 2048
HEAD_DIM = 64
FOX_HEADS = 8
FOX_WIDTH = FOX_HEADS * HEAD_DIM
MLSTM_HEADS = 8
MLSTM_QK_DIM = 32
MLSTM_V_DIM = 64
MLSTM_QK_WIDTH = MLSTM_HEADS * MLSTM_QK_DIM
MLSTM_V_WIDTH = MLSTM_HEADS * MLSTM_V_DIM
CONV_WIDTH = 4
MLSTM_CHUNK = 64
SPLIT_WIDTHS = (FOX_WIDTH, FOX_WIDTH, FOX_WIDTH, FOX_HEADS,
                MLSTM_QK_WIDTH, MLSTM_QK_WIDTH, MLSTM_V_WIDTH,
                MLSTM_HEADS, MLSTM_HEADS, MLSTM_V_WIDTH)
N_EXPERTS = 32
TOP_K = 4
D_EXPERT = D_MODEL
SWIGLU_ALPHA = 1.702
SWIGLU_LIMIT = 7.0
NORM_EPS = 1e-5
LOG2E = 1.4426950408889634

LANES = 128
SUBLANES = 8
V7X_VMEM_BYTES = 64 * 1024 * 1024

MAIN_WIDTH = 3 * FOX_WIDTH + 2 * MLSTM_QK_WIDTH + 2 * MLSTM_V_WIDTH
MLSTM_COLS = MAIN_WIDTH - 3 * FOX_WIDTH
GATE_ROWS = 32

NT_DIMS = (((1,), (1,)), ((), ()))


def _vmem_limit(nbytes):
    return int(min(nbytes + (8 << 20), V7X_VMEM_BYTES - (4 << 20)))


def _log_sigmoid(x):
    return jnp.minimum(x, 0.0) - jnp.log(1.0 + jnp.exp(-jnp.abs(x)))


def _split_hi_lo(x):
    hi = x.astype(BF16)
    lo = (x - hi.astype(F32)).astype(BF16)
    return hi, lo


def _rms(x, w):
    return x * lax.rsqrt(jnp.mean(x * x, axis=-1, keepdims=True) + NORM_EPS) * w


TOK_ROWS = D_MODEL // LANES


def _rows_to_tiles(x, tile_ref):
    m = x.shape[0]
    for j in range(TOK_ROWS):
        tile_ref[pl.ds(j, m, stride=TOK_ROWS), :] = x[:, j * LANES:(j + 1) * LANES]


def _tiles_to_rows(tile_ref, m):
    return jnp.concatenate(
        [tile_ref[pl.ds(j, m, stride=TOK_ROWS), :] for j in range(TOK_ROWS)], axis=1)


IN_TM = 512
IN_TILES_PER_SEQ = SEQ // IN_TM
IN_PARTS = 2
IN_PM = IN_TM // IN_PARTS
IN_SCAN_SHIFTS = tuple(1 << i for i in range(int(math.log2(IN_PM))))


def _pair_rms(x, w, lo_half):
    sq = x * x
    ms_lo = jnp.sum(jnp.where(lo_half, sq, 0.0), axis=-1, keepdims=True) * (1.0 / HEAD_DIM)
    ms_hi = jnp.sum(jnp.where(lo_half, 0.0, sq), axis=-1, keepdims=True) * (1.0 / HEAD_DIM)
    inv = jnp.where(lo_half, lax.rsqrt(ms_lo + NORM_EPS), lax.rsqrt(ms_hi + NORM_EPS))
    return x * inv * w


def _in_proj_kernel(x_ref, nw_ref, w_ref, wg_hi_ref, wg_lo_ref, br_ref, qw_ref, kw_ref,
                    qn_ref, kn_ref, aq_ref, ak_ref, vb_ref, projm_ref, gt_ref, carry_ref):
    i = pl.program_id(0)

    @pl.when(i % IN_TILES_PER_SEQ == 0)
    def _():
        carry_ref[...] = jnp.zeros_like(carry_ref)

    for part in range(IN_PARTS):
        rows = slice(part * IN_PM, (part + 1) * IN_PM)
        x = x_ref[rows, :]
        ms = jnp.mean(x * x, axis=-1, keepdims=True)
        y = x * lax.rsqrt(ms + NORM_EPS) * nw_ref[...]
        h_hi, h_lo = _split_hi_lo(y)
        main = jnp.dot(h_hi, w_ref[...], preferred_element_type=F32)

        w_hi, w_lo = wg_hi_ref[...], wg_lo_ref[...]
        g = (jnp.dot(h_hi, w_hi, preferred_element_type=F32)
             + jnp.dot(h_hi, w_lo, preferred_element_type=F32)
             + jnp.dot(h_lo, w_hi, preferred_element_type=F32))
        g = g + br_ref[...]
        lane = lax.broadcasted_iota(jnp.int32, g.shape, 1)
        is_input_gate = (lane >= FOX_HEADS) & (lane < FOX_HEADS + MLSTM_HEADS)
        gates = jnp.where(is_input_gate, g, _log_sigmoid(g))
        gt_ref[:, rows] = gates.T[:GATE_ROWS, :]

        rowi = lax.broadcasted_iota(jnp.int32, gates.shape, 0)
        c = gates
        for s in IN_SCAN_SHIFTS:
            c = c + jnp.where(rowi >= s, pltpu.roll(c, s, axis=0), 0.0)
        c = c + carry_ref[...]
        carry_ref[...] = c[IN_PM - 1:IN_PM, :]
        cum2 = c * LOG2E

        c8 = jnp.where(lane < FOX_HEADS, cum2, 0.0)
        c_hi = c8.astype(BF16).astype(F32)
        r1 = c8 - c_hi
        c_mid = r1.astype(BF16).astype(F32)
        c_lo = (r1 - c_mid).astype(BF16).astype(F32)
        ones_q = jnp.where((lane >= 3 * FOX_HEADS) & (lane < 6 * FOX_HEADS), 1.0, 0.0)
        ones_k = jnp.where(lane < 3 * FOX_HEADS, 1.0, 0.0)
        aq = (c_hi + pltpu.roll(c_mid, FOX_HEADS, axis=1) + pltpu.roll(c_lo, 2 * FOX_HEADS, axis=1)
              + ones_q)
        ak = ones_k - (pltpu.roll(c_hi, 3 * FOX_HEADS, axis=1) + pltpu.roll(c_mid, 4 * FOX_HEADS, axis=1)
                       + pltpu.roll(c_lo, 5 * FOX_HEADS, axis=1))
        aq_ref[rows, :] = aq.astype(BF16)
        ak_ref[rows, :] = ak.astype(BF16)

        lo_half = lane < HEAD_DIM
        q_scale = (HEAD_DIM ** -0.5) * LOG2E
        for p in range(FOX_HEADS // 2):
            ps = slice(p * LANES, (p + 1) * LANES)
            qn_ref[rows, ps] = (_pair_rms(main[:, ps], qw_ref[...], lo_half) * q_scale).astype(BF16)
            kn_ref[rows, ps] = _pair_rms(main[:, FOX_WIDTH + p * LANES:FOX_WIDTH + (p + 1) * LANES],
                                         kw_ref[...], lo_half).astype(BF16)

        vb_ref[rows, :] = main[:, 2 * FOX_WIDTH:3 * FOX_WIDTH].astype(BF16)
        projm_ref[rows, :] = main[:, 3 * FOX_WIDTH:]


def _in_proj(x2d, norm_w, w_main, wg_hi, wg_lo, bias_r, qw, kw):
    n = x2d.shape[0]
    tm = IN_TM
    const = lambda i: (0, 0)
    row = lambda w: pl.BlockSpec((tm, w), lambda i: (i, 0))
    vmem = (2 * (tm * D_MODEL * 4 + D_MODEL * MAIN_WIDTH * 2 + 3 * tm * FOX_WIDTH * 2 + 2 * tm * LANES * 2
                 + tm * MLSTM_COLS * 4 + GATE_ROWS * tm * 4 + tm * LANES * 4)
            + 2 * tm * MAIN_WIDTH * 4)
    return pl.pallas_call(
        _in_proj_kernel,
        grid=(n // tm,),
        in_specs=[
            row(D_MODEL),
            pl.BlockSpec((1, D_MODEL), const),
            pl.BlockSpec((D_MODEL, MAIN_WIDTH), const),
            pl.BlockSpec((D_MODEL, LANES), const),
            pl.BlockSpec((D_MODEL, LANES), const),
            pl.BlockSpec((1, LANES), const),
            pl.BlockSpec((1, LANES), const),
            pl.BlockSpec((1, LANES), const),
        ],
        out_specs=[
            row(FOX_WIDTH), row(FOX_WIDTH), row(LANES), row(LANES), row(FOX_WIDTH), row(MLSTM_COLS),
            pl.BlockSpec((GATE_ROWS, tm), lambda i: (0, i)),
        ],
        out_shape=[
            jax.ShapeDtypeStruct((n, FOX_WIDTH), BF16),
            jax.ShapeDtypeStruct((n, FOX_WIDTH), BF16),
            jax.ShapeDtypeStruct((n, LANES), BF16),
            jax.ShapeDtypeStruct((n, LANES), BF16),
            jax.ShapeDtypeStruct((n, FOX_WIDTH), BF16),
            jax.ShapeDtypeStruct((n, MLSTM_COLS), F32),
            jax.ShapeDtypeStruct((GATE_ROWS, n), F32),
        ],
        scratch_shapes=[pltpu.VMEM((1, LANES), F32)],
        compiler_params=pltpu.CompilerParams(
            dimension_semantics=("arbitrary",), vmem_limit_bytes=_vmem_limit(vmem)),
        name="in_proj",
    )(x2d, norm_w, w_main, wg_hi, wg_lo, bias_r, qw, kw)


FOX_TQ = 512
FOX_PAIRS = 4


def _fox_kernel(qn_ref, kn_ref, aq_ref, ak_ref, v_ref, ow_ref, o_ref, qa_ref, ka_ref, va_ref):
    step = pl.program_id(1)
    tri = (lax.broadcasted_iota(jnp.int32, (FOX_TQ, FOX_TQ), 1)
           <= lax.broadcasted_iota(jnp.int32, (FOX_TQ, FOX_TQ), 0))
    lo_q = lax.broadcasted_iota(jnp.int32, (FOX_TQ, LANES), 1) < HEAD_DIM
    lane = lax.broadcasted_iota(jnp.int32, (1, LANES), 1)
    sum_lanes = (HEAD_DIM, 0)
    for pp in range(FOX_PAIRS):
        ps = slice(pp * LANES, (pp + 1) * LANES)
        ka_ref[pp, :, :LANES] = kn_ref[0, :, ps]
        ka_ref[pp, :, LANES:] = ak_ref[0]
        for j in range(2):
            h = 2 * (step * FOX_PAIRS + pp) + j
            own_mask = jnp.where((lane < HEAD_DIM) if j == 0 else (lane >= HEAD_DIM), 1.0, 0.0).astype(BF16)
            bias_mask = jnp.where((lane < 6 * FOX_HEADS) & (lane % FOX_HEADS == h), 1.0, 0.0).astype(BF16)
            qa_ref[2 * pp + j, :, :LANES] = qn_ref[0, :, ps] * own_mask
            qa_ref[2 * pp + j, :, LANES:] = aq_ref[0] * bias_mask
            va_ref[2 * pp + j] = (v_ref[0, :, ps] * own_mask
                                  + jnp.where(lane == sum_lanes[j], 1.0, 0.0).astype(BF16))

    for pp in range(FOX_PAIRS):
        ps = slice(pp * LANES, (pp + 1) * LANES)
        for i in range(SEQ // FOX_TQ):
            qs = slice(i * FOX_TQ, (i + 1) * FOX_TQ)
            n = (i + 1) * FOX_TQ
            normed = []
            for j in range(2):
                own_q = lo_q if j == 0 else jnp.logical_not(lo_q)
                s = lax.dot_general(qa_ref[2 * pp + j, qs, :], ka_ref[pp, :n, :], NT_DIMS,
                                    preferred_element_type=F32)
                diag = jnp.where(tri, s[:, n - FOX_TQ:], -jnp.inf)
                m = jnp.max(diag, axis=-1, keepdims=True)
                if i > 0:
                    past = s[:, :n - FOX_TQ]
                    m = jnp.maximum(m, jnp.max(past, axis=-1, keepdims=True))
                    p = jnp.concatenate([jnp.exp2(past - m), jnp.exp2(diag - m)], axis=1)
                else:
                    p = jnp.exp2(diag - m)
                o = jnp.dot(p.astype(BF16), va_ref[2 * pp + j, :n, :],
                            preferred_element_type=F32)
                o = o / o[:, sum_lanes[j]:sum_lanes[j] + 1]
                ms = jnp.sum(jnp.where(own_q, o * o, 0.0), axis=-1, keepdims=True) * (1.0 / HEAD_DIM)
                normed.append(o * lax.rsqrt(ms + NORM_EPS))
            o_ref[0, qs, ps] = jnp.where(lo_q, normed[0], normed[1]) * ow_ref[:, ps]


def _fox(qn3, kn3, aq3, ak3, v3, ow):
    b = qn3.shape[0]
    width = FOX_PAIRS * LANES
    steps = FOX_WIDTH // width
    pairs = pl.BlockSpec((1, SEQ, width), lambda bi, st: (bi, 0, st))
    shared = pl.BlockSpec((1, SEQ, LANES), lambda bi, st: (bi, 0, 0))
    vmem = (2 * (3 * SEQ * width * 2 + 2 * SEQ * LANES * 2 + SEQ * width * 4)
            + FOX_PAIRS * SEQ * LANES * 2 * 8 + 8 * FOX_TQ * SEQ * 4)
    return pl.pallas_call(
        _fox_kernel,
        grid=(b, steps),
        in_specs=[pairs, pairs, shared, shared, pairs, pl.BlockSpec((1, width), lambda bi, st: (0, st))],
        out_specs=pairs,
        out_shape=jax.ShapeDtypeStruct((b, SEQ, FOX_WIDTH), F32),
        scratch_shapes=[pltpu.VMEM((2 * FOX_PAIRS, SEQ, 2 * LANES), BF16),
                        pltpu.VMEM((FOX_PAIRS, SEQ, 2 * LANES), BF16),
                        pltpu.VMEM((2 * FOX_PAIRS, SEQ, LANES), BF16)],
        compiler_params=pltpu.CompilerParams(
            dimension_semantics=("arbitrary", "arbitrary"), vmem_limit_bytes=_vmem_limit(vmem)),
        name="fox",
    )(qn3, kn3, aq3, ak3, v3, ow)


ML_L = MLSTM_CHUNK
ML_PAIRS = SEQ // (2 * ML_L)
ML_HL = FOX_HEADS
ML_AUG = MLSTM_V_WIDTH + LANES
ML_TILE = 256
SEG_SHIFTS = tuple(1 << i for i in range(int(math.log2(ML_L))))


def _split3(x):
    a = x.astype(BF16)
    r = x - a.astype(F32)
    b = r.astype(BF16)
    c = (r - b.astype(F32)).astype(BF16)
    return a, b, c


def _expand_heads(x, exp_bf):
    a, b, c = _split3(x)
    return (jnp.dot(a, exp_bf, preferred_element_type=F32)
            + jnp.dot(b, exp_bf, preferred_element_type=F32)
            + jnp.dot(c, exp_bf, preferred_element_type=F32))


def _seg_scan(x, axis, op, ident):
    idx = lax.broadcasted_iota(jnp.int32, x.shape, axis) % ML_L
    for s in SEG_SHIFTS:
        x = op(x, jnp.where(idx >= s, pltpu.roll(x, s, axis=axis), ident))
    return x


def _mlstm_kernel(q_ref, k_ref, v_ref, gi_ref, gf_ref, cw_ref, o_ref, den_ref,
                  qc_ref, kc_ref, kt_ref, rr_ref, cmr_ref, bcr_ref,
                  ealpha_ref, ew_ref, wint_ref, floor_ref, mfull_ref, caug_ref):
    def conv_silu(u, w):
        rowi = lax.broadcasted_iota(jnp.int32, u.shape, 0)
        acc = u * w[CONV_WIDTH - 1:CONV_WIDTH, :]
        for d in range(1, CONV_WIDTH):
            sh = jnp.where(rowi >= d, pltpu.roll(u, d, axis=0), 0.0)
            acc = acc + sh * w[CONV_WIDTH - 1 - d:CONV_WIDTH - d, :]
        return acc / (1.0 + jnp.exp(-acc))

    cw = cw_ref[...]
    qc_ref[...] = conv_silu(q_ref[0], cw[:, :MLSTM_QK_WIDTH]).astype(BF16)
    kc = conv_silu(k_ref[0], cw[:, MLSTM_QK_WIDTH:]) * (MLSTM_QK_DIM ** -0.5)
    kc_ref[...] = kc.astype(BF16)
    kt = kc.T
    for p in range(ML_PAIRS):
        kt_ref[p] = kt[:, p * LANES:(p + 1) * LANES]

    bcum_r = _seg_scan(gf_ref[...], 1, jnp.add, 0.0)
    r_r = gi_ref[...] - bcum_r
    cmx_r = _seg_scan(r_r, 1, jnp.maximum, -jnp.inf)
    for p in range(ML_PAIRS):
        ls = slice(p * LANES, (p + 1) * LANES)
        rr_ref[p] = r_r[:, ls]
        cmr_ref[p] = cmx_r[:, ls]
        bcr_ref[p] = bcum_r[:, ls]

    def to_columns(rows8):
        pad_lo = jnp.zeros((ML_HL, SEQ), F32)
        pad_hi = jnp.zeros((LANES - ML_HL - MLSTM_HEADS, SEQ), F32)
        return jnp.concatenate([pad_lo, rows8, pad_hi], axis=0).T

    bcum_c = to_columns(bcum_r)
    cmx_c = to_columns(cmx_r)
    m = jnp.zeros((1, LANES), F32)
    for c in range(SEQ // ML_L):
        mfull_ref[c * ML_L:(c + 1) * ML_L, :] = jnp.broadcast_to(m, (ML_L, LANES))
        last = (c + 1) * ML_L - 1
        m = bcum_c[last:last + 1, :] + jnp.maximum(m, cmx_c[last:last + 1, :])
    mfull = mfull_ref[...]
    mx = jnp.maximum(mfull, cmx_c)
    wint_ref[...] = jnp.exp(mfull - mx)
    floor_ref[...] = jnp.exp(-(bcum_c + mx))
    mfull_ref[...] = -mx

    lane_e = lax.broadcasted_iota(jnp.int32, (LANES, MLSTM_V_WIDTH), 1) // MLSTM_V_DIM
    row_e = lax.broadcasted_iota(jnp.int32, (LANES, MLSTM_V_WIDTH), 0)
    exp_bf = jnp.where(row_e == lane_e + ML_HL, 1.0, 0.0).astype(BF16)

    def expand_tile(i, carry):
        rows = pl.ds(pl.multiple_of(i * ML_TILE, ML_TILE), ML_TILE)
        ealpha_ref[rows, :] = _expand_heads(mfull_ref[rows, :], exp_bf)
        ew_ref[rows, :] = _expand_heads(wint_ref[rows, :], exp_bf)
        return carry

    lax.fori_loop(0, SEQ // ML_TILE, expand_tile, 0)

    kb_rowh = lax.broadcasted_iota(jnp.int32, (MLSTM_HEADS * ML_L, MLSTM_QK_WIDTH), 0) // ML_L
    kb_lane = lax.broadcasted_iota(jnp.int32, (MLSTM_HEADS * ML_L, MLSTM_QK_WIDTH), 1) // MLSTM_QK_DIM
    mask_k = jnp.where(kb_rowh == kb_lane, 1.0, 0.0).astype(BF16)
    va_rowh = lax.broadcasted_iota(jnp.int32, (MLSTM_HEADS * ML_L, ML_AUG), 0) // ML_L
    va_col = lax.broadcasted_iota(jnp.int32, (MLSTM_HEADS * ML_L, ML_AUG), 1)
    mask_v = jnp.where(
        (va_col // MLSTM_V_DIM == va_rowh) | (va_col == MLSTM_V_WIDTH + ML_HL + va_rowh),
        1.0, 0.0).astype(BF16)
    c_rowh = lax.broadcasted_iota(jnp.int32, (MLSTM_QK_WIDTH, ML_AUG), 0) // MLSTM_QK_DIM
    c_col = lax.broadcasted_iota(jnp.int32, (MLSTM_QK_WIDTH, ML_AUG), 1)
    mask_c = (c_col // MLSTM_V_DIM == c_rowh) | (c_col == MLSTM_V_WIDTH + ML_HL + c_rowh)
    lane128 = lax.broadcasted_iota(jnp.int32, (ML_L, LANES), 1)
    s_idx = lax.broadcasted_iota(jnp.int32, (ML_L, MLSTM_V_WIDTH), 1) % ML_L
    t_idx = lax.broadcasted_iota(jnp.int32, (ML_L, MLSTM_V_WIDTH), 0)
    causal = s_idx <= t_idx
    ones_aug = jnp.ones((ML_L, LANES), F32)

    caug_ref[...] = jnp.zeros_like(caug_ref)

    def pair_body(cp, m_row):
        r2 = rr_ref[cp]
        cm2 = cmr_ref[cp]
        b2 = bcr_ref[cp]
        r2r = pltpu.roll(r2, ML_L, axis=1)
        kt2 = kt_ref[cp]
        for cc in range(2):
            lo = cc * ML_L
            rows = pl.ds(pl.multiple_of(cp * (2 * ML_L), 2 * ML_L) + lo, ML_L)
            qa = qc_ref[rows, :]
            ka = kc_ref[rows, :]
            va_aug = jnp.concatenate([v_ref[0, rows, :], ones_aug], axis=1).astype(BF16)

            kbd = jnp.concatenate([ka] * MLSTM_HEADS, axis=0) * mask_k
            s = lax.dot_general(qa, kbd, NT_DIMS, preferred_element_type=F32)

            cmx_last = cm2[:, lo + ML_L - 1:lo + ML_L]
            b_last = b2[:, lo + ML_L - 1:lo + ML_L]
            mx_r = jnp.maximum(m_row, cmx_last)
            decay = jnp.exp(m_row - mx_r)
            wk = jnp.exp(r2[:, lo:lo + ML_L] - mx_r)
            m_row = b_last + mx_r

            src_e, src_o = (r2, r2r) if cc == 0 else (r2r, r2)
            cols = []
            for p in range(MLSTM_HEADS // 2):
                even = jnp.broadcast_to(src_e[2 * p:2 * p + 1, :], (ML_L, LANES))
                odd = jnp.broadcast_to(src_o[2 * p + 1:2 * p + 2, :], (ML_L, LANES))
                cols.append(jnp.where(lane128 < ML_L, even, odd))
            r_all = jnp.concatenate(cols, axis=1)
            arg = jnp.where(causal, ealpha_ref[rows, :] + r_all, -jnp.inf)
            p_all = (s * jnp.exp(arg)).astype(BF16)

            vbd = jnp.concatenate([va_aug] * MLSTM_HEADS, axis=0) * mask_v
            pv = jnp.dot(p_all, vbd, preferred_element_type=F32)
            qc_state = jnp.dot(qa, caug_ref[...].astype(BF16), preferred_element_type=F32)
            o_ref[0, rows, :] = (ew_ref[rows, :] * qc_state[:, :MLSTM_V_WIDTH]
                                 + pv[:, :MLSTM_V_WIDTH])
            den_ref[rows, :] = (wint_ref[rows, :] * qc_state[:, MLSTM_V_WIDTH:]
                                + pv[:, MLSTM_V_WIDTH:])

            wk_rows = jnp.concatenate(
                [jnp.broadcast_to(wk[h:h + 1, :], (MLSTM_QK_DIM, ML_L)) for h in range(MLSTM_HEADS)],
                axis=0)
            dec_rows = jnp.concatenate(
                [jnp.broadcast_to(decay[h:h + 1, :], (MLSTM_QK_DIM, 1)) for h in range(MLSTM_HEADS)],
                axis=0)
            ktw = (kt2[:, lo:lo + ML_L] * wk_rows).astype(BF16)
            upd = jnp.dot(ktw, va_aug, preferred_element_type=F32)
            caug_ref[...] = dec_rows * caug_ref[...] + jnp.where(mask_c, upd, 0.0)
        return m_row

    lax.fori_loop(0, ML_PAIRS, pair_body, jnp.zeros((MLSTM_HEADS, 1), F32), unroll=4)

    ob_row = lax.broadcasted_iota(jnp.int32, (MLSTM_V_WIDTH, LANES), 0) // MLSTM_V_DIM
    ob_col = lax.broadcasted_iota(jnp.int32, (MLSTM_V_WIDTH, LANES), 1)
    ones_bd = jnp.where(ob_col == ob_row + ML_HL, 1.0, 0.0).astype(BF16)

    def norm_tile(i, carry):
        rows = pl.ds(pl.multiple_of(i * ML_TILE, ML_TILE), ML_TILE)
        num = o_ref[0, rows, :]
        dn = jnp.maximum(jnp.abs(den_ref[rows, :]), floor_ref[rows, :])
        r = 1.0 / dn
        n2_hi, n2_lo = _split_hi_lo(num * num)
        msn = (jnp.dot(n2_hi, ones_bd, preferred_element_type=F32)
               + jnp.dot(n2_lo, ones_bd, preferred_element_type=F32)) * (1.0 / MLSTM_V_DIM)
        fac = r * lax.rsqrt(r * r * msn + NORM_EPS)
        o_ref[0, rows, :] = num * _expand_heads(fac, exp_bf)
        return carry

    lax.fori_loop(0, SEQ // ML_TILE, norm_tile, 0)


def _mlstm(projm3, gt, conv_w):
    b = projm3.shape[0]
    qk_blk = (1, SEQ, MLSTM_QK_WIDTH)
    v_blk = (1, SEQ, MLSTM_V_WIDTH)
    v_col = 2 * MLSTM_QK_WIDTH // MLSTM_V_WIDTH
    vmem = (2 * (2 * SEQ * MLSTM_QK_WIDTH * 4 + 2 * SEQ * MLSTM_V_WIDTH * 4 + 2 * SEQ * LANES * 4)
            + 2 * SEQ * MLSTM_QK_WIDTH * 2 + SEQ * MLSTM_QK_WIDTH * 4 + 2 * SEQ * MLSTM_V_WIDTH * 4
            + 3 * SEQ * LANES * 4 + (8 << 20))
    return pl.pallas_call(
        _mlstm_kernel,
        grid=(b,),
        in_specs=[
            pl.BlockSpec(qk_blk, lambda bi: (bi, 0, 0)),
            pl.BlockSpec(qk_blk, lambda bi: (bi, 0, 1)),
            pl.BlockSpec(v_blk, lambda bi: (bi, 0, v_col)),
            pl.BlockSpec((SUBLANES, SEQ), lambda bi: (1, bi)),
            pl.BlockSpec((SUBLANES, SEQ), lambda bi: (2, bi)),
            pl.BlockSpec((CONV_WIDTH, 2 * MLSTM_QK_WIDTH), lambda bi: (0, 0)),
        ],
        out_specs=pl.BlockSpec(v_blk, lambda bi: (bi, 0, 0)),
        out_shape=jax.ShapeDtypeStruct((b, SEQ, MLSTM_V_WIDTH), F32),
        scratch_shapes=[
            pltpu.VMEM((SEQ, LANES), F32),
            pltpu.VMEM((SEQ, MLSTM_QK_WIDTH), BF16),
            pltpu.VMEM((SEQ, MLSTM_QK_WIDTH), BF16),
            pltpu.VMEM((ML_PAIRS, MLSTM_QK_WIDTH, LANES), F32),
            pltpu.VMEM((ML_PAIRS, SUBLANES, LANES), F32),
            pltpu.VMEM((ML_PAIRS, SUBLANES, LANES), F32),
            pltpu.VMEM((ML_PAIRS, SUBLANES, LANES), F32),
            pltpu.VMEM((SEQ, MLSTM_V_WIDTH), F32),
            pltpu.VMEM((SEQ, MLSTM_V_WIDTH), F32),
            pltpu.VMEM((SEQ, LANES), F32),
            pltpu.VMEM((SEQ, LANES), F32),
            pltpu.VMEM((SEQ, LANES), F32),
            pltpu.VMEM((MLSTM_QK_WIDTH, ML_AUG), F32),
        ],
        compiler_params=pltpu.CompilerParams(
            dimension_semantics=("arbitrary",), vmem_limit_bytes=_vmem_limit(vmem)),
        name="mlstm",
    )(projm3, projm3, projm3, gt, gt, conv_w)


RT_TM = 512
RT_PARTS = 1


def _out_route_kernel(x_ref, fy_ref, my_ref, mo_ref, wo_ref, mg_ref, nw_ref, wr_hi_ref, wr_lo_ref,
                      rb_ref, x1_ref, h2_ref, eidx_ref, gate_ref, rank_ref, cnt_ref, carry_ref):
    i = pl.program_id(0)

    @pl.when(i == 0)
    def _():
        carry_ref[...] = jnp.zeros_like(carry_ref)

    pm = RT_TM // RT_PARTS
    src = lax.broadcasted_iota(jnp.int32, (pm, pm), 0)
    dst = lax.broadcasted_iota(jnp.int32, (pm, pm), 1)
    upper = jnp.where(src < dst, 1.0, 0.0).astype(BF16)
    zi = jnp.zeros((SUBLANES - TOP_K, pm), jnp.int32)

    for part in range(RT_PARTS):
        rows = slice(part * pm, (part + 1) * pm)
        my = my_ref[rows, :] * mg_ref[...] / (1.0 + jnp.exp(-mo_ref[rows, :]))
        mixed = (jnp.dot(fy_ref[rows, :].astype(BF16), wo_ref[:FOX_WIDTH, :], preferred_element_type=F32)
                 + jnp.dot(my.astype(BF16), wo_ref[FOX_WIDTH:, :], preferred_element_type=F32))
        x1 = x_ref[rows, :] + mixed
        x1_ref[rows, :] = x1
        h2 = _rms(x1, nw_ref[...])
        _rows_to_tiles(h2, h2_ref.at[pl.ds(part * pm * TOK_ROWS, pm * TOK_ROWS), :])

        h_hi, h_lo = _split_hi_lo(h2)
        wr_hi, wr_lo = wr_hi_ref[...], wr_lo_ref[...]
        logit = (lax.dot_general(wr_hi, h_hi, NT_DIMS, preferred_element_type=F32)
                 + lax.dot_general(wr_lo, h_hi, NT_DIMS, preferred_element_type=F32)
                 + lax.dot_general(wr_hi, h_lo, NT_DIMS, preferred_element_type=F32)) + rb_ref[...]

        e_iota = lax.broadcasted_iota(jnp.int32, logit.shape, 0).astype(F32)
        vals, idxs, hots = [], [], []
        for _ in range(TOP_K):
            mk = jnp.max(logit, axis=0, keepdims=True)
            idx = jnp.min(jnp.where(logit == mk, e_iota, float(N_EXPERTS)), axis=0, keepdims=True)
            hot = e_iota == idx
            logit = jnp.where(hot, -jnp.inf, logit)
            vals.append(mk)
            idxs.append(idx.astype(jnp.int32))
            hots.append(hot)
        exps = [jnp.exp(v - vals[0]) for v in vals]
        tot = exps[0] + exps[1] + exps[2] + exps[3]
        gates = [e / tot for e in exps]

        assign = jnp.zeros(logit.shape, F32)
        for hot in hots:
            assign = assign + jnp.where(hot, 1.0, 0.0)
        base = jnp.dot(assign.astype(BF16), upper, preferred_element_type=F32) + carry_ref[:, 0:1]
        ranks = [jnp.sum(jnp.where(hot, base, 0.0), axis=0, keepdims=True) for hot in hots]
        carry_ref[...] = carry_ref[...] + jnp.sum(assign, axis=1, keepdims=True)

        eidx_ref[:, rows] = jnp.concatenate(idxs + [zi], axis=0)
        rank_ref[:, rows] = jnp.concatenate([r.astype(jnp.int32) for r in ranks] + [zi], axis=0)
        gate_ref[:, rows] = jnp.concatenate(gates + [zi.astype(F32)], axis=0)

    cnt_ref[...] = carry_ref[...]


def _out_route(x2d, fox_y2d, mlstm_y2d, projm, w_out_bf, mlstm_gain, moe_norm_w, wr_hi, wr_lo, rb):
    n = x2d.shape[0]
    tm = RT_TM
    const = lambda i: (0, 0)
    mo_col = (MLSTM_COLS - MLSTM_V_WIDTH) // MLSTM_V_WIDTH
    row_blk = lambda w: pl.BlockSpec((tm, w), lambda i: (i, 0))
    lane_blk = pl.BlockSpec((SUBLANES, tm), lambda i: (0, i))
    vmem = (2 * (tm * D_MODEL * 4 * 3 + tm * FOX_WIDTH * 4 * 3 + D_MODEL * D_MODEL * 2)
            + 6 * tm * D_MODEL * 4 + tm * tm * 6)
    return pl.pallas_call(
        _out_route_kernel,
        grid=(n // tm,),
        in_specs=[
            row_blk(D_MODEL), row_blk(FOX_WIDTH), row_blk(MLSTM_V_WIDTH),
            pl.BlockSpec((tm, MLSTM_V_WIDTH), lambda i: (i, mo_col)),
            pl.BlockSpec((D_MODEL, D_MODEL), const),
            pl.BlockSpec((1, MLSTM_V_WIDTH), const),
            pl.BlockSpec((1, D_MODEL), const),
            pl.BlockSpec((N_EXPERTS, D_MODEL), const),
            pl.BlockSpec((N_EXPERTS, D_MODEL), const),
            pl.BlockSpec((N_EXPERTS, 1), const),
        ],
        out_specs=[row_blk(D_MODEL), pl.BlockSpec((tm * TOK_ROWS, LANES), lambda i: (i, 0)),
                   lane_blk, lane_blk, lane_blk, pl.BlockSpec((N_EXPERTS, LANES), const)],
        out_shape=[
            jax.ShapeDtypeStruct((n, D_MODEL), F32),
            jax.ShapeDtypeStruct((n * TOK_ROWS, LANES), F32),
            jax.ShapeDtypeStruct((SUBLANES, n), jnp.int32),
            jax.ShapeDtypeStruct((SUBLANES, n), F32),
            jax.ShapeDtypeStruct((SUBLANES, n), jnp.int32),
            jax.ShapeDtypeStruct((N_EXPERTS, LANES), F32),
        ],
        scratch_shapes=[pltpu.VMEM((N_EXPERTS, LANES), F32)],
        compiler_params=pltpu.CompilerParams(
            dimension_semantics=("arbitrary",), vmem_limit_bytes=_vmem_limit(vmem)),
        name="out_route",
    )(x2d, fox_y2d, mlstm_y2d, projm, w_out_bf, mlstm_gain, moe_norm_w, wr_hi, wr_lo, rb)


INV_CHUNK = 8192
INV_UNROLL = 16


def _invert_kernel(pos_ref, zeros_hbm, inv_ref, sem):
    i = pl.program_id(0)

    @pl.when(i == 0)
    def _():
        cp = pltpu.make_async_copy(zeros_hbm, inv_ref, sem.at[0])
        cp.start()
        cp.wait()

    base = i * INV_CHUNK

    def body(j, carry):
        inv_ref[pos_ref[0, 0, j]] = base + j
        return carry

    lax.fori_loop(0, INV_CHUNK, body, 0, unroll=INV_UNROLL)


def _invert(pos_flat, n_rows):
    n_slots = pos_flat.shape[0]
    steps = n_slots // INV_CHUNK
    return pl.pallas_call(
        _invert_kernel,
        grid=(steps,),
        in_specs=[
            pl.BlockSpec((1, 1, INV_CHUNK), lambda i: (i, 0, 0), memory_space=pltpu.SMEM),
            pl.BlockSpec(memory_space=pl.ANY),
        ],
        out_specs=pl.BlockSpec(memory_space=pltpu.SMEM),
        out_shape=jax.ShapeDtypeStruct((n_rows,), jnp.int32),
        scratch_shapes=[pltpu.SemaphoreType.DMA((1,))],
        compiler_params=pltpu.CompilerParams(dimension_semantics=("arbitrary",)),
        name="invert",
    )(pos_flat.reshape(steps, 1, INV_CHUNK), jnp.zeros((n_rows,), jnp.int32))


EX_BM = 256
EX_DRAIN_STEPS = 2


def _experts_kernel(be_ref, nu_ref, nv_ref, tok_ref, tokn_ref, dst_ref, h2_hbm, wgu_ref, bgu_ref,
                    wd_ref, bd_ref, y_hbm, xt_ref, yt_ref, wgu_bf_ref, wd_bf_ref, gsem, ssem):
    i = pl.program_id(0)
    last_blk = pl.num_programs(0) - 1 - EX_DRAIN_STEPS
    nu = nu_ref[0]
    slot = i % 2
    cur = jnp.minimum(i, last_blk)
    tile_rows = EX_BM * TOK_ROWS

    def tok_tile(ref, idx):
        return ref.at[pl.ds(pl.multiple_of(idx, TOK_ROWS), TOK_ROWS), :]

    def start_gather(idx_ref, s, n):
        @pl.when(n == EX_BM)
        def _():
            for r in range(EX_BM):
                pltpu.make_async_copy(tok_tile(h2_hbm, idx_ref[0, 0, r]),
                                      xt_ref.at[s, pl.ds(r * TOK_ROWS, TOK_ROWS), :], gsem.at[s]).start()

        @pl.when(n < EX_BM)
        def _():
            def body(r, carry):
                pltpu.make_async_copy(
                    tok_tile(h2_hbm, idx_ref[0, 0, r]),
                    xt_ref.at[s, pl.ds(pl.multiple_of(r * TOK_ROWS, TOK_ROWS), TOK_ROWS), :],
                    gsem.at[s]).start()
                return carry
            lax.fori_loop(0, n, body, 0)

    def scatter_row(r):
        pltpu.make_async_copy(yt_ref.at[slot, pl.ds(r * TOK_ROWS, TOK_ROWS), :],
                              tok_tile(y_hbm, dst_ref[0, 0, r]), ssem.at[slot]).start()

    def wait_block(sem, buf):
        pltpu.make_async_copy(h2_hbm.at[pl.ds(0, tile_rows), :], buf, sem).wait()

    def wait_tokens(sem, buf, n):
        @pl.when(n == EX_BM)
        def _():
            wait_block(sem, buf)

        @pl.when(n < EX_BM)
        def _():
            def body(r, carry):
                pltpu.make_async_copy(h2_hbm.at[pl.ds(0, TOK_ROWS), :],
                                      buf.at[pl.ds(0, TOK_ROWS), :], sem).wait()
                return carry
            lax.fori_loop(0, n, body, 0)

    @pl.when(i == 0)
    def _():
        xt_ref[...] = jnp.zeros_like(xt_ref)
        start_gather(tok_ref, 0, nv_ref[0])

    @pl.when(i + 1 < nu)
    def _():
        start_gather(tokn_ref, 1 - slot, nv_ref[jnp.minimum(i + 1, last_blk)])

    @pl.when((i >= 2) & (i - 2 < nu))
    def _():
        wait_tokens(ssem.at[slot], yt_ref.at[slot], nv_ref[jnp.clip(i - 2, 0, last_blk)])

    @pl.when(i < nu)
    def _():
        @pl.when((i == 0) | (be_ref[cur] != be_ref[jnp.maximum(cur - 1, 0)]))
        def _():
            wgu_bf_ref[...] = wgu_ref[0].astype(BF16)
            wd_bf_ref[...] = wd_ref[0].astype(BF16)

        wait_tokens(gsem.at[slot], xt_ref.at[slot], nv_ref[cur])
        xb = _tiles_to_rows(xt_ref.at[slot], EX_BM).astype(BF16)
        gu = jnp.dot(xb, wgu_bf_ref[...], preferred_element_type=F32) + bgu_ref[0]
        gate = jnp.minimum(gu[:, :D_EXPERT], SWIGLU_LIMIT)
        up = jnp.clip(gu[:, D_EXPERT:], -SWIGLU_LIMIT, SWIGLU_LIMIT)
        act = (up + 1.0) * (gate / (1.0 + jnp.exp(-SWIGLU_ALPHA * gate)))
        y = jnp.dot(act.astype(BF16), wd_bf_ref[...], preferred_element_type=F32) + bd_ref[0]
        _rows_to_tiles(y, yt_ref.at[slot])
        nv = nv_ref[cur]

        @pl.when(nv == EX_BM)
        def _():
            for r in range(EX_BM):
                scatter_row(r)

        @pl.when(nv < EX_BM)
        def _():
            def body(r, carry):
                pltpu.make_async_copy(
                    yt_ref.at[slot, pl.ds(pl.multiple_of(r * TOK_ROWS, TOK_ROWS), TOK_ROWS), :],
                    tok_tile(y_hbm, dst_ref[0, 0, r]), ssem.at[slot]).start()
                return carry
            lax.fori_loop(0, nv, body, 0)


def _experts(block_e, n_used, n_valid, buf_tok3, dst3, h2, w_gu, b_gu, w_down, b_down, n_slots):
    nb = buf_tok3.shape[0]
    idx_blk = lambda f: pl.BlockSpec((1, 1, EX_BM), f, memory_space=pltpu.SMEM)
    vmem = (2 * (D_MODEL * 2 * D_EXPERT * 4 + D_EXPERT * D_MODEL * 4)
            + D_MODEL * 2 * D_EXPERT * 2 + D_EXPERT * D_MODEL * 2
            + 3 * EX_BM * D_MODEL * 4 + 3 * EX_BM * 2 * D_EXPERT * 4)
    blk = lambda i: jnp.minimum(i, nb - 1)
    w_map = lambda i, be, nu, nv: (be[blk(i)], 0, 0)
    grid_spec = pltpu.PrefetchScalarGridSpec(
        num_scalar_prefetch=3,
        grid=(nb + EX_DRAIN_STEPS,),
        in_specs=[
            idx_blk(lambda i, be, nu, nv: (blk(i), 0, 0)),
            idx_blk(lambda i, be, nu, nv: (blk(i + 1), 0, 0)),
            idx_blk(lambda i, be, nu, nv: (blk(i), 0, 0)),
            pl.BlockSpec(memory_space=pl.ANY),
            pl.BlockSpec((1, D_MODEL, 2 * D_EXPERT), w_map),
            pl.BlockSpec((1, 1, 2 * D_EXPERT), w_map),
            pl.BlockSpec((1, D_EXPERT, D_MODEL), w_map),
            pl.BlockSpec((1, 1, D_MODEL), w_map),
        ],
        out_specs=pl.BlockSpec(memory_space=pl.ANY),
        scratch_shapes=[
            pltpu.VMEM((2, EX_BM * TOK_ROWS, LANES), F32),
            pltpu.VMEM((2, EX_BM * TOK_ROWS, LANES), F32),
            pltpu.VMEM((D_MODEL, 2 * D_EXPERT), BF16),
            pltpu.VMEM((D_EXPERT, D_MODEL), BF16),
            pltpu.SemaphoreType.DMA((2,)),
            pltpu.SemaphoreType.DMA((2,)),
        ],
    )
    return pl.pallas_call(
        _experts_kernel,
        grid_spec=grid_spec,
        out_shape=jax.ShapeDtypeStruct((n_slots * TOK_ROWS, LANES), F32),
        compiler_params=pltpu.CompilerParams(
            dimension_semantics=("arbitrary",), vmem_limit_bytes=_vmem_limit(vmem)),
        name="experts",
    )(block_e, n_used, n_valid, buf_tok3, buf_tok3, dst3, h2, w_gu, b_gu[:, None, :], w_down,
      b_down[:, None, :])


CB_TM = 512


def _combine_kernel(x1_ref, y0_ref, y1_ref, y2_ref, y3_ref, gate_ref, o_ref):
    tm = x1_ref.shape[0]
    g = jnp.concatenate([gate_ref[...], jnp.zeros((LANES - SUBLANES, tm), F32)], axis=0).T
    acc = x1_ref[...]
    for k, y_ref in enumerate((y0_ref, y1_ref, y2_ref, y3_ref)):
        acc = acc + g[:, k:k + 1] * _tiles_to_rows(y_ref, tm)
    o_ref[...] = acc


def _combine(x1, y_slots, gates):
    n = x1.shape[0]
    tm = CB_TM
    nt = n // tm
    vmem = 2 * (2 * tm * D_MODEL * 4 + tm * TOP_K * D_MODEL * 4) + 4 * tm * D_MODEL * 4
    y_spec = lambda k: pl.BlockSpec((tm * TOK_ROWS, LANES), lambda i: (k * nt + i, 0))
    return pl.pallas_call(
        _combine_kernel,
        grid=(nt,),
        in_specs=[pl.BlockSpec((tm, D_MODEL), lambda i: (i, 0))]
        + [y_spec(k) for k in range(TOP_K)]
        + [pl.BlockSpec((SUBLANES, tm), lambda i: (0, i))],
        out_specs=pl.BlockSpec((tm, D_MODEL), lambda i: (i, 0)),
        out_shape=jax.ShapeDtypeStruct((n, D_MODEL), F32),
        compiler_params=pltpu.CompilerParams(
            dimension_semantics=("arbitrary",), vmem_limit_bytes=_vmem_limit(vmem)),
        name="combine",
    )(x1, y_slots, y_slots, y_slots, y_slots, gates)


def _dispatch_plan(eidx, rank, counts):
    n = eidx.shape[1]
    n_slots = n * TOP_K
    nb = n_slots // EX_BM + N_EXPERTS
    counts = counts.astype(jnp.int32)
    padded = ((counts + EX_BM - 1) // EX_BM) * EX_BM
    pad_end = jnp.cumsum(padded)
    pad_start = pad_end - padded
    e = eidx[:TOP_K]
    start_of = jnp.sum(jnp.where(e[:, :, None] == jnp.arange(N_EXPERTS, dtype=jnp.int32),
                                 pad_start[None, None, :], 0), axis=-1)
    pos = start_of + rank[:TOP_K]
    inv = _invert(pos.reshape(-1), nb * EX_BM)
    buf_tok = (inv % n) * TOK_ROWS
    dst = inv * TOK_ROWS
    blk_start = jnp.arange(nb, dtype=jnp.int32) * EX_BM
    block_e = jnp.minimum(jnp.sum((pad_end[None, :] <= blk_start[:, None]).astype(jnp.int32), axis=1),
                          N_EXPERTS - 1)
    n_used = (pad_end[-1] // EX_BM).astype(jnp.int32).reshape(1)
    n_valid = jnp.clip(pad_start[block_e] + counts[block_e] - blk_start, 0, EX_BM).astype(jnp.int32)
    n_valid = jnp.where(blk_start < pad_end[-1], n_valid, 0)
    return (block_e, n_used, n_valid, buf_tok.reshape(nb, 1, EX_BM), dst.reshape(nb, 1, EX_BM), n_slots)


def _prep_in_proj_weights(w_in, fox_f_bias, mlstm_i_bias, mlstm_f_bias):
    split_at = []
    acc = 0
    for wdt in SPLIT_WIDTHS[:-1]:
        acc += wdt
        split_at.append(acc)
    fq, fk, fv, ff, mq, mk, mv, mi, mf, mo = jnp.split(w_in, split_at, axis=-1)
    w_main = jnp.concatenate([fq, fk, fv, mq, mk, mv, mo], axis=-1).astype(BF16)
    w_gate = jnp.concatenate([ff, mi, mf], axis=-1)
    bias = jnp.concatenate([fox_f_bias, mlstm_i_bias, mlstm_f_bias]).astype(F32)
    n_gate = w_gate.shape[1]
    wg_hi, wg_lo = _split_hi_lo(jnp.pad(w_gate, ((0, 0), (0, LANES - n_gate))))
    bias_r = jnp.pad(bias, (0, LANES - n_gate))[None, :]
    return w_main, wg_hi, wg_lo, bias_r


def kernel(x, attn_norm_w, w_in, fox_f_bias, fox_q_norm_w, fox_k_norm_w, fox_out_norm_w, mlstm_conv_w, mlstm_i_bias, mlstm_f_bias, mlstm_out_norm_w, w_out, moe_norm_w, router_w, router_b, expert_w_gate_up, expert_b_gate_up, expert_w_down, expert_b_down):
    bsz, seq, d = x.shape
    x2d = x.reshape(bsz * seq, d)
    prep = _prep_in_proj_weights(w_in[0], fox_f_bias[0], mlstm_i_bias[0], mlstm_f_bias[0])
    pair = lambda w: jnp.tile(w, LANES // HEAD_DIM)[None, :]
    qn, kn, aq, ak, vb, projm, gt = _in_proj(x2d, attn_norm_w[0][None, :], *prep,
                                             pair(fox_q_norm_w[0]), pair(fox_k_norm_w[0]))
    b3 = lambda a: a.reshape(bsz, seq, a.shape[-1])
    fox_y = _fox(b3(qn), b3(kn), b3(aq), b3(ak), b3(vb), fox_out_norm_w[0][None, :])
    mlstm_y = _mlstm(b3(projm), gt, mlstm_conv_w[0])
    return _channel_mixer(x2d, fox_y.reshape(-1, FOX_WIDTH), mlstm_y.reshape(-1, MLSTM_V_WIDTH), projm,
                          mlstm_out_norm_w[0], w_out[0], moe_norm_w[0], router_w[0], router_b[0],
                          expert_w_gate_up[0], expert_b_gate_up[0], expert_w_down[0],
                          expert_b_down[0]).reshape(bsz, seq, d)


def _channel_mixer(x2d, fox_y2d, mlstm_y2d, projm, mlstm_gain, w_out, moe_norm_w, router_w, router_b,
                   w_gu, b_gu, w_down, b_down):
    wr_hi, wr_lo = _split_hi_lo(router_w.T)
    x1, h2, eidx, gates, rank, counts = _out_route(
        x2d, fox_y2d, mlstm_y2d, projm, w_out.astype(BF16), mlstm_gain[None, :], moe_norm_w[None, :],
        wr_hi, wr_lo, router_b[:, None])
    block_e, n_used, n_valid, buf_tok3, dst3, n_slots = _dispatch_plan(eidx, rank, counts[:, 0])
    y_slots = _experts(block_e, n_used, n_valid, buf_tok3, dst3, h2, w_gu, b_gu, w_down, b_down, n_slots)
    return _combine(x1, y_slots, gates)
```

```python
import functools
import math

import jax
import jax.numpy as jnp
from jax import lax
from jax.experimental import pallas as pl
from jax.experimental.pallas import tpu as pltpu

F32 = jnp.float32
BF16 = jnp.bfloat16

D_MODEL = 1024
SEQ = 2048
HEAD_DIM = 64
FOX_HEADS = 8
FOX_WIDTH = FOX_HEADS * HEAD_DIM
MLSTM_HEADS = 8
MLSTM_QK_DIM = 32
MLSTM_V_DIM = 64
MLSTM_QK_WIDTH = MLSTM_HEADS * MLSTM_QK_DIM
MLSTM_V_WIDTH = MLSTM_HEADS * MLSTM_V_DIM
CONV_WIDTH = 4
MLSTM_CHUNK = 64
SPLIT_WIDTHS = (FOX_WIDTH, FOX_WIDTH, FOX_WIDTH, FOX_HEADS,
                MLSTM_QK_WIDTH, MLSTM_QK_WIDTH, MLSTM_V_WIDTH,
                MLSTM_HEADS, MLSTM_HEADS, MLSTM_V_WIDTH)
N_EXPERTS = 32
TOP_K = 4
D_EXPERT = D_MODEL
SWIGLU_ALPHA = 1.702
SWIGLU_LIMIT = 7.0
NORM_EPS = 1e-5
LOG2E = 1.4426950408889634

LANES = 128
SUBLANES = 8
V7X_VMEM_BYTES = 64 * 1024 * 1024

MAIN_WIDTH = 3 * FOX_WIDTH + 2 * MLSTM_QK_WIDTH + 2 * MLSTM_V_WIDTH
MLSTM_COLS = MAIN_WIDTH - 3 * FOX_WIDTH
GATE_ROWS = 32

NT_DIMS = (((1,), (1,)), ((), ()))


def _vmem_limit(nbytes):
    return int(min(nbytes + (8 << 20), V7X_VMEM_BYTES - (4 << 20)))


def _log_sigmoid(x):
    return jnp.minimum(x, 0.0) - jnp.log(1.0 + jnp.exp(-jnp.abs(x)))


def _split_hi_lo(x):
    hi = x.astype(BF16)
    lo = (x - hi.astype(F32)).astype(BF16)
    return hi, lo


def _rms(x, w):
    return x * lax.rsqrt(jnp.mean(x * x, axis=-1, keepdims=True) + NORM_EPS) * w


TOK_ROWS = D_MODEL // LANES


def _rows_to_tiles(x, tile_ref):
    m = x.shape[0]
    for j in range(TOK_ROWS):
        tile_ref[pl.ds(j, m, stride=TOK_ROWS), :] = x[:, j * LANES:(j + 1) * LANES]


def _tiles_to_rows(tile_ref, m):
    return jnp.concatenate(
        [tile_ref[pl.ds(j, m, stride=TOK_ROWS), :] for j in range(TOK_ROWS)], axis=1)


IN_TM = 1024
IN_TILES_PER_SEQ = SEQ // IN_TM
IN_PARTS = 2
IN_PM = IN_TM // IN_PARTS
IN_SCAN_SHIFTS = tuple(1 << i for i in range(int(math.log2(IN_PM))))


def _pair_rms(x, w, lo_half):
    sq = x * x
    ms_lo = jnp.sum(jnp.where(lo_half, sq, 0.0), axis=-1, keepdims=True) * (1.0 / HEAD_DIM)
    ms_hi = jnp.sum(jnp.where(lo_half, 0.0, sq), axis=-1, keepdims=True) * (1.0 / HEAD_DIM)
    inv = jnp.where(lo_half, lax.rsqrt(ms_lo + NORM_EPS), lax.rsqrt(ms_hi + NORM_EPS))
    return x * inv * w


def _in_proj_kernel(x_ref, nw_ref, w_ref, wg_hi_ref, wg_lo_ref, br_ref, qw_ref, kw_ref,
                    qn_ref, kn_ref, aq_ref, ak_ref, vb_ref, projm_ref, gt_ref, carry_ref):
    i = pl.program_id(0)

    @pl.when(i % IN_TILES_PER_SEQ == 0)
    def _():
        carry_ref[...] = jnp.zeros_like(carry_ref)

    for part in range(IN_PARTS):
        rows = slice(part * IN_PM, (part + 1) * IN_PM)
        x = x_ref[rows, :]
        ms = jnp.mean(x * x, axis=-1, keepdims=True)
        y = x * lax.rsqrt(ms + NORM_EPS) * nw_ref[...]
        h_hi, h_lo = _split_hi_lo(y)
        main = jnp.dot(h_hi, w_ref[...], preferred_element_type=F32)

        w_hi, w_lo = wg_hi_ref[...], wg_lo_ref[...]
        g = (jnp.dot(h_hi, w_hi, preferred_element_type=F32)
             + jnp.dot(h_hi, w_lo, preferred_element_type=F32)
             + jnp.dot(h_lo, w_hi, preferred_element_type=F32))
        g = g + br_ref[...]
        lane = lax.broadcasted_iota(jnp.int32, g.shape, 1)
        is_input_gate = (lane >= FOX_HEADS) & (lane < FOX_HEADS + MLSTM_HEADS)
        gates = jnp.where(is_input_gate, g, _log_sigmoid(g))
        gt_ref[:, rows] = gates.T[:GATE_ROWS, :]

        rowi = lax.broadcasted_iota(jnp.int32, gates.shape, 0)
        c = gates
        for s in IN_SCAN_SHIFTS:
            c = c + jnp.where(rowi >= s, pltpu.roll(c, s, axis=0), 0.0)
        c = c + carry_ref[...]
        carry_ref[...] = c[IN_PM - 1:IN_PM, :]
        cum2 = c * LOG2E

        c8 = jnp.where(lane < FOX_HEADS, cum2, 0.0)
        c_hi = c8.astype(BF16).astype(F32)
        r1 = c8 - c_hi
        c_mid = r1.astype(BF16).astype(F32)
        c_lo = (r1 - c_mid).astype(BF16).astype(F32)
        ones_q = jnp.where((lane >= 3 * FOX_HEADS) & (lane < 6 * FOX_HEADS), 1.0, 0.0)
        ones_k = jnp.where(lane < 3 * FOX_HEADS, 1.0, 0.0)
        aq = (c_hi + pltpu.roll(c_mid, FOX_HEADS, axis=1) + pltpu.roll(c_lo, 2 * FOX_HEADS, axis=1)
              + ones_q)
        ak = ones_k - (pltpu.roll(c_hi, 3 * FOX_HEADS, axis=1) + pltpu.roll(c_mid, 4 * FOX_HEADS, axis=1)
                       + pltpu.roll(c_lo, 5 * FOX_HEADS, axis=1))
        aq_ref[rows, :] = aq.astype(BF16)
        ak_ref[rows, :] = ak.astype(BF16)

        lo_half = lane < HEAD_DIM
        q_scale = (HEAD_DIM ** -0.5) * LOG2E
        for p in range(FOX_HEADS // 2):
            ps = slice(p * LANES, (p + 1) * LANES)
            qn_ref[rows, ps] = (_pair_rms(main[:, ps], qw_ref[...], lo_half) * q_scale).astype(BF16)
            kn_ref[rows, ps] = _pair_rms(main[:, FOX_WIDTH + p * LANES:FOX_WIDTH + (p + 1) * LANES],
                                         kw_ref[...], lo_half).astype(BF16)

        vb_ref[rows, :] = main[:, 2 * FOX_WIDTH:3 * FOX_WIDTH].astype(BF16)
        projm_ref[rows, :] = main[:, 3 * FOX_WIDTH:]


def _in_proj(x2d, norm_w, w_main, wg_hi, wg_lo, bias_r, qw, kw):
    n = x2d.shape[0]
    tm = IN_TM
    const = lambda i: (0, 0)
    row = lambda w: pl.BlockSpec((tm, w), lambda i: (i, 0))
    vmem = (2 * (tm * D_MODEL * 4 + D_MODEL * MAIN_WIDTH * 2 + 3 * tm * FOX_WIDTH * 2 + 2 * tm * LANES * 2
                 + tm * MLSTM_COLS * 4 + GATE_ROWS * tm * 4 + tm * LANES * 4)
            + 2 * tm * MAIN_WIDTH * 4)
    return pl.pallas_call(
        _in_proj_kernel,
        grid=(n // tm,),
        in_specs=[
            row(D_MODEL),
            pl.BlockSpec((1, D_MODEL), const),
            pl.BlockSpec((D_MODEL, MAIN_WIDTH), const),
            pl.BlockSpec((D_MODEL, LANES), const),
            pl.BlockSpec((D_MODEL, LANES), const),
            pl.BlockSpec((1, LANES), const),
            pl.BlockSpec((1, LANES), const),
            pl.BlockSpec((1, LANES), const),
        ],
        out_specs=[
            row(FOX_WIDTH), row(FOX_WIDTH), row(LANES), row(LANES), row(FOX_WIDTH), row(MLSTM_COLS),
            pl.BlockSpec((GATE_ROWS, tm), lambda i: (0, i)),
        ],
        out_shape=[
            jax.ShapeDtypeStruct((n, FOX_WIDTH), BF16),
            jax.ShapeDtypeStruct((n, FOX_WIDTH), BF16),
            jax.ShapeDtypeStruct((n, LANES), BF16),
            jax.ShapeDtypeStruct((n, LANES), BF16),
            jax.ShapeDtypeStruct((n, FOX_WIDTH), BF16),
            jax.ShapeDtypeStruct((n, MLSTM_COLS), F32),
            jax.ShapeDtypeStruct((GATE_ROWS, n), F32),
        ],
        scratch_shapes=[pltpu.VMEM((1, LANES), F32)],
        compiler_params=pltpu.CompilerParams(
            dimension_semantics=("arbitrary",), vmem_limit_bytes=_vmem_limit(vmem)),
        name="in_proj",
    )(x2d, norm_w, w_main, wg_hi, wg_lo, bias_r, qw, kw)


FOX_TQ = 512
FOX_PAIRS = 4


def _fox_kernel(qn_ref, kn_ref, aq_ref, ak_ref, v_ref, ow_ref, o_ref, qa_ref, ka_ref, va_ref):
    step = pl.program_id(1)
    tri = (lax.broadcasted_iota(jnp.int32, (FOX_TQ, FOX_TQ), 1)
           <= lax.broadcasted_iota(jnp.int32, (FOX_TQ, FOX_TQ), 0))
    lo_q = lax.broadcasted_iota(jnp.int32, (FOX_TQ, LANES), 1) < HEAD_DIM
    lane = lax.broadcasted_iota(jnp.int32, (1, LANES), 1)
    sum_lanes = (HEAD_DIM, 0)
    for pp in range(FOX_PAIRS):
        ps = slice(pp * LANES, (pp + 1) * LANES)
        ka_ref[pp, :, :LANES] = kn_ref[0, :, ps]
        ka_ref[pp, :, LANES:] = ak_ref[0]
        for j in range(2):
            h = 2 * (step * FOX_PAIRS + pp) + j
            own_mask = jnp.where((lane < HEAD_DIM) if j == 0 else (lane >= HEAD_DIM), 1.0, 0.0).astype(BF16)
            bias_mask = jnp.where((lane < 6 * FOX_HEADS) & (lane % FOX_HEADS == h), 1.0, 0.0).astype(BF16)
            qa_ref[2 * pp + j, :, :LANES] = qn_ref[0, :, ps] * own_mask
            qa_ref[2 * pp + j, :, LANES:] = aq_ref[0] * bias_mask
            va_ref[2 * pp + j] = (v_ref[0, :, ps] * own_mask
                                  + jnp.where(lane == sum_lanes[j], 1.0, 0.0).astype(BF16))

    for pp in range(FOX_PAIRS):
        ps = slice(pp * LANES, (pp + 1) * LANES)
        for i in range(SEQ // FOX_TQ):
            qs = slice(i * FOX_TQ, (i + 1) * FOX_TQ)
            n = (i + 1) * FOX_TQ
            normed = []
            for j in range(2):
                own_q = lo_q if j == 0 else jnp.logical_not(lo_q)
                s = lax.dot_general(qa_ref[2 * pp + j, qs, :], ka_ref[pp, :n, :], NT_DIMS,
                                    preferred_element_type=F32)
                diag = jnp.where(tri, s[:, n - FOX_TQ:], -jnp.inf)
                m = jnp.max(diag, axis=-1, keepdims=True)
                if i > 0:
                    past = s[:, :n - FOX_TQ]
                    m = jnp.maximum(m, jnp.max(past, axis=-1, keepdims=True))
                    p = jnp.concatenate([jnp.exp2(past - m), jnp.exp2(diag - m)], axis=1)
                else:
                    p = jnp.exp2(diag - m)
                o = jnp.dot(p.astype(BF16), va_ref[2 * pp + j, :n, :],
                            preferred_element_type=F32)
                o = o / o[:, sum_lanes[j]:sum_lanes[j] + 1]
                ms = jnp.sum(jnp.where(own_q, o * o, 0.0), axis=-1, keepdims=True) * (1.0 / HEAD_DIM)
                normed.append(o * lax.rsqrt(ms + NORM_EPS))
            o_ref[0, qs, ps] = jnp.where(lo_q, normed[0], normed[1]) * ow_ref[:, ps]


def _fox(qn3, kn3, aq3, ak3, v3, ow):
    b = qn3.shape[0]
    width = FOX_PAIRS * LANES
    steps = FOX_WIDTH // width
    pairs = pl.BlockSpec((1, SEQ, width), lambda bi, st: (bi, 0, st))
    shared = pl.BlockSpec((1, SEQ, LANES), lambda bi, st: (bi, 0, 0))
    vmem = (2 * (3 * SEQ * width * 2 + 2 * SEQ * LANES * 2 + SEQ * width * 4)
            + FOX_PAIRS * SEQ * LANES * 2 * 8 + 8 * FOX_TQ * SEQ * 4)
    return pl.pallas_call(
        _fox_kernel,
        grid=(b, steps),
        in_specs=[pairs, pairs, shared, shared, pairs, pl.BlockSpec((1, width), lambda bi, st: (0, st))],
        out_specs=pairs,
        out_shape=jax.ShapeDtypeStruct((b, SEQ, FOX_WIDTH), F32),
        scratch_shapes=[pltpu.VMEM((2 * FOX_PAIRS, SEQ, 2 * LANES), BF16),
                        pltpu.VMEM((FOX_PAIRS, SEQ, 2 * LANES), BF16),
                        pltpu.VMEM((2 * FOX_PAIRS, SEQ, LANES), BF16)],
        compiler_params=pltpu.CompilerParams(
            dimension_semantics=("arbitrary", "arbitrary"), vmem_limit_bytes=_vmem_limit(vmem)),
        name="fox",
    )(qn3, kn3, aq3, ak3, v3, ow)


ML_L = MLSTM_CHUNK
ML_PAIRS = SEQ // (2 * ML_L)
ML_HL = FOX_HEADS
ML_AUG = MLSTM_V_WIDTH + LANES
ML_TILE = 256
ML_PAIR_UNROLL = 4
SEG_SHIFTS = tuple(1 << i for i in range(int(math.log2(ML_L))))


def _split3(x):
    a = x.astype(BF16)
    r = x - a.astype(F32)
    b = r.astype(BF16)
    c = (r - b.astype(F32)).astype(BF16)
    return a, b, c


def _expand_heads(x, exp_bf):
    a, b, c = _split3(x)
    return (jnp.dot(a, exp_bf, preferred_element_type=F32)
            + jnp.dot(b, exp_bf, preferred_element_type=F32)
            + jnp.dot(c, exp_bf, preferred_element_type=F32))


def _seg_scan(x, axis, op, ident):
    idx = lax.broadcasted_iota(jnp.int32, x.shape, axis) % ML_L
    for s in SEG_SHIFTS:
        x = op(x, jnp.where(idx >= s, pltpu.roll(x, s, axis=axis), ident))
    return x


def _mlstm_kernel(q_ref, k_ref, v_ref, gi_ref, gf_ref, cw_ref, o_ref, den_ref,
                  qc_ref, kc_ref, kt_ref, rr_ref, cmr_ref, bcr_ref,
                  ealpha_ref, ew_ref, wint_ref, floor_ref, mfull_ref, caug_ref):
    def conv_silu(u, w):
        rowi = lax.broadcasted_iota(jnp.int32, u.shape, 0)
        acc = u * w[CONV_WIDTH - 1:CONV_WIDTH, :]
        for d in range(1, CONV_WIDTH):
            sh = jnp.where(rowi >= d, pltpu.roll(u, d, axis=0), 0.0)
            acc = acc + sh * w[CONV_WIDTH - 1 - d:CONV_WIDTH - d, :]
        return acc / (1.0 + jnp.exp(-acc))

    cw = cw_ref[...]
    qc_ref[...] = conv_silu(q_ref[0], cw[:, :MLSTM_QK_WIDTH]).astype(BF16)
    kc = conv_silu(k_ref[0], cw[:, MLSTM_QK_WIDTH:]) * (MLSTM_QK_DIM ** -0.5)
    kc_ref[...] = kc.astype(BF16)
    kt = kc.T
    for p in range(ML_PAIRS):
        kt_ref[p] = kt[:, p * LANES:(p + 1) * LANES]

    bcum_r = _seg_scan(gf_ref[...], 1, jnp.add, 0.0)
    r_r = gi_ref[...] - bcum_r
    cmx_r = _seg_scan(r_r, 1, jnp.maximum, -jnp.inf)
    for p in range(ML_PAIRS):
        ls = slice(p * LANES, (p + 1) * LANES)
        rr_ref[p] = r_r[:, ls]
        cmr_ref[p] = cmx_r[:, ls]
        bcr_ref[p] = bcum_r[:, ls]

    def to_columns(rows8):
        pad_lo = jnp.zeros((ML_HL, SEQ), F32)
        pad_hi = jnp.zeros((LANES - ML_HL - MLSTM_HEADS, SEQ), F32)
        return jnp.concatenate([pad_lo, rows8, pad_hi], axis=0).T

    bcum_c = to_columns(bcum_r)
    cmx_c = to_columns(cmx_r)
    m = jnp.zeros((1, LANES), F32)
    for c in range(SEQ // ML_L):
        mfull_ref[c * ML_L:(c + 1) * ML_L, :] = jnp.broadcast_to(m, (ML_L, LANES))
        last = (c + 1) * ML_L - 1
        m = bcum_c[last:last + 1, :] + jnp.maximum(m, cmx_c[last:last + 1, :])
    mfull = mfull_ref[...]
    mx = jnp.maximum(mfull, cmx_c)
    wint_ref[...] = jnp.exp(mfull - mx)
    floor_ref[...] = jnp.exp(-(bcum_c + mx))
    mfull_ref[...] = -mx

    lane_e = lax.broadcasted_iota(jnp.int32, (LANES, MLSTM_V_WIDTH), 1) // MLSTM_V_DIM
    row_e = lax.broadcasted_iota(jnp.int32, (LANES, MLSTM_V_WIDTH), 0)
    exp_bf = jnp.where(row_e == lane_e + ML_HL, 1.0, 0.0).astype(BF16)

    def expand_tile(i, carry):
        rows = pl.ds(pl.multiple_of(i * ML_TILE, ML_TILE), ML_TILE)
        ealpha_ref[rows, :] = _expand_heads(mfull_ref[rows, :], exp_bf)
        ew_ref[rows, :] = _expand_heads(wint_ref[rows, :], exp_bf)
        return carry

    lax.fori_loop(0, SEQ // ML_TILE, expand_tile, 0)

    kb_rowh = lax.broadcasted_iota(jnp.int32, (MLSTM_HEADS * ML_L, MLSTM_QK_WIDTH), 0) // ML_L
    kb_lane = lax.broadcasted_iota(jnp.int32, (MLSTM_HEADS * ML_L, MLSTM_QK_WIDTH), 1) // MLSTM_QK_DIM
    mask_k = jnp.where(kb_rowh == kb_lane, 1.0, 0.0).astype(BF16)
    va_rowh = lax.broadcasted_iota(jnp.int32, (MLSTM_HEADS * ML_L, ML_AUG), 0) // ML_L
    va_col = lax.broadcasted_iota(jnp.int32, (MLSTM_HEADS * ML_L, ML_AUG), 1)
    mask_v = jnp.where(
        (va_col // MLSTM_V_DIM == va_rowh) | (va_col == MLSTM_V_WIDTH + ML_HL + va_rowh),
        1.0, 0.0).astype(BF16)
    c_rowh = lax.broadcasted_iota(jnp.int32, (MLSTM_QK_WIDTH, ML_AUG), 0) // MLSTM_QK_DIM
    c_col = lax.broadcasted_iota(jnp.int32, (MLSTM_QK_WIDTH, ML_AUG), 1)
    mask_c = (c_col // MLSTM_V_DIM == c_rowh) | (c_col == MLSTM_V_WIDTH + ML_HL + c_rowh)
    lane128 = lax.broadcasted_iota(jnp.int32, (ML_L, LANES), 1)
    s_idx = lax.broadcasted_iota(jnp.int32, (ML_L, MLSTM_V_WIDTH), 1) % ML_L
    t_idx = lax.broadcasted_iota(jnp.int32, (ML_L, MLSTM_V_WIDTH), 0)
    causal = s_idx <= t_idx
    ones_aug = jnp.ones((ML_L, LANES), F32)

    caug_ref[...] = jnp.zeros_like(caug_ref)

    def pair_body(cp, m_row):
        r2 = rr_ref[cp]
        cm2 = cmr_ref[cp]
        b2 = bcr_ref[cp]
        r2r = pltpu.roll(r2, ML_L, axis=1)
        kt2 = kt_ref[cp]
        for cc in range(2):
            lo = cc * ML_L
            rows = pl.ds(pl.multiple_of(cp * (2 * ML_L), 2 * ML_L) + lo, ML_L)
            qa = qc_ref[rows, :]
            ka = kc_ref[rows, :]
            va_aug = jnp.concatenate([v_ref[0, rows, :], ones_aug], axis=1).astype(BF16)

            kbd = jnp.concatenate([ka] * MLSTM_HEADS, axis=0) * mask_k
            s = lax.dot_general(qa, kbd, NT_DIMS, preferred_element_type=F32)

            cmx_last = cm2[:, lo + ML_L - 1:lo + ML_L]
            b_last = b2[:, lo + ML_L - 1:lo + ML_L]
            mx_r = jnp.maximum(m_row, cmx_last)
            decay = jnp.exp(m_row - mx_r)
            wk = jnp.exp(r2[:, lo:lo + ML_L] - mx_r)
            m_row = b_last + mx_r

            src_e, src_o = (r2, r2r) if cc == 0 else (r2r, r2)
            cols = []
            for p in range(MLSTM_HEADS // 2):
                even = jnp.broadcast_to(src_e[2 * p:2 * p + 1, :], (ML_L, LANES))
                odd = jnp.broadcast_to(src_o[2 * p + 1:2 * p + 2, :], (ML_L, LANES))
                cols.append(jnp.where(lane128 < ML_L, even, odd))
            r_all = jnp.concatenate(cols, axis=1)
            arg = jnp.where(causal, ealpha_ref[rows, :] + r_all, -jnp.inf)
            p_all = (s * jnp.exp(arg)).astype(BF16)

            vbd = jnp.concatenate([va_aug] * MLSTM_HEADS, axis=0) * mask_v
            pv = jnp.dot(p_all, vbd, preferred_element_type=F32)
            qc_state = jnp.dot(qa, caug_ref[...].astype(BF16), preferred_element_type=F32)
            o_ref[0, rows, :] = (ew_ref[rows, :] * qc_state[:, :MLSTM_V_WIDTH]
                                 + pv[:, :MLSTM_V_WIDTH])
            den_ref[rows, :] = (wint_ref[rows, :] * qc_state[:, MLSTM_V_WIDTH:]
                                + pv[:, MLSTM_V_WIDTH:])

            wk_rows = jnp.concatenate(
                [jnp.broadcast_to(wk[h:h + 1, :], (MLSTM_QK_DIM, ML_L)) for h in range(MLSTM_HEADS)],
                axis=0)
            dec_rows = jnp.concatenate(
                [jnp.broadcast_to(decay[h:h + 1, :], (MLSTM_QK_DIM, 1)) for h in range(MLSTM_HEADS)],
                axis=0)
            ktw = (kt2[:, lo:lo + ML_L] * wk_rows).astype(BF16)
            upd = jnp.dot(ktw, va_aug, preferred_element_type=F32)
            caug_ref[...] = dec_rows * caug_ref[...] + jnp.where(mask_c, upd, 0.0)
        return m_row

    lax.fori_loop(0, ML_PAIRS, pair_body, jnp.zeros((MLSTM_HEADS, 1), F32), unroll=ML_PAIR_UNROLL)

    ob_row = lax.broadcasted_iota(jnp.int32, (MLSTM_V_WIDTH, LANES), 0) // MLSTM_V_DIM
    ob_col = lax.broadcasted_iota(jnp.int32, (MLSTM_V_WIDTH, LANES), 1)
    ones_bd = jnp.where(ob_col == ob_row + ML_HL, 1.0, 0.0).astype(BF16)

    def norm_tile(i, carry):
        rows = pl.ds(pl.multiple_of(i * ML_TILE, ML_TILE), ML_TILE)
        num = o_ref[0, rows, :]
        dn = jnp.maximum(jnp.abs(den_ref[rows, :]), floor_ref[rows, :])
        r = 1.0 / dn
        n2_hi, n2_lo = _split_hi_lo(num * num)
        msn = (jnp.dot(n2_hi, ones_bd, preferred_element_type=F32)
               + jnp.dot(n2_lo, ones_bd, preferred_element_type=F32)) * (1.0 / MLSTM_V_DIM)
        fac = r * lax.rsqrt(r * r * msn + NORM_EPS)
        o_ref[0, rows, :] = num * _expand_heads(fac, exp_bf)
        return carry

    lax.fori_loop(0, SEQ // ML_TILE, norm_tile, 0)


def _mlstm(projm3, gt, conv_w):
    b = projm3.shape[0]
    qk_blk = (1, SEQ, MLSTM_QK_WIDTH)
    v_blk = (1, SEQ, MLSTM_V_WIDTH)
    v_col = 2 * MLSTM_QK_WIDTH // MLSTM_V_WIDTH
    vmem = (2 * (2 * SEQ * MLSTM_QK_WIDTH * 4 + 2 * SEQ * MLSTM_V_WIDTH * 4 + 2 * SEQ * LANES * 4)
            + 2 * SEQ * MLSTM_QK_WIDTH * 2 + SEQ * MLSTM_QK_WIDTH * 4 + 2 * SEQ * MLSTM_V_WIDTH * 4
            + 3 * SEQ * LANES * 4 + (8 << 20))
    return pl.pallas_call(
        _mlstm_kernel,
        grid=(b,),
        in_specs=[
            pl.BlockSpec(qk_blk, lambda bi: (bi, 0, 0)),
            pl.BlockSpec(qk_blk, lambda bi: (bi, 0, 1)),
            pl.BlockSpec(v_blk, lambda bi: (bi, 0, v_col)),
            pl.BlockSpec((SUBLANES, SEQ), lambda bi: (1, bi)),
            pl.BlockSpec((SUBLANES, SEQ), lambda bi: (2, bi)),
            pl.BlockSpec((CONV_WIDTH, 2 * MLSTM_QK_WIDTH), lambda bi: (0, 0)),
        ],
        out_specs=pl.BlockSpec(v_blk, lambda bi: (bi, 0, 0)),
        out_shape=jax.ShapeDtypeStruct((b, SEQ, MLSTM_V_WIDTH), F32),
        scratch_shapes=[
            pltpu.VMEM((SEQ, LANES), F32),
            pltpu.VMEM((SEQ, MLSTM_QK_WIDTH), BF16),
            pltpu.VMEM((SEQ, MLSTM_QK_WIDTH), BF16),
            pltpu.VMEM((ML_PAIRS, MLSTM_QK_WIDTH, LANES), F32),
            pltpu.VMEM((ML_PAIRS, SUBLANES, LANES), F32),
            pltpu.VMEM((ML_PAIRS, SUBLANES, LANES), F32),
            pltpu.VMEM((ML_PAIRS, SUBLANES, LANES), F32),
            pltpu.VMEM((SEQ, MLSTM_V_WIDTH), F32),
            pltpu.VMEM((SEQ, MLSTM_V_WIDTH), F32),
            pltpu.VMEM((SEQ, LANES), F32),
            pltpu.VMEM((SEQ, LANES), F32),
            pltpu.VMEM((SEQ, LANES), F32),
            pltpu.VMEM((MLSTM_QK_WIDTH, ML_AUG), F32),
        ],
        compiler_params=pltpu.CompilerParams(
            dimension_semantics=("arbitrary",), vmem_limit_bytes=_vmem_limit(vmem)),
        name="mlstm",
    )(projm3, projm3, projm3, gt, gt, conv_w)


RT_TM = 512
RT_PARTS = 1


def _out_route_kernel(x_ref, fy_ref, my_ref, mo_ref, wo_ref, mg_ref, nw_ref, wr_hi_ref, wr_lo_ref,
                      rb_ref, x1_ref, h2_ref, eidx_ref, gate_ref, rank_ref, cnt_ref, carry_ref):
    i = pl.program_id(0)

    @pl.when(i == 0)
    def _():
        carry_ref[...] = jnp.zeros_like(carry_ref)

    pm = RT_TM // RT_PARTS
    src = lax.broadcasted_iota(jnp.int32, (pm, pm), 0)
    dst = lax.broadcasted_iota(jnp.int32, (pm, pm), 1)
    upper = jnp.where(src < dst, 1.0, 0.0).astype(BF16)
    zi = jnp.zeros((SUBLANES - TOP_K, pm), jnp.int32)

    for part in range(RT_PARTS):
        rows = slice(part * pm, (part + 1) * pm)
        my = my_ref[rows, :] * mg_ref[...] / (1.0 + jnp.exp(-mo_ref[rows, :]))
        mixed = (jnp.dot(fy_ref[rows, :].astype(BF16), wo_ref[:FOX_WIDTH, :], preferred_element_type=F32)
                 + jnp.dot(my.astype(BF16), wo_ref[FOX_WIDTH:, :], preferred_element_type=F32))
        x1 = x_ref[rows, :] + mixed
        x1_ref[rows, :] = x1
        h2 = _rms(x1, nw_ref[...])
        _rows_to_tiles(h2, h2_ref.at[pl.ds(part * pm * TOK_ROWS, pm * TOK_ROWS), :])

        h_hi, h_lo = _split_hi_lo(h2)
        wr_hi, wr_lo = wr_hi_ref[...], wr_lo_ref[...]
        logit = (lax.dot_general(wr_hi, h_hi, NT_DIMS, preferred_element_type=F32)
                 + lax.dot_general(wr_lo, h_hi, NT_DIMS, preferred_element_type=F32)
                 + lax.dot_general(wr_hi, h_lo, NT_DIMS, preferred_element_type=F32)) + rb_ref[...]

        e_iota = lax.broadcasted_iota(jnp.int32, logit.shape, 0).astype(F32)
        vals, idxs, hots = [], [], []
        for _ in range(TOP_K):
            mk = jnp.max(logit, axis=0, keepdims=True)
            idx = jnp.min(jnp.where(logit == mk, e_iota, float(N_EXPERTS)), axis=0, keepdims=True)
            hot = e_iota == idx
            logit = jnp.where(hot, -jnp.inf, logit)
            vals.append(mk)
            idxs.append(idx.astype(jnp.int32))
            hots.append(hot)
        exps = [jnp.exp(v - vals[0]) for v in vals]
        tot = exps[0] + exps[1] + exps[2] + exps[3]
        gates = [e / tot for e in exps]

        assign = jnp.zeros(logit.shape, F32)
        for hot in hots:
            assign = assign + jnp.where(hot, 1.0, 0.0)
        base = jnp.dot(assign.astype(BF16), upper, preferred_element_type=F32) + carry_ref[:, 0:1]
        ranks = [jnp.sum(jnp.where(hot, base, 0.0), axis=0, keepdims=True) for hot in hots]
        carry_ref[...] = carry_ref[...] + jnp.sum(assign, axis=1, keepdims=True)

        eidx_ref[:, rows] = jnp.concatenate(idxs + [zi], axis=0)
        rank_ref[:, rows] = jnp.concatenate([r.astype(jnp.int32) for r in ranks] + [zi], axis=0)
        gate_ref[:, rows] = jnp.concatenate(gates + [zi.astype(F32)], axis=0)

    cnt_ref[...] = carry_ref[...]


def _out_route(x2d, fox_y2d, mlstm_y2d, projm, w_out_bf, mlstm_gain, moe_norm_w, wr_hi, wr_lo, rb):
    n = x2d.shape[0]
    tm = RT_TM
    const = lambda i: (0, 0)
    mo_col = (MLSTM_COLS - MLSTM_V_WIDTH) // MLSTM_V_WIDTH
    row_blk = lambda w: pl.BlockSpec((tm, w), lambda i: (i, 0))
    lane_blk = pl.BlockSpec((SUBLANES, tm), lambda i: (0, i))
    vmem = (2 * (tm * D_MODEL * 4 * 3 + tm * FOX_WIDTH * 4 * 3 + D_MODEL * D_MODEL * 2)
            + 6 * tm * D_MODEL * 4 + tm * tm * 6)
    return pl.pallas_call(
        _out_route_kernel,
        grid=(n // tm,),
        in_specs=[
            row_blk(D_MODEL), row_blk(FOX_WIDTH), row_blk(MLSTM_V_WIDTH),
            pl.BlockSpec((tm, MLSTM_V_WIDTH), lambda i: (i, mo_col)),
            pl.BlockSpec((D_MODEL, D_MODEL), const),
            pl.BlockSpec((1, MLSTM_V_WIDTH), const),
            pl.BlockSpec((1, D_MODEL), const),
            pl.BlockSpec((N_EXPERTS, D_MODEL), const),
            pl.BlockSpec((N_EXPERTS, D_MODEL), const),
            pl.BlockSpec((N_EXPERTS, 1), const),
        ],
        out_specs=[row_blk(D_MODEL), pl.BlockSpec((tm * TOK_ROWS, LANES), lambda i: (i, 0)),
                   lane_blk, lane_blk, lane_blk, pl.BlockSpec((N_EXPERTS, LANES), const)],
        out_shape=[
            jax.ShapeDtypeStruct((n, D_MODEL), F32),
            jax.ShapeDtypeStruct((n * TOK_ROWS, LANES), F32),
            jax.ShapeDtypeStruct((SUBLANES, n), jnp.int32),
            jax.ShapeDtypeStruct((SUBLANES, n), F32),
            jax.ShapeDtypeStruct((SUBLANES, n), jnp.int32),
            jax.ShapeDtypeStruct((N_EXPERTS, LANES), F32),
        ],
        scratch_shapes=[pltpu.VMEM((N_EXPERTS, LANES), F32)],
        compiler_params=pltpu.CompilerParams(
            dimension_semantics=("arbitrary",), vmem_limit_bytes=_vmem_limit(vmem)),
        name="out_route",
    )(x2d, fox_y2d, mlstm_y2d, projm, w_out_bf, mlstm_gain, moe_norm_w, wr_hi, wr_lo, rb)


INV_CHUNK = 8192
INV_UNROLL = 32


def _invert_kernel(pos_ref, zeros_hbm, inv_ref, sem):
    i = pl.program_id(0)

    @pl.when(i == 0)
    def _():
        cp = pltpu.make_async_copy(zeros_hbm, inv_ref, sem.at[0])
        cp.start()
        cp.wait()

    base = i * INV_CHUNK

    def body(j, carry):
        inv_ref[pos_ref[0, 0, j]] = base + j
        return carry

    lax.fori_loop(0, INV_CHUNK, body, 0, unroll=INV_UNROLL)


def _invert(pos_flat, n_rows):
    n_slots = pos_flat.shape[0]
    steps = n_slots // INV_CHUNK
    return pl.pallas_call(
        _invert_kernel,
        grid=(steps,),
        in_specs=[
            pl.BlockSpec((1, 1, INV_CHUNK), lambda i: (i, 0, 0), memory_space=pltpu.SMEM),
            pl.BlockSpec(memory_space=pl.ANY),
        ],
        out_specs=pl.BlockSpec(memory_space=pltpu.SMEM),
        out_shape=jax.ShapeDtypeStruct((n_rows,), jnp.int32),
        scratch_shapes=[pltpu.SemaphoreType.DMA((1,))],
        compiler_params=pltpu.CompilerParams(dimension_semantics=("arbitrary",)),
        name="invert",
    )(pos_flat.reshape(steps, 1, INV_CHUNK), jnp.zeros((n_rows,), jnp.int32))


EX_BM = 256
EX_DRAIN_STEPS = 2


def _experts_kernel(be_ref, nu_ref, nv_ref, tok_ref, tokn_ref, dst_ref, h2_hbm, wgu_ref, bgu_ref,
                    wd_ref, bd_ref, y_hbm, xt_ref, yt_ref, wgu_bf_ref, wd_bf_ref, gsem, ssem):
    i = pl.program_id(0)
    last_blk = pl.num_programs(0) - 1 - EX_DRAIN_STEPS
    nu = nu_ref[0]
    slot = i % 2
    cur = jnp.minimum(i, last_blk)
    tile_rows = EX_BM * TOK_ROWS

    def tok_tile(ref, idx):
        return ref.at[pl.ds(pl.multiple_of(idx, TOK_ROWS), TOK_ROWS), :]

    def start_gather(idx_ref, s, n):
        @pl.when(n == EX_BM)
        def _():
            for r in range(EX_BM):
                pltpu.make_async_copy(tok_tile(h2_hbm, idx_ref[0, 0, r]),
                                      xt_ref.at[s, pl.ds(r * TOK_ROWS, TOK_ROWS), :], gsem.at[s]).start()

        @pl.when(n < EX_BM)
        def _():
            def body(r, carry):
                pltpu.make_async_copy(
                    tok_tile(h2_hbm, idx_ref[0, 0, r]),
                    xt_ref.at[s, pl.ds(pl.multiple_of(r * TOK_ROWS, TOK_ROWS), TOK_ROWS), :],
                    gsem.at[s]).start()
                return carry
            lax.fori_loop(0, n, body, 0)

    def scatter_row(r):
        pltpu.make_async_copy(yt_ref.at[slot, pl.ds(r * TOK_ROWS, TOK_ROWS), :],
                              tok_tile(y_hbm, dst_ref[0, 0, r]), ssem.at[slot]).start()

    def wait_block(sem, buf):
        pltpu.make_async_copy(h2_hbm.at[pl.ds(0, tile_rows), :], buf, sem).wait()

    def wait_tokens(sem, buf, n):
        @pl.when(n == EX_BM)
        def _():
            wait_block(sem, buf)

        @pl.when(n < EX_BM)
        def _():
            def body(r, carry):
                pltpu.make_async_copy(h2_hbm.at[pl.ds(0, TOK_ROWS), :],
                                      buf.at[pl.ds(0, TOK_ROWS), :], sem).wait()
                return carry
            lax.fori_loop(0, n, body, 0)

    @pl.when(i == 0)
    def _():
        xt_ref[...] = jnp.zeros_like(xt_ref)
        start_gather(tok_ref, 0, nv_ref[0])

    @pl.when(i + 1 < nu)
    def _():
        start_gather(tokn_ref, 1 - slot, nv_ref[jnp.minimum(i + 1, last_blk)])

    @pl.when((i >= 2) & (i - 2 < nu))
    def _():
        wait_tokens(ssem.at[slot], yt_ref.at[slot], nv_ref[jnp.clip(i - 2, 0, last_blk)])

    @pl.when(i < nu)
    def _():
        @pl.when((i == 0) | (be_ref[cur] != be_ref[jnp.maximum(cur - 1, 0)]))
        def _():
            wgu_bf_ref[...] = wgu_ref[0].astype(BF16)
            wd_bf_ref[...] = wd_ref[0].astype(BF16)

        wait_tokens(gsem.at[slot], xt_ref.at[slot], nv_ref[cur])
        xb = _tiles_to_rows(xt_ref.at[slot], EX_BM).astype(BF16)
        gu = jnp.dot(xb, wgu_bf_ref[...], preferred_element_type=F32) + bgu_ref[0]
        gate = jnp.minimum(gu[:, :D_EXPERT], SWIGLU_LIMIT)
        up = jnp.clip(gu[:, D_EXPERT:], -SWIGLU_LIMIT, SWIGLU_LIMIT)
        act = (up + 1.0) * (gate / (1.0 + jnp.exp(-SWIGLU_ALPHA * gate)))
        y = jnp.dot(act.astype(BF16), wd_bf_ref[...], preferred_element_type=F32) + bd_ref[0]
        _rows_to_tiles(y, yt_ref.at[slot])
        nv = nv_ref[cur]

        @pl.when(nv == EX_BM)
        def _():
            for r in range(EX_BM):
                scatter_row(r)

        @pl.when(nv < EX_BM)
        def _():
            def body(r, carry):
                pltpu.make_async_copy(
                    yt_ref.at[slot, pl.ds(pl.multiple_of(r * TOK_ROWS, TOK_ROWS), TOK_ROWS), :],
                    tok_tile(y_hbm, dst_ref[0, 0, r]), ssem.at[slot]).start()
                return carry
            lax.fori_loop(0, nv, body, 0)


def _experts(block_e, n_used, n_valid, buf_tok3, dst3, h2, w_gu, b_gu, w_down, b_down, n_slots):
    nb = buf_tok3.shape[0]
    idx_blk = lambda f: pl.BlockSpec((1, 1, EX_BM), f, memory_space=pltpu.SMEM)
    vmem = (2 * (D_MODEL * 2 * D_EXPERT * 4 + D_EXPERT * D_MODEL * 4)
            + D_MODEL * 2 * D_EXPERT * 2 + D_EXPERT * D_MODEL * 2
            + 3 * EX_BM * D_MODEL * 4 + 3 * EX_BM * 2 * D_EXPERT * 4)
    blk = lambda i: jnp.minimum(i, nb - 1)
    w_map = lambda i, be, nu, nv: (be[blk(i)], 0, 0)
    grid_spec = pltpu.PrefetchScalarGridSpec(
        num_scalar_prefetch=3,
        grid=(nb + EX_DRAIN_STEPS,),
        in_specs=[
            idx_blk(lambda i, be, nu, nv: (blk(i), 0, 0)),
            idx_blk(lambda i, be, nu, nv: (blk(i + 1), 0, 0)),
            idx_blk(lambda i, be, nu, nv: (blk(i), 0, 0)),
            pl.BlockSpec(memory_space=pl.ANY),
            pl.BlockSpec((1, D_MODEL, 2 * D_EXPERT), w_map),
            pl.BlockSpec((1, 1, 2 * D_EXPERT), w_map),
            pl.BlockSpec((1, D_EXPERT, D_MODEL), w_map),
            pl.BlockSpec((1, 1, D_MODEL), w_map),
        ],
        out_specs=pl.BlockSpec(memory_space=pl.ANY),
        scratch_shapes=[
            pltpu.VMEM((2, EX_BM * TOK_ROWS, LANES), F32),
            pltpu.VMEM((2, EX_BM * TOK_ROWS, LANES), F32),
            pltpu.VMEM((D_MODEL, 2 * D_EXPERT), BF16),
            pltpu.VMEM((D_EXPERT, D_MODEL), BF16),
            pltpu.SemaphoreType.DMA((2,)),
            pltpu.SemaphoreType.DMA((2,)),
        ],
    )
    return pl.pallas_call(
        _experts_kernel,
        grid_spec=grid_spec,
        out_shape=jax.ShapeDtypeStruct((n_slots * TOK_ROWS, LANES), F32),
        compiler_params=pltpu.CompilerParams(
            dimension_semantics=("arbitrary",), vmem_limit_bytes=_vmem_limit(vmem)),
        name="experts",
    )(block_e, n_used, n_valid, buf_tok3, buf_tok3, dst3, h2, w_gu, b_gu[:, None, :], w_down,
      b_down[:, None, :])


CB_TM = 512


def _combine_kernel(x1_ref, y0_ref, y1_ref, y2_ref, y3_ref, gate_ref, o_ref):
    tm = x1_ref.shape[0]
    g = jnp.concatenate([gate_ref[...], jnp.zeros((LANES - SUBLANES, tm), F32)], axis=0).T
    acc = x1_ref[...]
    for k, y_ref in enumerate((y0_ref, y1_ref, y2_ref, y3_ref)):
        acc = acc + g[:, k:k + 1] * _tiles_to_rows(y_ref, tm)
    o_ref[...] = acc


def _combine(x1, y_slots, gates):
    n = x1.shape[0]
    tm = CB_TM
    nt = n // tm
    vmem = 2 * (2 * tm * D_MODEL * 4 + tm * TOP_K * D_MODEL * 4) + 4 * tm * D_MODEL * 4
    y_spec = lambda k: pl.BlockSpec((tm * TOK_ROWS, LANES), lambda i: (k * nt + i, 0))
    return pl.pallas_call(
        _combine_kernel,
        grid=(nt,),
        in_specs=[pl.BlockSpec((tm, D_MODEL), lambda i: (i, 0))]
        + [y_spec(k) for k in range(TOP_K)]
        + [pl.BlockSpec((SUBLANES, tm), lambda i: (0, i))],
        out_specs=pl.BlockSpec((tm, D_MODEL), lambda i: (i, 0)),
        out_shape=jax.ShapeDtypeStruct((n, D_MODEL), F32),
        compiler_params=pltpu.CompilerParams(
            dimension_semantics=("arbitrary",), vmem_limit_bytes=_vmem_limit(vmem)),
        name="combine",
    )(x1, y_slots, y_slots, y_slots, y_slots, gates)


def _dispatch_plan(eidx, rank, counts):
    n = eidx.shape[1]
    n_slots = n * TOP_K
    nb = n_slots // EX_BM + N_EXPERTS
    counts = counts.astype(jnp.int32)
    padded = ((counts + EX_BM - 1) // EX_BM) * EX_BM
    pad_end = jnp.cumsum(padded)
    pad_start = pad_end - padded
    e = eidx[:TOP_K]
    start_of = jnp.sum(jnp.where(e[:, :, None] == jnp.arange(N_EXPERTS, dtype=jnp.int32),
                                 pad_start[None, None, :], 0), axis=-1)
    pos = start_of + rank[:TOP_K]
    inv = _invert(pos.reshape(-1), nb * EX_BM)
    buf_tok = (inv % n) * TOK_ROWS
    dst = inv * TOK_ROWS
    blk_start = jnp.arange(nb, dtype=jnp.int32) * EX_BM
    block_e = jnp.minimum(jnp.sum((pad_end[None, :] <= blk_start[:, None]).astype(jnp.int32), axis=1),
                          N_EXPERTS - 1)
    n_used = (pad_end[-1] // EX_BM).astype(jnp.int32).reshape(1)
    n_valid = jnp.clip(pad_start[block_e] + counts[block_e] - blk_start, 0, EX_BM).astype(jnp.int32)
    n_valid = jnp.where(blk_start < pad_end[-1], n_valid, 0)
    return (block_e, n_used, n_valid, buf_tok.reshape(nb, 1, EX_BM), dst.reshape(nb, 1, EX_BM), n_slots)


def _prep_in_proj_weights(w_in, fox_f_bias, mlstm_i_bias, mlstm_f_bias):
    split_at = []
    acc = 0
    for wdt in SPLIT_WIDTHS[:-1]:
        acc += wdt
        split_at.append(acc)
    fq, fk, fv, ff, mq, mk, mv, mi, mf, mo = jnp.split(w_in, split_at, axis=-1)
    w_main = jnp.concatenate([fq, fk, fv, mq, mk, mv, mo], axis=-1).astype(BF16)
    w_gate = jnp.concatenate([ff, mi, mf], axis=-1)
    bias = jnp.concatenate([fox_f_bias, mlstm_i_bias, mlstm_f_bias]).astype(F32)
    n_gate = w_gate.shape[1]
    wg_hi, wg_lo = _split_hi_lo(jnp.pad(w_gate, ((0, 0), (0, LANES - n_gate))))
    bias_r = jnp.pad(bias, (0, LANES - n_gate))[None, :]
    return w_main, wg_hi, wg_lo, bias_r


def kernel(x, attn_norm_w, w_in, fox_f_bias, fox_q_norm_w, fox_k_norm_w, fox_out_norm_w, mlstm_conv_w, mlstm_i_bias, mlstm_f_bias, mlstm_out_norm_w, w_out, moe_norm_w, router_w, router_b, expert_w_gate_up, expert_b_gate_up, expert_w_down, expert_b_down):
    bsz, seq, d = x.shape
    x2d = x.reshape(bsz * seq, d)
    prep = _prep_in_proj_weights(w_in[0], fox_f_bias[0], mlstm_i_bias[0], mlstm_f_bias[0])
    pair = lambda w: jnp.tile(w, LANES // HEAD_DIM)[None, :]
    qn, kn, aq, ak, vb, projm, gt = _in_proj(x2d, attn_norm_w[0][None, :], *prep,
                                             pair(fox_q_norm_w[0]), pair(fox_k_norm_w[0]))
    b3 = lambda a: a.reshape(bsz, seq, a.shape[-1])
    fox_y = _fox(b3(qn), b3(kn), b3(aq), b3(ak), b3(vb), fox_out_norm_w[0][None, :])
    mlstm_y = _mlstm(b3(projm), gt, mlstm_conv_w[0])
    return _channel_mixer(x2d, fox_y.reshape(-1, FOX_WIDTH), mlstm_y.reshape(-1, MLSTM_V_WIDTH), projm,
                          mlstm_out_norm_w[0], w_out[0], moe_norm_w[0], router_w[0], router_b[0],
                          expert_w_gate_up[0], expert_b_gate_up[0], expert_w_down[0],
                          expert_b_down[0]).reshape(bsz, seq, d)


def _channel_mixer(x2d, fox_y2d, mlstm_y2d, projm, mlstm_gain, w_out, moe_norm_w, router_w, router_b,
                   w_gu, b_gu, w_down, b_down):
    wr_hi, wr_lo = _split_hi_lo(router_w.T)
    x1, h2, eidx, gates, rank, counts = _out_route(
        x2d, fox_y2d, mlstm_y2d, projm, w_out.astype(BF16), mlstm_gain[None, :], moe_norm_w[None, :],
        wr_hi, wr_lo, router_b[:, None])
    block_e, n_used, n_valid, buf_tok3, dst3, n_slots = _dispatch_plan(eidx, rank, counts[:, 0])
    y_slots = _experts(block_e, n_used, n_valid, buf_tok3, dst3, h2, w_gu, b_gu, w_down, b_down, n_slots)
    return _combine(x1, y_slots, gates)
```

```python
import math

import jax
import jax.numpy as jnp
from jax import lax
from jax.experimental import pallas as pl
from jax.experimental.pallas import tpu as pltpu

F32 = jnp.float32
BF16 = jnp.bfloat16

D_MODEL = 1024
SEQ = 2048
HEAD_DIM = 64
FOX_HEADS = 8
FOX_WIDTH = FOX_HEADS * HEAD_DIM
MLSTM_HEADS = 8
MLSTM_QK_DIM = 32
MLSTM_V_DIM = 64
MLSTM_QK_WIDTH = MLSTM_HEADS * MLSTM_QK_DIM
MLSTM_V_WIDTH = MLSTM_HEADS * MLSTM_V_DIM
CONV_WIDTH = 4
MLSTM_CHUNK = 64
SPLIT_WIDTHS = (FOX_WIDTH, FOX_WIDTH, FOX_WIDTH, FOX_HEADS,
                MLSTM_QK_WIDTH, MLSTM_QK_WIDTH, MLSTM_V_WIDTH,
                MLSTM_HEADS, MLSTM_HEADS, MLSTM_V_WIDTH)
N_EXPERTS = 32
TOP_K = 4
D_EXPERT = D_MODEL
SWIGLU_ALPHA = 1.702
SWIGLU_LIMIT = 7.0
NORM_EPS = 1e-5
LOG2E = 1.4426950408889634

LANES = 128
SUBLANES = 8
V7X_VMEM_BYTES = 64 * 1024 * 1024

MAIN_WIDTH = 3 * FOX_WIDTH + 2 * MLSTM_QK_WIDTH + 2 * MLSTM_V_WIDTH
MLSTM_COLS = MAIN_WIDTH - 3 * FOX_WIDTH
GATE_ROWS = 32

NT_DIMS = (((1,), (1,)), ((), ()))


VMEM_TEMPORARIES_BYTES = 8 << 20
VMEM_RESERVED_BYTES = 4 << 20


def _vmem_limit(block_bytes):
    return int(min(block_bytes + VMEM_TEMPORARIES_BYTES, V7X_VMEM_BYTES - VMEM_RESERVED_BYTES))


def _log_sigmoid(x):
    return jnp.minimum(x, 0.0) - jnp.log(1.0 + jnp.exp(-jnp.abs(x)))


def _split_hi_lo(x):
    hi = x.astype(BF16)
    lo = (x - hi.astype(F32)).astype(BF16)
    return hi, lo


def _rms(x, w):
    return x * lax.rsqrt(jnp.mean(x * x, axis=-1, keepdims=True) + NORM_EPS) * w


TOK_ROWS = D_MODEL // LANES


def _rows_to_tiles(x, tile_ref):
    m = x.shape[0]
    for j in range(TOK_ROWS):
        tile_ref[pl.ds(j, m, stride=TOK_ROWS), :] = x[:, j * LANES:(j + 1) * LANES]


def _tiles_to_rows(tile_ref, m):
    return jnp.concatenate(
        [tile_ref[pl.ds(j, m, stride=TOK_ROWS), :] for j in range(TOK_ROWS)], axis=1)


IN_TM = 1024
IN_TILES_PER_SEQ = SEQ // IN_TM
IN_PARTS = 2
IN_PM = IN_TM // IN_PARTS
IN_SCAN_SHIFTS = tuple(1 << i for i in range(int(math.log2(IN_PM))))


def _pair_rms(x, w, lo_half):
    sq = x * x
    ms_lo = jnp.sum(jnp.where(lo_half, sq, 0.0), axis=-1, keepdims=True) * (1.0 / HEAD_DIM)
    ms_hi = jnp.sum(jnp.where(lo_half, 0.0, sq), axis=-1, keepdims=True) * (1.0 / HEAD_DIM)
    inv = jnp.where(lo_half, lax.rsqrt(ms_lo + NORM_EPS), lax.rsqrt(ms_hi + NORM_EPS))
    return x * inv * w


def _in_proj_kernel(x_ref, nw_ref, w_ref, wg_hi_ref, wg_lo_ref, br_ref, qw_ref, kw_ref,
                    qn_ref, kn_ref, aq_ref, ak_ref, vb_ref, projm_ref, gt_ref, carry_ref):
    i = pl.program_id(0)

    @pl.when(i % IN_TILES_PER_SEQ == 0)
    def _():
        carry_ref[...] = jnp.zeros_like(carry_ref)

    for part in range(IN_PARTS):
        rows = slice(part * IN_PM, (part + 1) * IN_PM)
        x = x_ref[rows, :]
        ms = jnp.mean(x * x, axis=-1, keepdims=True)
        y = x * lax.rsqrt(ms + NORM_EPS) * nw_ref[...]
        h_hi, h_lo = _split_hi_lo(y)
        main = jnp.dot(h_hi, w_ref[...], preferred_element_type=F32)

        w_hi, w_lo = wg_hi_ref[...], wg_lo_ref[...]
        g = (jnp.dot(h_hi, w_hi, preferred_element_type=F32)
             + jnp.dot(h_hi, w_lo, preferred_element_type=F32)
             + jnp.dot(h_lo, w_hi, preferred_element_type=F32))
        g = g + br_ref[...]
        lane = lax.broadcasted_iota(jnp.int32, g.shape, 1)
        is_input_gate = (lane >= FOX_HEADS) & (lane < FOX_HEADS + MLSTM_HEADS)
        gates = jnp.where(is_input_gate, g, _log_sigmoid(g))
        gt_ref[:, rows] = gates.T[:GATE_ROWS, :]

        rowi = lax.broadcasted_iota(jnp.int32, gates.shape, 0)
        c = gates
        for s in IN_SCAN_SHIFTS:
            c = c + jnp.where(rowi >= s, pltpu.roll(c, s, axis=0), 0.0)
        c = c + carry_ref[...]
        carry_ref[...] = c[IN_PM - 1:IN_PM, :]
        cum2 = c * LOG2E

        c8 = jnp.where(lane < FOX_HEADS, cum2, 0.0)
        c_hi = c8.astype(BF16).astype(F32)
        r1 = c8 - c_hi
        c_mid = r1.astype(BF16).astype(F32)
        c_lo = (r1 - c_mid).astype(BF16).astype(F32)
        ones_q = jnp.where((lane >= 3 * FOX_HEADS) & (lane < 6 * FOX_HEADS), 1.0, 0.0)
        ones_k = jnp.where(lane < 3 * FOX_HEADS, 1.0, 0.0)
        aq = (c_hi + pltpu.roll(c_mid, FOX_HEADS, axis=1) + pltpu.roll(c_lo, 2 * FOX_HEADS, axis=1)
              + ones_q)
        ak = ones_k - (pltpu.roll(c_hi, 3 * FOX_HEADS, axis=1) + pltpu.roll(c_mid, 4 * FOX_HEADS, axis=1)
                       + pltpu.roll(c_lo, 5 * FOX_HEADS, axis=1))
        aq_ref[rows, :] = aq.astype(BF16)
        ak_ref[rows, :] = ak.astype(BF16)

        lo_half = lane < HEAD_DIM
        q_scale = (HEAD_DIM ** -0.5) * LOG2E
        for p in range(FOX_HEADS // 2):
            ps = slice(p * LANES, (p + 1) * LANES)
            qn_ref[rows, ps] = (_pair_rms(main[:, ps], qw_ref[...], lo_half) * q_scale).astype(BF16)
            kn_ref[rows, ps] = _pair_rms(main[:, FOX_WIDTH + p * LANES:FOX_WIDTH + (p + 1) * LANES],
                                         kw_ref[...], lo_half).astype(BF16)

        vb_ref[rows, :] = main[:, 2 * FOX_WIDTH:3 * FOX_WIDTH].astype(BF16)
        projm_ref[rows, :] = main[:, 3 * FOX_WIDTH:]


def _in_proj(x2d, norm_w, w_main, wg_hi, wg_lo, bias_r, qw, kw):
    n = x2d.shape[0]
    tm = IN_TM
    const = lambda i: (0, 0)
    row = lambda w: pl.BlockSpec((tm, w), lambda i: (i, 0))
    vmem = (2 * (tm * D_MODEL * 4 + D_MODEL * MAIN_WIDTH * 2 + 3 * tm * FOX_WIDTH * 2 + 2 * tm * LANES * 2
                 + tm * MLSTM_COLS * 4 + GATE_ROWS * tm * 4 + tm * LANES * 4)
            + 2 * tm * MAIN_WIDTH * 4)
    return pl.pallas_call(
        _in_proj_kernel,
        grid=(n // tm,),
        in_specs=[
            row(D_MODEL),
            pl.BlockSpec((1, D_MODEL), const),
            pl.BlockSpec((D_MODEL, MAIN_WIDTH), const),
            pl.BlockSpec((D_MODEL, LANES), const),
            pl.BlockSpec((D_MODEL, LANES), const),
            pl.BlockSpec((1, LANES), const),
            pl.BlockSpec((1, LANES), const),
            pl.BlockSpec((1, LANES), const),
        ],
        out_specs=[
            row(FOX_WIDTH), row(FOX_WIDTH), row(LANES), row(LANES), row(FOX_WIDTH), row(MLSTM_COLS),
            pl.BlockSpec((GATE_ROWS, tm), lambda i: (0, i)),
        ],
        out_shape=[
            jax.ShapeDtypeStruct((n, FOX_WIDTH), BF16),
            jax.ShapeDtypeStruct((n, FOX_WIDTH), BF16),
            jax.ShapeDtypeStruct((n, LANES), BF16),
            jax.ShapeDtypeStruct((n, LANES), BF16),
            jax.ShapeDtypeStruct((n, FOX_WIDTH), BF16),
            jax.ShapeDtypeStruct((n, MLSTM_COLS), F32),
            jax.ShapeDtypeStruct((GATE_ROWS, n), F32),
        ],
        scratch_shapes=[pltpu.VMEM((1, LANES), F32)],
        compiler_params=pltpu.CompilerParams(
            dimension_semantics=("arbitrary",), vmem_limit_bytes=_vmem_limit(vmem)),
        name="in_proj",
    )(x2d, norm_w, w_main, wg_hi, wg_lo, bias_r, qw, kw)


FOX_TQ = 512
FOX_PAIRS = 4


def _fox_kernel(qn_ref, kn_ref, aq_ref, ak_ref, v_ref, ow_ref, o_ref, qa_ref, ka_ref, va_ref):
    step = pl.program_id(1)
    tri = (lax.broadcasted_iota(jnp.int32, (FOX_TQ, FOX_TQ), 1)
           <= lax.broadcasted_iota(jnp.int32, (FOX_TQ, FOX_TQ), 0))
    lo_q = lax.broadcasted_iota(jnp.int32, (FOX_TQ, LANES), 1) < HEAD_DIM
    lane = lax.broadcasted_iota(jnp.int32, (1, LANES), 1)
    sum_lanes = (HEAD_DIM, 0)
    for pp in range(FOX_PAIRS):
        ps = slice(pp * LANES, (pp + 1) * LANES)
        ka_ref[pp, :, :LANES] = kn_ref[0, :, ps]
        ka_ref[pp, :, LANES:] = ak_ref[0]
        for j in range(2):
            h = 2 * (step * FOX_PAIRS + pp) + j
            own_mask = jnp.where((lane < HEAD_DIM) if j == 0 else (lane >= HEAD_DIM), 1.0, 0.0).astype(BF16)
            bias_mask = jnp.where((lane < 6 * FOX_HEADS) & (lane % FOX_HEADS == h), 1.0, 0.0).astype(BF16)
            qa_ref[2 * pp + j, :, :LANES] = qn_ref[0, :, ps] * own_mask
            qa_ref[2 * pp + j, :, LANES:] = aq_ref[0] * bias_mask
            va_ref[2 * pp + j] = (v_ref[0, :, ps] * own_mask
                                  + jnp.where(lane == sum_lanes[j], 1.0, 0.0).astype(BF16))

    for pp in range(FOX_PAIRS):
        ps = slice(pp * LANES, (pp + 1) * LANES)
        for i in range(SEQ // FOX_TQ):
            qs = slice(i * FOX_TQ, (i + 1) * FOX_TQ)
            n = (i + 1) * FOX_TQ
            normed = []
            for j in range(2):
                own_q = lo_q if j == 0 else jnp.logical_not(lo_q)
                s = lax.dot_general(qa_ref[2 * pp + j, qs, :], ka_ref[pp, :n, :], NT_DIMS,
                                    preferred_element_type=F32)
                diag = jnp.where(tri, s[:, n - FOX_TQ:], -jnp.inf)
                m = jnp.max(diag, axis=-1, keepdims=True)
                if i > 0:
                    past = s[:, :n - FOX_TQ]
                    m = jnp.maximum(m, jnp.max(past, axis=-1, keepdims=True))
                    p = jnp.concatenate([jnp.exp2(past - m), jnp.exp2(diag - m)], axis=1)
                else:
                    p = jnp.exp2(diag - m)
                o = jnp.dot(p.astype(BF16), va_ref[2 * pp + j, :n, :],
                            preferred_element_type=F32)
                o = o / o[:, sum_lanes[j]:sum_lanes[j] + 1]
                ms = jnp.sum(jnp.where(own_q, o * o, 0.0), axis=-1, keepdims=True) * (1.0 / HEAD_DIM)
                normed.append(o * lax.rsqrt(ms + NORM_EPS))
            o_ref[0, qs, ps] = jnp.where(lo_q, normed[0], normed[1]) * ow_ref[:, ps]


def _fox(qn3, kn3, aq3, ak3, v3, ow):
    b = qn3.shape[0]
    width = FOX_PAIRS * LANES
    steps = FOX_WIDTH // width
    pairs = pl.BlockSpec((1, SEQ, width), lambda bi, st: (bi, 0, st))
    shared = pl.BlockSpec((1, SEQ, LANES), lambda bi, st: (bi, 0, 0))
    vmem = (2 * (3 * SEQ * width * 2 + 2 * SEQ * LANES * 2 + SEQ * width * 4)
            + FOX_PAIRS * SEQ * LANES * 2 * 8 + 8 * FOX_TQ * SEQ * 4)
    return pl.pallas_call(
        _fox_kernel,
        grid=(b, steps),
        in_specs=[pairs, pairs, shared, shared, pairs, pl.BlockSpec((1, width), lambda bi, st: (0, st))],
        out_specs=pairs,
        out_shape=jax.ShapeDtypeStruct((b, SEQ, FOX_WIDTH), F32),
        scratch_shapes=[pltpu.VMEM((2 * FOX_PAIRS, SEQ, 2 * LANES), BF16),
                        pltpu.VMEM((FOX_PAIRS, SEQ, 2 * LANES), BF16),
                        pltpu.VMEM((2 * FOX_PAIRS, SEQ, LANES), BF16)],
        compiler_params=pltpu.CompilerParams(
            dimension_semantics=("arbitrary", "arbitrary"), vmem_limit_bytes=_vmem_limit(vmem)),
        name="fox",
    )(qn3, kn3, aq3, ak3, v3, ow)


ML_L = MLSTM_CHUNK
ML_PAIRS = SEQ // (2 * ML_L)
ML_HL = FOX_HEADS
ML_AUG = MLSTM_V_WIDTH + LANES
ML_TILE = 256
ML_PAIR_UNROLL = 4
SEG_SHIFTS = tuple(1 << i for i in range(int(math.log2(ML_L))))


def _split3(x):
    a = x.astype(BF16)
    r = x - a.astype(F32)
    b = r.astype(BF16)
    c = (r - b.astype(F32)).astype(BF16)
    return a, b, c


def _expand_heads(x, exp_bf):
    a, b, c = _split3(x)
    return (jnp.dot(a, exp_bf, preferred_element_type=F32)
            + jnp.dot(b, exp_bf, preferred_element_type=F32)
            + jnp.dot(c, exp_bf, preferred_element_type=F32))


def _seg_scan(x, axis, op, ident):
    idx = lax.broadcasted_iota(jnp.int32, x.shape, axis) % ML_L
    for s in SEG_SHIFTS:
        x = op(x, jnp.where(idx >= s, pltpu.roll(x, s, axis=axis), ident))
    return x


def _mlstm_kernel(q_ref, k_ref, v_ref, gi_ref, gf_ref, cw_ref, o_ref, den_ref,
                  qc_ref, kc_ref, kt_ref, rr_ref, cmr_ref, bcr_ref,
                  ealpha_ref, ew_ref, wint_ref, floor_ref, mfull_ref, caug_ref):
    def conv_silu(u, w):
        rowi = lax.broadcasted_iota(jnp.int32, u.shape, 0)
        acc = u * w[CONV_WIDTH - 1:CONV_WIDTH, :]
        for d in range(1, CONV_WIDTH):
            sh = jnp.where(rowi >= d, pltpu.roll(u, d, axis=0), 0.0)
            acc = acc + sh * w[CONV_WIDTH - 1 - d:CONV_WIDTH - d, :]
        return acc / (1.0 + jnp.exp(-acc))

    cw = cw_ref[...]
    qc_ref[...] = conv_silu(q_ref[0], cw[:, :MLSTM_QK_WIDTH]).astype(BF16)
    kc = conv_silu(k_ref[0], cw[:, MLSTM_QK_WIDTH:]) * (MLSTM_QK_DIM ** -0.5)
    kc_ref[...] = kc.astype(BF16)
    kt = kc.T
    for p in range(ML_PAIRS):
        kt_ref[p] = kt[:, p * LANES:(p + 1) * LANES]

    bcum_r = _seg_scan(gf_ref[...], 1, jnp.add, 0.0)
    r_r = gi_ref[...] - bcum_r
    cmx_r = _seg_scan(r_r, 1, jnp.maximum, -jnp.inf)
    for p in range(ML_PAIRS):
        ls = slice(p * LANES, (p + 1) * LANES)
        rr_ref[p] = r_r[:, ls]
        cmr_ref[p] = cmx_r[:, ls]
        bcr_ref[p] = bcum_r[:, ls]

    def to_columns(rows8):
        pad_lo = jnp.zeros((ML_HL, SEQ), F32)
        pad_hi = jnp.zeros((LANES - ML_HL - MLSTM_HEADS, SEQ), F32)
        return jnp.concatenate([pad_lo, rows8, pad_hi], axis=0).T

    bcum_c = to_columns(bcum_r)
    cmx_c = to_columns(cmx_r)
    m = jnp.zeros((1, LANES), F32)
    for c in range(SEQ // ML_L):
        mfull_ref[c * ML_L:(c + 1) * ML_L, :] = jnp.broadcast_to(m, (ML_L, LANES))
        last = (c + 1) * ML_L - 1
        m = bcum_c[last:last + 1, :] + jnp.maximum(m, cmx_c[last:last + 1, :])
    mfull = mfull_ref[...]
    mx = jnp.maximum(mfull, cmx_c)
    wint_ref[...] = jnp.exp(mfull - mx)
    floor_ref[...] = jnp.exp(-(bcum_c + mx))
    mfull_ref[...] = -mx

    lane_e = lax.broadcasted_iota(jnp.int32, (LANES, MLSTM_V_WIDTH), 1) // MLSTM_V_DIM
    row_e = lax.broadcasted_iota(jnp.int32, (LANES, MLSTM_V_WIDTH), 0)
    exp_bf = jnp.where(row_e == lane_e + ML_HL, 1.0, 0.0).astype(BF16)

    def expand_tile(i, carry):
        rows = pl.ds(pl.multiple_of(i * ML_TILE, ML_TILE), ML_TILE)
        ealpha_ref[rows, :] = _expand_heads(mfull_ref[rows, :], exp_bf)
        ew_ref[rows, :] = _expand_heads(wint_ref[rows, :], exp_bf)
        return carry

    lax.fori_loop(0, SEQ // ML_TILE, expand_tile, 0)

    kb_rowh = lax.broadcasted_iota(jnp.int32, (MLSTM_HEADS * ML_L, MLSTM_QK_WIDTH), 0) // ML_L
    kb_lane = lax.broadcasted_iota(jnp.int32, (MLSTM_HEADS * ML_L, MLSTM_QK_WIDTH), 1) // MLSTM_QK_DIM
    mask_k = jnp.where(kb_rowh == kb_lane, 1.0, 0.0).astype(BF16)
    va_rowh = lax.broadcasted_iota(jnp.int32, (MLSTM_HEADS * ML_L, ML_AUG), 0) // ML_L
    va_col = lax.broadcasted_iota(jnp.int32, (MLSTM_HEADS * ML_L, ML_AUG), 1)
    mask_v = jnp.where(
        (va_col // MLSTM_V_DIM == va_rowh) | (va_col == MLSTM_V_WIDTH + ML_HL + va_rowh),
        1.0, 0.0).astype(BF16)
    c_rowh = lax.broadcasted_iota(jnp.int32, (MLSTM_QK_WIDTH, ML_AUG), 0) // MLSTM_QK_DIM
    c_col = lax.broadcasted_iota(jnp.int32, (MLSTM_QK_WIDTH, ML_AUG), 1)
    mask_c = (c_col // MLSTM_V_DIM == c_rowh) | (c_col == MLSTM_V_WIDTH + ML_HL + c_rowh)
    lane128 = lax.broadcasted_iota(jnp.int32, (ML_L, LANES), 1)
    s_idx = lax.broadcasted_iota(jnp.int32, (ML_L, MLSTM_V_WIDTH), 1) % ML_L
    t_idx = lax.broadcasted_iota(jnp.int32, (ML_L, MLSTM_V_WIDTH), 0)
    causal = s_idx <= t_idx
    ones_aug = jnp.ones((ML_L, LANES), F32)

    caug_ref[...] = jnp.zeros_like(caug_ref)

    def pair_body(cp, m_row):
        r2 = rr_ref[cp]
        cm2 = cmr_ref[cp]
        b2 = bcr_ref[cp]
        r2r = pltpu.roll(r2, ML_L, axis=1)
        kt2 = kt_ref[cp]
        for cc in range(2):
            lo = cc * ML_L
            rows = pl.ds(pl.multiple_of(cp * (2 * ML_L), 2 * ML_L) + lo, ML_L)
            qa = qc_ref[rows, :]
            ka = kc_ref[rows, :]
            va_aug = jnp.concatenate([v_ref[0, rows, :], ones_aug], axis=1).astype(BF16)

            kbd = jnp.concatenate([ka] * MLSTM_HEADS, axis=0) * mask_k
            s = lax.dot_general(qa, kbd, NT_DIMS, preferred_element_type=F32)

            cmx_last = cm2[:, lo + ML_L - 1:lo + ML_L]
            b_last = b2[:, lo + ML_L - 1:lo + ML_L]
            mx_r = jnp.maximum(m_row, cmx_last)
            decay = jnp.exp(m_row - mx_r)
            wk = jnp.exp(r2[:, lo:lo + ML_L] - mx_r)
            m_row = b_last + mx_r

            src_e, src_o = (r2, r2r) if cc == 0 else (r2r, r2)
            cols = []
            for p in range(MLSTM_HEADS // 2):
                even = jnp.broadcast_to(src_e[2 * p:2 * p + 1, :], (ML_L, LANES))
                odd = jnp.broadcast_to(src_o[2 * p + 1:2 * p + 2, :], (ML_L, LANES))
                cols.append(jnp.where(lane128 < ML_L, even, odd))
            r_all = jnp.concatenate(cols, axis=1)
            arg = jnp.where(causal, ealpha_ref[rows, :] + r_all, -jnp.inf)
            p_all = (s * jnp.exp(arg)).astype(BF16)

            vbd = jnp.concatenate([va_aug] * MLSTM_HEADS, axis=0) * mask_v
            pv = jnp.dot(p_all, vbd, preferred_element_type=F32)
            qc_state = jnp.dot(qa, caug_ref[...].astype(BF16), preferred_element_type=F32)
            o_ref[0, rows, :] = (ew_ref[rows, :] * qc_state[:, :MLSTM_V_WIDTH]
                                 + pv[:, :MLSTM_V_WIDTH])
            den_ref[rows, :] = (wint_ref[rows, :] * qc_state[:, MLSTM_V_WIDTH:]
                                + pv[:, MLSTM_V_WIDTH:])

            wk_rows = jnp.concatenate(
                [jnp.broadcast_to(wk[h:h + 1, :], (MLSTM_QK_DIM, ML_L)) for h in range(MLSTM_HEADS)],
                axis=0)
            dec_rows = jnp.concatenate(
                [jnp.broadcast_to(decay[h:h + 1, :], (MLSTM_QK_DIM, 1)) for h in range(MLSTM_HEADS)],
                axis=0)
            ktw = (kt2[:, lo:lo + ML_L] * wk_rows).astype(BF16)
            upd = jnp.dot(ktw, va_aug, preferred_element_type=F32)
            caug_ref[...] = dec_rows * caug_ref[...] + jnp.where(mask_c, upd, 0.0)
        return m_row

    lax.fori_loop(0, ML_PAIRS, pair_body, jnp.zeros((MLSTM_HEADS, 1), F32), unroll=ML_PAIR_UNROLL)

    ob_row = lax.broadcasted_iota(jnp.int32, (MLSTM_V_WIDTH, LANES), 0) // MLSTM_V_DIM
    ob_col = lax.broadcasted_iota(jnp.int32, (MLSTM_V_WIDTH, LANES), 1)
    ones_bd = jnp.where(ob_col == ob_row + ML_HL, 1.0, 0.0).astype(BF16)

    def norm_tile(i, carry):
        rows = pl.ds(pl.multiple_of(i * ML_TILE, ML_TILE), ML_TILE)
        num = o_ref[0, rows, :]
        dn = jnp.maximum(jnp.abs(den_ref[rows, :]), floor_ref[rows, :])
        r = 1.0 / dn
        n2_hi, n2_lo = _split_hi_lo(num * num)
        msn = (jnp.dot(n2_hi, ones_bd, preferred_element_type=F32)
               + jnp.dot(n2_lo, ones_bd, preferred_element_type=F32)) * (1.0 / MLSTM_V_DIM)
        fac = r * lax.rsqrt(r * r * msn + NORM_EPS)
        o_ref[0, rows, :] = num * _expand_heads(fac, exp_bf)
        return carry

    lax.fori_loop(0, SEQ // ML_TILE, norm_tile, 0)


def _mlstm(projm3, gt, conv_w):
    b = projm3.shape[0]
    qk_blk = (1, SEQ, MLSTM_QK_WIDTH)
    v_blk = (1, SEQ, MLSTM_V_WIDTH)
    v_col = 2 * MLSTM_QK_WIDTH // MLSTM_V_WIDTH
    vmem = (2 * (2 * SEQ * MLSTM_QK_WIDTH * 4 + 2 * SEQ * MLSTM_V_WIDTH * 4 + 2 * SEQ * LANES * 4)
            + 2 * SEQ * MLSTM_QK_WIDTH * 2 + SEQ * MLSTM_QK_WIDTH * 4 + 2 * SEQ * MLSTM_V_WIDTH * 4
            + 3 * SEQ * LANES * 4 + (8 << 20))
    return pl.pallas_call(
        _mlstm_kernel,
        grid=(b,),
        in_specs=[
            pl.BlockSpec(qk_blk, lambda bi: (bi, 0, 0)),
            pl.BlockSpec(qk_blk, lambda bi: (bi, 0, 1)),
            pl.BlockSpec(v_blk, lambda bi: (bi, 0, v_col)),
            pl.BlockSpec((SUBLANES, SEQ), lambda bi: (1, bi)),
            pl.BlockSpec((SUBLANES, SEQ), lambda bi: (2, bi)),
            pl.BlockSpec((CONV_WIDTH, 2 * MLSTM_QK_WIDTH), lambda bi: (0, 0)),
        ],
        out_specs=pl.BlockSpec(v_blk, lambda bi: (bi, 0, 0)),
        out_shape=jax.ShapeDtypeStruct((b, SEQ, MLSTM_V_WIDTH), F32),
        scratch_shapes=[
            pltpu.VMEM((SEQ, LANES), F32),
            pltpu.VMEM((SEQ, MLSTM_QK_WIDTH), BF16),
            pltpu.VMEM((SEQ, MLSTM_QK_WIDTH), BF16),
            pltpu.VMEM((ML_PAIRS, MLSTM_QK_WIDTH, LANES), F32),
            pltpu.VMEM((ML_PAIRS, SUBLANES, LANES), F32),
            pltpu.VMEM((ML_PAIRS, SUBLANES, LANES), F32),
            pltpu.VMEM((ML_PAIRS, SUBLANES, LANES), F32),
            pltpu.VMEM((SEQ, MLSTM_V_WIDTH), F32),
            pltpu.VMEM((SEQ, MLSTM_V_WIDTH), F32),
            pltpu.VMEM((SEQ, LANES), F32),
            pltpu.VMEM((SEQ, LANES), F32),
            pltpu.VMEM((SEQ, LANES), F32),
            pltpu.VMEM((MLSTM_QK_WIDTH, ML_AUG), F32),
        ],
        compiler_params=pltpu.CompilerParams(
            dimension_semantics=("arbitrary",), vmem_limit_bytes=_vmem_limit(vmem)),
        name="mlstm",
    )(projm3, projm3, projm3, gt, gt, conv_w)


RT_TM = 1024
RT_PARTS = 1


def _out_route_kernel(x_ref, fy_ref, my_ref, mo_ref, wo_ref, mg_ref, nw_ref, wr_hi_ref, wr_lo_ref,
                      rb_ref, x1_ref, h2_ref, eidx_ref, gate_ref, rank_ref, cnt_ref, carry_ref):
    i = pl.program_id(0)

    @pl.when(i == 0)
    def _():
        carry_ref[...] = jnp.zeros_like(carry_ref)

    pm = RT_TM // RT_PARTS
    src = lax.broadcasted_iota(jnp.int32, (pm, pm), 0)
    dst = lax.broadcasted_iota(jnp.int32, (pm, pm), 1)
    upper = jnp.where(src < dst, 1.0, 0.0).astype(BF16)
    zi = jnp.zeros((SUBLANES - TOP_K, pm), jnp.int32)

    for part in range(RT_PARTS):
        rows = slice(part * pm, (part + 1) * pm)
        my = my_ref[rows, :] * mg_ref[...] / (1.0 + jnp.exp(-mo_ref[rows, :]))
        mixed = (jnp.dot(fy_ref[rows, :].astype(BF16), wo_ref[:FOX_WIDTH, :], preferred_element_type=F32)
                 + jnp.dot(my.astype(BF16), wo_ref[FOX_WIDTH:, :], preferred_element_type=F32))
        x1 = x_ref[rows, :] + mixed
        x1_ref[rows, :] = x1
        h2 = _rms(x1, nw_ref[...])
        _rows_to_tiles(h2, h2_ref.at[pl.ds(part * pm * TOK_ROWS, pm * TOK_ROWS), :])

        h_hi, h_lo = _split_hi_lo(h2)
        wr_hi, wr_lo = wr_hi_ref[...], wr_lo_ref[...]
        logit = (lax.dot_general(wr_hi, h_hi, NT_DIMS, preferred_element_type=F32)
                 + lax.dot_general(wr_lo, h_hi, NT_DIMS, preferred_element_type=F32)
                 + lax.dot_general(wr_hi, h_lo, NT_DIMS, preferred_element_type=F32)) + rb_ref[...]

        e_iota = lax.broadcasted_iota(jnp.int32, logit.shape, 0).astype(F32)
        vals, idxs, hots = [], [], []
        for _ in range(TOP_K):
            mk = jnp.max(logit, axis=0, keepdims=True)
            idx = jnp.min(jnp.where(logit == mk, e_iota, float(N_EXPERTS)), axis=0, keepdims=True)
            hot = e_iota == idx
            logit = jnp.where(hot, -jnp.inf, logit)
            vals.append(mk)
            idxs.append(idx.astype(jnp.int32))
            hots.append(hot)
        exps = [jnp.exp(v - vals[0]) for v in vals]
        tot = exps[0] + exps[1] + exps[2] + exps[3]
        gates = [e / tot for e in exps]

        assign = jnp.zeros(logit.shape, F32)
        for hot in hots:
            assign = assign + jnp.where(hot, 1.0, 0.0)
        base = jnp.dot(assign.astype(BF16), upper, preferred_element_type=F32) + carry_ref[:, 0:1]
        ranks = [jnp.sum(jnp.where(hot, base, 0.0), axis=0, keepdims=True) for hot in hots]
        carry_ref[...] = carry_ref[...] + jnp.sum(assign, axis=1, keepdims=True)

        eidx_ref[:, rows] = jnp.concatenate(idxs + [zi], axis=0)
        rank_ref[:, rows] = jnp.concatenate([r.astype(jnp.int32) for r in ranks] + [zi], axis=0)
        gate_ref[:, rows] = jnp.concatenate(gates + [zi.astype(F32)], axis=0)

    cnt_ref[...] = carry_ref[...]


def _out_route(x2d, fox_y2d, mlstm_y2d, projm, w_out_bf, mlstm_gain, moe_norm_w, wr_hi, wr_lo, rb):
    n = x2d.shape[0]
    tm = RT_TM
    const = lambda i: (0, 0)
    mo_col = (MLSTM_COLS - MLSTM_V_WIDTH) // MLSTM_V_WIDTH
    row_blk = lambda w: pl.BlockSpec((tm, w), lambda i: (i, 0))
    lane_blk = pl.BlockSpec((SUBLANES, tm), lambda i: (0, i))
    vmem = (2 * (tm * D_MODEL * 4 * 3 + tm * FOX_WIDTH * 4 * 3 + D_MODEL * D_MODEL * 2)
            + 6 * tm * D_MODEL * 4 + tm * tm * 6)
    return pl.pallas_call(
        _out_route_kernel,
        grid=(n // tm,),
        in_specs=[
            row_blk(D_MODEL), row_blk(FOX_WIDTH), row_blk(MLSTM_V_WIDTH),
            pl.BlockSpec((tm, MLSTM_V_WIDTH), lambda i: (i, mo_col)),
            pl.BlockSpec((D_MODEL, D_MODEL), const),
            pl.BlockSpec((1, MLSTM_V_WIDTH), const),
            pl.BlockSpec((1, D_MODEL), const),
            pl.BlockSpec((N_EXPERTS, D_MODEL), const),
            pl.BlockSpec((N_EXPERTS, D_MODEL), const),
            pl.BlockSpec((N_EXPERTS, 1), const),
        ],
        out_specs=[row_blk(D_MODEL), pl.BlockSpec((tm * TOK_ROWS, LANES), lambda i: (i, 0)),
                   lane_blk, lane_blk, lane_blk, pl.BlockSpec((N_EXPERTS, LANES), const)],
        out_shape=[
            jax.ShapeDtypeStruct((n, D_MODEL), F32),
            jax.ShapeDtypeStruct((n * TOK_ROWS, LANES), F32),
            jax.ShapeDtypeStruct((SUBLANES, n), jnp.int32),
            jax.ShapeDtypeStruct((SUBLANES, n), F32),
            jax.ShapeDtypeStruct((SUBLANES, n), jnp.int32),
            jax.ShapeDtypeStruct((N_EXPERTS, LANES), F32),
        ],
        scratch_shapes=[pltpu.VMEM((N_EXPERTS, LANES), F32)],
        compiler_params=pltpu.CompilerParams(
            dimension_semantics=("arbitrary",), vmem_limit_bytes=_vmem_limit(vmem)),
        name="out_route",
    )(x2d, fox_y2d, mlstm_y2d, projm, w_out_bf, mlstm_gain, moe_norm_w, wr_hi, wr_lo, rb)


INV_CHUNK = 8192
INV_UNROLL = 32


def _invert_kernel(pos_ref, zeros_hbm, inv_ref, sem):
    i = pl.program_id(0)

    @pl.when(i == 0)
    def _():
        cp = pltpu.make_async_copy(zeros_hbm, inv_ref, sem.at[0])
        cp.start()
        cp.wait()

    base = i * INV_CHUNK

    def body(j, carry):
        inv_ref[pos_ref[0, 0, j]] = base + j
        return carry

    lax.fori_loop(0, INV_CHUNK, body, 0, unroll=INV_UNROLL)


def _invert(pos_flat, n_rows):
    n_slots = pos_flat.shape[0]
    steps = n_slots // INV_CHUNK
    return pl.pallas_call(
        _invert_kernel,
        grid=(steps,),
        in_specs=[
            pl.BlockSpec((1, 1, INV_CHUNK), lambda i: (i, 0, 0), memory_space=pltpu.SMEM),
            pl.BlockSpec(memory_space=pl.ANY),
        ],
        out_specs=pl.BlockSpec(memory_space=pltpu.SMEM),
        out_shape=jax.ShapeDtypeStruct((n_rows,), jnp.int32),
        scratch_shapes=[pltpu.SemaphoreType.DMA((1,))],
        compiler_params=pltpu.CompilerParams(dimension_semantics=("arbitrary",)),
        name="invert",
    )(pos_flat.reshape(steps, 1, INV_CHUNK), jnp.zeros((n_rows,), jnp.int32))


EX_BM = 256
EX_DRAIN_STEPS = 2


def _experts_kernel(be_ref, nu_ref, nv_ref, tok_ref, tokn_ref, dst_ref, h2_hbm, wgu_ref, bgu_ref,
                    wd_ref, bd_ref, y_hbm, xt_ref, yt_ref, wgu_bf_ref, wd_bf_ref, gsem, ssem):
    i = pl.program_id(0)
    last_blk = pl.num_programs(0) - 1 - EX_DRAIN_STEPS
    nu = nu_ref[0]
    slot = i % 2
    cur = jnp.minimum(i, last_blk)
    tile_rows = EX_BM * TOK_ROWS

    def tok_tile(ref, idx):
        return ref.at[pl.ds(pl.multiple_of(idx, TOK_ROWS), TOK_ROWS), :]

    def start_gather(idx_ref, s, n):
        @pl.when(n == EX_BM)
        def _():
            for r in range(EX_BM):
                pltpu.make_async_copy(tok_tile(h2_hbm, idx_ref[0, 0, r]),
                                      xt_ref.at[s, pl.ds(r * TOK_ROWS, TOK_ROWS), :], gsem.at[s]).start()

        @pl.when(n < EX_BM)
        def _():
            def body(r, carry):
                pltpu.make_async_copy(
                    tok_tile(h2_hbm, idx_ref[0, 0, r]),
                    xt_ref.at[s, pl.ds(pl.multiple_of(r * TOK_ROWS, TOK_ROWS), TOK_ROWS), :],
                    gsem.at[s]).start()
                return carry
            lax.fori_loop(0, n, body, 0)

    def scatter_row(r):
        pltpu.make_async_copy(yt_ref.at[slot, pl.ds(r * TOK_ROWS, TOK_ROWS), :],
                              tok_tile(y_hbm, dst_ref[0, 0, r]), ssem.at[slot]).start()

    def wait_block(sem, buf):
        pltpu.make_async_copy(h2_hbm.at[pl.ds(0, tile_rows), :], buf, sem).wait()

    def wait_tokens(sem, buf, n):
        @pl.when(n == EX_BM)
        def _():
            wait_block(sem, buf)

        @pl.when(n < EX_BM)
        def _():
            def body(r, carry):
                pltpu.make_async_copy(h2_hbm.at[pl.ds(0, TOK_ROWS), :],
                                      buf.at[pl.ds(0, TOK_ROWS), :], sem).wait()
                return carry
            lax.fori_loop(0, n, body, 0)

    @pl.when(i == 0)
    def _():
        xt_ref[...] = jnp.zeros_like(xt_ref)
        start_gather(tok_ref, 0, nv_ref[0])

    @pl.when(i + 1 < nu)
    def _():
        start_gather(tokn_ref, 1 - slot, nv_ref[jnp.minimum(i + 1, last_blk)])

    @pl.when((i >= 2) & (i - 2 < nu))
    def _():
        wait_tokens(ssem.at[slot], yt_ref.at[slot], nv_ref[jnp.clip(i - 2, 0, last_blk)])

    @pl.when(i < nu)
    def _():
        @pl.when((i == 0) | (be_ref[cur] != be_ref[jnp.maximum(cur - 1, 0)]))
        def _():
            wgu_bf_ref[...] = wgu_ref[0].astype(BF16)
            wd_bf_ref[...] = wd_ref[0].astype(BF16)

        wait_tokens(gsem.at[slot], xt_ref.at[slot], nv_ref[cur])
        xb = _tiles_to_rows(xt_ref.at[slot], EX_BM).astype(BF16)
        gu = jnp.dot(xb, wgu_bf_ref[...], preferred_element_type=F32) + bgu_ref[0]
        gate = jnp.minimum(gu[:, :D_EXPERT], SWIGLU_LIMIT)
        up = jnp.clip(gu[:, D_EXPERT:], -SWIGLU_LIMIT, SWIGLU_LIMIT)
        act = (up + 1.0) * (gate / (1.0 + jnp.exp(-SWIGLU_ALPHA * gate)))
        y = jnp.dot(act.astype(BF16), wd_bf_ref[...], preferred_element_type=F32) + bd_ref[0]
        _rows_to_tiles(y, yt_ref.at[slot])
        nv = nv_ref[cur]

        @pl.when(nv == EX_BM)
        def _():
            for r in range(EX_BM):
                scatter_row(r)

        @pl.when(nv < EX_BM)
        def _():
            def body(r, carry):
                pltpu.make_async_copy(
                    yt_ref.at[slot, pl.ds(pl.multiple_of(r * TOK_ROWS, TOK_ROWS), TOK_ROWS), :],
                    tok_tile(y_hbm, dst_ref[0, 0, r]), ssem.at[slot]).start()
                return carry
            lax.fori_loop(0, nv, body, 0)


def _experts(block_e, n_used, n_valid, buf_tok3, dst3, h2, w_gu, b_gu, w_down, b_down, n_slots):
    nb = buf_tok3.shape[0]
    idx_blk = lambda f: pl.BlockSpec((1, 1, EX_BM), f, memory_space=pltpu.SMEM)
    vmem = (2 * (D_MODEL * 2 * D_EXPERT * 4 + D_EXPERT * D_MODEL * 4)
            + D_MODEL * 2 * D_EXPERT * 2 + D_EXPERT * D_MODEL * 2
            + 3 * EX_BM * D_MODEL * 4 + 3 * EX_BM * 2 * D_EXPERT * 4)
    blk = lambda i: jnp.minimum(i, nb - 1)
    w_map = lambda i, be, nu, nv: (be[blk(i)], 0, 0)
    grid_spec = pltpu.PrefetchScalarGridSpec(
        num_scalar_prefetch=3,
        grid=(nb + EX_DRAIN_STEPS,),
        in_specs=[
            idx_blk(lambda i, be, nu, nv: (blk(i), 0, 0)),
            idx_blk(lambda i, be, nu, nv: (blk(i + 1), 0, 0)),
            idx_blk(lambda i, be, nu, nv: (blk(i), 0, 0)),
            pl.BlockSpec(memory_space=pl.ANY),
            pl.BlockSpec((1, D_MODEL, 2 * D_EXPERT), w_map),
            pl.BlockSpec((1, 1, 2 * D_EXPERT), w_map),
            pl.BlockSpec((1, D_EXPERT, D_MODEL), w_map),
            pl.BlockSpec((1, 1, D_MODEL), w_map),
        ],
        out_specs=pl.BlockSpec(memory_space=pl.ANY),
        scratch_shapes=[
            pltpu.VMEM((2, EX_BM * TOK_ROWS, LANES), F32),
            pltpu.VMEM((2, EX_BM * TOK_ROWS, LANES), F32),
            pltpu.VMEM((D_MODEL, 2 * D_EXPERT), BF16),
            pltpu.VMEM((D_EXPERT, D_MODEL), BF16),
            pltpu.SemaphoreType.DMA((2,)),
            pltpu.SemaphoreType.DMA((2,)),
        ],
    )
    return pl.pallas_call(
        _experts_kernel,
        grid_spec=grid_spec,
        out_shape=jax.ShapeDtypeStruct((n_slots * TOK_ROWS, LANES), F32),
        compiler_params=pltpu.CompilerParams(
            dimension_semantics=("arbitrary",), vmem_limit_bytes=_vmem_limit(vmem)),
        name="experts",
    )(block_e, n_used, n_valid, buf_tok3, buf_tok3, dst3, h2, w_gu, b_gu[:, None, :], w_down,
      b_down[:, None, :])


CB_TM = 512


def _combine_kernel(x1_ref, y0_ref, y1_ref, y2_ref, y3_ref, gate_ref, o_ref):
    tm = x1_ref.shape[0]
    g = jnp.concatenate([gate_ref[...], jnp.zeros((LANES - SUBLANES, tm), F32)], axis=0).T
    acc = x1_ref[...]
    for k, y_ref in enumerate((y0_ref, y1_ref, y2_ref, y3_ref)):
        acc = acc + g[:, k:k + 1] * _tiles_to_rows(y_ref, tm)
    o_ref[...] = acc


def _combine(x1, y_slots, gates):
    n = x1.shape[0]
    tm = CB_TM
    nt = n // tm
    vmem = 2 * (2 * tm * D_MODEL * 4 + tm * TOP_K * D_MODEL * 4) + 4 * tm * D_MODEL * 4
    y_spec = lambda k: pl.BlockSpec((tm * TOK_ROWS, LANES), lambda i: (k * nt + i, 0))
    return pl.pallas_call(
        _combine_kernel,
        grid=(nt,),
        in_specs=[pl.BlockSpec((tm, D_MODEL), lambda i: (i, 0))]
        + [y_spec(k) for k in range(TOP_K)]
        + [pl.BlockSpec((SUBLANES, tm), lambda i: (0, i))],
        out_specs=pl.BlockSpec((tm, D_MODEL), lambda i: (i, 0)),
        out_shape=jax.ShapeDtypeStruct((n, D_MODEL), F32),
        compiler_params=pltpu.CompilerParams(
            dimension_semantics=("arbitrary",), vmem_limit_bytes=_vmem_limit(vmem)),
        name="combine",
    )(x1, y_slots, y_slots, y_slots, y_slots, gates)


def _dispatch_plan(eidx, rank, counts):
    n = eidx.shape[1]
    n_slots = n * TOP_K
    nb = n_slots // EX_BM + N_EXPERTS
    counts = counts.astype(jnp.int32)
    padded = ((counts + EX_BM - 1) // EX_BM) * EX_BM
    pad_end = jnp.cumsum(padded)
    pad_start = pad_end - padded
    e = eidx[:TOP_K]
    start_of = jnp.sum(jnp.where(e[:, :, None] == jnp.arange(N_EXPERTS, dtype=jnp.int32),
                                 pad_start[None, None, :], 0), axis=-1)
    pos = start_of + rank[:TOP_K]
    inv = _invert(pos.reshape(-1), nb * EX_BM)
    buf_tok = (inv % n) * TOK_ROWS
    dst = inv * TOK_ROWS
    blk_start = jnp.arange(nb, dtype=jnp.int32) * EX_BM
    block_e = jnp.minimum(jnp.sum((pad_end[None, :] <= blk_start[:, None]).astype(jnp.int32), axis=1),
                          N_EXPERTS - 1)
    n_used = (pad_end[-1] // EX_BM).astype(jnp.int32).reshape(1)
    n_valid = jnp.clip(pad_start[block_e] + counts[block_e] - blk_start, 0, EX_BM).astype(jnp.int32)
    n_valid = jnp.where(blk_start < pad_end[-1], n_valid, 0)
    return (block_e, n_used, n_valid, buf_tok.reshape(nb, 1, EX_BM), dst.reshape(nb, 1, EX_BM), n_slots)


def _prep_in_proj_weights(w_in, fox_f_bias, mlstm_i_bias, mlstm_f_bias):
    split_at = []
    acc = 0
    for wdt in SPLIT_WIDTHS[:-1]:
        acc += wdt
        split_at.append(acc)
    fq, fk, fv, ff, mq, mk, mv, mi, mf, mo = jnp.split(w_in, split_at, axis=-1)
    w_main = jnp.concatenate([fq, fk, fv, mq, mk, mv, mo], axis=-1).astype(BF16)
    w_gate = jnp.concatenate([ff, mi, mf], axis=-1)
    bias = jnp.concatenate([fox_f_bias, mlstm_i_bias, mlstm_f_bias]).astype(F32)
    n_gate = w_gate.shape[1]
    wg_hi, wg_lo = _split_hi_lo(jnp.pad(w_gate, ((0, 0), (0, LANES - n_gate))))
    bias_r = jnp.pad(bias, (0, LANES - n_gate))[None, :]
    return w_main, wg_hi, wg_lo, bias_r


def kernel(x, attn_norm_w, w_in, fox_f_bias, fox_q_norm_w, fox_k_norm_w, fox_out_norm_w, mlstm_conv_w, mlstm_i_bias, mlstm_f_bias, mlstm_out_norm_w, w_out, moe_norm_w, router_w, router_b, expert_w_gate_up, expert_b_gate_up, expert_w_down, expert_b_down):
    bsz, seq, d = x.shape
    x2d = x.reshape(bsz * seq, d)
    prep = _prep_in_proj_weights(w_in[0], fox_f_bias[0], mlstm_i_bias[0], mlstm_f_bias[0])
    pair = lambda w: jnp.tile(w, LANES // HEAD_DIM)[None, :]
    qn, kn, aq, ak, vb, projm, gt = _in_proj(x2d, attn_norm_w[0][None, :], *prep,
                                             pair(fox_q_norm_w[0]), pair(fox_k_norm_w[0]))
    b3 = lambda a: a.reshape(bsz, seq, a.shape[-1])
    fox_y = _fox(b3(qn), b3(kn), b3(aq), b3(ak), b3(vb), fox_out_norm_w[0][None, :])
    mlstm_y = _mlstm(b3(projm), gt, mlstm_conv_w[0])
    return _channel_mixer(x2d, fox_y.reshape(-1, FOX_WIDTH), mlstm_y.reshape(-1, MLSTM_V_WIDTH), projm,
                          mlstm_out_norm_w[0], w_out[0], moe_norm_w[0], router_w[0], router_b[0],
                          expert_w_gate_up[0], expert_b_gate_up[0], expert_w_down[0],
                          expert_b_down[0]).reshape(bsz, seq, d)


def _channel_mixer(x2d, fox_y2d, mlstm_y2d, projm, mlstm_gain, w_out, moe_norm_w, router_w, router_b,
                   w_gu, b_gu, w_down, b_down):
    wr_hi, wr_lo = _split_hi_lo(router_w.T)
    x1, h2, eidx, gates, rank, counts = _out_route(
        x2d, fox_y2d, mlstm_y2d, projm, w_out.astype(BF16), mlstm_gain[None, :], moe_norm_w[None, :],
        wr_hi, wr_lo, router_b[:, None])
    block_e, n_used, n_valid, buf_tok3, dst3, n_slots = _dispatch_plan(eidx, rank, counts[:, 0])
    y_slots = _experts(block_e, n_used, n_valid, buf_tok3, dst3, h2, w_gu, b_gu, w_down, b_down, n_slots)
    return _combine(x1, y_slots, gates)
```

```python
import math

import jax
import jax.numpy as jnp
from jax import lax
from jax.experimental import pallas as pl
from jax.experimental.pallas import tpu as pltpu

F32 = jnp.float32
BF16 = jnp.bfloat16

D_MODEL = 1024
SEQ = 2048
HEAD_DIM = 64
FOX_HEADS = 8
FOX_WIDTH = FOX_HEADS * HEAD_DIM
MLSTM_HEADS = 8
MLSTM_QK_DIM = 32
MLSTM_V_DIM = 64
MLSTM_QK_WIDTH = MLSTM_HEADS * MLSTM_QK_DIM
MLSTM_V_WIDTH = MLSTM_HEADS * MLSTM_V_DIM
CONV_WIDTH = 4
MLSTM_CHUNK = 64
SPLIT_WIDTHS = (FOX_WIDTH, FOX_WIDTH, FOX_WIDTH, FOX_HEADS,
                MLSTM_QK_WIDTH, MLSTM_QK_WIDTH, MLSTM_V_WIDTH,
                MLSTM_HEADS, MLSTM_HEADS, MLSTM_V_WIDTH)
N_EXPERTS = 32
TOP_K = 4
D_EXPERT = D_MODEL
SWIGLU_ALPHA = 1.702
SWIGLU_LIMIT = 7.0
NORM_EPS = 1e-5
LOG2E = 1.4426950408889634

LANES = 128
SUBLANES = 8
V7X_VMEM_BYTES = 64 * 1024 * 1024

MAIN_WIDTH = 3 * FOX_WIDTH + 2 * MLSTM_QK_WIDTH + 2 * MLSTM_V_WIDTH
MLSTM_COLS = MAIN_WIDTH - 3 * FOX_WIDTH
GATE_ROWS = 32

NT_DIMS = (((1,), (1,)), ((), ()))


VMEM_TEMPORARIES_BYTES = 8 << 20
VMEM_RESERVED_BYTES = 4 << 20


def _vmem_limit(block_bytes):
    return int(min(block_bytes + VMEM_TEMPORARIES_BYTES, V7X_VMEM_BYTES - VMEM_RESERVED_BYTES))


def _log_sigmoid(x):
    return jnp.minimum(x, 0.0) - jnp.log(1.0 + jnp.exp(-jnp.abs(x)))


def _split_hi_lo(x):
    hi = x.astype(BF16)
    lo = (x - hi.astype(F32)).astype(BF16)
    return hi, lo


def _rms(x, w):
    return x * lax.rsqrt(jnp.mean(x * x, axis=-1, keepdims=True) + NORM_EPS) * w


TOK_ROWS = D_MODEL // LANES


def _rows_to_tiles(x, tile_ref):
    m = x.shape[0]
    for j in range(TOK_ROWS):
        tile_ref[pl.ds(j, m, stride=TOK_ROWS), :] = x[:, j * LANES:(j + 1) * LANES]


def _tiles_to_rows(tile_ref, m):
    return jnp.concatenate(
        [tile_ref[pl.ds(j, m, stride=TOK_ROWS), :] for j in range(TOK_ROWS)], axis=1)


IN_TM = 1024
IN_TILES_PER_SEQ = SEQ // IN_TM
IN_PARTS = 2
IN_PM = IN_TM // IN_PARTS
IN_SCAN_SHIFTS = tuple(1 << i for i in range(int(math.log2(IN_PM))))
IN_W_ROWS = 256
IN_W_WIDTH = sum(SPLIT_WIDTHS)
IN_W_IS_MAIN = (True, True, True, False, True, True, True, False, False, True)


def _main_segments():
    segs, src, dst = [], 0, 0
    for width, is_main in zip(SPLIT_WIDTHS, IN_W_IS_MAIN):
        if is_main:
            if segs and segs[-1][0] + segs[-1][2] == src:
                segs[-1] = (segs[-1][0], segs[-1][1], segs[-1][2] + width)
            else:
                segs.append((src, dst, width))
            dst += width
        src += width
    return tuple(segs)


IN_MAIN_SEGMENTS = _main_segments()


def _pair_rms(x, w, lo_half):
    sq = x * x
    ms_lo = jnp.sum(jnp.where(lo_half, sq, 0.0), axis=-1, keepdims=True) * (1.0 / HEAD_DIM)
    ms_hi = jnp.sum(jnp.where(lo_half, 0.0, sq), axis=-1, keepdims=True) * (1.0 / HEAD_DIM)
    inv = jnp.where(lo_half, lax.rsqrt(ms_lo + NORM_EPS), lax.rsqrt(ms_hi + NORM_EPS))
    return x * inv * w


def _in_proj_kernel(x_ref, nw_ref, w_hbm, wg_hi_ref, wg_lo_ref, br_ref, qw_ref, kw_ref,
                    qn_ref, kn_ref, aq_ref, ak_ref, vb_ref, projm_ref, gt_ref, carry_ref,
                    w_ref, wstage_ref, wsem):
    i = pl.program_id(0)

    @pl.when(i == 0)
    def _():
        for c in range(D_MODEL // IN_W_ROWS):
            rows = slice(c * IN_W_ROWS, (c + 1) * IN_W_ROWS)
            copy = pltpu.make_async_copy(w_hbm.at[rows, :], wstage_ref, wsem)
            copy.start()
            copy.wait()
            for src, dst, width in IN_MAIN_SEGMENTS:
                w_ref[rows, dst:dst + width] = wstage_ref[:, src:src + width].astype(BF16)

    @pl.when(i % IN_TILES_PER_SEQ == 0)
    def _():
        carry_ref[...] = jnp.zeros_like(carry_ref)

    for part in range(IN_PARTS):
        rows = slice(part * IN_PM, (part + 1) * IN_PM)
        x = x_ref[rows, :]
        ms = jnp.mean(x * x, axis=-1, keepdims=True)
        y = x * lax.rsqrt(ms + NORM_EPS) * nw_ref[...]
        h_hi, h_lo = _split_hi_lo(y)
        main = jnp.dot(h_hi, w_ref[...], preferred_element_type=F32)

        w_hi, w_lo = wg_hi_ref[...], wg_lo_ref[...]
        g = (jnp.dot(h_hi, w_hi, preferred_element_type=F32)
             + jnp.dot(h_hi, w_lo, preferred_element_type=F32)
             + jnp.dot(h_lo, w_hi, preferred_element_type=F32))
        g = g + br_ref[...]
        lane = lax.broadcasted_iota(jnp.int32, g.shape, 1)
        is_input_gate = (lane >= FOX_HEADS) & (lane < FOX_HEADS + MLSTM_HEADS)
        gates = jnp.where(is_input_gate, g, _log_sigmoid(g))
        gt_ref[:, rows] = gates.T[:GATE_ROWS, :]

        rowi = lax.broadcasted_iota(jnp.int32, gates.shape, 0)
        c = gates
        for s in IN_SCAN_SHIFTS:
            c = c + jnp.where(rowi >= s, pltpu.roll(c, s, axis=0), 0.0)
        c = c + carry_ref[...]
        carry_ref[...] = c[IN_PM - 1:IN_PM, :]
        cum2 = c * LOG2E

        c8 = jnp.where(lane < FOX_HEADS, cum2, 0.0)
        c_hi = c8.astype(BF16).astype(F32)
        r1 = c8 - c_hi
        c_mid = r1.astype(BF16).astype(F32)
        c_lo = (r1 - c_mid).astype(BF16).astype(F32)
        ones_q = jnp.where((lane >= 3 * FOX_HEADS) & (lane < 6 * FOX_HEADS), 1.0, 0.0)
        ones_k = jnp.where(lane < 3 * FOX_HEADS, 1.0, 0.0)
        aq = (c_hi + pltpu.roll(c_mid, FOX_HEADS, axis=1) + pltpu.roll(c_lo, 2 * FOX_HEADS, axis=1)
              + ones_q)
        ak = ones_k - (pltpu.roll(c_hi, 3 * FOX_HEADS, axis=1) + pltpu.roll(c_mid, 4 * FOX_HEADS, axis=1)
                       + pltpu.roll(c_lo, 5 * FOX_HEADS, axis=1))
        aq_ref[rows, :] = aq.astype(BF16)
        ak_ref[rows, :] = ak.astype(BF16)

        lo_half = lane < HEAD_DIM
        q_scale = (HEAD_DIM ** -0.5) * LOG2E
        for p in range(FOX_HEADS // 2):
            ps = slice(p * LANES, (p + 1) * LANES)
            qn_ref[rows, ps] = (_pair_rms(main[:, ps], qw_ref[...], lo_half) * q_scale).astype(BF16)
            kn_ref[rows, ps] = _pair_rms(main[:, FOX_WIDTH + p * LANES:FOX_WIDTH + (p + 1) * LANES],
                                         kw_ref[...], lo_half).astype(BF16)

        vb_ref[rows, :] = main[:, 2 * FOX_WIDTH:3 * FOX_WIDTH].astype(BF16)
        projm_ref[rows, :] = main[:, 3 * FOX_WIDTH:]


def _in_proj(x2d, norm_w, w_in, wg_hi, wg_lo, bias_r, qw, kw):
    n = x2d.shape[0]
    tm = IN_TM
    assert w_in.shape == (D_MODEL, IN_W_WIDTH) and w_in.dtype == F32
    const = lambda i: (0, 0)
    row = lambda w: pl.BlockSpec((tm, w), lambda i: (i, 0))
    vmem = (2 * (tm * D_MODEL * 4 + 3 * tm * FOX_WIDTH * 2 + 2 * tm * LANES * 2
                 + tm * MLSTM_COLS * 4 + GATE_ROWS * tm * 4 + tm * LANES * 4)
            + D_MODEL * MAIN_WIDTH * 2 + IN_W_ROWS * (IN_W_WIDTH + LANES) * 4
            + 2 * tm * MAIN_WIDTH * 4)
    return pl.pallas_call(
        _in_proj_kernel,
        grid=(n // tm,),
        in_specs=[
            row(D_MODEL),
            pl.BlockSpec((1, D_MODEL), const),
            pl.BlockSpec(memory_space=pl.ANY),
            pl.BlockSpec((D_MODEL, LANES), const),
            pl.BlockSpec((D_MODEL, LANES), const),
            pl.BlockSpec((1, LANES), const),
            pl.BlockSpec((1, LANES), const),
            pl.BlockSpec((1, LANES), const),
        ],
        out_specs=[
            row(FOX_WIDTH), row(FOX_WIDTH), row(LANES), row(LANES), row(FOX_WIDTH), row(MLSTM_COLS),
            pl.BlockSpec((GATE_ROWS, tm), lambda i: (0, i)),
        ],
        out_shape=[
            jax.ShapeDtypeStruct((n, FOX_WIDTH), BF16),
            jax.ShapeDtypeStruct((n, FOX_WIDTH), BF16),
            jax.ShapeDtypeStruct((n, LANES), BF16),
            jax.ShapeDtypeStruct((n, LANES), BF16),
            jax.ShapeDtypeStruct((n, FOX_WIDTH), BF16),
            jax.ShapeDtypeStruct((n, MLSTM_COLS), F32),
            jax.ShapeDtypeStruct((GATE_ROWS, n), F32),
        ],
        scratch_shapes=[
            pltpu.VMEM((1, LANES), F32),
            pltpu.VMEM((D_MODEL, MAIN_WIDTH), BF16),
            pltpu.VMEM((IN_W_ROWS, IN_W_WIDTH), F32),
            pltpu.SemaphoreType.DMA(()),
        ],
        compiler_params=pltpu.CompilerParams(
            dimension_semantics=("arbitrary",), vmem_limit_bytes=_vmem_limit(vmem)),
        name="in_proj",
    )(x2d, norm_w, w_in, wg_hi, wg_lo, bias_r, qw, kw)


FOX_TQ = 512
FOX_PAIRS = 4


def _fox_kernel(qn_ref, kn_ref, aq_ref, ak_ref, v_ref, ow_ref, o_ref, qa_ref, ka_ref, va_ref):
    step = pl.program_id(1)
    tri = (lax.broadcasted_iota(jnp.int32, (FOX_TQ, FOX_TQ), 1)
           <= lax.broadcasted_iota(jnp.int32, (FOX_TQ, FOX_TQ), 0))
    lo_q = lax.broadcasted_iota(jnp.int32, (FOX_TQ, LANES), 1) < HEAD_DIM
    lane = lax.broadcasted_iota(jnp.int32, (1, LANES), 1)
    sum_lanes = (HEAD_DIM, 0)
    for pp in range(FOX_PAIRS):
        ps = slice(pp * LANES, (pp + 1) * LANES)
        ka_ref[pp, :, :LANES] = kn_ref[0, :, ps]
        ka_ref[pp, :, LANES:] = ak_ref[0]
        for j in range(2):
            h = 2 * (step * FOX_PAIRS + pp) + j
            own_mask = jnp.where((lane < HEAD_DIM) if j == 0 else (lane >= HEAD_DIM), 1.0, 0.0).astype(BF16)
            bias_mask = jnp.where((lane < 6 * FOX_HEADS) & (lane % FOX_HEADS == h), 1.0, 0.0).astype(BF16)
            qa_ref[2 * pp + j, :, :LANES] = qn_ref[0, :, ps] * own_mask
            qa_ref[2 * pp + j, :, LANES:] = aq_ref[0] * bias_mask
            va_ref[2 * pp + j] = (v_ref[0, :, ps] * own_mask
                                  + jnp.where(lane == sum_lanes[j], 1.0, 0.0).astype(BF16))

    for pp in range(FOX_PAIRS):
        ps = slice(pp * LANES, (pp + 1) * LANES)
        for i in range(SEQ // FOX_TQ):
            qs = slice(i * FOX_TQ, (i + 1) * FOX_TQ)
            n = (i + 1) * FOX_TQ
            normed = []
            for j in range(2):
                own_q = lo_q if j == 0 else jnp.logical_not(lo_q)
                s = lax.dot_general(qa_ref[2 * pp + j, qs, :], ka_ref[pp, :n, :], NT_DIMS,
                                    preferred_element_type=F32)
                diag = jnp.where(tri, s[:, n - FOX_TQ:], -jnp.inf)
                m = jnp.max(diag, axis=-1, keepdims=True)
                if i > 0:
                    past = s[:, :n - FOX_TQ]
                    m = jnp.maximum(m, jnp.max(past, axis=-1, keepdims=True))
                    p = jnp.concatenate([jnp.exp2(past - m), jnp.exp2(diag - m)], axis=1)
                else:
                    p = jnp.exp2(diag - m)
                o = jnp.dot(p.astype(BF16), va_ref[2 * pp + j, :n, :],
                            preferred_element_type=F32)
                o = o / o[:, sum_lanes[j]:sum_lanes[j] + 1]
                ms = jnp.sum(jnp.where(own_q, o * o, 0.0), axis=-1, keepdims=True) * (1.0 / HEAD_DIM)
                normed.append(o * lax.rsqrt(ms + NORM_EPS))
            o_ref[0, qs, ps] = jnp.where(lo_q, normed[0], normed[1]) * ow_ref[:, ps]


def _fox(qn3, kn3, aq3, ak3, v3, ow):
    b = qn3.shape[0]
    width = FOX_PAIRS * LANES
    steps = FOX_WIDTH // width
    pairs = pl.BlockSpec((1, SEQ, width), lambda bi, st: (bi, 0, st))
    shared = pl.BlockSpec((1, SEQ, LANES), lambda bi, st: (bi, 0, 0))
    vmem = (2 * (3 * SEQ * width * 2 + 2 * SEQ * LANES * 2 + SEQ * width * 4)
            + FOX_PAIRS * SEQ * LANES * 2 * 8 + 8 * FOX_TQ * SEQ * 4)
    return pl.pallas_call(
        _fox_kernel,
        grid=(b, steps),
        in_specs=[pairs, pairs, shared, shared, pairs, pl.BlockSpec((1, width), lambda bi, st: (0, st))],
        out_specs=pairs,
        out_shape=jax.ShapeDtypeStruct((b, SEQ, FOX_WIDTH), F32),
        scratch_shapes=[pltpu.VMEM((2 * FOX_PAIRS, SEQ, 2 * LANES), BF16),
                        pltpu.VMEM((FOX_PAIRS, SEQ, 2 * LANES), BF16),
                        pltpu.VMEM((2 * FOX_PAIRS, SEQ, LANES), BF16)],
        compiler_params=pltpu.CompilerParams(
            dimension_semantics=("arbitrary", "arbitrary"), vmem_limit_bytes=_vmem_limit(vmem)),
        name="fox",
    )(qn3, kn3, aq3, ak3, v3, ow)


ML_L = MLSTM_CHUNK
ML_PAIRS = SEQ // (2 * ML_L)
ML_HL = FOX_HEADS
ML_AUG = MLSTM_V_WIDTH + LANES
ML_TILE = 256
ML_PAIR_UNROLL = 4
SEG_SHIFTS = tuple(1 << i for i in range(int(math.log2(ML_L))))


def _split3(x):
    a = x.astype(BF16)
    r = x - a.astype(F32)
    b = r.astype(BF16)
    c = (r - b.astype(F32)).astype(BF16)
    return a, b, c


def _expand_heads(x, exp_bf):
    a, b, c = _split3(x)
    return (jnp.dot(a, exp_bf, preferred_element_type=F32)
            + jnp.dot(b, exp_bf, preferred_element_type=F32)
            + jnp.dot(c, exp_bf, preferred_element_type=F32))


def _seg_scan(x, axis, op, ident):
    idx = lax.broadcasted_iota(jnp.int32, x.shape, axis) % ML_L
    for s in SEG_SHIFTS:
        x = op(x, jnp.where(idx >= s, pltpu.roll(x, s, axis=axis), ident))
    return x


def _mlstm_kernel(q_ref, k_ref, v_ref, gi_ref, gf_ref, cw_ref, o_ref, den_ref,
                  qc_ref, kc_ref, kt_ref, rr_ref, cmr_ref, bcr_ref,
                  ealpha_ref, ew_ref, wint_ref, floor_ref, mfull_ref, caug_ref):
    def conv_silu(u, w):
        rowi = lax.broadcasted_iota(jnp.int32, u.shape, 0)
        acc = u * w[CONV_WIDTH - 1:CONV_WIDTH, :]
        for d in range(1, CONV_WIDTH):
            sh = jnp.where(rowi >= d, pltpu.roll(u, d, axis=0), 0.0)
            acc = acc + sh * w[CONV_WIDTH - 1 - d:CONV_WIDTH - d, :]
        return acc / (1.0 + jnp.exp(-acc))

    cw = cw_ref[...]
    qc_ref[...] = conv_silu(q_ref[0], cw[:, :MLSTM_QK_WIDTH]).astype(BF16)
    kc = conv_silu(k_ref[0], cw[:, MLSTM_QK_WIDTH:]) * (MLSTM_QK_DIM ** -0.5)
    kc_ref[...] = kc.astype(BF16)
    kt = kc.T
    for p in range(ML_PAIRS):
        kt_ref[p] = kt[:, p * LANES:(p + 1) * LANES]

    bcum_r = _seg_scan(gf_ref[...], 1, jnp.add, 0.0)
    r_r = gi_ref[...] - bcum_r
    cmx_r = _seg_scan(r_r, 1, jnp.maximum, -jnp.inf)
    for p in range(ML_PAIRS):
        ls = slice(p * LANES, (p + 1) * LANES)
        rr_ref[p] = r_r[:, ls]
        cmr_ref[p] = cmx_r[:, ls]
        bcr_ref[p] = bcum_r[:, ls]

    def to_columns(rows8):
        pad_lo = jnp.zeros((ML_HL, SEQ), F32)
        pad_hi = jnp.zeros((LANES - ML_HL - MLSTM_HEADS, SEQ), F32)
        return jnp.concatenate([pad_lo, rows8, pad_hi], axis=0).T

    bcum_c = to_columns(bcum_r)
    cmx_c = to_columns(cmx_r)
    m = jnp.zeros((1, LANES), F32)
    for c in range(SEQ // ML_L):
        mfull_ref[c * ML_L:(c + 1) * ML_L, :] = jnp.broadcast_to(m, (ML_L, LANES))
        last = (c + 1) * ML_L - 1
        m = bcum_c[last:last + 1, :] + jnp.maximum(m, cmx_c[last:last + 1, :])
    mfull = mfull_ref[...]
    mx = jnp.maximum(mfull, cmx_c)
    wint_ref[...] = jnp.exp(mfull - mx)
    floor_ref[...] = jnp.exp(-(bcum_c + mx))
    mfull_ref[...] = -mx

    lane_e = lax.broadcasted_iota(jnp.int32, (LANES, MLSTM_V_WIDTH), 1) // MLSTM_V_DIM
    row_e = lax.broadcasted_iota(jnp.int32, (LANES, MLSTM_V_WIDTH), 0)
    exp_bf = jnp.where(row_e == lane_e + ML_HL, 1.0, 0.0).astype(BF16)

    def expand_tile(i, carry):
        rows = pl.ds(pl.multiple_of(i * ML_TILE, ML_TILE), ML_TILE)
        ealpha_ref[rows, :] = _expand_heads(mfull_ref[rows, :], exp_bf)
        ew_ref[rows, :] = _expand_heads(wint_ref[rows, :], exp_bf)
        return carry

    lax.fori_loop(0, SEQ // ML_TILE, expand_tile, 0)

    kb_rowh = lax.broadcasted_iota(jnp.int32, (MLSTM_HEADS * ML_L, MLSTM_QK_WIDTH), 0) // ML_L
    kb_lane = lax.broadcasted_iota(jnp.int32, (MLSTM_HEADS * ML_L, MLSTM_QK_WIDTH), 1) // MLSTM_QK_DIM
    mask_k = jnp.where(kb_rowh == kb_lane, 1.0, 0.0).astype(BF16)
    va_rowh = lax.broadcasted_iota(jnp.int32, (MLSTM_HEADS * ML_L, ML_AUG), 0) // ML_L
    va_col = lax.broadcasted_iota(jnp.int32, (MLSTM_HEADS * ML_L, ML_AUG), 1)
    mask_v = jnp.where(
        (va_col // MLSTM_V_DIM == va_rowh) | (va_col == MLSTM_V_WIDTH + ML_HL + va_rowh),
        1.0, 0.0).astype(BF16)
    c_rowh = lax.broadcasted_iota(jnp.int32, (MLSTM_QK_WIDTH, ML_AUG), 0) // MLSTM_QK_DIM
    c_col = lax.broadcasted_iota(jnp.int32, (MLSTM_QK_WIDTH, ML_AUG), 1)
    mask_c = (c_col // MLSTM_V_DIM == c_rowh) | (c_col == MLSTM_V_WIDTH + ML_HL + c_rowh)
    lane128 = lax.broadcasted_iota(jnp.int32, (ML_L, LANES), 1)
    s_idx = lax.broadcasted_iota(jnp.int32, (ML_L, MLSTM_V_WIDTH), 1) % ML_L
    t_idx = lax.broadcasted_iota(jnp.int32, (ML_L, MLSTM_V_WIDTH), 0)
    causal = s_idx <= t_idx
    ones_aug = jnp.ones((ML_L, LANES), F32)

    caug_ref[...] = jnp.zeros_like(caug_ref)

    def pair_body(cp, m_row):
        r2 = rr_ref[cp]
        cm2 = cmr_ref[cp]
        b2 = bcr_ref[cp]
        r2r = pltpu.roll(r2, ML_L, axis=1)
        kt2 = kt_ref[cp]
        for cc in range(2):
            lo = cc * ML_L
            rows = pl.ds(pl.multiple_of(cp * (2 * ML_L), 2 * ML_L) + lo, ML_L)
            qa = qc_ref[rows, :]
            ka = kc_ref[rows, :]
            va_aug = jnp.concatenate([v_ref[0, rows, :], ones_aug], axis=1).astype(BF16)

            kbd = jnp.concatenate([ka] * MLSTM_HEADS, axis=0) * mask_k
            s = lax.dot_general(qa, kbd, NT_DIMS, preferred_element_type=F32)

            cmx_last = cm2[:, lo + ML_L - 1:lo + ML_L]
            b_last = b2[:, lo + ML_L - 1:lo + ML_L]
            mx_r = jnp.maximum(m_row, cmx_last)
            decay = jnp.exp(m_row - mx_r)
            wk = jnp.exp(r2[:, lo:lo + ML_L] - mx_r)
            m_row = b_last + mx_r

            src_e, src_o = (r2, r2r) if cc == 0 else (r2r, r2)
            cols = []
            for p in range(MLSTM_HEADS // 2):
                even = jnp.broadcast_to(src_e[2 * p:2 * p + 1, :], (ML_L, LANES))
                odd = jnp.broadcast_to(src_o[2 * p + 1:2 * p + 2, :], (ML_L, LANES))
                cols.append(jnp.where(lane128 < ML_L, even, odd))
            r_all = jnp.concatenate(cols, axis=1)
            arg = jnp.where(causal, ealpha_ref[rows, :] + r_all, -jnp.inf)
            p_all = (s * jnp.exp(arg)).astype(BF16)

            vbd = jnp.concatenate([va_aug] * MLSTM_HEADS, axis=0) * mask_v
            pv = jnp.dot(p_all, vbd, preferred_element_type=F32)
            qc_state = jnp.dot(qa, caug_ref[...].astype(BF16), preferred_element_type=F32)
            o_ref[0, rows, :] = (ew_ref[rows, :] * qc_state[:, :MLSTM_V_WIDTH]
                                 + pv[:, :MLSTM_V_WIDTH])
            den_ref[rows, :] = (wint_ref[rows, :] * qc_state[:, MLSTM_V_WIDTH:]
                                + pv[:, MLSTM_V_WIDTH:])

            wk_rows = jnp.concatenate(
                [jnp.broadcast_to(wk[h:h + 1, :], (MLSTM_QK_DIM, ML_L)) for h in range(MLSTM_HEADS)],
                axis=0)
            dec_rows = jnp.concatenate(
                [jnp.broadcast_to(decay[h:h + 1, :], (MLSTM_QK_DIM, 1)) for h in range(MLSTM_HEADS)],
                axis=0)
            ktw = (kt2[:, lo:lo + ML_L] * wk_rows).astype(BF16)
            upd = jnp.dot(ktw, va_aug, preferred_element_type=F32)
            caug_ref[...] = dec_rows * caug_ref[...] + jnp.where(mask_c, upd, 0.0)
        return m_row

    lax.fori_loop(0, ML_PAIRS, pair_body, jnp.zeros((MLSTM_HEADS, 1), F32), unroll=ML_PAIR_UNROLL)

    ob_row = lax.broadcasted_iota(jnp.int32, (MLSTM_V_WIDTH, LANES), 0) // MLSTM_V_DIM
    ob_col = lax.broadcasted_iota(jnp.int32, (MLSTM_V_WIDTH, LANES), 1)
    ones_bd = jnp.where(ob_col == ob_row + ML_HL, 1.0, 0.0).astype(BF16)

    def norm_tile(i, carry):
        rows = pl.ds(pl.multiple_of(i * ML_TILE, ML_TILE), ML_TILE)
        num = o_ref[0, rows, :]
        dn = jnp.maximum(jnp.abs(den_ref[rows, :]), floor_ref[rows, :])
        r = 1.0 / dn
        n2_hi, n2_lo = _split_hi_lo(num * num)
        msn = (jnp.dot(n2_hi, ones_bd, preferred_element_type=F32)
               + jnp.dot(n2_lo, ones_bd, preferred_element_type=F32)) * (1.0 / MLSTM_V_DIM)
        fac = r * lax.rsqrt(r * r * msn + NORM_EPS)
        o_ref[0, rows, :] = num * _expand_heads(fac, exp_bf)
        return carry

    lax.fori_loop(0, SEQ // ML_TILE, norm_tile, 0)


def _mlstm(projm3, gt, conv_w):
    b = projm3.shape[0]
    qk_blk = (1, SEQ, MLSTM_QK_WIDTH)
    v_blk = (1, SEQ, MLSTM_V_WIDTH)
    v_col = 2 * MLSTM_QK_WIDTH // MLSTM_V_WIDTH
    vmem = (2 * (2 * SEQ * MLSTM_QK_WIDTH * 4 + 2 * SEQ * MLSTM_V_WIDTH * 4 + 2 * SEQ * LANES * 4)
            + 2 * SEQ * MLSTM_QK_WIDTH * 2 + SEQ * MLSTM_QK_WIDTH * 4 + 2 * SEQ * MLSTM_V_WIDTH * 4
            + 3 * SEQ * LANES * 4 + (8 << 20))
    return pl.pallas_call(
        _mlstm_kernel,
        grid=(b,),
        in_specs=[
            pl.BlockSpec(qk_blk, lambda bi: (bi, 0, 0)),
            pl.BlockSpec(qk_blk, lambda bi: (bi, 0, 1)),
            pl.BlockSpec(v_blk, lambda bi: (bi, 0, v_col)),
            pl.BlockSpec((SUBLANES, SEQ), lambda bi: (1, bi)),
            pl.BlockSpec((SUBLANES, SEQ), lambda bi: (2, bi)),
            pl.BlockSpec((CONV_WIDTH, 2 * MLSTM_QK_WIDTH), lambda bi: (0, 0)),
        ],
        out_specs=pl.BlockSpec(v_blk, lambda bi: (bi, 0, 0)),
        out_shape=jax.ShapeDtypeStruct((b, SEQ, MLSTM_V_WIDTH), F32),
        scratch_shapes=[
            pltpu.VMEM((SEQ, LANES), F32),
            pltpu.VMEM((SEQ, MLSTM_QK_WIDTH), BF16),
            pltpu.VMEM((SEQ, MLSTM_QK_WIDTH), BF16),
            pltpu.VMEM((ML_PAIRS, MLSTM_QK_WIDTH, LANES), F32),
            pltpu.VMEM((ML_PAIRS, SUBLANES, LANES), F32),
            pltpu.VMEM((ML_PAIRS, SUBLANES, LANES), F32),
            pltpu.VMEM((ML_PAIRS, SUBLANES, LANES), F32),
            pltpu.VMEM((SEQ, MLSTM_V_WIDTH), F32),
            pltpu.VMEM((SEQ, MLSTM_V_WIDTH), F32),
            pltpu.VMEM((SEQ, LANES), F32),
            pltpu.VMEM((SEQ, LANES), F32),
            pltpu.VMEM((SEQ, LANES), F32),
            pltpu.VMEM((MLSTM_QK_WIDTH, ML_AUG), F32),
        ],
        compiler_params=pltpu.CompilerParams(
            dimension_semantics=("arbitrary",), vmem_limit_bytes=_vmem_limit(vmem)),
        name="mlstm",
    )(projm3, projm3, projm3, gt, gt, conv_w)


RT_TM = 1024
RT_PARTS = 1


def _out_route_kernel(x_ref, fy_ref, my_ref, mo_ref, wo_ref, mg_ref, nw_ref, wr_hi_ref, wr_lo_ref,
                      rb_ref, x1_ref, h2_ref, eidx_ref, gate_ref, rank_ref, cnt_ref, carry_ref):
    i = pl.program_id(0)

    @pl.when(i == 0)
    def _():
        carry_ref[...] = jnp.zeros_like(carry_ref)

    pm = RT_TM // RT_PARTS
    src = lax.broadcasted_iota(jnp.int32, (pm, pm), 0)
    dst = lax.broadcasted_iota(jnp.int32, (pm, pm), 1)
    upper = jnp.where(src < dst, 1.0, 0.0).astype(BF16)
    zi = jnp.zeros((SUBLANES - TOP_K, pm), jnp.int32)

    for part in range(RT_PARTS):
        rows = slice(part * pm, (part + 1) * pm)
        my = my_ref[rows, :] * mg_ref[...] / (1.0 + jnp.exp(-mo_ref[rows, :]))
        mixed = (jnp.dot(fy_ref[rows, :].astype(BF16), wo_ref[:FOX_WIDTH, :], preferred_element_type=F32)
                 + jnp.dot(my.astype(BF16), wo_ref[FOX_WIDTH:, :], preferred_element_type=F32))
        x1 = x_ref[rows, :] + mixed
        x1_ref[rows, :] = x1
        h2 = _rms(x1, nw_ref[...])
        _rows_to_tiles(h2, h2_ref.at[pl.ds(part * pm * TOK_ROWS, pm * TOK_ROWS), :])

        h_hi, h_lo = _split_hi_lo(h2)
        wr_hi, wr_lo = wr_hi_ref[...], wr_lo_ref[...]
        logit = (lax.dot_general(wr_hi, h_hi, NT_DIMS, preferred_element_type=F32)
                 + lax.dot_general(wr_lo, h_hi, NT_DIMS, preferred_element_type=F32)
                 + lax.dot_general(wr_hi, h_lo, NT_DIMS, preferred_element_type=F32)) + rb_ref[...]

        e_iota = lax.broadcasted_iota(jnp.int32, logit.shape, 0).astype(F32)
        vals, idxs, hots = [], [], []
        for _ in range(TOP_K):
            mk = jnp.max(logit, axis=0, keepdims=True)
            idx = jnp.min(jnp.where(logit == mk, e_iota, float(N_EXPERTS)), axis=0, keepdims=True)
            hot = e_iota == idx
            logit = jnp.where(hot, -jnp.inf, logit)
            vals.append(mk)
            idxs.append(idx.astype(jnp.int32))
            hots.append(hot)
        exps = [jnp.exp(v - vals[0]) for v in vals]
        tot = exps[0] + exps[1] + exps[2] + exps[3]
        gates = [e / tot for e in exps]

        assign = jnp.zeros(logit.shape, F32)
        for hot in hots:
            assign = assign + jnp.where(hot, 1.0, 0.0)
        base = jnp.dot(assign.astype(BF16), upper, preferred_element_type=F32) + carry_ref[:, 0:1]
        ranks = [jnp.sum(jnp.where(hot, base, 0.0), axis=0, keepdims=True) for hot in hots]
        carry_ref[...] = carry_ref[...] + jnp.sum(assign, axis=1, keepdims=True)

        eidx_ref[:, rows] = jnp.concatenate(idxs + [zi], axis=0)
        rank_ref[:, rows] = jnp.concatenate([r.astype(jnp.int32) for r in ranks] + [zi], axis=0)
        gate_ref[:, rows] = jnp.concatenate(gates + [zi.astype(F32)], axis=0)

    cnt_ref[...] = carry_ref[...]


def _out_route(x2d, fox_y2d, mlstm_y2d, projm, w_out_bf, mlstm_gain, moe_norm_w, wr_hi, wr_lo, rb):
    n = x2d.shape[0]
    tm = RT_TM
    const = lambda i: (0, 0)
    mo_col = (MLSTM_COLS - MLSTM_V_WIDTH) // MLSTM_V_WIDTH
    row_blk = lambda w: pl.BlockSpec((tm, w), lambda i: (i, 0))
    lane_blk = pl.BlockSpec((SUBLANES, tm), lambda i: (0, i))
    vmem = (2 * (tm * D_MODEL * 4 * 3 + tm * FOX_WIDTH * 4 * 3 + D_MODEL * D_MODEL * 2)
            + 6 * tm * D_MODEL * 4 + tm * tm * 6)
    return pl.pallas_call(
        _out_route_kernel,
        grid=(n // tm,),
        in_specs=[
            row_blk(D_MODEL), row_blk(FOX_WIDTH), row_blk(MLSTM_V_WIDTH),
            pl.BlockSpec((tm, MLSTM_V_WIDTH), lambda i: (i, mo_col)),
            pl.BlockSpec((D_MODEL, D_MODEL), const),
            pl.BlockSpec((1, MLSTM_V_WIDTH), const),
            pl.BlockSpec((1, D_MODEL), const),
            pl.BlockSpec((N_EXPERTS, D_MODEL), const),
            pl.BlockSpec((N_EXPERTS, D_MODEL), const),
            pl.BlockSpec((N_EXPERTS, 1), const),
        ],
        out_specs=[row_blk(D_MODEL), pl.BlockSpec((tm * TOK_ROWS, LANES), lambda i: (i, 0)),
                   lane_blk, lane_blk, lane_blk, pl.BlockSpec((N_EXPERTS, LANES), const)],
        out_shape=[
            jax.ShapeDtypeStruct((n, D_MODEL), F32),
            jax.ShapeDtypeStruct((n * TOK_ROWS, LANES), F32),
            jax.ShapeDtypeStruct((SUBLANES, n), jnp.int32),
            jax.ShapeDtypeStruct((SUBLANES, n), F32),
            jax.ShapeDtypeStruct((SUBLANES, n), jnp.int32),
            jax.ShapeDtypeStruct((N_EXPERTS, LANES), F32),
        ],
        scratch_shapes=[pltpu.VMEM((N_EXPERTS, LANES), F32)],
        compiler_params=pltpu.CompilerParams(
            dimension_semantics=("arbitrary",), vmem_limit_bytes=_vmem_limit(vmem)),
        name="out_route",
    )(x2d, fox_y2d, mlstm_y2d, projm, w_out_bf, mlstm_gain, moe_norm_w, wr_hi, wr_lo, rb)


INV_CHUNK = 8192
INV_UNROLL = 32


def _invert_kernel(pos_ref, zeros_hbm, inv_ref, sem):
    i = pl.program_id(0)

    @pl.when(i == 0)
    def _():
        cp = pltpu.make_async_copy(zeros_hbm, inv_ref, sem.at[0])
        cp.start()
        cp.wait()

    base = i * INV_CHUNK

    def body(j, carry):
        inv_ref[pos_ref[0, 0, j]] = base + j
        return carry

    lax.fori_loop(0, INV_CHUNK, body, 0, unroll=INV_UNROLL)


def _invert(pos_flat, n_rows):
    n_slots = pos_flat.shape[0]
    steps = n_slots // INV_CHUNK
    return pl.pallas_call(
        _invert_kernel,
        grid=(steps,),
        in_specs=[
            pl.BlockSpec((1, 1, INV_CHUNK), lambda i: (i, 0, 0), memory_space=pltpu.SMEM),
            pl.BlockSpec(memory_space=pl.ANY),
        ],
        out_specs=pl.BlockSpec(memory_space=pltpu.SMEM),
        out_shape=jax.ShapeDtypeStruct((n_rows,), jnp.int32),
        scratch_shapes=[pltpu.SemaphoreType.DMA((1,))],
        compiler_params=pltpu.CompilerParams(dimension_semantics=("arbitrary",)),
        name="invert",
    )(pos_flat.reshape(steps, 1, INV_CHUNK), jnp.zeros((n_rows,), jnp.int32))


EX_BM = 256
EX_DRAIN_STEPS = 2


def _experts_kernel(be_ref, nu_ref, nv_ref, tok_ref, tokn_ref, dst_ref, h2_hbm, wgu_ref, bgu_ref,
                    wd_ref, bd_ref, y_hbm, xt_ref, yt_ref, wgu_bf_ref, wd_bf_ref, gsem, ssem):
    i = pl.program_id(0)
    last_blk = pl.num_programs(0) - 1 - EX_DRAIN_STEPS
    nu = nu_ref[0]
    slot = i % 2
    cur = jnp.minimum(i, last_blk)
    tile_rows = EX_BM * TOK_ROWS

    def tok_tile(ref, idx):
        return ref.at[pl.ds(pl.multiple_of(idx, TOK_ROWS), TOK_ROWS), :]

    def start_gather(idx_ref, s, n):
        @pl.when(n == EX_BM)
        def _():
            for r in range(EX_BM):
                pltpu.make_async_copy(tok_tile(h2_hbm, idx_ref[0, 0, r]),
                                      xt_ref.at[s, pl.ds(r * TOK_ROWS, TOK_ROWS), :], gsem.at[s]).start()

        @pl.when(n < EX_BM)
        def _():
            def body(r, carry):
                pltpu.make_async_copy(
                    tok_tile(h2_hbm, idx_ref[0, 0, r]),
                    xt_ref.at[s, pl.ds(pl.multiple_of(r * TOK_ROWS, TOK_ROWS), TOK_ROWS), :],
                    gsem.at[s]).start()
                return carry
            lax.fori_loop(0, n, body, 0)

    def scatter_row(r):
        pltpu.make_async_copy(yt_ref.at[slot, pl.ds(r * TOK_ROWS, TOK_ROWS), :],
                              tok_tile(y_hbm, dst_ref[0, 0, r]), ssem.at[slot]).start()

    def wait_block(sem, buf):
        pltpu.make_async_copy(h2_hbm.at[pl.ds(0, tile_rows), :], buf, sem).wait()

    def wait_tokens(sem, buf, n):
        @pl.when(n == EX_BM)
        def _():
            wait_block(sem, buf)

        @pl.when(n < EX_BM)
        def _():
            def body(r, carry):
                pltpu.make_async_copy(h2_hbm.at[pl.ds(0, TOK_ROWS), :],
                                      buf.at[pl.ds(0, TOK_ROWS), :], sem).wait()
                return carry
            lax.fori_loop(0, n, body, 0)

    @pl.when(i == 0)
    def _():
        xt_ref[...] = jnp.zeros_like(xt_ref)
        start_gather(tok_ref, 0, nv_ref[0])

    @pl.when(i + 1 < nu)
    def _():
        start_gather(tokn_ref, 1 - slot, nv_ref[jnp.minimum(i + 1, last_blk)])

    @pl.when((i >= 2) & (i - 2 < nu))
    def _():
        wait_tokens(ssem.at[slot], yt_ref.at[slot], nv_ref[jnp.clip(i - 2, 0, last_blk)])

    @pl.when(i < nu)
    def _():
        @pl.when((i == 0) | (be_ref[cur] != be_ref[jnp.maximum(cur - 1, 0)]))
        def _():
            wgu_bf_ref[...] = wgu_ref[0].astype(BF16)
            wd_bf_ref[...] = wd_ref[0].astype(BF16)

        wait_tokens(gsem.at[slot], xt_ref.at[slot], nv_ref[cur])
        xb = _tiles_to_rows(xt_ref.at[slot], EX_BM).astype(BF16)
        gu = jnp.dot(xb, wgu_bf_ref[...], preferred_element_type=F32) + bgu_ref[0]
        gate = jnp.minimum(gu[:, :D_EXPERT], SWIGLU_LIMIT)
        up = jnp.clip(gu[:, D_EXPERT:], -SWIGLU_LIMIT, SWIGLU_LIMIT)
        act = (up + 1.0) * (gate / (1.0 + jnp.exp(-SWIGLU_ALPHA * gate)))
        y = jnp.dot(act.astype(BF16), wd_bf_ref[...], preferred_element_type=F32) + bd_ref[0]
        _rows_to_tiles(y, yt_ref.at[slot])
        nv = nv_ref[cur]

        @pl.when(nv == EX_BM)
        def _():
            for r in range(EX_BM):
                scatter_row(r)

        @pl.when(nv < EX_BM)
        def _():
            def body(r, carry):
                pltpu.make_async_copy(
                    yt_ref.at[slot, pl.ds(pl.multiple_of(r * TOK_ROWS, TOK_ROWS), TOK_ROWS), :],
                    tok_tile(y_hbm, dst_ref[0, 0, r]), ssem.at[slot]).start()
                return carry
            lax.fori_loop(0, nv, body, 0)


def _experts(block_e, n_used, n_valid, buf_tok3, dst3, h2, w_gu, b_gu, w_down, b_down, n_slots):
    nb = buf_tok3.shape[0]
    idx_blk = lambda f: pl.BlockSpec((1, 1, EX_BM), f, memory_space=pltpu.SMEM)
    vmem = (2 * (D_MODEL * 2 * D_EXPERT * 4 + D_EXPERT * D_MODEL * 4)
            + D_MODEL * 2 * D_EXPERT * 2 + D_EXPERT * D_MODEL * 2
            + 3 * EX_BM * D_MODEL * 4 + 3 * EX_BM * 2 * D_EXPERT * 4)
    blk = lambda i: jnp.minimum(i, nb - 1)
    w_map = lambda i, be, nu, nv: (be[blk(i)], 0, 0)
    grid_spec = pltpu.PrefetchScalarGridSpec(
        num_scalar_prefetch=3,
        grid=(nb + EX_DRAIN_STEPS,),
        in_specs=[
            idx_blk(lambda i, be, nu, nv: (blk(i), 0, 0)),
            idx_blk(lambda i, be, nu, nv: (blk(i + 1), 0, 0)),
            idx_blk(lambda i, be, nu, nv: (blk(i), 0, 0)),
            pl.BlockSpec(memory_space=pl.ANY),
            pl.BlockSpec((1, D_MODEL, 2 * D_EXPERT), w_map),
            pl.BlockSpec((1, 1, 2 * D_EXPERT), w_map),
            pl.BlockSpec((1, D_EXPERT, D_MODEL), w_map),
            pl.BlockSpec((1, 1, D_MODEL), w_map),
        ],
        out_specs=pl.BlockSpec(memory_space=pl.ANY),
        scratch_shapes=[
            pltpu.VMEM((2, EX_BM * TOK_ROWS, LANES), F32),
            pltpu.VMEM((2, EX_BM * TOK_ROWS, LANES), F32),
            pltpu.VMEM((D_MODEL, 2 * D_EXPERT), BF16),
            pltpu.VMEM((D_EXPERT, D_MODEL), BF16),
            pltpu.SemaphoreType.DMA((2,)),
            pltpu.SemaphoreType.DMA((2,)),
        ],
    )
    return pl.pallas_call(
        _experts_kernel,
        grid_spec=grid_spec,
        out_shape=jax.ShapeDtypeStruct((n_slots * TOK_ROWS, LANES), F32),
        compiler_params=pltpu.CompilerParams(
            dimension_semantics=("arbitrary",), vmem_limit_bytes=_vmem_limit(vmem)),
        name="experts",
    )(block_e, n_used, n_valid, buf_tok3, buf_tok3, dst3, h2, w_gu, b_gu[:, None, :], w_down,
      b_down[:, None, :])


CB_TM = 512


def _combine_kernel(x1_ref, y0_ref, y1_ref, y2_ref, y3_ref, gate_ref, o_ref):
    tm = x1_ref.shape[0]
    g = jnp.concatenate([gate_ref[...], jnp.zeros((LANES - SUBLANES, tm), F32)], axis=0).T
    acc = x1_ref[...]
    for k, y_ref in enumerate((y0_ref, y1_ref, y2_ref, y3_ref)):
        acc = acc + g[:, k:k + 1] * _tiles_to_rows(y_ref, tm)
    o_ref[...] = acc


def _combine(x1, y_slots, gates):
    n = x1.shape[0]
    tm = CB_TM
    nt = n // tm
    vmem = 2 * (2 * tm * D_MODEL * 4 + tm * TOP_K * D_MODEL * 4) + 4 * tm * D_MODEL * 4
    y_spec = lambda k: pl.BlockSpec((tm * TOK_ROWS, LANES), lambda i: (k * nt + i, 0))
    return pl.pallas_call(
        _combine_kernel,
        grid=(nt,),
        in_specs=[pl.BlockSpec((tm, D_MODEL), lambda i: (i, 0))]
        + [y_spec(k) for k in range(TOP_K)]
        + [pl.BlockSpec((SUBLANES, tm), lambda i: (0, i))],
        out_specs=pl.BlockSpec((tm, D_MODEL), lambda i: (i, 0)),
        out_shape=jax.ShapeDtypeStruct((n, D_MODEL), F32),
        compiler_params=pltpu.CompilerParams(
            dimension_semantics=("arbitrary",), vmem_limit_bytes=_vmem_limit(vmem)),
        name="combine",
    )(x1, y_slots, y_slots, y_slots, y_slots, gates)


def _dispatch_plan(eidx, rank, counts):
    n = eidx.shape[1]
    n_slots = n * TOP_K
    nb = n_slots // EX_BM + N_EXPERTS
    counts = counts.astype(jnp.int32)
    padded = ((counts + EX_BM - 1) // EX_BM) * EX_BM
    pad_end = jnp.cumsum(padded)
    pad_start = pad_end - padded
    e = eidx[:TOP_K]
    start_of = jnp.sum(jnp.where(e[:, :, None] == jnp.arange(N_EXPERTS, dtype=jnp.int32),
                                 pad_start[None, None, :], 0), axis=-1)
    pos = start_of + rank[:TOP_K]
    inv = _invert(pos.reshape(-1), nb * EX_BM)
    buf_tok = (inv % n) * TOK_ROWS
    dst = inv * TOK_ROWS
    blk_start = jnp.arange(nb, dtype=jnp.int32) * EX_BM
    block_e = jnp.minimum(jnp.sum((pad_end[None, :] <= blk_start[:, None]).astype(jnp.int32), axis=1),
                          N_EXPERTS - 1)
    n_used = (pad_end[-1] // EX_BM).astype(jnp.int32).reshape(1)
    n_valid = jnp.clip(pad_start[block_e] + counts[block_e] - blk_start, 0, EX_BM).astype(jnp.int32)
    n_valid = jnp.where(blk_start < pad_end[-1], n_valid, 0)
    return (block_e, n_used, n_valid, buf_tok.reshape(nb, 1, EX_BM), dst.reshape(nb, 1, EX_BM), n_slots)


def _prep_in_proj_weights(w_in, fox_f_bias, mlstm_i_bias, mlstm_f_bias):
    split_at = []
    acc = 0
    for wdt in SPLIT_WIDTHS[:-1]:
        acc += wdt
        split_at.append(acc)
    starts = [0] + split_at
    gate_cols = [w_in[:, s:s + wdt] for s, wdt, is_main in zip(starts, SPLIT_WIDTHS, IN_W_IS_MAIN)
                 if not is_main]
    w_gate = jnp.concatenate(gate_cols, axis=-1)
    bias = jnp.concatenate([fox_f_bias, mlstm_i_bias, mlstm_f_bias]).astype(F32)
    n_gate = w_gate.shape[1]
    wg_hi, wg_lo = _split_hi_lo(jnp.pad(w_gate, ((0, 0), (0, LANES - n_gate))))
    bias_r = jnp.pad(bias, (0, LANES - n_gate))[None, :]
    return w_in, wg_hi, wg_lo, bias_r


def kernel(x, attn_norm_w, w_in, fox_f_bias, fox_q_norm_w, fox_k_norm_w, fox_out_norm_w, mlstm_conv_w, mlstm_i_bias, mlstm_f_bias, mlstm_out_norm_w, w_out, moe_norm_w, router_w, router_b, expert_w_gate_up, expert_b_gate_up, expert_w_down, expert_b_down):
    bsz, seq, d = x.shape
    x2d = x.reshape(bsz * seq, d)
    prep = _prep_in_proj_weights(w_in[0], fox_f_bias[0], mlstm_i_bias[0], mlstm_f_bias[0])
    pair = lambda w: jnp.tile(w, LANES // HEAD_DIM)[None, :]
    qn, kn, aq, ak, vb, projm, gt = _in_proj(x2d, attn_norm_w[0][None, :], *prep,
                                             pair(fox_q_norm_w[0]), pair(fox_k_norm_w[0]))
    b3 = lambda a: a.reshape(bsz, seq, a.shape[-1])
    fox_y = _fox(b3(qn), b3(kn), b3(aq), b3(ak), b3(vb), fox_out_norm_w[0][None, :])
    mlstm_y = _mlstm(b3(projm), gt, mlstm_conv_w[0])
    return _channel_mixer(x2d, fox_y.reshape(-1, FOX_WIDTH), mlstm_y.reshape(-1, MLSTM_V_WIDTH), projm,
                          mlstm_out_norm_w[0], w_out[0], moe_norm_w[0], router_w[0], router_b[0],
                          expert_w_gate_up[0], expert_b_gate_up[0], expert_w_down[0],
                          expert_b_down[0]).reshape(bsz, seq, d)


def _channel_mixer(x2d, fox_y2d, mlstm_y2d, projm, mlstm_gain, w_out, moe_norm_w, router_w, router_b,
                   w_gu, b_gu, w_down, b_down):
    wr_hi, wr_lo = _split_hi_lo(router_w.T)
    x1, h2, eidx, gates, rank, counts = _out_route(
        x2d, fox_y2d, mlstm_y2d, projm, w_out.astype(BF16), mlstm_gain[None, :], moe_norm_w[None, :],
        wr_hi, wr_lo, router_b[:, None])
    block_e, n_used, n_valid, buf_tok3, dst3, n_slots = _dispatch_plan(eidx, rank, counts[:, 0])
    y_slots = _experts(block_e, n_used, n_valid, buf_tok3, dst3, h2, w_gu, b_gu, w_down, b_down, n_slots)
    return _combine(x1, y_slots, gates)
```

```python
import math

import jax
import jax.numpy as jnp
from jax import lax
from jax.experimental import pallas as pl
from jax.experimental.pallas import tpu as pltpu

F32 = jnp.float32
BF16 = jnp.bfloat16

D_MODEL = 1024
SEQ = 2048
HEAD_DIM = 64
FOX_HEADS = 8
FOX_WIDTH = FOX_HEADS * HEAD_DIM
MLSTM_HEADS = 8
MLSTM_QK_DIM = 32
MLSTM_V_DIM = 64
MLSTM_QK_WIDTH = MLSTM_HEADS * MLSTM_QK_DIM
MLSTM_V_WIDTH = MLSTM_HEADS * MLSTM_V_DIM
CONV_WIDTH = 4
MLSTM_CHUNK = 64
SPLIT_WIDTHS = (FOX_WIDTH, FOX_WIDTH, FOX_WIDTH, FOX_HEADS,
                MLSTM_QK_WIDTH, MLSTM_QK_WIDTH, MLSTM_V_WIDTH,
                MLSTM_HEADS, MLSTM_HEADS, MLSTM_V_WIDTH)
N_EXPERTS = 32
TOP_K = 4
D_EXPERT = D_MODEL
SWIGLU_ALPHA = 1.702
SWIGLU_LIMIT = 7.0
NORM_EPS = 1e-5
LOG2E = 1.4426950408889634

LANES = 128
SUBLANES = 8
V7X_VMEM_BYTES = 64 * 1024 * 1024

MAIN_WIDTH = 3 * FOX_WIDTH + 2 * MLSTM_QK_WIDTH + 2 * MLSTM_V_WIDTH
MLSTM_COLS = MAIN_WIDTH - 3 * FOX_WIDTH
GATE_ROWS = 32

NT_DIMS = (((1,), (1,)), ((), ()))


VMEM_TEMPORARIES_BYTES = 8 << 20
VMEM_RESERVED_BYTES = 4 << 20


def _vmem_limit(block_bytes):
    return int(min(block_bytes + VMEM_TEMPORARIES_BYTES, V7X_VMEM_BYTES - VMEM_RESERVED_BYTES))


def _log_sigmoid(x):
    return jnp.minimum(x, 0.0) - jnp.log(1.0 + jnp.exp(-jnp.abs(x)))


def _split_hi_lo(x):
    hi = x.astype(BF16)
    lo = (x - hi.astype(F32)).astype(BF16)
    return hi, lo


def _rms(x, w):
    return x * lax.rsqrt(jnp.mean(x * x, axis=-1, keepdims=True) + NORM_EPS) * w


TOK_ROWS = D_MODEL // LANES


def _rows_to_tiles(x, tile_ref):
    m = x.shape[0]
    for j in range(TOK_ROWS):
        tile_ref[pl.ds(j, m, stride=TOK_ROWS), :] = x[:, j * LANES:(j + 1) * LANES]


def _tiles_to_rows(tile_ref, m):
    return jnp.concatenate(
        [tile_ref[pl.ds(j, m, stride=TOK_ROWS), :] for j in range(TOK_ROWS)], axis=1)


IN_TM = 1024
IN_TILES_PER_SEQ = SEQ // IN_TM
IN_PARTS = 2
IN_PM = IN_TM // IN_PARTS
IN_SCAN_SHIFTS = tuple(1 << i for i in range(int(math.log2(IN_PM))))
IN_W_ROWS = 256
IN_W_WIDTH = sum(SPLIT_WIDTHS)
IN_W_IS_MAIN = (True, True, True, False, True, True, True, False, False, True)


def _main_segments():
    segs, src, dst = [], 0, 0
    for width, is_main in zip(SPLIT_WIDTHS, IN_W_IS_MAIN):
        if is_main:
            if segs and segs[-1][0] + segs[-1][2] == src:
                segs[-1] = (segs[-1][0], segs[-1][1], segs[-1][2] + width)
            else:
                segs.append((src, dst, width))
            dst += width
        src += width
    return tuple(segs)


IN_MAIN_SEGMENTS = _main_segments()


def _pair_rms(x, w, lo_half):
    sq = x * x
    ms_lo = jnp.sum(jnp.where(lo_half, sq, 0.0), axis=-1, keepdims=True) * (1.0 / HEAD_DIM)
    ms_hi = jnp.sum(jnp.where(lo_half, 0.0, sq), axis=-1, keepdims=True) * (1.0 / HEAD_DIM)
    inv = jnp.where(lo_half, lax.rsqrt(ms_lo + NORM_EPS), lax.rsqrt(ms_hi + NORM_EPS))
    return x * inv * w


def _in_proj_kernel(x_ref, nw_ref, w_hbm, wg_hi_ref, wg_lo_ref, br_ref, qw_ref, kw_ref,
                    qn_ref, kn_ref, aq_ref, ak_ref, vb_ref, projm_ref, gt_ref, carry_ref,
                    w_ref, wstage_ref, wsem):
    i = pl.program_id(0)

    @pl.when(i == 0)
    def _():
        for c in range(D_MODEL // IN_W_ROWS):
            rows = slice(c * IN_W_ROWS, (c + 1) * IN_W_ROWS)
            copy = pltpu.make_async_copy(w_hbm.at[rows, :], wstage_ref, wsem)
            copy.start()
            copy.wait()
            for src, dst, width in IN_MAIN_SEGMENTS:
                w_ref[rows, dst:dst + width] = wstage_ref[:, src:src + width].astype(BF16)

    @pl.when(i % IN_TILES_PER_SEQ == 0)
    def _():
        carry_ref[...] = jnp.zeros_like(carry_ref)

    for part in range(IN_PARTS):
        rows = slice(part * IN_PM, (part + 1) * IN_PM)
        x = x_ref[rows, :]
        ms = jnp.mean(x * x, axis=-1, keepdims=True)
        y = x * lax.rsqrt(ms + NORM_EPS) * nw_ref[...]
        h_hi, h_lo = _split_hi_lo(y)
        main = jnp.dot(h_hi, w_ref[...], preferred_element_type=F32)

        w_hi, w_lo = wg_hi_ref[...], wg_lo_ref[...]
        g = (jnp.dot(h_hi, w_hi, preferred_element_type=F32)
             + jnp.dot(h_hi, w_lo, preferred_element_type=F32)
             + jnp.dot(h_lo, w_hi, preferred_element_type=F32))
        g = g + br_ref[...]
        lane = lax.broadcasted_iota(jnp.int32, g.shape, 1)
        is_input_gate = (lane >= FOX_HEADS) & (lane < FOX_HEADS + MLSTM_HEADS)
        gates = jnp.where(is_input_gate, g, _log_sigmoid(g))
        gt_ref[:, rows] = gates.T[:GATE_ROWS, :]

        rowi = lax.broadcasted_iota(jnp.int32, gates.shape, 0)
        c = gates
        for s in IN_SCAN_SHIFTS:
            c = c + jnp.where(rowi >= s, pltpu.roll(c, s, axis=0), 0.0)
        c = c + carry_ref[...]
        carry_ref[...] = c[IN_PM - 1:IN_PM, :]
        cum2 = c * LOG2E

        c8 = jnp.where(lane < FOX_HEADS, cum2, 0.0)
        c_hi = c8.astype(BF16).astype(F32)
        r1 = c8 - c_hi
        c_mid = r1.astype(BF16).astype(F32)
        c_lo = (r1 - c_mid).astype(BF16).astype(F32)
        ones_q = jnp.where((lane >= 3 * FOX_HEADS) & (lane < 6 * FOX_HEADS), 1.0, 0.0)
        ones_k = jnp.where(lane < 3 * FOX_HEADS, 1.0, 0.0)
        aq = (c_hi + pltpu.roll(c_mid, FOX_HEADS, axis=1) + pltpu.roll(c_lo, 2 * FOX_HEADS, axis=1)
              + ones_q)
        ak = ones_k - (pltpu.roll(c_hi, 3 * FOX_HEADS, axis=1) + pltpu.roll(c_mid, 4 * FOX_HEADS, axis=1)
                       + pltpu.roll(c_lo, 5 * FOX_HEADS, axis=1))
        aq_ref[rows, :] = aq.astype(BF16)
        ak_ref[rows, :] = ak.astype(BF16)

        lo_half = lane < HEAD_DIM
        q_scale = (HEAD_DIM ** -0.5) * LOG2E
        for p in range(FOX_HEADS // 2):
            ps = slice(p * LANES, (p + 1) * LANES)
            qn_ref[rows, ps] = (_pair_rms(main[:, ps], qw_ref[...], lo_half) * q_scale).astype(BF16)
            kn_ref[rows, ps] = _pair_rms(main[:, FOX_WIDTH + p * LANES:FOX_WIDTH + (p + 1) * LANES],
                                         kw_ref[...], lo_half).astype(BF16)

        vb_ref[rows, :] = main[:, 2 * FOX_WIDTH:3 * FOX_WIDTH].astype(BF16)
        projm_ref[rows, :] = main[:, 3 * FOX_WIDTH:]


def _in_proj(x2d, norm_w, w_in, wg_hi, wg_lo, bias_r, qw, kw):
    n = x2d.shape[0]
    tm = IN_TM
    assert w_in.shape == (D_MODEL, IN_W_WIDTH) and w_in.dtype == F32
    const = lambda i: (0, 0)
    row = lambda w: pl.BlockSpec((tm, w), lambda i: (i, 0))
    vmem = (2 * (tm * D_MODEL * 4 + 3 * tm * FOX_WIDTH * 2 + 2 * tm * LANES * 2
                 + tm * MLSTM_COLS * 4 + GATE_ROWS * tm * 4 + tm * LANES * 4)
            + D_MODEL * MAIN_WIDTH * 2 + IN_W_ROWS * (IN_W_WIDTH + LANES) * 4
            + 2 * tm * MAIN_WIDTH * 4)
    return pl.pallas_call(
        _in_proj_kernel,
        grid=(n // tm,),
        in_specs=[
            row(D_MODEL),
            pl.BlockSpec((1, D_MODEL), const),
            pl.BlockSpec(memory_space=pl.ANY),
            pl.BlockSpec((D_MODEL, LANES), const),
            pl.BlockSpec((D_MODEL, LANES), const),
            pl.BlockSpec((1, LANES), const),
            pl.BlockSpec((1, LANES), const),
            pl.BlockSpec((1, LANES), const),
        ],
        out_specs=[
            row(FOX_WIDTH), row(FOX_WIDTH), row(LANES), row(LANES), row(FOX_WIDTH), row(MLSTM_COLS),
            pl.BlockSpec((GATE_ROWS, tm), lambda i: (0, i)),
        ],
        out_shape=[
            jax.ShapeDtypeStruct((n, FOX_WIDTH), BF16),
            jax.ShapeDtypeStruct((n, FOX_WIDTH), BF16),
            jax.ShapeDtypeStruct((n, LANES), BF16),
            jax.ShapeDtypeStruct((n, LANES), BF16),
            jax.ShapeDtypeStruct((n, FOX_WIDTH), BF16),
            jax.ShapeDtypeStruct((n, MLSTM_COLS), F32),
            jax.ShapeDtypeStruct((GATE_ROWS, n), F32),
        ],
        scratch_shapes=[
            pltpu.VMEM((1, LANES), F32),
            pltpu.VMEM((D_MODEL, MAIN_WIDTH), BF16),
            pltpu.VMEM((IN_W_ROWS, IN_W_WIDTH), F32),
            pltpu.SemaphoreType.DMA(()),
        ],
        compiler_params=pltpu.CompilerParams(
            dimension_semantics=("arbitrary",), vmem_limit_bytes=_vmem_limit(vmem)),
        name="in_proj",
    )(x2d, norm_w, w_in, wg_hi, wg_lo, bias_r, qw, kw)


FOX_TQ = 512
FOX_PAIRS = 4


def _fox_kernel(qn_ref, kn_ref, aq_ref, ak_ref, v_ref, ow_ref, o_ref, qa_ref, ka_ref, va_ref):
    step = pl.program_id(1)
    tri = (lax.broadcasted_iota(jnp.int32, (FOX_TQ, FOX_TQ), 1)
           <= lax.broadcasted_iota(jnp.int32, (FOX_TQ, FOX_TQ), 0))
    lo_q = lax.broadcasted_iota(jnp.int32, (FOX_TQ, LANES), 1) < HEAD_DIM
    lane = lax.broadcasted_iota(jnp.int32, (1, LANES), 1)
    sum_lanes = (HEAD_DIM, 0)
    for pp in range(FOX_PAIRS):
        ps = slice(pp * LANES, (pp + 1) * LANES)
        ka_ref[pp, :, :LANES] = kn_ref[0, :, ps]
        ka_ref[pp, :, LANES:] = ak_ref[0]
        for j in range(2):
            h = 2 * (step * FOX_PAIRS + pp) + j
            own_mask = jnp.where((lane < HEAD_DIM) if j == 0 else (lane >= HEAD_DIM), 1.0, 0.0).astype(BF16)
            bias_mask = jnp.where((lane < 6 * FOX_HEADS) & (lane % FOX_HEADS == h), 1.0, 0.0).astype(BF16)
            qa_ref[2 * pp + j, :, :LANES] = qn_ref[0, :, ps] * own_mask
            qa_ref[2 * pp + j, :, LANES:] = aq_ref[0] * bias_mask
            va_ref[2 * pp + j] = (v_ref[0, :, ps] * own_mask
                                  + jnp.where(lane == sum_lanes[j], 1.0, 0.0).astype(BF16))

    for pp in range(FOX_PAIRS):
        ps = slice(pp * LANES, (pp + 1) * LANES)
        for i in range(SEQ // FOX_TQ):
            qs = slice(i * FOX_TQ, (i + 1) * FOX_TQ)
            n = (i + 1) * FOX_TQ
            normed = []
            for j in range(2):
                own_q = lo_q if j == 0 else jnp.logical_not(lo_q)
                s = lax.dot_general(qa_ref[2 * pp + j, qs, :], ka_ref[pp, :n, :], NT_DIMS,
                                    preferred_element_type=F32)
                diag = jnp.where(tri, s[:, n - FOX_TQ:], -jnp.inf)
                m = jnp.max(diag, axis=-1, keepdims=True)
                if i > 0:
                    past = s[:, :n - FOX_TQ]
                    m = jnp.maximum(m, jnp.max(past, axis=-1, keepdims=True))
                    p = jnp.concatenate([jnp.exp2(past - m), jnp.exp2(diag - m)], axis=1)
                else:
                    p = jnp.exp2(diag - m)
                o = jnp.dot(p.astype(BF16), va_ref[2 * pp + j, :n, :],
                            preferred_element_type=F32)
                o = o / o[:, sum_lanes[j]:sum_lanes[j] + 1]
                ms = jnp.sum(jnp.where(own_q, o * o, 0.0), axis=-1, keepdims=True) * (1.0 / HEAD_DIM)
                normed.append(o * lax.rsqrt(ms + NORM_EPS))
            o_ref[0, qs, ps] = jnp.where(lo_q, normed[0], normed[1]) * ow_ref[:, ps]


def _fox(qn3, kn3, aq3, ak3, v3, ow):
    b = qn3.shape[0]
    width = FOX_PAIRS * LANES
    steps = FOX_WIDTH // width
    pairs = pl.BlockSpec((1, SEQ, width), lambda bi, st: (bi, 0, st))
    shared = pl.BlockSpec((1, SEQ, LANES), lambda bi, st: (bi, 0, 0))
    vmem = (2 * (3 * SEQ * width * 2 + 2 * SEQ * LANES * 2 + SEQ * width * 4)
            + FOX_PAIRS * SEQ * LANES * 2 * 8 + 8 * FOX_TQ * SEQ * 4)
    return pl.pallas_call(
        _fox_kernel,
        grid=(b, steps),
        in_specs=[pairs, pairs, shared, shared, pairs, pl.BlockSpec((1, width), lambda bi, st: (0, st))],
        out_specs=pairs,
        out_shape=jax.ShapeDtypeStruct((b, SEQ, FOX_WIDTH), F32),
        scratch_shapes=[pltpu.VMEM((2 * FOX_PAIRS, SEQ, 2 * LANES), BF16),
                        pltpu.VMEM((FOX_PAIRS, SEQ, 2 * LANES), BF16),
                        pltpu.VMEM((2 * FOX_PAIRS, SEQ, LANES), BF16)],
        compiler_params=pltpu.CompilerParams(
            dimension_semantics=("arbitrary", "arbitrary"), vmem_limit_bytes=_vmem_limit(vmem)),
        name="fox",
    )(qn3, kn3, aq3, ak3, v3, ow)


ML_L = MLSTM_CHUNK
ML_PAIRS = SEQ // (2 * ML_L)
ML_HL = FOX_HEADS
ML_AUG = MLSTM_V_WIDTH + LANES
ML_TILE = 256
ML_PAIR_UNROLL = 4
SEG_SHIFTS = tuple(1 << i for i in range(int(math.log2(ML_L))))


def _split3(x):
    a = x.astype(BF16)
    r = x - a.astype(F32)
    b = r.astype(BF16)
    c = (r - b.astype(F32)).astype(BF16)
    return a, b, c


def _expand_heads(x, exp_bf):
    a, b, c = _split3(x)
    return (jnp.dot(a, exp_bf, preferred_element_type=F32)
            + jnp.dot(b, exp_bf, preferred_element_type=F32)
            + jnp.dot(c, exp_bf, preferred_element_type=F32))


def _seg_scan(x, axis, op, ident):
    idx = lax.broadcasted_iota(jnp.int32, x.shape, axis) % ML_L
    for s in SEG_SHIFTS:
        x = op(x, jnp.where(idx >= s, pltpu.roll(x, s, axis=axis), ident))
    return x


def _mlstm_kernel(q_ref, k_ref, v_ref, gi_ref, gf_ref, cw_ref, o_ref, den_ref,
                  qc_ref, kc_ref, kt_ref, rr_ref, cmr_ref, bcr_ref,
                  ealpha_ref, ew_ref, wint_ref, floor_ref, mfull_ref, caug_ref):
    def conv_silu(u, w):
        rowi = lax.broadcasted_iota(jnp.int32, u.shape, 0)
        acc = u * w[CONV_WIDTH - 1:CONV_WIDTH, :]
        for d in range(1, CONV_WIDTH):
            sh = jnp.where(rowi >= d, pltpu.roll(u, d, axis=0), 0.0)
            acc = acc + sh * w[CONV_WIDTH - 1 - d:CONV_WIDTH - d, :]
        return acc / (1.0 + jnp.exp(-acc))

    cw = cw_ref[...]
    qc_ref[...] = conv_silu(q_ref[0], cw[:, :MLSTM_QK_WIDTH]).astype(BF16)
    kc = conv_silu(k_ref[0], cw[:, MLSTM_QK_WIDTH:]) * (MLSTM_QK_DIM ** -0.5)
    kc_ref[...] = kc.astype(BF16)
    kt = kc.T
    for p in range(ML_PAIRS):
        kt_ref[p] = kt[:, p * LANES:(p + 1) * LANES]

    bcum_r = _seg_scan(gf_ref[...], 1, jnp.add, 0.0)
    r_r = gi_ref[...] - bcum_r
    cmx_r = _seg_scan(r_r, 1, jnp.maximum, -jnp.inf)
    for p in range(ML_PAIRS):
        ls = slice(p * LANES, (p + 1) * LANES)
        rr_ref[p] = r_r[:, ls]
        cmr_ref[p] = cmx_r[:, ls]
        bcr_ref[p] = bcum_r[:, ls]

    def to_columns(rows8):
        pad_lo = jnp.zeros((ML_HL, SEQ), F32)
        pad_hi = jnp.zeros((LANES - ML_HL - MLSTM_HEADS, SEQ), F32)
        return jnp.concatenate([pad_lo, rows8, pad_hi], axis=0).T

    bcum_c = to_columns(bcum_r)
    cmx_c = to_columns(cmx_r)
    m = jnp.zeros((1, LANES), F32)
    for c in range(SEQ // ML_L):
        mfull_ref[c * ML_L:(c + 1) * ML_L, :] = jnp.broadcast_to(m, (ML_L, LANES))
        last = (c + 1) * ML_L - 1
        m = bcum_c[last:last + 1, :] + jnp.maximum(m, cmx_c[last:last + 1, :])
    mfull = mfull_ref[...]
    mx = jnp.maximum(mfull, cmx_c)
    wint_ref[...] = jnp.exp(mfull - mx)
    floor_ref[...] = jnp.exp(-(bcum_c + mx))
    mfull_ref[...] = -mx

    lane_e = lax.broadcasted_iota(jnp.int32, (LANES, MLSTM_V_WIDTH), 1) // MLSTM_V_DIM
    row_e = lax.broadcasted_iota(jnp.int32, (LANES, MLSTM_V_WIDTH), 0)
    exp_bf = jnp.where(row_e == lane_e + ML_HL, 1.0, 0.0).astype(BF16)

    def expand_tile(i, carry):
        rows = pl.ds(pl.multiple_of(i * ML_TILE, ML_TILE), ML_TILE)
        ealpha_ref[rows, :] = _expand_heads(mfull_ref[rows, :], exp_bf)
        ew_ref[rows, :] = _expand_heads(wint_ref[rows, :], exp_bf)
        return carry

    lax.fori_loop(0, SEQ // ML_TILE, expand_tile, 0)

    kb_rowh = lax.broadcasted_iota(jnp.int32, (MLSTM_HEADS * ML_L, MLSTM_QK_WIDTH), 0) // ML_L
    kb_lane = lax.broadcasted_iota(jnp.int32, (MLSTM_HEADS * ML_L, MLSTM_QK_WIDTH), 1) // MLSTM_QK_DIM
    mask_k = jnp.where(kb_rowh == kb_lane, 1.0, 0.0).astype(BF16)
    va_rowh = lax.broadcasted_iota(jnp.int32, (MLSTM_HEADS * ML_L, ML_AUG), 0) // ML_L
    va_col = lax.broadcasted_iota(jnp.int32, (MLSTM_HEADS * ML_L, ML_AUG), 1)
    mask_v = jnp.where(
        (va_col // MLSTM_V_DIM == va_rowh) | (va_col == MLSTM_V_WIDTH + ML_HL + va_rowh),
        1.0, 0.0).astype(BF16)
    c_rowh = lax.broadcasted_iota(jnp.int32, (MLSTM_QK_WIDTH, ML_AUG), 0) // MLSTM_QK_DIM
    c_col = lax.broadcasted_iota(jnp.int32, (MLSTM_QK_WIDTH, ML_AUG), 1)
    mask_c = (c_col // MLSTM_V_DIM == c_rowh) | (c_col == MLSTM_V_WIDTH + ML_HL + c_rowh)
    lane128 = lax.broadcasted_iota(jnp.int32, (ML_L, LANES), 1)
    s_idx = lax.broadcasted_iota(jnp.int32, (ML_L, MLSTM_V_WIDTH), 1) % ML_L
    t_idx = lax.broadcasted_iota(jnp.int32, (ML_L, MLSTM_V_WIDTH), 0)
    causal = s_idx <= t_idx
    ones_aug = jnp.ones((ML_L, LANES), F32)

    caug_ref[...] = jnp.zeros_like(caug_ref)

    def pair_body(cp, m_row):
        r2 = rr_ref[cp]
        cm2 = cmr_ref[cp]
        b2 = bcr_ref[cp]
        r2r = pltpu.roll(r2, ML_L, axis=1)
        kt2 = kt_ref[cp]
        for cc in range(2):
            lo = cc * ML_L
            rows = pl.ds(pl.multiple_of(cp * (2 * ML_L), 2 * ML_L) + lo, ML_L)
            qa = qc_ref[rows, :]
            ka = kc_ref[rows, :]
            va_aug = jnp.concatenate([v_ref[0, rows, :], ones_aug], axis=1).astype(BF16)

            kbd = jnp.concatenate([ka] * MLSTM_HEADS, axis=0) * mask_k
            s = lax.dot_general(qa, kbd, NT_DIMS, preferred_element_type=F32)

            cmx_last = cm2[:, lo + ML_L - 1:lo + ML_L]
            b_last = b2[:, lo + ML_L - 1:lo + ML_L]
            mx_r = jnp.maximum(m_row, cmx_last)
            decay = jnp.exp(m_row - mx_r)
            wk = jnp.exp(r2[:, lo:lo + ML_L] - mx_r)
            m_row = b_last + mx_r

            src_e, src_o = (r2, r2r) if cc == 0 else (r2r, r2)
            cols = []
            for p in range(MLSTM_HEADS // 2):
                even = jnp.broadcast_to(src_e[2 * p:2 * p + 1, :], (ML_L, LANES))
                odd = jnp.broadcast_to(src_o[2 * p + 1:2 * p + 2, :], (ML_L, LANES))
                cols.append(jnp.where(lane128 < ML_L, even, odd))
            r_all = jnp.concatenate(cols, axis=1)
            arg = jnp.where(causal, ealpha_ref[rows, :] + r_all, -jnp.inf)
            p_all = (s * jnp.exp(arg)).astype(BF16)

            vbd = jnp.concatenate([va_aug] * MLSTM_HEADS, axis=0) * mask_v
            pv = jnp.dot(p_all, vbd, preferred_element_type=F32)
            qc_state = jnp.dot(qa, caug_ref[...].astype(BF16), preferred_element_type=F32)
            o_ref[0, rows, :] = (ew_ref[rows, :] * qc_state[:, :MLSTM_V_WIDTH]
                                 + pv[:, :MLSTM_V_WIDTH])
            den_ref[rows, :] = (wint_ref[rows, :] * qc_state[:, MLSTM_V_WIDTH:]
                                + pv[:, MLSTM_V_WIDTH:])

            wk_rows = jnp.concatenate(
                [jnp.broadcast_to(wk[h:h + 1, :], (MLSTM_QK_DIM, ML_L)) for h in range(MLSTM_HEADS)],
                axis=0)
            dec_rows = jnp.concatenate(
                [jnp.broadcast_to(decay[h:h + 1, :], (MLSTM_QK_DIM, 1)) for h in range(MLSTM_HEADS)],
                axis=0)
            ktw = (kt2[:, lo:lo + ML_L] * wk_rows).astype(BF16)
            upd = jnp.dot(ktw, va_aug, preferred_element_type=F32)
            caug_ref[...] = dec_rows * caug_ref[...] + jnp.where(mask_c, upd, 0.0)
        return m_row

    lax.fori_loop(0, ML_PAIRS, pair_body, jnp.zeros((MLSTM_HEADS, 1), F32), unroll=ML_PAIR_UNROLL)

    ob_row = lax.broadcasted_iota(jnp.int32, (MLSTM_V_WIDTH, LANES), 0) // MLSTM_V_DIM
    ob_col = lax.broadcasted_iota(jnp.int32, (MLSTM_V_WIDTH, LANES), 1)
    ones_bd = jnp.where(ob_col == ob_row + ML_HL, 1.0, 0.0).astype(BF16)

    def norm_tile(i, carry):
        rows = pl.ds(pl.multiple_of(i * ML_TILE, ML_TILE), ML_TILE)
        num = o_ref[0, rows, :]
        dn = jnp.maximum(jnp.abs(den_ref[rows, :]), floor_ref[rows, :])
        r = 1.0 / dn
        n2_hi, n2_lo = _split_hi_lo(num * num)
        msn = (jnp.dot(n2_hi, ones_bd, preferred_element_type=F32)
               + jnp.dot(n2_lo, ones_bd, preferred_element_type=F32)) * (1.0 / MLSTM_V_DIM)
        fac = r * lax.rsqrt(r * r * msn + NORM_EPS)
        o_ref[0, rows, :] = num * _expand_heads(fac, exp_bf)
        return carry

    lax.fori_loop(0, SEQ // ML_TILE, norm_tile, 0)


def _mlstm(projm3, gt, conv_w):
    b = projm3.shape[0]
    qk_blk = (1, SEQ, MLSTM_QK_WIDTH)
    v_blk = (1, SEQ, MLSTM_V_WIDTH)
    v_col = 2 * MLSTM_QK_WIDTH // MLSTM_V_WIDTH
    vmem = (2 * (2 * SEQ * MLSTM_QK_WIDTH * 4 + 2 * SEQ * MLSTM_V_WIDTH * 4 + 2 * SEQ * LANES * 4)
            + 2 * SEQ * MLSTM_QK_WIDTH * 2 + SEQ * MLSTM_QK_WIDTH * 4 + 2 * SEQ * MLSTM_V_WIDTH * 4
            + 3 * SEQ * LANES * 4 + (8 << 20))
    return pl.pallas_call(
        _mlstm_kernel,
        grid=(b,),
        in_specs=[
            pl.BlockSpec(qk_blk, lambda bi: (bi, 0, 0)),
            pl.BlockSpec(qk_blk, lambda bi: (bi, 0, 1)),
            pl.BlockSpec(v_blk, lambda bi: (bi, 0, v_col)),
            pl.BlockSpec((SUBLANES, SEQ), lambda bi: (1, bi)),
            pl.BlockSpec((SUBLANES, SEQ), lambda bi: (2, bi)),
            pl.BlockSpec((CONV_WIDTH, 2 * MLSTM_QK_WIDTH), lambda bi: (0, 0)),
        ],
        out_specs=pl.BlockSpec(v_blk, lambda bi: (bi, 0, 0)),
        out_shape=jax.ShapeDtypeStruct((b, SEQ, MLSTM_V_WIDTH), F32),
        scratch_shapes=[
            pltpu.VMEM((SEQ, LANES), F32),
            pltpu.VMEM((SEQ, MLSTM_QK_WIDTH), BF16),
            pltpu.VMEM((SEQ, MLSTM_QK_WIDTH), BF16),
            pltpu.VMEM((ML_PAIRS, MLSTM_QK_WIDTH, LANES), F32),
            pltpu.VMEM((ML_PAIRS, SUBLANES, LANES), F32),
            pltpu.VMEM((ML_PAIRS, SUBLANES, LANES), F32),
            pltpu.VMEM((ML_PAIRS, SUBLANES, LANES), F32),
            pltpu.VMEM((SEQ, MLSTM_V_WIDTH), F32),
            pltpu.VMEM((SEQ, MLSTM_V_WIDTH), F32),
            pltpu.VMEM((SEQ, LANES), F32),
            pltpu.VMEM((SEQ, LANES), F32),
            pltpu.VMEM((SEQ, LANES), F32),
            pltpu.VMEM((MLSTM_QK_WIDTH, ML_AUG), F32),
        ],
        compiler_params=pltpu.CompilerParams(
            dimension_semantics=("arbitrary",), vmem_limit_bytes=_vmem_limit(vmem)),
        name="mlstm",
    )(projm3, projm3, projm3, gt, gt, conv_w)


RT_TM = 1024
RT_PARTS = 1


def _out_route_kernel(x_ref, fy_ref, my_ref, mo_ref, wo_ref, mg_ref, nw_ref, wr_hi_ref, wr_lo_ref,
                      rb_ref, x1_ref, h2_ref, eidx_ref, gate_ref, rank_ref, cnt_ref, carry_ref):
    i = pl.program_id(0)

    @pl.when(i == 0)
    def _():
        carry_ref[...] = jnp.zeros_like(carry_ref)

    pm = RT_TM // RT_PARTS
    src = lax.broadcasted_iota(jnp.int32, (pm, pm), 0)
    dst = lax.broadcasted_iota(jnp.int32, (pm, pm), 1)
    upper = jnp.where(src < dst, 1.0, 0.0).astype(BF16)
    zi = jnp.zeros((SUBLANES - TOP_K, pm), jnp.int32)

    for part in range(RT_PARTS):
        rows = slice(part * pm, (part + 1) * pm)
        my = my_ref[rows, :] * mg_ref[...] / (1.0 + jnp.exp(-mo_ref[rows, :]))
        mixed = (jnp.dot(fy_ref[rows, :].astype(BF16), wo_ref[:FOX_WIDTH, :], preferred_element_type=F32)
                 + jnp.dot(my.astype(BF16), wo_ref[FOX_WIDTH:, :], preferred_element_type=F32))
        x1 = x_ref[rows, :] + mixed
        x1_ref[rows, :] = x1
        h2 = _rms(x1, nw_ref[...])
        _rows_to_tiles(h2, h2_ref.at[pl.ds(part * pm * TOK_ROWS, pm * TOK_ROWS), :])

        h_hi, h_lo = _split_hi_lo(h2)
        wr_hi, wr_lo = wr_hi_ref[...], wr_lo_ref[...]
        logit = (lax.dot_general(wr_hi, h_hi, NT_DIMS, preferred_element_type=F32)
                 + lax.dot_general(wr_lo, h_hi, NT_DIMS, preferred_element_type=F32)
                 + lax.dot_general(wr_hi, h_lo, NT_DIMS, preferred_element_type=F32)) + rb_ref[...]

        e_iota = lax.broadcasted_iota(jnp.int32, logit.shape, 0).astype(F32)
        vals, idxs, hots = [], [], []
        for _ in range(TOP_K):
            mk = jnp.max(logit, axis=0, keepdims=True)
            idx = jnp.min(jnp.where(logit == mk, e_iota, float(N_EXPERTS)), axis=0, keepdims=True)
            hot = e_iota == idx
            logit = jnp.where(hot, -jnp.inf, logit)
            vals.append(mk)
            idxs.append(idx.astype(jnp.int32))
            hots.append(hot)
        exps = [jnp.exp(v - vals[0]) for v in vals]
        tot = exps[0] + exps[1] + exps[2] + exps[3]
        gates = [e / tot for e in exps]

        assign = jnp.zeros(logit.shape, F32)
        for hot in hots:
            assign = assign + jnp.where(hot, 1.0, 0.0)
        base = jnp.dot(assign.astype(BF16), upper, preferred_element_type=F32) + carry_ref[:, 0:1]
        ranks = [jnp.sum(jnp.where(hot, base, 0.0), axis=0, keepdims=True) for hot in hots]
        carry_ref[...] = carry_ref[...] + jnp.sum(assign, axis=1, keepdims=True)

        eidx_ref[:, rows] = jnp.concatenate(idxs + [zi], axis=0)
        rank_ref[:, rows] = jnp.concatenate([r.astype(jnp.int32) for r in ranks] + [zi], axis=0)
        gate_ref[:, rows] = jnp.concatenate(gates + [zi.astype(F32)], axis=0)

    cnt_ref[...] = carry_ref[...]


def _out_route(x2d, fox_y2d, mlstm_y2d, projm, w_out_bf, mlstm_gain, moe_norm_w, wr_hi, wr_lo, rb):
    n = x2d.shape[0]
    tm = RT_TM
    const = lambda i: (0, 0)
    mo_col = (MLSTM_COLS - MLSTM_V_WIDTH) // MLSTM_V_WIDTH
    row_blk = lambda w: pl.BlockSpec((tm, w), lambda i: (i, 0))
    lane_blk = pl.BlockSpec((SUBLANES, tm), lambda i: (0, i))
    vmem = (2 * (tm * D_MODEL * 4 * 3 + tm * FOX_WIDTH * 4 * 3 + D_MODEL * D_MODEL * 2)
            + 6 * tm * D_MODEL * 4 + tm * tm * 6)
    return pl.pallas_call(
        _out_route_kernel,
        grid=(n // tm,),
        in_specs=[
            row_blk(D_MODEL), row_blk(FOX_WIDTH), row_blk(MLSTM_V_WIDTH),
            pl.BlockSpec((tm, MLSTM_V_WIDTH), lambda i: (i, mo_col)),
            pl.BlockSpec((D_MODEL, D_MODEL), const),
            pl.BlockSpec((1, MLSTM_V_WIDTH), const),
            pl.BlockSpec((1, D_MODEL), const),
            pl.BlockSpec((N_EXPERTS, D_MODEL), const),
            pl.BlockSpec((N_EXPERTS, D_MODEL), const),
            pl.BlockSpec((N_EXPERTS, 1), const),
        ],
        out_specs=[row_blk(D_MODEL), pl.BlockSpec((tm * TOK_ROWS, LANES), lambda i: (i, 0)),
                   lane_blk, lane_blk, lane_blk, pl.BlockSpec((N_EXPERTS, LANES), const)],
        out_shape=[
            jax.ShapeDtypeStruct((n, D_MODEL), F32),
            jax.ShapeDtypeStruct((n * TOK_ROWS, LANES), F32),
            jax.ShapeDtypeStruct((SUBLANES, n), jnp.int32),
            jax.ShapeDtypeStruct((SUBLANES, n), F32),
            jax.ShapeDtypeStruct((SUBLANES, n), jnp.int32),
            jax.ShapeDtypeStruct((N_EXPERTS, LANES), F32),
        ],
        scratch_shapes=[pltpu.VMEM((N_EXPERTS, LANES), F32)],
        compiler_params=pltpu.CompilerParams(
            dimension_semantics=("arbitrary",), vmem_limit_bytes=_vmem_limit(vmem)),
        name="out_route",
    )(x2d, fox_y2d, mlstm_y2d, projm, w_out_bf, mlstm_gain, moe_norm_w, wr_hi, wr_lo, rb)


INV_CHUNK = 8192
INV_UNROLL = 32


def _invert_kernel(pos_ref, zeros_hbm, inv_ref, sem):
    i = pl.program_id(0)

    @pl.when(i == 0)
    def _():
        cp = pltpu.make_async_copy(zeros_hbm, inv_ref, sem.at[0])
        cp.start()
        cp.wait()

    base = i * INV_CHUNK

    def body(j, carry):
        inv_ref[pos_ref[0, 0, j]] = base + j
        return carry

    lax.fori_loop(0, INV_CHUNK, body, 0, unroll=INV_UNROLL)


def _invert(pos_flat, n_rows):
    n_slots = pos_flat.shape[0]
    steps = n_slots // INV_CHUNK
    return pl.pallas_call(
        _invert_kernel,
        grid=(steps,),
        in_specs=[
            pl.BlockSpec((1, 1, INV_CHUNK), lambda i: (i, 0, 0), memory_space=pltpu.SMEM),
            pl.BlockSpec(memory_space=pl.ANY),
        ],
        out_specs=pl.BlockSpec(memory_space=pltpu.SMEM),
        out_shape=jax.ShapeDtypeStruct((n_rows,), jnp.int32),
        scratch_shapes=[pltpu.SemaphoreType.DMA((1,))],
        compiler_params=pltpu.CompilerParams(dimension_semantics=("arbitrary",)),
        name="invert",
    )(pos_flat.reshape(steps, 1, INV_CHUNK), jnp.zeros((n_rows,), jnp.int32))


EX_BM = 256
EX_DRAIN_STEPS = 2


def _experts_kernel(be_ref, nu_ref, nv_ref, tok_ref, tokn_ref, dst_ref, h2_hbm, wgu_ref, bgu_ref,
                    wd_ref, bd_ref, y_hbm, xt_ref, yt_ref, wgu_bf_ref, wd_bf_ref, gsem, ssem):
    i = pl.program_id(0)
    last_blk = pl.num_programs(0) - 1 - EX_DRAIN_STEPS
    nu = nu_ref[0]
    slot = i % 2
    cur = jnp.minimum(i, last_blk)
    tile_rows = EX_BM * TOK_ROWS

    def tok_tile(ref, idx):
        return ref.at[pl.ds(pl.multiple_of(idx, TOK_ROWS), TOK_ROWS), :]

    def start_gather(idx_ref, s, n):
        @pl.when(n == EX_BM)
        def _():
            for r in range(EX_BM):
                pltpu.make_async_copy(tok_tile(h2_hbm, idx_ref[0, 0, r]),
                                      xt_ref.at[s, pl.ds(r * TOK_ROWS, TOK_ROWS), :], gsem.at[s]).start()

        @pl.when(n < EX_BM)
        def _():
            def body(r, carry):
                pltpu.make_async_copy(
                    tok_tile(h2_hbm, idx_ref[0, 0, r]),
                    xt_ref.at[s, pl.ds(pl.multiple_of(r * TOK_ROWS, TOK_ROWS), TOK_ROWS), :],
                    gsem.at[s]).start()
                return carry
            lax.fori_loop(0, n, body, 0)

    def scatter_row(r):
        pltpu.make_async_copy(yt_ref.at[slot, pl.ds(r * TOK_ROWS, TOK_ROWS), :],
                              tok_tile(y_hbm, dst_ref[0, 0, r]), ssem.at[slot]).start()

    def wait_block(sem, buf):
        pltpu.make_async_copy(h2_hbm.at[pl.ds(0, tile_rows), :], buf, sem).wait()

    def wait_tokens(sem, buf, n):
        @pl.when(n == EX_BM)
        def _():
            wait_block(sem, buf)

        @pl.when(n < EX_BM)
        def _():
            def body(r, carry):
                pltpu.make_async_copy(h2_hbm.at[pl.ds(0, TOK_ROWS), :],
                                      buf.at[pl.ds(0, TOK_ROWS), :], sem).wait()
                return carry
            lax.fori_loop(0, n, body, 0)

    @pl.when(i == 0)
    def _():
        xt_ref[...] = jnp.zeros_like(xt_ref)
        start_gather(tok_ref, 0, nv_ref[0])

    @pl.when(i + 1 < nu)
    def _():
        start_gather(tokn_ref, 1 - slot, nv_ref[jnp.minimum(i + 1, last_blk)])

    @pl.when((i >= 2) & (i - 2 < nu))
    def _():
        wait_tokens(ssem.at[slot], yt_ref.at[slot], nv_ref[jnp.clip(i - 2, 0, last_blk)])

    @pl.when(i < nu)
    def _():
        @pl.when((i == 0) | (be_ref[cur] != be_ref[jnp.maximum(cur - 1, 0)]))
        def _():
            wgu_bf_ref[...] = wgu_ref[0].astype(BF16)
            wd_bf_ref[...] = wd_ref[0].astype(BF16)

        wait_tokens(gsem.at[slot], xt_ref.at[slot], nv_ref[cur])
        xb = _tiles_to_rows(xt_ref.at[slot], EX_BM).astype(BF16)
        gu = jnp.dot(xb, wgu_bf_ref[...], preferred_element_type=F32) + bgu_ref[0]
        gate = jnp.minimum(gu[:, :D_EXPERT], SWIGLU_LIMIT)
        up = jnp.clip(gu[:, D_EXPERT:], -SWIGLU_LIMIT, SWIGLU_LIMIT)
        act = (up + 1.0) * (gate / (1.0 + jnp.exp(-SWIGLU_ALPHA * gate)))
        y = jnp.dot(act.astype(BF16), wd_bf_ref[...], preferred_element_type=F32) + bd_ref[0]
        _rows_to_tiles(y, yt_ref.at[slot])
        nv = nv_ref[cur]

        @pl.when(nv == EX_BM)
        def _():
            for r in range(EX_BM):
                scatter_row(r)

        @pl.when(nv < EX_BM)
        def _():
            def body(r, carry):
                pltpu.make_async_copy(
                    yt_ref.at[slot, pl.ds(pl.multiple_of(r * TOK_ROWS, TOK_ROWS), TOK_ROWS), :],
                    tok_tile(y_hbm, dst_ref[0, 0, r]), ssem.at[slot]).start()
                return carry
            lax.fori_loop(0, nv, body, 0)


def _experts(block_e, n_used, n_valid, buf_tok3, dst3, h2, w_gu, b_gu, w_down, b_down, n_slots):
    nb = buf_tok3.shape[0]
    idx_blk = lambda f: pl.BlockSpec((1, 1, EX_BM), f, memory_space=pltpu.SMEM)
    vmem = (2 * (D_MODEL * 2 * D_EXPERT * 4 + D_EXPERT * D_MODEL * 4)
            + D_MODEL * 2 * D_EXPERT * 2 + D_EXPERT * D_MODEL * 2
            + 3 * EX_BM * D_MODEL * 4 + 3 * EX_BM * 2 * D_EXPERT * 4)
    blk = lambda i: jnp.minimum(i, nb - 1)
    w_map = lambda i, be, nu, nv: (be[blk(i)], 0, 0)
    grid_spec = pltpu.PrefetchScalarGridSpec(
        num_scalar_prefetch=3,
        grid=(nb + EX_DRAIN_STEPS,),
        in_specs=[
            idx_blk(lambda i, be, nu, nv: (blk(i), 0, 0)),
            idx_blk(lambda i, be, nu, nv: (blk(i + 1), 0, 0)),
            idx_blk(lambda i, be, nu, nv: (blk(i), 0, 0)),
            pl.BlockSpec(memory_space=pl.ANY),
            pl.BlockSpec((1, D_MODEL, 2 * D_EXPERT), w_map),
            pl.BlockSpec((1, 1, 2 * D_EXPERT), w_map),
            pl.BlockSpec((1, D_EXPERT, D_MODEL), w_map),
            pl.BlockSpec((1, 1, D_MODEL), w_map),
        ],
        out_specs=pl.BlockSpec(memory_space=pl.ANY),
        scratch_shapes=[
            pltpu.VMEM((2, EX_BM * TOK_ROWS, LANES), F32),
            pltpu.VMEM((2, EX_BM * TOK_ROWS, LANES), F32),
            pltpu.VMEM((D_MODEL, 2 * D_EXPERT), BF16),
            pltpu.VMEM((D_EXPERT, D_MODEL), BF16),
            pltpu.SemaphoreType.DMA((2,)),
            pltpu.SemaphoreType.DMA((2,)),
        ],
    )
    return pl.pallas_call(
        _experts_kernel,
        grid_spec=grid_spec,
        out_shape=jax.ShapeDtypeStruct((n_slots * TOK_ROWS, LANES), F32),
        compiler_params=pltpu.CompilerParams(
            dimension_semantics=("arbitrary",), vmem_limit_bytes=_vmem_limit(vmem)),
        name="experts",
    )(block_e, n_used, n_valid, buf_tok3, buf_tok3, dst3, h2, w_gu, b_gu[:, None, :], w_down,
      b_down[:, None, :])


CB_TM = 512


def _combine_kernel(x1_ref, y0_ref, y1_ref, y2_ref, y3_ref, gate_ref, o_ref):
    tm = x1_ref.shape[0]
    g = jnp.concatenate([gate_ref[...], jnp.zeros((LANES - SUBLANES, tm), F32)], axis=0).T
    acc = x1_ref[...]
    for k, y_ref in enumerate((y0_ref, y1_ref, y2_ref, y3_ref)):
        acc = acc + g[:, k:k + 1] * _tiles_to_rows(y_ref, tm)
    o_ref[...] = acc


def _combine(x1, y_slots, gates):
    n = x1.shape[0]
    tm = CB_TM
    nt = n // tm
    vmem = 2 * (2 * tm * D_MODEL * 4 + tm * TOP_K * D_MODEL * 4) + 4 * tm * D_MODEL * 4
    y_spec = lambda k: pl.BlockSpec((tm * TOK_ROWS, LANES), lambda i: (k * nt + i, 0))
    return pl.pallas_call(
        _combine_kernel,
        grid=(nt,),
        in_specs=[pl.BlockSpec((tm, D_MODEL), lambda i: (i, 0))]
        + [y_spec(k) for k in range(TOP_K)]
        + [pl.BlockSpec((SUBLANES, tm), lambda i: (0, i))],
        out_specs=pl.BlockSpec((tm, D_MODEL), lambda i: (i, 0)),
        out_shape=jax.ShapeDtypeStruct((n, D_MODEL), F32),
        compiler_params=pltpu.CompilerParams(
            dimension_semantics=("arbitrary",), vmem_limit_bytes=_vmem_limit(vmem)),
        name="combine",
    )(x1, y_slots, y_slots, y_slots, y_slots, gates)


def _dispatch_plan(eidx, rank, counts):
    n = eidx.shape[1]
    n_slots = n * TOP_K
    nb = n_slots // EX_BM + N_EXPERTS
    counts = counts.astype(jnp.int32)
    padded = ((counts + EX_BM - 1) // EX_BM) * EX_BM
    ids = jnp.arange(N_EXPERTS, dtype=jnp.int32)
    pad_end = jnp.sum(jnp.where(ids[None, :] <= ids[:, None], padded[None, :], 0), axis=1)
    pad_start = pad_end - padded
    e = eidx[:TOP_K]
    start_of = jnp.sum(jnp.where(e[:, :, None] == jnp.arange(N_EXPERTS, dtype=jnp.int32),
                                 pad_start[None, None, :], 0), axis=-1)
    pos = start_of + rank[:TOP_K]
    inv = _invert(pos.reshape(-1), nb * EX_BM)
    buf_tok = (inv % n) * TOK_ROWS
    dst = inv * TOK_ROWS
    blk_start = jnp.arange(nb, dtype=jnp.int32) * EX_BM
    block_e = jnp.minimum(jnp.sum((pad_end[None, :] <= blk_start[:, None]).astype(jnp.int32), axis=1),
                          N_EXPERTS - 1)
    n_used = (pad_end[-1] // EX_BM).astype(jnp.int32).reshape(1)
    n_valid = jnp.clip(pad_start[block_e] + counts[block_e] - blk_start, 0, EX_BM).astype(jnp.int32)
    n_valid = jnp.where(blk_start < pad_end[-1], n_valid, 0)
    return (block_e, n_used, n_valid, buf_tok.reshape(nb, 1, EX_BM), dst.reshape(nb, 1, EX_BM), n_slots)


def _prep_in_proj_weights(w_in, fox_f_bias, mlstm_i_bias, mlstm_f_bias):
    split_at = []
    acc = 0
    for wdt in SPLIT_WIDTHS[:-1]:
        acc += wdt
        split_at.append(acc)
    starts = [0] + split_at
    gate_cols = [w_in[:, s:s + wdt] for s, wdt, is_main in zip(starts, SPLIT_WIDTHS, IN_W_IS_MAIN)
                 if not is_main]
    w_gate = jnp.concatenate(gate_cols, axis=-1)
    bias = jnp.concatenate([fox_f_bias, mlstm_i_bias, mlstm_f_bias]).astype(F32)
    n_gate = w_gate.shape[1]
    wg_hi, wg_lo = _split_hi_lo(jnp.pad(w_gate, ((0, 0), (0, LANES - n_gate))))
    bias_r = jnp.pad(bias, (0, LANES - n_gate))[None, :]
    return w_in, wg_hi, wg_lo, bias_r


def kernel(x, attn_norm_w, w_in, fox_f_bias, fox_q_norm_w, fox_k_norm_w, fox_out_norm_w, mlstm_conv_w, mlstm_i_bias, mlstm_f_bias, mlstm_out_norm_w, w_out, moe_norm_w, router_w, router_b, expert_w_gate_up, expert_b_gate_up, expert_w_down, expert_b_down):
    bsz, seq, d = x.shape
    x2d = x.reshape(bsz * seq, d)
    prep = _prep_in_proj_weights(w_in[0], fox_f_bias[0], mlstm_i_bias[0], mlstm_f_bias[0])
    pair = lambda w: jnp.tile(w, LANES // HEAD_DIM)[None, :]
    qn, kn, aq, ak, vb, projm, gt = _in_proj(x2d, attn_norm_w[0][None, :], *prep,
                                             pair(fox_q_norm_w[0]), pair(fox_k_norm_w[0]))
    b3 = lambda a: a.reshape(bsz, seq, a.shape[-1])
    fox_y = _fox(b3(qn), b3(kn), b3(aq), b3(ak), b3(vb), fox_out_norm_w[0][None, :])
    mlstm_y = _mlstm(b3(projm), gt, mlstm_conv_w[0])
    return _channel_mixer(x2d, fox_y.reshape(-1, FOX_WIDTH), mlstm_y.reshape(-1, MLSTM_V_WIDTH), projm,
                          mlstm_out_norm_w[0], w_out[0], moe_norm_w[0], router_w[0], router_b[0],
                          expert_w_gate_up[0], expert_b_gate_up[0], expert_w_down[0],
                          expert_b_down[0]).reshape(bsz, seq, d)


def _channel_mixer(x2d, fox_y2d, mlstm_y2d, projm, mlstm_gain, w_out, moe_norm_w, router_w, router_b,
                   w_gu, b_gu, w_down, b_down):
    wr_hi, wr_lo = _split_hi_lo(router_w.T)
    x1, h2, eidx, gates, rank, counts = _out_route(
        x2d, fox_y2d, mlstm_y2d, projm, w_out.astype(BF16), mlstm_gain[None, :], moe_norm_w[None, :],
        wr_hi, wr_lo, router_b[:, None])
    block_e, n_used, n_valid, buf_tok3, dst3, n_slots = _dispatch_plan(eidx, rank, counts[:, 0])
    y_slots = _experts(block_e, n_used, n_valid, buf_tok3, dst3, h2, w_gu, b_gu, w_down, b_down, n_slots)
    return _combine(x1, y_slots, gates)
```
